```python
import jax, jax.numpy as jnp
from jax import lax
import numpy as np

D_MODEL = 2048
BATCH = 2
SEQ = 4096
DEPTH = 1
DEC_BATCH = 128
DEC_SEQ = 8
PAST_LEN = 2048
PAGE_SIZE = 128

HEAD_DIM = 64
MIX_WIDTH = D_MODEL
RWKV_WIDTH = MIX_WIDTH // 2
FOX_WIDTH = MIX_WIDTH - RWKV_WIDTH
RWKV_HEADS = RWKV_WIDTH // HEAD_DIM
FOX_HEADS = FOX_WIDTH // HEAD_DIM
DECAY_LORA = 64
AAA_LORA = 64
GATE_LORA = 160
RWKV_IN = 3 * RWKV_WIDTH + DECAY_LORA + AAA_LORA + GATE_LORA
FOX_IN = 4 * FOX_WIDTH + FOX_HEADS
IN_COLS = RWKV_IN + FOX_IN
RWKV_SPLITS = [RWKV_WIDTH, 2 * RWKV_WIDTH, 3 * RWKV_WIDTH, 3 * RWKV_WIDTH + DECAY_LORA, 3 * RWKV_WIDTH + DECAY_LORA + AAA_LORA]
FOX_SPLITS = [FOX_WIDTH, 2 * FOX_WIDTH, 3 * FOX_WIDTH, 4 * FOX_WIDTH]
Q_BLOCK = 128
N_GROUPS = 8
EXPERTS_PER_GROUP = 8
N_EXPERTS = N_GROUPS * EXPERTS_PER_GROUP
TOP_K_IN_GROUP = 2
D_EXPERT = 512
MOE_BLOCK = 128
PLE_DIM = 256
RMS_EPS = 1e-6
LNX_EPS = 64e-5

kernel_name = 'hymba_rwkv7_fox_hmoe_decode_step'


def rms_norm(x, g):
    xf = x.astype(jnp.float32)
    y = xf * lax.rsqrt(jnp.mean(xf * xf, axis=-1, keepdims=True) + RMS_EPS)
    return (y * g.astype(jnp.float32)).astype(x.dtype)


def mixer_inputs(x, g_attn, w_in, b_f, q_norm, k_norm):
    B, T = x.shape[:2]
    z = rms_norm(x, g_attn) @ w_in
    z_rwkv, z_fox = z[..., :RWKV_IN], z[..., RWKV_IN:]
    q, k, v, og, f = jnp.split(z_fox, FOX_SPLITS, axis=-1)
    heads = lambda t: t.reshape(B, T, FOX_HEADS, HEAD_DIM)
    q = rms_norm(heads(q), q_norm)
    k = rms_norm(heads(k), k_norm)
    logf = jax.nn.log_sigmoid((f + b_f).astype(jnp.float32))
    return z_rwkv, q, k, heads(v), og, logf


def wkv_scan(S0, r, w, k, v, a, b):
    def step(S, inp):
        r_t, w_t, k_t, v_t, a_t, b_t = inp
        sa = jnp.einsum('bhvk,bhk->bhv', S, a_t)
        S = S * w_t[:, :, None, :] + sa[..., None] * b_t[:, :, None, :] + v_t[..., None] * k_t[:, :, None, :]
        return S, jnp.einsum('bhvk,bhk->bhv', S, r_t)
    xs = tuple(jnp.moveaxis(t.astype(jnp.float32), 1, 0) for t in (r, w, k, v, a, b))
    S, ys = lax.scan(step, S0.astype(jnp.float32), xs)
    return S, jnp.moveaxis(ys, 0, 1)


def rwkv_mix(z, shift_prev, wkv_prev, mu_shift, w0, w_up, a0, a_up, g_up, k_k, k_a, r_k, lnx_w, lnx_b):
    B, T, _ = z.shape
    z_prev = jnp.concatenate([shift_prev[:, None, :].astype(z.dtype), z[:, :-1]], axis=1)
    zm = z + (z_prev - z) * mu_shift
    r, k, v, wd, ad, gd = jnp.split(zm, RWKV_SPLITS, axis=-1)
    log_w = -jax.nn.softplus(-(w0 + jnp.tanh(wd) @ w_up).astype(jnp.float32)) - 0.5
    decay = jnp.exp(-jnp.exp(log_w))
    a = jax.nn.sigmoid((a0 + ad @ a_up).astype(jnp.float32))
    g = (jax.nn.sigmoid(gd) @ g_up).astype(jnp.float32)
    hd = lambda t: t.reshape(B, T, RWKV_HEADS, HEAD_DIM)
    kk = hd((k * k_k).astype(jnp.float32))
    kk = kk / jnp.maximum(jnp.sqrt(jnp.sum(kk * kk, axis=-1, keepdims=True)), 1e-12)
    kf = k.astype(jnp.float32) * (1.0 + (a - 1.0) * k_a.astype(jnp.float32))
    rh, kh, vh, ah = hd(r.astype(jnp.float32)), hd(kf), hd(v.astype(jnp.float32)), hd(a)
    wkv_new, y = wkv_scan(wkv_prev, rh, hd(decay), kh, vh, -kk, kk * ah)
    mu = jnp.mean(y, axis=-1, keepdims=True)
    var = jnp.mean(jnp.square(y - mu), axis=-1, keepdims=True)
    yn = ((y - mu) * lax.rsqrt(var + LNX_EPS)).reshape(B, T, RWKV_WIDTH) * lnx_w.astype(jnp.float32) + lnx_b.astype(jnp.float32)
    bonus = jnp.sum(rh * kh * r_k.astype(jnp.float32), axis=-1, keepdims=True) * vh
    out = ((yn + bonus.reshape(B, T, RWKV_WIDTH)) * g).astype(z.dtype)
    return out, wkv_new.astype(z.dtype), z[:, -1]


def fox_attend(q, k, v, c_q, c_k, q_pos, k_pos):
    s = jnp.einsum('bqhd,bkhd->bhqk', q, k).astype(jnp.float32) * (HEAD_DIM ** -0.5)
    s = s + jnp.moveaxis(c_q, 1, 2)[..., None] - jnp.moveaxis(c_k, 1, 2)[:, :, None, :]
    s = jnp.where(k_pos[None, :] <= q_pos[:, None], s, -jnp.inf)
    p = jax.nn.softmax(s, axis=-1)
    return jnp.einsum('bhqk,bkhd->bqhd', p.astype(v.dtype), v)


def fox_prompt(q, k, v, logf):
    B, T, H, Dh = q.shape
    c = jnp.cumsum(logf, axis=1)
    nb = T // Q_BLOCK
    pos = jnp.arange(T)
    qb = q.reshape(B, nb, Q_BLOCK, H, Dh).swapaxes(0, 1)
    cb = c.reshape(B, nb, Q_BLOCK, H).swapaxes(0, 1)
    pb = pos.reshape(nb, Q_BLOCK)
    o = lax.map(lambda blk: fox_attend(blk[0], k, v, blk[1], c, blk[2], pos), (qb, cb, pb))
    return o.swapaxes(0, 1).reshape(B, T, H, Dh)


def fox_sample(q, k, v, logf, cache_k, cache_v, cache_logf, page_table):
    DB, T = q.shape[:2]
    past = page_table.shape[1] * PAGE_SIZE
    gather = lambda pool: pool[page_table].reshape((DB, past) + pool.shape[2:])
    k_all = jnp.concatenate([gather(cache_k).astype(k.dtype), k], axis=1)
    v_all = jnp.concatenate([gather(cache_v).astype(v.dtype), v], axis=1)
    lf_all = jnp.concatenate([gather(cache_logf).astype(jnp.float32), logf], axis=1)
    c = jnp.cumsum(lf_all, axis=1)
    return fox_attend(q, k_all, v_all, c[:, past:], c, past + jnp.arange(T), jnp.arange(past + T))


def fox_output(o, og, g_fox_o):
    B, T = o.shape[:2]
    on = rms_norm(o, g_fox_o.reshape(FOX_HEADS, HEAD_DIM))
    return on.reshape(B, T, FOX_WIDTH) * jax.nn.sigmoid(og)


def hier_moe(h, w_rg, b_rg, w_re, b_re, w_e_gate, w_e_up, w_e_down):
    B, T, D = h.shape
    hf = h.reshape(-1, D)
    N = hf.shape[0]
    pg = jax.nn.softmax((hf @ w_rg + b_rg).astype(jnp.float32), axis=-1)
    pg_top, g_sel = lax.top_k(pg, 1)
    le = (hf @ w_re + b_re).astype(jnp.float32).reshape(N, N_GROUPS, EXPERTS_PER_GROUP)
    le = jnp.take_along_axis(le, g_sel[:, :, None], axis=1)[:, 0]
    pe_top, e_in = lax.top_k(jax.nn.softmax(le, axis=-1), TOP_K_IN_GROUP)
    gate = pg_top * pe_top / jnp.sum(pe_top, axis=-1, keepdims=True)
    expert = g_sel * EXPERTS_PER_GROUP + e_in
    A = N * TOP_K_IN_GROUP
    flat_e = expert.reshape(-1)
    flat_g = gate.reshape(-1)
    order = jnp.argsort(flat_e)
    se = flat_e[order]
    stok = (order // TOP_K_IN_GROUP).astype(jnp.int32)
    counts = jnp.bincount(flat_e, length=N_EXPERTS)
    pcounts = (counts + MOE_BLOCK - 1) // MOE_BLOCK * MOE_BLOCK
    pend = jnp.cumsum(pcounts)
    pstart = pend - pcounts
    start = jnp.cumsum(counts) - counts
    dest = pstart[se] + jnp.arange(A) - start[se]
    n_blocks = -(-A // MOE_BLOCK) + N_EXPERTS
    P = n_blocks * MOE_BLOCK
    slot_tok = jnp.full((P,), N, jnp.int32).at[dest].set(stok)
    slot_w = jnp.zeros((P,), jnp.float32).at[dest].set(flat_g[order])
    block_e = jnp.minimum(jnp.searchsorted(pend, jnp.arange(n_blocks) * MOE_BLOCK, side='right'), N_EXPERTS - 1)
    hp = jnp.concatenate([hf, jnp.zeros((1, D), hf.dtype)], axis=0)
    xb = hp[slot_tok].reshape(n_blocks, MOE_BLOCK, D)
    def run(blk):
        xe, e = blk
        return (jax.nn.silu(xe @ w_e_gate[e]) * (xe @ w_e_up[e])) @ w_e_down[e]
    yb = lax.map(run, (xb, block_e)).reshape(P, D)
    y = jnp.zeros((N + 1, D), jnp.float32).at[slot_tok].add(yb.astype(jnp.float32) * slot_w[:, None])[:N]
    return y.astype(h.dtype).reshape(B, T, D)


def layer(x, p, shift_prev, wkv_prev, fox_fn, lw):
    (g_attn, w_in, mu_shift, w0, w_up, a0, a_up, g_up, k_k, k_a, r_k, lnx_w, lnx_b, b_f, q_norm, k_norm,
     g_fox_o, w_out, g_ffn, w_rg, b_rg, w_re, b_re, w_e_gate, w_e_up, w_e_down, g_ple, w_pg, b_pg, w_pp) = lw
    z_rwkv, q, k, v, og, logf = mixer_inputs(x, g_attn, w_in, b_f, q_norm, k_norm)
    o_r, wkv_new, shift_new = rwkv_mix(z_rwkv, shift_prev, wkv_prev, mu_shift, w0, w_up, a0, a_up, g_up,
                                       k_k, k_a, r_k, lnx_w, lnx_b)
    o_f = fox_output(fox_fn(q, k, v, logf), og, g_fox_o)
    x = x + jnp.concatenate([o_r, o_f.astype(o_r.dtype)], axis=-1) @ w_out
    x = x + hier_moe(rms_norm(x, g_ffn), w_rg, b_rg, w_re, b_re, w_e_gate, w_e_up, w_e_down)
    x = x + jax.nn.sigmoid(rms_norm(x, g_ple) @ w_pg + b_pg) * (p @ w_pp)
    return x, k, v, logf.astype(x.dtype), wkv_new, shift_new


def setup_inputs(seed: int = 0) -> dict:
    key = jax.random.key(seed)
    ks = iter(jax.random.split(key, 64))
    nrm = lambda shape, scale=1.0: jax.random.normal(next(ks), shape, jnp.float32) * scale
    uni = lambda shape, lo, hi: jax.random.uniform(next(ks), shape, jnp.float32, lo, hi)
    n_pages = PAST_LEN // PAGE_SIZE
    n_used = DEC_BATCH * n_pages
    n_pool = (n_used * 5 + 3) // 4
    L = DEPTH
    return {
        'x_prompt': nrm((BATCH, SEQ, D_MODEL)),
        'x_sample': nrm((DEC_BATCH, DEC_SEQ, D_MODEL)),
        'cache_k': nrm((L, n_pool, PAGE_SIZE, FOX_HEADS, HEAD_DIM)),
        'cache_v': nrm((L, n_pool, PAGE_SIZE, FOX_HEADS, HEAD_DIM)),
        'cache_logf': jax.nn.log_sigmoid(nrm((L, n_pool, PAGE_SIZE, FOX_HEADS), 1.5) + 3.5),
        'state_wkv': nrm((L, DEC_BATCH, RWKV_HEADS, HEAD_DIM, HEAD_DIM), 0.3),
        'state_shift': nrm((L, DEC_BATCH, RWKV_IN)),
        'page_table': jax.random.permutation(next(ks), n_pool)[:n_used].reshape(DEC_BATCH, n_pages).astype(jnp.int32),
        'p_prompt': nrm((L, BATCH, SEQ, PLE_DIM)),
        'p_sample': nrm((L, DEC_BATCH, DEC_SEQ, PLE_DIM)),
        'g_attn': 1.0 + nrm((L, D_MODEL), 0.02),
        'w_in': nrm((L, D_MODEL, IN_COLS), D_MODEL ** -0.5),
        'mu_shift': uni((L, RWKV_IN), 0.0, 1.0),
        'w0': uni((L, RWKV_WIDTH), -4.0, 0.0),
        'w_up': nrm((L, DECAY_LORA, RWKV_WIDTH), 0.5 * DECAY_LORA ** -0.5),
        'a0': nrm((L, RWKV_WIDTH), 0.1),
        'a_up': nrm((L, AAA_LORA, RWKV_WIDTH), AAA_LORA ** -0.5),
        'g_up': nrm((L, GATE_LORA, RWKV_WIDTH), GATE_LORA ** -0.5),
        'k_k': 0.85 + nrm((L, RWKV_WIDTH), 0.05),
        'k_a': 1.0 + nrm((L, RWKV_WIDTH), 0.05),
        'r_k': nrm((L, RWKV_HEADS, HEAD_DIM), 0.1),
        'lnx_w': 1.0 + nrm((L, RWKV_WIDTH), 0.02),
        'lnx_b': nrm((L, RWKV_WIDTH), 0.02),
        'b_f': uni((L, FOX_HEADS), 1.0, 6.0),
        'q_norm': 1.0 + nrm((L, HEAD_DIM), 0.02),
        'k_norm': 1.0 + nrm((L, HEAD_DIM), 0.02),
        'g_fox_o': 1.0 + nrm((L, FOX_WIDTH), 0.02),
        'w_out': nrm((L, MIX_WIDTH, D_MODEL), MIX_WIDTH ** -0.5),
        'g_ffn': 1.0 + nrm((L, D_MODEL), 0.02),
        'w_rg': nrm((L, D_MODEL, N_GROUPS), D_MODEL ** -0.5),
        'b_rg': nrm((L, N_GROUPS), 0.01),
        'w_re': nrm((L, D_MODEL, N_EXPERTS), D_MODEL ** -0.5),
        'b_re': nrm((L, N_EXPERTS), 0.01),
        'w_e_gate': nrm((L, N_EXPERTS, D_MODEL, D_EXPERT), D_MODEL ** -0.5),
        'w_e_up': nrm((L, N_EXPERTS, D_MODEL, D_EXPERT), D_MODEL ** -0.5),
        'w_e_down': nrm((L, N_EXPERTS, D_EXPERT, D_MODEL), D_EXPERT ** -0.5),
        'g_ple': 1.0 + nrm((L, D_MODEL), 0.02),
        'w_pg': nrm((L, D_MODEL, D_MODEL), D_MODEL ** -0.5),
        'b_pg': nrm((L, D_MODEL), 0.01),
        'w_pp': nrm((L, PLE_DIM, D_MODEL), PLE_DIM ** -0.5),
        'g_final': 1.0 + nrm((D_MODEL,), 0.02),
    }


def reference(x_prompt, x_sample, cache_k, cache_v, cache_logf, state_wkv, state_shift, page_table, p_prompt, p_sample,
              g_attn, w_in, mu_shift, w0, w_up, a0, a_up, g_up, k_k, k_a, r_k, lnx_w, lnx_b, b_f, q_norm, k_norm,
              g_fox_o, w_out, g_ffn, w_rg, b_rg, w_re, b_re, w_e_gate, w_e_up, w_e_down, g_ple, w_pg, b_pg, w_pp, g_final):
    xp, xs = x_prompt, x_sample
    outs_p, outs_s = [], []
    for i in range(DEPTH):
        lw = (g_attn[i], w_in[i], mu_shift[i], w0[i], w_up[i], a0[i], a_up[i], g_up[i], k_k[i], k_a[i], r_k[i],
              lnx_w[i], lnx_b[i], b_f[i], q_norm[i], k_norm[i], g_fox_o[i], w_out[i], g_ffn[i], w_rg[i], b_rg[i],
              w_re[i], b_re[i], w_e_gate[i], w_e_up[i], w_e_down[i], g_ple[i], w_pg[i], b_pg[i], w_pp[i])
        shift0 = jnp.zeros((xp.shape[0], RWKV_IN), xp.dtype)
        wkv0 = jnp.zeros((xp.shape[0], RWKV_HEADS, HEAD_DIM, HEAD_DIM), xp.dtype)
        xp, *st_p = layer(xp, p_prompt[i], shift0, wkv0, fox_prompt, lw)
        outs_p.append(st_p)
        fox_s = lambda q, k, v, lf, ck=cache_k[i], cv=cache_v[i], cl=cache_logf[i]: fox_sample(q, k, v, lf, ck, cv, cl, page_table)
        xs, *st_s = layer(xs, p_sample[i], state_shift[i], state_wkv[i], fox_s, lw)
        outs_s.append(st_s)
    y_prompt = rms_norm(xp, g_final)
    y_sample = rms_norm(xs, g_final)
    k_p, v_p, lf_p, wkv_p, sh_p = [jnp.stack(t) for t in zip(*outs_p)]
    k_s, v_s, lf_s, wkv_s, sh_s = [jnp.stack(t) for t in zip(*outs_s)]
    return (y_prompt, y_sample, k_p, v_p, lf_p, wkv_p, sh_p, k_s, v_s, lf_s, wkv_s, sh_s)
```

```python
import functools

import jax
import jax.numpy as jnp
from jax import lax
from jax.experimental import pallas as pl
from jax.experimental.pallas import tpu as pltpu

F32 = jnp.float32
BF16 = jnp.bfloat16

HEAD_DIM = 64
LANES = 128
RMS_EPS = 1e-6
LNX_EPS = 64e-5
PAGE = 128
MOE_BLOCK = 128
N_GROUPS = 8
N_EXPERTS = 64
DEC_SEQ = 8

C_LORA = 3072
C_F = 3456
RWKV_COLS = 3584
C_Q, C_FK, C_FV, C_OG = 4096, 5120, 6144, 7168
Z_COLS = 8192
WIDTH = 1024

VMEM_LIMIT = 52 * 1024 * 1024


def _cp(sem, vmem=VMEM_LIMIT):
    return pltpu.CompilerParams(dimension_semantics=sem, vmem_limit_bytes=vmem)


def _tile(n, pref):
    for t in (1024, 512, 256, 128, 64, 32, 16, 8):
        if t <= pref and n % t == 0:
            return t
    raise ValueError(f"no tile for {n}")


def _split2(x):
    hi = x.astype(BF16)
    lo = (x - hi.astype(F32)).astype(BF16)
    return hi, lo


def _split3(x):
    hi = x.astype(BF16)
    r1 = x - hi.astype(F32)
    mid = r1.astype(BF16)
    lo = (r1 - mid.astype(F32)).astype(BF16)
    return hi, mid, lo


def _dot(a, b):
    return jnp.dot(a, b, preferred_element_type=F32)


def _dot_nt(a, b):
    return lax.dot_general(a, b, (((1,), (1,)), ((), ())), preferred_element_type=F32)


def _block_diag_ones():
    r = lax.broadcasted_iota(jnp.int32, (LANES, LANES), 0)
    c = lax.broadcasted_iota(jnp.int32, (LANES, LANES), 1)
    return jnp.where((r >> 6) == (c >> 6), 1.0, 0.0).astype(BF16)


def _segsum(x, bd):
    outs = []
    for j in range(x.shape[1] // LANES):
        hi, lo = _split2(x[:, j * LANES:(j + 1) * LANES])
        outs.append(_dot(hi, bd) + _dot(lo, bd))
    return outs[0] if len(outs) == 1 else jnp.concatenate(outs, axis=1)


def _softplus(x):
    return jnp.maximum(x, 0.0) + jnp.log1p(jnp.exp(-jnp.abs(x)))


def _rms_rows(x, g):
    return x * lax.rsqrt(jnp.mean(x * x, axis=-1, keepdims=True) + RMS_EPS) * g


def _inproj_kernel(x_ref, g_ref, w_ref, o_ref, h_ref):
    @pl.when(pl.program_id(1) == 0)
    def _():
        h_ref[...] = _rms_rows(x_ref[...], g_ref[...]).astype(BF16)

    o_ref[...] = _dot(h_ref[...], w_ref[...])


def _inproj(x, g, w):
    m, d = x.shape
    n = w.shape[1]
    tm, tn = _tile(m, 1024), _tile(n, 512)
    return pl.pallas_call(
        _inproj_kernel,
        grid=(m // tm, n // tn),
        in_specs=[pl.BlockSpec((tm, d), lambda i, j: (i, 0)),
                  pl.BlockSpec((1, d), lambda i, j: (0, 0)),
                  pl.BlockSpec((d, tn), lambda i, j: (0, j))],
        out_specs=pl.BlockSpec((tm, tn), lambda i, j: (i, j)),
        out_shape=jax.ShapeDtypeStruct((m, n), F32),
        scratch_shapes=[pltpu.VMEM((tm, d), BF16)],
        compiler_params=_cp(("arbitrary", "arbitrary")),
        name="inproj",
    )(x, g, w)


def _foxpost_kernel(q_ref, k_ref, v_ref, f_ref, qn_ref, kn_ref, bf_ref, oq, ok, ov, olf, oc, carry, *, tiles_per_seq):
    i = pl.program_id(0)
    bd = _block_diag_ones()

    def head_norm(x, g):
        ms = _segsum(x * x, bd) * (1.0 / HEAD_DIM)
        return x * lax.rsqrt(ms + RMS_EPS) * g

    oq[...] = head_norm(q_ref[...], qn_ref[...])
    ok[...] = head_norm(k_ref[...], kn_ref[...])
    ov[...] = v_ref[...]
    lf = -_softplus(-(f_ref[...] + bf_ref[...]))
    olf[...] = lf

    @pl.when(i % tiles_per_seq == 0)
    def _():
        carry[...] = jnp.zeros_like(carry)

    tm = lf.shape[0]
    r = lax.broadcasted_iota(jnp.int32, (tm, tm), 0)
    c = lax.broadcasted_iota(jnp.int32, (tm, tm), 1)
    tri = jnp.where(c <= r, 1.0, 0.0).astype(BF16)
    hi, mid, lo = _split3(lf)
    cum = _dot(tri, hi) + _dot(tri, mid) + _dot(tri, lo) + carry[...]
    oc[...] = cum
    carry[...] = cum[tm - 1:tm, :]


def _foxpost(z, q_norm_t, k_norm_t, bf_pad, seq_len):
    m = z.shape[0]
    tm = _tile(m, 512)
    tm = min(tm, seq_len)
    row = lambda c: pl.BlockSpec((tm, WIDTH), lambda i, c=c: (i, c // WIDTH))
    vec = pl.BlockSpec((1, WIDTH), lambda i: (0, 0))
    small = pl.BlockSpec((tm, LANES), lambda i: (i, 0))
    return pl.pallas_call(
        functools.partial(_foxpost_kernel, tiles_per_seq=seq_len // tm),
        grid=(m // tm,),
        in_specs=[row(C_Q), row(C_FK), row(C_FV),
                  pl.BlockSpec((tm, LANES), lambda i: (i, C_F // LANES)),
                  vec, vec, pl.BlockSpec((1, LANES), lambda i: (0, 0))],
        out_specs=[pl.BlockSpec((tm, WIDTH), lambda i: (i, 0))] * 3 + [small, small],
        out_shape=[jax.ShapeDtypeStruct((m, WIDTH), F32)] * 3 + [jax.ShapeDtypeStruct((m, LANES), F32)] * 2,
        scratch_shapes=[pltpu.VMEM((1, LANES), F32)],
        compiler_params=_cp(("arbitrary",)),
        name="foxpost",
    )(z, z, z, z, q_norm_t, k_norm_t, bf_pad)


def _flash_kernel(qi_tab, ki_tab, q_ref, k_ref, v_ref, cq_ref, ck_ref, og_ref, g_ref, o_ref, m_s, l_s, acc_s):
    s_idx = pl.program_id(2)
    qi = qi_tab[s_idx]
    ki = ki_tab[s_idx]
    tq = q_ref.shape[0]
    tk = k_ref.shape[0]

    @pl.when(ki == 0)
    def _():
        m_s[...] = jnp.full_like(m_s, -jnp.inf)
        l_s[...] = jnp.zeros_like(l_s)
        acc_s[...] = jnp.zeros_like(acc_s)

    def update(masked):
        for hh in range(2):
            sl = slice(hh * HEAD_DIM, (hh + 1) * HEAD_DIM)
            q = (q_ref[:, sl] * (HEAD_DIM ** -0.5)).astype(BF16)
            k = k_ref[:, sl].astype(BF16)
            s = _dot_nt(q, k)
            s = s + cq_ref[0, 0, :, hh:hh + 1] - ck_ref[0, 0, hh:hh + 1, :]
            if masked:
                r = lax.broadcasted_iota(jnp.int32, (tq, tk), 0)
                c = lax.broadcasted_iota(jnp.int32, (tq, tk), 1)
                s = jnp.where(c <= r, s, -jnp.inf)
            m_prev = m_s[hh]
            m_new = jnp.maximum(m_prev, jnp.max(s, axis=-1, keepdims=True))
            alpha = jnp.exp(m_prev - m_new)
            p = jnp.exp(s - m_new)
            l_s[hh] = alpha * l_s[hh] + jnp.sum(p, axis=-1, keepdims=True)
            acc_s[hh] = alpha * acc_s[hh] + _dot(p.astype(BF16), v_ref[:, sl].astype(BF16))
            m_s[hh] = m_new

    @pl.when(ki < qi)
    def _():
        update(False)

    @pl.when(ki == qi)
    def _():
        update(True)
        outs = []
        for hh in range(2):
            sl = slice(hh * HEAD_DIM, (hh + 1) * HEAD_DIM)
            o = acc_s[hh] / l_s[hh]
            on = _rms_rows(o, g_ref[:, sl])
            outs.append(on * jax.nn.sigmoid(og_ref[:, sl]))
        o_ref[...] = jnp.concatenate(outs, axis=1)


def _flash_prompt(qn, kn, v, c4, ct4, z, g_fox_o, batch, seq):
    tq = _tile(seq, 512)
    nq = seq // tq
    pairs = [(a, b) for a in range(nq) for b in range(a + 1)]
    qi_tab = jnp.array([a for a, _ in pairs], jnp.int32)
    ki_tab = jnp.array([b for _, b in pairs], jnp.int32)
    npairs = WIDTH // LANES
    qmap = lambda b, p, s, qt, kt: (b * nq + qt[s], p)
    kmap = lambda b, p, s, qt, kt: (b * nq + kt[s], p)
    gs = pltpu.PrefetchScalarGridSpec(
        num_scalar_prefetch=2,
        grid=(batch, npairs, len(pairs)),
        in_specs=[pl.BlockSpec((tq, LANES), qmap),
                  pl.BlockSpec((tq, LANES), kmap),
                  pl.BlockSpec((tq, LANES), kmap),
                  pl.BlockSpec((1, 1, tq, 2), lambda b, p, s, qt, kt: (b, p, qt[s], 0)),
                  pl.BlockSpec((1, 1, 2, tq), lambda b, p, s, qt, kt: (b, p, 0, kt[s])),
                  pl.BlockSpec((tq, LANES), lambda b, p, s, qt, kt: (b * nq + qt[s], C_OG // LANES + p)),
                  pl.BlockSpec((1, LANES), lambda b, p, s, qt, kt: (0, p))],
        out_specs=pl.BlockSpec((tq, LANES), qmap),
        scratch_shapes=[pltpu.VMEM((2, tq, 1), F32), pltpu.VMEM((2, tq, 1), F32), pltpu.VMEM((2, tq, HEAD_DIM), F32)],
    )
    return pl.pallas_call(
        _flash_kernel,
        grid_spec=gs,
        out_shape=jax.ShapeDtypeStruct((batch * seq, WIDTH), F32),
        compiler_params=_cp(("arbitrary", "arbitrary", "arbitrary")),
        name="flash_prompt",
    )(qi_tab, ki_tab, qn, kn, v, c4, ct4, z, g_fox_o)


def _pagecum_kernel(pt_ref, page_ref, new_ref, o_ref, carry, *, n_pages):
    j = pl.program_id(1)

    @pl.when(j == 0)
    def _():
        carry[...] = jnp.zeros_like(carry)

    x = jnp.where(j < n_pages, page_ref[0], new_ref[0])
    r = lax.broadcasted_iota(jnp.int32, (PAGE, PAGE), 0)
    c = lax.broadcasted_iota(jnp.int32, (PAGE, PAGE), 1)
    upper = jnp.where(r <= c, 1.0, 0.0).astype(BF16)
    hi, mid, lo = _split3(x)
    cum = _dot(hi, upper) + _dot(mid, upper) + _dot(lo, upper) + carry[...]
    o_ref[0] = cum
    carry[...] = jnp.broadcast_to(cum[:, PAGE - 1:PAGE], carry.shape)


def _pagecum(page_table, lf_pages_t, lf_new_t):
    db, n_pages = page_table.shape
    nh = lf_pages_t.shape[1]
    gs = pltpu.PrefetchScalarGridSpec(
        num_scalar_prefetch=1,
        grid=(db, n_pages + 1),
        in_specs=[pl.BlockSpec((1, nh, PAGE), lambda b, j, pt: (pt[b, jnp.minimum(j, n_pages - 1)], 0, 0)),
                  pl.BlockSpec((1, nh, PAGE), lambda b, j, pt: (b, 0, 0))],
        out_specs=pl.BlockSpec((1, nh, PAGE), lambda b, j, pt: (b, 0, j)),
        scratch_shapes=[pltpu.VMEM((nh, PAGE), F32)],
    )
    return pl.pallas_call(
        functools.partial(_pagecum_kernel, n_pages=n_pages),
        grid_spec=gs,
        out_shape=jax.ShapeDtypeStruct((db, nh, (n_pages + 1) * PAGE), F32),
        compiler_params=_cp(("arbitrary", "arbitrary")),
        name="pagecum",
    )(page_table, lf_pages_t, lf_new_t)


def _paged_kernel(pt_ref, q_ref, kp_ref, vp_ref, kn_ref, vn_ref, ct_ref, cq_ref, og_ref, g_ref, o_ref,
                  qbd, m_s, l_s, acc_s, *, n_pages):
    j = pl.program_id(1)
    nh = WIDTH // HEAD_DIM
    rows = DEC_SEQ * nh
    hrow = lax.broadcasted_iota(jnp.int32, (nh, WIDTH), 0)
    hcol = lax.broadcasted_iota(jnp.int32, (nh, WIDTH), 1) >> 6
    head_mask = hrow == hcol

    @pl.when(j == 0)
    def _():
        q = q_ref[...] * (HEAD_DIM ** -0.5)
        pieces = [jnp.where(head_mask, q[t:t + 1, :], 0.0) for t in range(DEC_SEQ)]
        qbd[...] = jnp.concatenate(pieces, axis=0).astype(BF16)
        m_s[...] = jnp.full_like(m_s, -jnp.inf)
        l_s[...] = jnp.zeros_like(l_s)
        acc_s[...] = jnp.zeros_like(acc_s)

    def update(k, v, valid):
        s = _dot_nt(qbd[...], k.astype(BF16))
        ck = jnp.concatenate([ct_ref[0]] * DEC_SEQ, axis=0)
        s = s + cq_ref[0] - ck
        if valid is not None:
            s = jnp.where(valid, s, -jnp.inf)
        m_prev = m_s[...]
        m_new = jnp.maximum(m_prev, jnp.max(s, axis=-1, keepdims=True))
        alpha = jnp.exp(m_prev - m_new)
        p = jnp.exp(s - m_new)
        l_s[...] = alpha * l_s[...] + jnp.sum(p, axis=-1, keepdims=True)
        acc_s[...] = alpha * acc_s[...] + _dot(p.astype(BF16), v.astype(BF16))
        m_s[...] = m_new

    @pl.when(j < n_pages)
    def _():
        update(kp_ref[0], vp_ref[0], None)

    @pl.when(j == n_pages)
    def _():
        pad = jnp.zeros((PAGE - DEC_SEQ, WIDTH), F32)
        kn = jnp.concatenate([kn_ref[...], pad], axis=0)
        vn = jnp.concatenate([vn_ref[...], pad], axis=0)
        r = lax.broadcasted_iota(jnp.int32, (rows, PAGE), 0)
        c = lax.broadcasted_iota(jnp.int32, (rows, PAGE), 1)
        update(kn, vn, c <= (r >> 4))
        o = acc_s[...] / l_s[...]
        mask = jnp.concatenate([head_mask] * DEC_SEQ, axis=0)
        o = jnp.where(mask, o, 0.0)
        ms = jnp.sum(o * o, axis=-1, keepdims=True) * (1.0 / HEAD_DIM)
        on = o * lax.rsqrt(ms + RMS_EPS)
        out = jnp.concatenate([jnp.sum(on[t * nh:(t + 1) * nh, :], axis=0, keepdims=True) for t in range(DEC_SEQ)], axis=0)
        o_ref[...] = out * g_ref[...] * jax.nn.sigmoid(og_ref[...])


def _paged_sample(page_table, qn, kn, v, cache_k, cache_v, ct_all, cq, z, g_fox_o, row0):
    db, n_pages = page_table.shape
    nh = WIDTH // HEAD_DIM
    rb = row0 // DEC_SEQ
    pmap = lambda b, j, pt: (pt[b, jnp.minimum(j, n_pages - 1)], 0, 0)
    newmap = lambda b, j, pt: (rb + b, 0)
    gs = pltpu.PrefetchScalarGridSpec(
        num_scalar_prefetch=1,
        grid=(db, n_pages + 1),
        in_specs=[pl.BlockSpec((DEC_SEQ, WIDTH), newmap),
                  pl.BlockSpec((1, PAGE, WIDTH), pmap),
                  pl.BlockSpec((1, PAGE, WIDTH), pmap),
                  pl.BlockSpec((DEC_SEQ, WIDTH), newmap),
                  pl.BlockSpec((DEC_SEQ, WIDTH), newmap),
                  pl.BlockSpec((1, nh, PAGE), lambda b, j, pt: (b, 0, j)),
                  pl.BlockSpec((1, DEC_SEQ * nh, 1), lambda b, j, pt: (b, 0, 0)),
                  pl.BlockSpec((DEC_SEQ, WIDTH), lambda b, j, pt: (rb + b, C_OG // WIDTH)),
                  pl.BlockSpec((1, WIDTH), lambda b, j, pt: (0, 0))],
        out_specs=pl.BlockSpec((DEC_SEQ, WIDTH), lambda b, j, pt: (b, 0)),
        scratch_shapes=[pltpu.VMEM((DEC_SEQ * nh, WIDTH), BF16), pltpu.VMEM((DEC_SEQ * nh, 1), F32),
                        pltpu.VMEM((DEC_SEQ * nh, 1), F32), pltpu.VMEM((DEC_SEQ * nh, WIDTH), F32)],
    )
    return pl.pallas_call(
        functools.partial(_paged_kernel, n_pages=n_pages),
        grid_spec=gs,
        out_shape=jax.ShapeDtypeStruct((db * DEC_SEQ, WIDTH), F32),
        compiler_params=_cp(("arbitrary", "arbitrary")),
        name="paged_sample",
    )(page_table, qn, cache_k, cache_v, kn, v, ct_all, cq, z, g_fox_o)


def _prep_kernel(z_ref, first_ref, mu_ref, w12_ref, w3_ref, w0_ref, a0_ref, kk_ref, ka_ref, rk_ref,
                 o_r, o_w, o_k, o_v, o_a, o_b, o_g, o_bonus, carry, *, seq_len, tiles_per_seq):
    z = z_ref[...]
    tm = z.shape[0]
    row = lax.broadcasted_iota(jnp.int32, (tm, 1), 0)
    rolled = pltpu.roll(z, 1, 0)
    if tiles_per_seq >= 1 and seq_len >= tm:
        li = pl.program_id(0) % tiles_per_seq

        @pl.when(li == 0)
        def _():
            carry[...] = first_ref[0]

        zp = jnp.where(row == 0, carry[...], rolled)
        carry[...] = z[tm - 1:tm, :]
    else:
        nseq = tm // seq_len
        first = first_ref[...]
        exp = jnp.broadcast_to(first, (nseq, seq_len, first.shape[-1])).reshape(tm, first.shape[-1])
        zp = jnp.where((row & (seq_len - 1)) == 0, exp, rolled)
    zm = z + (zp - z) * mu_ref[...]
    r = zm[:, 0:WIDTH]
    k = zm[:, WIDTH:2 * WIDTH]
    v = zm[:, 2 * WIDTH:3 * WIDTH]
    lo = zm[:, C_LORA:C_LORA + 384]
    lane = lax.broadcasted_iota(jnp.int32, (tm, 384), 1)
    act = jnp.where(lane < 64, jnp.tanh(lo), jnp.where(lane < 128, lo, jax.nn.sigmoid(lo))).astype(BF16)
    l12 = _dot(act[:, 0:128], w12_ref[...])
    g = _dot(act[:, 128:384], w3_ref[...])
    log_w = -_softplus(-(w0_ref[...] + l12[:, 0:WIDTH])) - 0.5
    decay = jnp.exp(-jnp.exp(log_w))
    asig = jax.nn.sigmoid(a0_ref[...] + l12[:, WIDTH:2 * WIDTH])
    bd = _block_diag_ones()
    kk = k * kk_ref[...]
    kk = kk / jnp.maximum(jnp.sqrt(_segsum(kk * kk, bd)), 1e-12)
    kf = k * (1.0 + (asig - 1.0) * ka_ref[...])
    o_r[...] = r
    o_w[...] = decay
    o_k[...] = kf
    o_v[...] = v
    o_a[...] = -kk
    o_b[...] = kk * asig
    o_g[...] = g
    o_bonus[...] = _segsum(r * kf * rk_ref[...], bd) * v


def _rwkv_prep(z, first, mu, w12, w3, w0, a0, k_k, k_a, r_k, row0, n_rows, seq_len):
    if seq_len >= 256:
        tm = _tile(seq_len, 256)
        first_spec = pl.BlockSpec((1, 1, RWKV_COLS), lambda i: (i // (seq_len // tm), 0, 0))
    else:
        tm = _tile(n_rows, 128)
        nseq = tm // seq_len
        first_spec = pl.BlockSpec((nseq, 1, RWKV_COLS), lambda i: (i, 0, 0))
    rb = row0 // tm
    vec = lambda n: pl.BlockSpec((1, n), lambda i: (0, 0))
    out = pl.BlockSpec((tm, WIDTH), lambda i: (i, 0))
    return pl.pallas_call(
        functools.partial(_prep_kernel, seq_len=seq_len, tiles_per_seq=max(seq_len // tm, 1)),
        grid=(n_rows // tm,),
        in_specs=[pl.BlockSpec((tm, RWKV_COLS), lambda i: (rb + i, 0)), first_spec, vec(RWKV_COLS),
                  pl.BlockSpec((128, 2 * WIDTH), lambda i: (0, 0)), pl.BlockSpec((256, WIDTH), lambda i: (0, 0)),
                  vec(WIDTH), vec(WIDTH), vec(WIDTH), vec(WIDTH), vec(WIDTH)],
        out_specs=[out] * 8,
        out_shape=[jax.ShapeDtypeStruct((n_rows, WIDTH), F32)] * 8,
        scratch_shapes=[pltpu.VMEM((1, RWKV_COLS), F32)],
        compiler_params=_cp(("arbitrary",)),
        name=f"rwkv_prep_{seq_len}",
    )(z, first, mu, w12, w3, w0, a0, k_k, k_a, r_k)


SCAN_SUB = 64


def _wkv_kernel(r_ref, w_ref, k_ref, v_ref, a_ref, b_ref, s0_ref, y_ref, sT_ref, S_s, Z_s, *, n_pairs, tb_len):
    tb = pl.program_id(1)
    bd = _block_diag_ones()
    vrow = lax.broadcasted_iota(jnp.int32, (HEAD_DIM, LANES), 0)
    lane = lax.broadcasted_iota(jnp.int32, (HEAD_DIM, LANES), 1)
    eye2 = (lane & (HEAD_DIM - 1)) == vrow

    @pl.when(tb == 0)
    def _():
        for p in range(n_pairs):
            S_s[p] = jnp.concatenate([s0_ref[0, 2 * p], s0_ref[0, 2 * p + 1]], axis=1)
        Z_s[...] = jnp.zeros_like(Z_s)

    def seg(x):
        hi, lo = _split2(x)
        return _dot(hi, bd) + _dot(lo, bd)

    def run(t0, n):
        def step(i, carry):
            t = t0 + i
            rr = r_ref[pl.ds(t, 1), :]
            ww = w_ref[pl.ds(t, 1), :]
            kr = k_ref[pl.ds(t, 1), :]
            vv = v_ref[pl.ds(t, 1), :]
            aa = a_ref[pl.ds(t, 1), :]
            bb = b_ref[pl.ds(t, 1), :]
            hit = (lane & (SCAN_SUB - 1)) == i
            for p in range(n_pairs):
                sl = slice(p * LANES, (p + 1) * LANES)
                S = S_s[p]
                sa = seg(S * aa[:, sl])
                vb = seg(jnp.where(eye2, vv[:, sl], 0.0))
                S = S * ww[:, sl] + sa * bb[:, sl] + vb * kr[:, sl]
                S_s[p] = S
                yb = seg(S * rr[:, sl])
                Z_s[p] = jnp.where(hit, yb, Z_s[p])
            return carry

        lax.fori_loop(0, n, step, 0)
        for p in range(n_pairs):
            zt = Z_s[p].T
            y_ref[pl.ds(t0, n), p * LANES:p * LANES + HEAD_DIM] = zt[0:n, :]
            y_ref[pl.ds(t0, n), p * LANES + HEAD_DIM:(p + 1) * LANES] = zt[SCAN_SUB:SCAN_SUB + n, :]

    if tb_len <= SCAN_SUB:
        run(0, tb_len)
    else:
        def outer(blk, carry):
            run(pl.multiple_of(blk * SCAN_SUB, SCAN_SUB), SCAN_SUB)
            return carry
        lax.fori_loop(0, tb_len // SCAN_SUB, outer, 0)

    @pl.when(tb == pl.num_programs(1) - 1)
    def _():
        for p in range(n_pairs):
            S = S_s[p]
            sT_ref[0, 2 * p] = S[:, 0:HEAD_DIM]
            sT_ref[0, 2 * p + 1] = S[:, HEAD_DIM:LANES]


def _wkv_scan(r, w, k, v, a, b, s0, n_seq, seq_len):
    nh = WIDTH // HEAD_DIM
    tb_len = min(seq_len, 256)
    nt = seq_len // tb_len
    blk = pl.BlockSpec((tb_len, WIDTH), lambda s, t: (s * nt + t, 0))
    st = pl.BlockSpec((1, nh, HEAD_DIM, HEAD_DIM), lambda s, t: (s, 0, 0, 0))
    n_pairs = WIDTH // LANES
    return pl.pallas_call(
        functools.partial(_wkv_kernel, n_pairs=n_pairs, tb_len=tb_len),
        grid=(n_seq, nt),
        in_specs=[blk] * 6 + [st],
        out_specs=[blk, st],
        out_shape=[jax.ShapeDtypeStruct((n_seq * seq_len, WIDTH), F32),
                   jax.ShapeDtypeStruct((n_seq, nh, HEAD_DIM, HEAD_DIM), F32)],
        scratch_shapes=[pltpu.VMEM((n_pairs, HEAD_DIM, LANES), F32), pltpu.VMEM((n_pairs, HEAD_DIM, LANES), F32)],
        compiler_params=_cp(("arbitrary", "arbitrary")),
        name=f"wkv_scan_{seq_len}",
    )(r, w, k, v, a, b, s0)


def _outproj_kernel(x_ref, y_ref, bonus_ref, g_ref, of_ref, lw_ref, lb_ref, wo_ref, gf_ref, wrh_ref, wrl_ref, br_ref,
                    x1_ref, h_ref, route_ref):
    bd = _block_diag_ones()
    y = y_ref[...]
    mu = _segsum(y, bd) * (1.0 / HEAD_DIM)
    d = y - mu
    var = _segsum(d * d, bd) * (1.0 / HEAD_DIM)
    yn = d * lax.rsqrt(var + LNX_EPS) * lw_ref[...] + lb_ref[...]
    o_r = ((yn + bonus_ref[...]) * g_ref[...]).astype(BF16)
    o_f = of_ref[...].astype(BF16)
    x1 = x_ref[...] + _dot(o_r, wo_ref[0:WIDTH, :]) + _dot(o_f, wo_ref[WIDTH:2 * WIDTH, :])
    x1_ref[...] = x1
    h = _rms_rows(x1, gf_ref[...])
    h_ref[...] = h
    hi, lo = _split2(h)
    logits = _dot(hi, wrh_ref[...]) + _dot(lo, wrh_ref[...]) + _dot(hi, wrl_ref[...]) + br_ref[...]
    tm = logits.shape[0]
    lane = lax.broadcasted_iota(jnp.int32, (tm, LANES), 1)
    neg = -jnp.inf
    lg = jnp.where(lane < N_GROUPS, logits, neg)
    mg = jnp.max(lg, axis=-1, keepdims=True)
    pg_top = 1.0 / jnp.sum(jnp.exp(lg - mg), axis=-1, keepdims=True)
    g_sel = jnp.min(jnp.where(lg == mg, lane, LANES), axis=-1, keepdims=True)
    in_grp = (lane >= N_GROUPS) & (lane < N_GROUPS + N_EXPERTS) & (((lane - N_GROUPS) >> 3) == g_sel)
    le = jnp.where(in_grp, logits, neg)
    m1 = jnp.max(le, axis=-1, keepdims=True)
    i1 = jnp.min(jnp.where(le == m1, lane, LANES), axis=-1, keepdims=True)
    le2 = jnp.where(lane == i1, neg, le)
    m2 = jnp.max(le2, axis=-1, keepdims=True)
    i2 = jnp.min(jnp.where(le2 == m2, lane, LANES), axis=-1, keepdims=True)
    e2 = jnp.exp(m2 - m1)
    gate1 = pg_top / (1.0 + e2)
    gate2 = pg_top * e2 / (1.0 + e2)
    route = jnp.where(lane == 0, (i1 - N_GROUPS).astype(F32),
                      jnp.where(lane == 1, (i2 - N_GROUPS).astype(F32),
                                jnp.where(lane == 2, gate1, jnp.where(lane == 3, gate2, 0.0))))
    route_ref[...] = route


def _outproj(x, y, bonus, g, o_f, lnx_w, lnx_b, w_out, g_ffn, wr_hi, wr_lo, b_r):
    m, d = x.shape
    tm = _tile(m, 256)
    row = lambda n: pl.BlockSpec((tm, n), lambda i: (i, 0))
    vec = lambda n: pl.BlockSpec((1, n), lambda i: (0, 0))
    full = lambda a: pl.BlockSpec(a.shape, lambda i: (0, 0))
    return pl.pallas_call(
        _outproj_kernel,
        grid=(m // tm,),
        in_specs=[row(d), row(WIDTH), row(WIDTH), row(WIDTH), row(WIDTH), vec(WIDTH), vec(WIDTH),
                  full(w_out), vec(d), full(wr_hi), full(wr_lo), vec(LANES)],
        out_specs=[row(d), row(d), row(LANES)],
        out_shape=[jax.ShapeDtypeStruct((m, d), F32), jax.ShapeDtypeStruct((m, d), F32),
                   jax.ShapeDtypeStruct((m, LANES), F32)],
        compiler_params=_cp(("arbitrary",)),
        name="outproj_router",
    )(x, y, bonus, g, o_f, lnx_w, lnx_b, w_out, g_ffn, wr_hi, wr_lo, b_r)


def _expert_kernel(be_ref, tok_ref, nu_ref, h_hbm, wg_ref, wu_ref, wd_ref, o_ref, xbuf, sem):
    i = pl.program_id(0)

    def row_copy(r, tok):
        return pltpu.make_async_copy(h_hbm.at[pl.ds(tok, 1), :], xbuf.at[pl.ds(r, 1), :], sem)

    @pl.when(i < nu_ref[0])
    def _():
        def start(r, c):
            row_copy(r, tok_ref[i * MOE_BLOCK + r]).start()
            return c

        def wait(r, c):
            row_copy(r, 0).wait()
            return c

        lax.fori_loop(0, MOE_BLOCK, start, 0)
        lax.fori_loop(0, MOE_BLOCK, wait, 0)
        x = xbuf[...].astype(BF16)
        gate = _dot(x, wg_ref[0].astype(BF16))
        up = _dot(x, wu_ref[0].astype(BF16))
        act = (gate * jax.nn.sigmoid(gate) * up).astype(BF16)
        o_ref[...] = _dot(act, wd_ref[0].astype(BF16))

    @pl.when(i >= nu_ref[0])
    def _():
        o_ref[...] = jnp.zeros_like(o_ref)


def _experts(block_e, slot_tok, n_used, h, w_gate, w_up, w_down):
    n_blocks = block_e.shape[0]
    d = h.shape[1]
    de = w_gate.shape[2]
    gs = pltpu.PrefetchScalarGridSpec(
        num_scalar_prefetch=3,
        grid=(n_blocks,),
        in_specs=[pl.BlockSpec(memory_space=pl.ANY),
                  pl.BlockSpec((1, d, de), lambda i, be, tok, nu: (be[i], 0, 0)),
                  pl.BlockSpec((1, d, de), lambda i, be, tok, nu: (be[i], 0, 0)),
                  pl.BlockSpec((1, de, d), lambda i, be, tok, nu: (be[i], 0, 0))],
        out_specs=pl.BlockSpec((MOE_BLOCK, d), lambda i, be, tok, nu: (i, 0)),
        scratch_shapes=[pltpu.VMEM((MOE_BLOCK, d), F32), pltpu.SemaphoreType.DMA(())],
    )
    return pl.pallas_call(
        _expert_kernel,
        grid_spec=gs,
        out_shape=jax.ShapeDtypeStruct((n_blocks * MOE_BLOCK, d), F32),
        compiler_params=_cp(("arbitrary",)),
        name="experts",
    )(block_e, slot_tok, n_used, h, w_gate, w_up, w_down)


def _ple_kernel(dest_ref, x1_ref, route_ref, p_ref, yb_hbm, gp_ref, wpg_ref, bpg_ref, wpp_ref, gfin_ref, o_ref, buf, sem):
    i = pl.program_id(0)
    tm = x1_ref.shape[0]

    def row_copy(r, k, slot):
        return pltpu.make_async_copy(yb_hbm.at[pl.ds(slot, 1), :], buf.at[k, pl.ds(r, 1), :], sem)

    def start(r, c):
        for k in range(2):
            row_copy(r, k, dest_ref[(i * tm + r) * 2 + k]).start()
        return c

    def wait(r, c):
        for k in range(2):
            row_copy(r, k, 0).wait()
        return c

    lax.fori_loop(0, tm, start, 0)
    lax.fori_loop(0, tm, wait, 0)
    route = route_ref[...]
    moe = buf[0] * route[:, 2:3] + buf[1] * route[:, 3:4]
    x2 = x1_ref[...] + moe
    hn = _rms_rows(x2, gp_ref[...]).astype(BF16)
    gate = jax.nn.sigmoid(_dot(hn, wpg_ref[...]) + bpg_ref[...])
    pe = _dot(p_ref[...].astype(BF16), wpp_ref[...])
    x3 = x2 + gate * pe
    o_ref[...] = _rms_rows(x3, gfin_ref[...])


def _ple(dest, x1, route, p, yb, g_ple, w_pg, b_pg, w_pp, g_final):
    m, d = x1.shape
    tm = _tile(m, 256)
    pd = p.shape[1]
    gs = pltpu.PrefetchScalarGridSpec(
        num_scalar_prefetch=1,
        grid=(m // tm,),
        in_specs=[pl.BlockSpec((tm, d), lambda i, ds: (i, 0)),
                  pl.BlockSpec((tm, LANES), lambda i, ds: (i, 0)),
                  pl.BlockSpec((tm, pd), lambda i, ds: (i, 0)),
                  pl.BlockSpec(memory_space=pl.ANY),
                  pl.BlockSpec((1, d), lambda i, ds: (0, 0)),
                  pl.BlockSpec((d, d), lambda i, ds: (0, 0)),
                  pl.BlockSpec((1, d), lambda i, ds: (0, 0)),
                  pl.BlockSpec((pd, d), lambda i, ds: (0, 0)),
                  pl.BlockSpec((1, d), lambda i, ds: (0, 0))],
        out_specs=pl.BlockSpec((tm, d), lambda i, ds: (i, 0)),
        scratch_shapes=[pltpu.VMEM((2, tm, d), F32), pltpu.SemaphoreType.DMA(())],
    )
    return pl.pallas_call(
        _ple_kernel,
        grid_spec=gs,
        out_shape=jax.ShapeDtypeStruct((m, d), F32),
        compiler_params=_cp(("arbitrary",)),
        name="moe_combine_ple",
    )(dest, x1, route, p, yb, g_ple, w_pg, b_pg, w_pp, g_final)


def _dispatch(route, n_tok):
    expert = route[:, 0:2].astype(jnp.int32)
    flat_e = expert.reshape(-1)
    n_assign = flat_e.shape[0]
    onehot = (flat_e[:, None] == jnp.arange(N_EXPERTS, dtype=jnp.int32)[None, :]).astype(jnp.int32)
    csum = jnp.cumsum(onehot, axis=0)
    rank = jnp.sum(csum * onehot, axis=1) - 1
    counts = csum[-1]
    pcounts = (counts + MOE_BLOCK - 1) // MOE_BLOCK * MOE_BLOCK
    pend = jnp.cumsum(pcounts)
    pstart = pend - pcounts
    dest = (pstart[flat_e] + rank).astype(jnp.int32)
    n_blocks = -(-n_assign // MOE_BLOCK) + N_EXPERTS
    slot_tok = jnp.zeros((n_blocks * MOE_BLOCK,), jnp.int32).at[dest].set(jnp.arange(n_assign, dtype=jnp.int32) // 2)
    n_used = (pend[-1] // MOE_BLOCK).astype(jnp.int32)
    blk = jnp.arange(n_blocks, dtype=jnp.int32)
    block_e = jnp.searchsorted(pend, jnp.minimum(blk, n_used - 1) * MOE_BLOCK, side='right').astype(jnp.int32)
    block_e = jnp.minimum(block_e, N_EXPERTS - 1)
    return block_e, slot_tok, n_used.reshape(1), dest


def kernel(x_prompt, x_sample, cache_k, cache_v, cache_logf, state_wkv, state_shift, page_table, p_prompt, p_sample, g_attn, w_in, mu_shift, w0, w_up, a0, a_up, g_up, k_k, k_a, r_k, lnx_w, lnx_b, b_f, q_norm, k_norm, g_fox_o, w_out, g_ffn, w_rg, b_rg, w_re, b_re, w_e_gate, w_e_up, w_e_down, g_ple, w_pg, b_pg, w_pp, g_final):
    depth = g_attn.shape[0]
    assert depth == 1
    batch, seq, d = x_prompt.shape
    db, dec_seq, _ = x_sample.shape
    assert dec_seq == DEC_SEQ
    nh = WIDTH // HEAD_DIM
    n_p = batch * seq
    n_s = db * dec_seq
    m = n_p + n_s
    rwkv_in = mu_shift.shape[1]
    n_pool = cache_k.shape[1]
    n_pages = page_table.shape[1]
    row = lambda a: a.reshape(1, -1)

    x_all = jnp.concatenate([x_prompt.reshape(n_p, d), x_sample.reshape(n_s, d)], axis=0)
    p_all = jnp.concatenate([p_prompt[0].reshape(n_p, -1), p_sample[0].reshape(n_s, -1)], axis=0)

    wi = w_in[0]
    zc = lambda n: jnp.zeros((d, n), F32)
    w_z = jnp.concatenate([wi[:, :rwkv_in], zc(C_F - rwkv_in), wi[:, rwkv_in + 4 * WIDTH:], zc(C_Q - C_F - nh),
                           wi[:, rwkv_in:rwkv_in + 4 * WIDTH]], axis=1).astype(BF16)
    w12 = jnp.zeros((128, 2 * WIDTH), F32).at[0:64, 0:WIDTH].set(w_up[0]).at[64:128, WIDTH:].set(a_up[0]).astype(BF16)
    w3 = jnp.zeros((256, WIDTH), F32).at[0:g_up.shape[1]].set(g_up[0]).astype(BF16)
    w_r = jnp.zeros((d, LANES), F32).at[:, 0:N_GROUPS].set(w_rg[0]).at[:, N_GROUPS:N_GROUPS + N_EXPERTS].set(w_re[0])
    wr_hi = w_r.astype(BF16)
    wr_lo = (w_r - wr_hi.astype(F32)).astype(BF16)
    b_r = jnp.zeros((1, LANES), F32).at[0, 0:N_GROUPS].set(b_rg[0]).at[0, N_GROUPS:N_GROUPS + N_EXPERTS].set(b_re[0])
    pad_cols = lambda a, n: jnp.pad(a, ((0, 0), (0, n - a.shape[1])))
    mu_pad = pad_cols(mu_shift, RWKV_COLS)
    bf_pad = pad_cols(b_f, LANES)
    qn_t = jnp.tile(q_norm, (1, nh))
    kn_t = jnp.tile(k_norm, (1, nh))

    z = _inproj(x_all, g_attn, w_z)

    qn, kn, vv, logf, cum = _foxpost(z, qn_t, kn_t, bf_pad, seq)

    c_p = cum[:n_p, :nh].reshape(batch, seq, nh // 2, 2)
    c4 = c_p.transpose(0, 2, 1, 3)
    ct4 = c_p.transpose(0, 2, 3, 1)
    of_p = _flash_prompt(qn, kn, vv, c4, ct4, z, g_fox_o, batch, seq)

    lf_pages_t = cache_logf[0].transpose(0, 2, 1)
    lf_new_t = pad_cols(logf[n_p:, :nh].reshape(db, dec_seq, nh).transpose(0, 2, 1).reshape(db * nh, dec_seq), PAGE)
    ct_all = _pagecum(page_table, lf_pages_t, lf_new_t.reshape(db, nh, PAGE))
    past = n_pages * PAGE
    cq = ct_all[:, :, past:past + dec_seq].transpose(0, 2, 1).reshape(db, dec_seq * nh, 1)
    of_s = _paged_sample(page_table, qn, kn, vv, cache_k[0].reshape(n_pool, PAGE, WIDTH),
                         cache_v[0].reshape(n_pool, PAGE, WIDTH), ct_all, cq, z, g_fox_o, n_p)
    o_f = jnp.concatenate([of_p, of_s], axis=0)

    first_p = jnp.zeros((batch, 1, RWKV_COLS), F32)
    first_s = pad_cols(state_shift[0], RWKV_COLS).reshape(db, 1, RWKV_COLS)
    prep_args = (mu_pad, w12, w3, w0, a0, k_k, k_a, row(r_k))
    prep_p = _rwkv_prep(z, first_p, *prep_args, 0, n_p, seq)
    prep_s = _rwkv_prep(z, first_s, *prep_args, n_p, n_s, dec_seq)
    y_p, wkv_p = _wkv_scan(*prep_p[:6], jnp.zeros((batch, nh, HEAD_DIM, HEAD_DIM), F32), batch, seq)
    y_s, wkv_s = _wkv_scan(*prep_s[:6], state_wkv[0], db, dec_seq)
    y = jnp.concatenate([y_p, y_s], axis=0)
    g_ = jnp.concatenate([prep_p[6], prep_s[6]], axis=0)
    bonus = jnp.concatenate([prep_p[7], prep_s[7]], axis=0)

    x1, h2, route = _outproj(x_all, y, bonus, g_, o_f, lnx_w, lnx_b, w_out[0].astype(BF16), g_ffn, wr_hi, wr_lo, b_r)

    block_e, slot_tok, n_used, dest = _dispatch(route, m)
    yb = _experts(block_e, slot_tok, n_used, h2, w_e_gate[0], w_e_up[0], w_e_down[0])

    y_all = _ple(dest, x1, route, p_all, yb, g_ple, w_pg[0].astype(BF16), b_pg, w_pp[0].astype(BF16), row(g_final))

    shift_p = z[seq - 1:n_p:seq, :rwkv_in]
    shift_s = z[n_p + dec_seq - 1::dec_seq, :rwkv_in]
    heads = lambda a, b_, t: a.reshape(1, b_, t, nh, HEAD_DIM)
    return (y_all[:n_p].reshape(batch, seq, d), y_all[n_p:].reshape(db, dec_seq, d),
            heads(kn[:n_p], batch, seq), heads(vv[:n_p], batch, seq), logf[:n_p, :nh].reshape(1, batch, seq, nh),
            wkv_p[None], shift_p[None],
            heads(kn[n_p:], db, dec_seq), heads(vv[n_p:], db, dec_seq), logf[n_p:, :nh].reshape(1, db, dec_seq, nh),
            wkv_s[None], shift_s[None])
```

```python
import functools

import jax
import jax.numpy as jnp
import numpy as np
from jax import lax
from jax.experimental import pallas as pl
from jax.experimental.pallas import tpu as pltpu

F32 = jnp.float32
BF16 = jnp.bfloat16

HEAD_DIM = 64
LANES = 128
RMS_EPS = 1e-6
LNX_EPS = 64e-5
PAGE = 128
MOE_BLOCK = 128
N_GROUPS = 8
N_EXPERTS = 64
DEC_SEQ = 8

C_LORA = 3072
C_F = 3456
RWKV_COLS = 3584
C_Q, C_FK, C_FV, C_OG = 4096, 5120, 6144, 7168
Z_COLS = 8192
WIDTH = 1024

VMEM_LIMIT = 52 * 1024 * 1024


def _cp(sem, vmem=VMEM_LIMIT):
    return pltpu.CompilerParams(dimension_semantics=sem, vmem_limit_bytes=vmem)


def _tile(n, pref):
    for t in (1024, 512, 256, 128, 64, 32, 16, 8):
        if t <= pref and n % t == 0:
            return t
    raise ValueError(f"no tile for {n}")


def _split2(x):
    hi = x.astype(BF16)
    lo = (x - hi.astype(F32)).astype(BF16)
    return hi, lo


def _split3(x):
    hi = x.astype(BF16)
    r1 = x - hi.astype(F32)
    mid = r1.astype(BF16)
    lo = (r1 - mid.astype(F32)).astype(BF16)
    return hi, mid, lo


def _dot(a, b):
    return jnp.dot(a, b, preferred_element_type=F32)


def _dot_nt(a, b):
    return lax.dot_general(a, b, (((1,), (1,)), ((), ())), preferred_element_type=F32)


def _block_diag_ones():
    r = lax.broadcasted_iota(jnp.int32, (LANES, LANES), 0)
    c = lax.broadcasted_iota(jnp.int32, (LANES, LANES), 1)
    return jnp.where((r >> 6) == (c >> 6), 1.0, 0.0).astype(BF16)


def _segsum(x, bd):
    outs = []
    for j in range(x.shape[1] // LANES):
        hi, lo = _split2(x[:, j * LANES:(j + 1) * LANES])
        outs.append(_dot(hi, bd) + _dot(lo, bd))
    return outs[0] if len(outs) == 1 else jnp.concatenate(outs, axis=1)


def _softplus(x):
    return jnp.maximum(x, 0.0) + jnp.log1p(jnp.exp(-jnp.abs(x)))


def _rms_rows(x, g):
    return x * lax.rsqrt(jnp.mean(x * x, axis=-1, keepdims=True) + RMS_EPS) * g


def _inproj_kernel(x_ref, g_ref, w_ref, o_ref, h_ref):
    @pl.when(pl.program_id(1) == 0)
    def _():
        h_ref[...] = _rms_rows(x_ref[...], g_ref[...]).astype(BF16)

    o_ref[...] = _dot(h_ref[...], w_ref[...])


def _inproj(x, g, w):
    m, d = x.shape
    n = w.shape[1]
    tm, tn = _tile(m, 1024), _tile(n, 512)
    return pl.pallas_call(
        _inproj_kernel,
        grid=(m // tm, n // tn),
        in_specs=[pl.BlockSpec((tm, d), lambda i, j: (i, 0)),
                  pl.BlockSpec((1, d), lambda i, j: (0, 0)),
                  pl.BlockSpec((d, tn), lambda i, j: (0, j))],
        out_specs=pl.BlockSpec((tm, tn), lambda i, j: (i, j)),
        out_shape=jax.ShapeDtypeStruct((m, n), F32),
        scratch_shapes=[pltpu.VMEM((tm, d), BF16)],
        compiler_params=_cp(("arbitrary", "arbitrary")),
        name="inproj",
    )(x, g, w)


AUG = 2 * HEAD_DIM


def _aug_constants():
    nh = WIDTH // HEAD_DIM
    pq = np.zeros((3 * LANES, WIDTH), np.float32)
    pk = np.zeros((3 * LANES, WIDTH), np.float32)
    one_q = np.zeros((1, WIDTH), np.float32)
    one_k = np.zeros((1, WIDTH), np.float32)
    for h in range(nh):
        for comp in range(3):
            pq[comp * LANES + h, h * HEAD_DIM + comp] = 1.0
            pk[comp * LANES + h, h * HEAD_DIM + 3 + comp] = -1.0
            one_q[0, h * HEAD_DIM + 3 + comp] = 1.0
            one_k[0, h * HEAD_DIM + comp] = 1.0
    return jnp.asarray(pq, BF16), jnp.asarray(pk, BF16), jnp.asarray(one_q), jnp.asarray(one_k)


def _foxpost_kernel(q_ref, k_ref, v_ref, f_ref, qn_ref, kn_ref, bf_ref, pq_ref, pk_ref, oneq_ref, onek_ref,
                    oq, ok, ov, olf, oqa, oka, ovt, carry, *, tiles_per_seq):
    i = pl.program_id(0)
    bd = _block_diag_ones()
    nh = WIDTH // HEAD_DIM

    def head_norm(x, g):
        ms = _segsum(x * x, bd) * (1.0 / HEAD_DIM)
        return x * lax.rsqrt(ms + RMS_EPS) * g

    qn = head_norm(q_ref[...], qn_ref[...])
    kn = head_norm(k_ref[...], kn_ref[...])
    v = v_ref[...]
    oq[...] = qn
    ok[...] = kn
    ov[...] = v
    lf = -_softplus(-(f_ref[...] + bf_ref[...]))
    olf[...] = lf

    @pl.when(i % tiles_per_seq == 0)
    def _():
        carry[...] = jnp.zeros_like(carry)

    tm = lf.shape[0]
    r = lax.broadcasted_iota(jnp.int32, (tm, tm), 0)
    c = lax.broadcasted_iota(jnp.int32, (tm, tm), 1)
    tri = jnp.where(c <= r, 1.0, 0.0).astype(BF16)
    hi, mid, lo = _split3(lf)
    cum = _dot(tri, hi) + _dot(tri, mid) + _dot(tri, lo) + carry[...]
    carry[...] = cum[tm - 1:tm, :]

    c3 = jnp.concatenate(_split3(cum), axis=1)
    aug_q = _dot(c3, pq_ref[...]) + oneq_ref[...]
    aug_k = _dot(c3, pk_ref[...]) + onek_ref[...]
    qs = qn * (HEAD_DIM ** -0.5)

    def interleave(x, aug):
        pieces = []
        for h in range(nh):
            sl = slice(h * HEAD_DIM, (h + 1) * HEAD_DIM)
            pieces += [x[:, sl], aug[:, sl]]
        return jnp.concatenate(pieces, axis=1).astype(BF16)

    oqa[...] = interleave(qs, aug_q)
    oka[...] = interleave(kn, aug_k)
    ovt[...] = v.T.astype(BF16)


def _foxpost(z, q_norm_t, k_norm_t, bf_pad, seq_len):
    m = z.shape[0]
    nh = WIDTH // HEAD_DIM
    tm = _tile(m, 512)
    tm = min(tm, seq_len)
    row = lambda c: pl.BlockSpec((tm, WIDTH), lambda i, c=c: (i, c // WIDTH))
    vec = pl.BlockSpec((1, WIDTH), lambda i: (0, 0))
    small = pl.BlockSpec((tm, LANES), lambda i: (i, 0))
    wide = pl.BlockSpec((tm, nh * AUG), lambda i: (i, 0))
    place = pl.BlockSpec((3 * LANES, WIDTH), lambda i: (0, 0))
    return pl.pallas_call(
        functools.partial(_foxpost_kernel, tiles_per_seq=seq_len // tm),
        grid=(m // tm,),
        in_specs=[row(C_Q), row(C_FK), row(C_FV),
                  pl.BlockSpec((tm, LANES), lambda i: (i, C_F // LANES)),
                  vec, vec, pl.BlockSpec((1, LANES), lambda i: (0, 0)), place, place, vec, vec],
        out_specs=[pl.BlockSpec((tm, WIDTH), lambda i: (i, 0))] * 3 + [small, wide, wide,
                                                                       pl.BlockSpec((WIDTH, tm), lambda i: (0, i))],
        out_shape=[jax.ShapeDtypeStruct((m, WIDTH), F32)] * 3 + [jax.ShapeDtypeStruct((m, LANES), F32),
                                                                  jax.ShapeDtypeStruct((m, nh * AUG), BF16),
                                                                  jax.ShapeDtypeStruct((m, nh * AUG), BF16),
                                                                  jax.ShapeDtypeStruct((WIDTH, m), BF16)],
        scratch_shapes=[pltpu.VMEM((1, LANES), F32)],
        compiler_params=_cp(("arbitrary",)),
        name="foxpost",
    )(z, z, z, z, q_norm_t, k_norm_t, bf_pad, *_aug_constants())


def _flash_kernel(qi_tab, ki_tab, qa_ref, ka_ref, vt_ref, og_ref, g_ref, o_ref, m_s, l_s, acc_s):
    s_idx = pl.program_id(2)
    qi = qi_tab[s_idx]
    ki = ki_tab[s_idx]
    tq = qa_ref.shape[0]
    tk = ka_ref.shape[0]

    @pl.when(ki == 0)
    def _():
        m_s[...] = jnp.full_like(m_s, -jnp.inf)
        l_s[...] = jnp.zeros_like(l_s)
        acc_s[...] = jnp.zeros_like(acc_s)

    def update(masked):
        for hh in range(2):
            st = _dot_nt(ka_ref[:, hh * AUG:(hh + 1) * AUG], qa_ref[:, hh * AUG:(hh + 1) * AUG])
            if masked:
                r = lax.broadcasted_iota(jnp.int32, (tk, tq), 0)
                c = lax.broadcasted_iota(jnp.int32, (tk, tq), 1)
                st = jnp.where(r <= c, st, -jnp.inf)
            m_prev = m_s[hh]
            m_new = jnp.maximum(m_prev, jnp.max(st, axis=0, keepdims=True))
            alpha = jnp.exp(m_prev - m_new)
            p = jnp.exp(st - m_new)
            l_s[hh] = alpha * l_s[hh] + jnp.sum(p, axis=0, keepdims=True)
            acc_s[hh] = alpha * acc_s[hh] + _dot(vt_ref[hh * HEAD_DIM:(hh + 1) * HEAD_DIM, :], p.astype(BF16))
            m_s[hh] = m_new

    @pl.when(ki < qi)
    def _():
        update(False)

    @pl.when(ki == qi)
    def _():
        update(True)
        outs = []
        for hh in range(2):
            sl = slice(hh * HEAD_DIM, (hh + 1) * HEAD_DIM)
            ot = acc_s[hh] / l_s[hh]
            ont = ot * lax.rsqrt(jnp.mean(ot * ot, axis=0, keepdims=True) + RMS_EPS)
            outs.append(ont.T * g_ref[:, sl] * jax.nn.sigmoid(og_ref[:, sl]))
        o_ref[...] = jnp.concatenate(outs, axis=1)


def _flash_prompt(qa, ka, vt, z, g_fox_o, batch, seq):
    tq = _tile(seq, 512)
    nq = seq // tq
    pairs = [(a, b) for a in range(nq) for b in range(a + 1)]
    qi_tab = jnp.array([a for a, _ in pairs], jnp.int32)
    ki_tab = jnp.array([b for _, b in pairs], jnp.int32)
    npairs = WIDTH // LANES
    qmap = lambda b, p, s, qt, kt: (b * nq + qt[s], p)
    kmap = lambda b, p, s, qt, kt: (b * nq + kt[s], p)
    gs = pltpu.PrefetchScalarGridSpec(
        num_scalar_prefetch=2,
        grid=(batch, npairs, len(pairs)),
        in_specs=[pl.BlockSpec((tq, 2 * AUG), qmap),
                  pl.BlockSpec((tq, 2 * AUG), kmap),
                  pl.BlockSpec((LANES, tq), lambda b, p, s, qt, kt: (p, b * nq + kt[s])),
                  pl.BlockSpec((tq, LANES), lambda b, p, s, qt, kt: (b * nq + qt[s], C_OG // LANES + p)),
                  pl.BlockSpec((1, LANES), lambda b, p, s, qt, kt: (0, p))],
        out_specs=pl.BlockSpec((tq, LANES), qmap),
        scratch_shapes=[pltpu.VMEM((2, 1, tq), F32), pltpu.VMEM((2, 1, tq), F32), pltpu.VMEM((2, HEAD_DIM, tq), F32)],
    )
    return pl.pallas_call(
        _flash_kernel,
        grid_spec=gs,
        out_shape=jax.ShapeDtypeStruct((batch * seq, WIDTH), F32),
        compiler_params=_cp(("arbitrary", "arbitrary", "arbitrary")),
        name="flash_prompt",
    )(qi_tab, ki_tab, qa, ka, vt, z, g_fox_o)


def _paged_kernel(pt_ref, q_ref, kn_ref, vn_ref, lfn_ref, og_ref, g_ref, ck_hbm, cv_hbm, lf_hbm, o_ref,
                  kbuf, vbuf, lbuf, sem, *, n_pages):
    b = pl.program_id(0)
    nb = pl.num_programs(0)
    nh = WIDTH // HEAD_DIM
    rows = nh * DEC_SEQ

    def copies(page, slot):
        return (pltpu.make_async_copy(ck_hbm.at[page], kbuf.at[slot], sem.at[0, slot]),
                pltpu.make_async_copy(cv_hbm.at[page], vbuf.at[slot], sem.at[1, slot]),
                pltpu.make_async_copy(lf_hbm.at[page], lbuf.at[slot], sem.at[2, slot]))

    def fetch(bb, j, slot):
        for c in copies(pt_ref[bb, j], slot):
            c.start()

    @pl.when(b == 0)
    def _():
        fetch(0, 0, 0)

    q = (q_ref[...] * (HEAD_DIM ** -0.5)).astype(BF16)
    qh = [q[:, h * HEAD_DIM:(h + 1) * HEAD_DIM] for h in range(nh)]
    r = lax.broadcasted_iota(jnp.int32, (PAGE, PAGE), 0)
    c = lax.broadcasted_iota(jnp.int32, (PAGE, PAGE), 1)
    upper = jnp.where(r <= c, 1.0, 0.0).astype(BF16)

    def cumulate(lf, carry):
        hi, mid, lo = _split3(lf)
        cum = _dot(hi, upper) + _dot(mid, upper) + _dot(lo, upper) + carry
        return cum, jnp.broadcast_to(cum[:, PAGE - 1:PAGE], cum.shape)

    def attend(k_heads, v_heads, ck, valid, m, l, acc):
        s = jnp.concatenate([_dot_nt(qh[h], k_heads[h]) for h in range(nh)], axis=0)
        s = s - jnp.concatenate([jnp.broadcast_to(ck[h:h + 1, :], (DEC_SEQ, PAGE)) for h in range(nh)], axis=0)
        if valid is not None:
            s = jnp.where(valid, s, -jnp.inf)
        m_new = jnp.maximum(m, jnp.max(s, axis=-1, keepdims=True))
        alpha = jnp.exp(m - m_new)
        p = jnp.exp(s - m_new)
        l = alpha * l + jnp.sum(p, axis=-1, keepdims=True)
        pb = p.astype(BF16)
        pv = jnp.concatenate([_dot(pb[h * DEC_SEQ:(h + 1) * DEC_SEQ, :], v_heads[h]) for h in range(nh)], axis=0)
        return m_new, l, alpha * acc + pv

    def page_step(j, carry):
        m, l, acc, ccar = carry
        slot = (b * n_pages + j) & 1
        for cp in copies(0, slot):
            cp.wait()
        more = j + 1 < n_pages

        @pl.when(more | (b + 1 < nb))
        def _():
            fetch(jnp.where(more, b, b + 1), jnp.where(more, j + 1, 0), 1 - slot)

        ck, ccar = cumulate(lbuf[slot], ccar)
        k_heads = [kbuf[slot, pl.ds(h, PAGE, stride=nh), :].astype(BF16) for h in range(nh)]
        v_heads = [vbuf[slot, pl.ds(h, PAGE, stride=nh), :].astype(BF16) for h in range(nh)]
        m, l, acc = attend(k_heads, v_heads, ck, None, m, l, acc)
        return m, l, acc, ccar

    init = (jnp.full((rows, 1), -jnp.inf, F32), jnp.zeros((rows, 1), F32), jnp.zeros((rows, HEAD_DIM), F32),
            jnp.zeros((nh, PAGE), F32))
    m, l, acc, ccar = lax.fori_loop(0, n_pages, page_step, init)

    pad = jnp.zeros((PAGE - DEC_SEQ, HEAD_DIM), BF16)
    kn = kn_ref[...].astype(BF16)
    vn = vn_ref[...].astype(BF16)
    k_heads = [jnp.concatenate([kn[:, h * HEAD_DIM:(h + 1) * HEAD_DIM], pad], axis=0) for h in range(nh)]
    v_heads = [jnp.concatenate([vn[:, h * HEAD_DIM:(h + 1) * HEAD_DIM], pad], axis=0) for h in range(nh)]
    ck, _ = cumulate(lfn_ref[0], ccar)
    rr = lax.broadcasted_iota(jnp.int32, (rows, PAGE), 0)
    cc = lax.broadcasted_iota(jnp.int32, (rows, PAGE), 1)
    m, l, acc = attend(k_heads, v_heads, ck, cc <= (rr & (DEC_SEQ - 1)), m, l, acc)
    o = acc / l
    on = o * lax.rsqrt(jnp.mean(o * o, axis=-1, keepdims=True) + RMS_EPS)
    out = jnp.concatenate([on[h * DEC_SEQ:(h + 1) * DEC_SEQ, :] for h in range(nh)], axis=1)
    o_ref[...] = out * g_ref[...] * jax.nn.sigmoid(og_ref[...])


def _paged_sample(page_table, qn, kn, v, lf_new_t, cache_k, cache_v, lf_pages_t, z, g_fox_o, row0):
    db, n_pages = page_table.shape
    nh = WIDTH // HEAD_DIM
    rb = row0 // DEC_SEQ
    newmap = lambda b, pt: (rb + b, 0)
    hbm = pl.BlockSpec(memory_space=pl.ANY)
    gs = pltpu.PrefetchScalarGridSpec(
        num_scalar_prefetch=1,
        grid=(db,),
        in_specs=[pl.BlockSpec((DEC_SEQ, WIDTH), newmap),
                  pl.BlockSpec((DEC_SEQ, WIDTH), newmap),
                  pl.BlockSpec((DEC_SEQ, WIDTH), newmap),
                  pl.BlockSpec((1, nh, PAGE), lambda b, pt: (b, 0, 0)),
                  pl.BlockSpec((DEC_SEQ, WIDTH), lambda b, pt: (rb + b, C_OG // WIDTH)),
                  pl.BlockSpec((1, WIDTH), lambda b, pt: (0, 0)),
                  hbm, hbm, hbm],
        out_specs=pl.BlockSpec((DEC_SEQ, WIDTH), lambda b, pt: (b, 0)),
        scratch_shapes=[pltpu.VMEM((2, PAGE * nh, HEAD_DIM), F32), pltpu.VMEM((2, PAGE * nh, HEAD_DIM), F32),
                        pltpu.VMEM((2, nh, PAGE), F32), pltpu.SemaphoreType.DMA((3, 2))],
    )
    return pl.pallas_call(
        functools.partial(_paged_kernel, n_pages=n_pages),
        grid_spec=gs,
        out_shape=jax.ShapeDtypeStruct((db * DEC_SEQ, WIDTH), F32),
        compiler_params=_cp(("arbitrary",)),
        name="paged_sample",
    )(page_table, qn, kn, v, lf_new_t, z, g_fox_o, cache_k, cache_v, lf_pages_t)


def _prep_kernel(z_ref, first_ref, mu_ref, w12_ref, w3_ref, w0_ref, a0_ref, kk_ref, ka_ref, rk_ref,
                 o_r, o_w, o_k, o_v, o_a, o_b, o_g, o_bonus, carry, *, seq_len, tiles_per_seq):
    z = z_ref[...]
    tm = z.shape[0]
    row = lax.broadcasted_iota(jnp.int32, (tm, 1), 0)
    rolled = pltpu.roll(z, 1, 0)
    if tiles_per_seq >= 1 and seq_len >= tm:
        li = pl.program_id(0) % tiles_per_seq

        @pl.when(li == 0)
        def _():
            carry[...] = first_ref[0]

        zp = jnp.where(row == 0, carry[...], rolled)
        carry[...] = z[tm - 1:tm, :]
    else:
        nseq = tm // seq_len
        first = first_ref[...]
        exp = jnp.broadcast_to(first, (nseq, seq_len, first.shape[-1])).reshape(tm, first.shape[-1])
        zp = jnp.where((row & (seq_len - 1)) == 0, exp, rolled)
    zm = z + (zp - z) * mu_ref[...]
    r = zm[:, 0:WIDTH]
    k = zm[:, WIDTH:2 * WIDTH]
    v = zm[:, 2 * WIDTH:3 * WIDTH]
    lo = zm[:, C_LORA:C_LORA + 384]
    lane = lax.broadcasted_iota(jnp.int32, (tm, 384), 1)
    act = jnp.where(lane < 64, jnp.tanh(lo), jnp.where(lane < 128, lo, jax.nn.sigmoid(lo))).astype(BF16)
    l12 = _dot(act[:, 0:128], w12_ref[...])
    g = _dot(act[:, 128:384], w3_ref[...])
    log_w = -_softplus(-(w0_ref[...] + l12[:, 0:WIDTH])) - 0.5
    decay = jnp.exp(-jnp.exp(log_w))
    asig = jax.nn.sigmoid(a0_ref[...] + l12[:, WIDTH:2 * WIDTH])
    bd = _block_diag_ones()
    kk = k * kk_ref[...]
    kk = kk / jnp.maximum(jnp.sqrt(_segsum(kk * kk, bd)), 1e-12)
    kf = k * (1.0 + (asig - 1.0) * ka_ref[...])
    o_r[...] = r
    o_w[...] = decay
    o_k[...] = kf
    o_v[...] = v
    o_a[...] = -kk
    o_b[...] = kk * asig
    o_g[...] = g
    o_bonus[...] = _segsum(r * kf * rk_ref[...], bd) * v


def _rwkv_prep(z, first, mu, w12, w3, w0, a0, k_k, k_a, r_k, row0, n_rows, seq_len):
    if seq_len >= 256:
        tm = _tile(seq_len, 256)
        first_spec = pl.BlockSpec((1, 1, RWKV_COLS), lambda i: (i // (seq_len // tm), 0, 0))
    else:
        tm = _tile(n_rows, 128)
        nseq = tm // seq_len
        first_spec = pl.BlockSpec((nseq, 1, RWKV_COLS), lambda i: (i, 0, 0))
    rb = row0 // tm
    vec = lambda n: pl.BlockSpec((1, n), lambda i: (0, 0))
    out = pl.BlockSpec((tm, WIDTH), lambda i: (i, 0))
    return pl.pallas_call(
        functools.partial(_prep_kernel, seq_len=seq_len, tiles_per_seq=max(seq_len // tm, 1)),
        grid=(n_rows // tm,),
        in_specs=[pl.BlockSpec((tm, RWKV_COLS), lambda i: (rb + i, 0)), first_spec, vec(RWKV_COLS),
                  pl.BlockSpec((128, 2 * WIDTH), lambda i: (0, 0)), pl.BlockSpec((256, WIDTH), lambda i: (0, 0)),
                  vec(WIDTH), vec(WIDTH), vec(WIDTH), vec(WIDTH), vec(WIDTH)],
        out_specs=[out] * 8,
        out_shape=[jax.ShapeDtypeStruct((n_rows, WIDTH), F32)] * 8,
        scratch_shapes=[pltpu.VMEM((1, RWKV_COLS), F32)],
        compiler_params=_cp(("arbitrary",)),
        name=f"rwkv_prep_{seq_len}",
    )(z, first, mu, w12, w3, w0, a0, k_k, k_a, r_k)


SCAN_SUB = 64


def _wkv_kernel(r_ref, w_ref, k_ref, v_ref, a_ref, b_ref, s0_ref, y_ref, sT_ref, S_s, Z_s, *, groups, n_pairs, tb_len):
    tb = pl.program_id(1)
    r2 = lax.broadcasted_iota(jnp.int32, (2 * LANES, 2 * LANES), 0)
    c2 = lax.broadcasted_iota(jnp.int32, (2 * LANES, 2 * LANES), 1)
    ones_bd = jnp.where((r2 >> 6) == (c2 >> 6), 1.0, 0.0).astype(BF16)
    vrow = lax.broadcasted_iota(jnp.int32, (HEAD_DIM, LANES), 0)
    lane = lax.broadcasted_iota(jnp.int32, (HEAD_DIM, LANES), 1)
    eye2 = (lane & (HEAD_DIM - 1)) == vrow
    zero_half = jnp.zeros((HEAD_DIM, LANES), BF16)
    H = HEAD_DIM

    @pl.when(tb == 0)
    def _():
        for g in range(groups):
            for p in range(n_pairs):
                S_s[g * n_pairs + p] = jnp.concatenate([s0_ref[g, 2 * p], s0_ref[g, 2 * p + 1]], axis=1)
        Z_s[...] = jnp.zeros_like(Z_s)

    def lanes(p):
        return slice(p * LANES, (p + 1) * LANES)

    def run(t0, n):
        def step(i, carry):
            t = t0 + i
            tp = jnp.maximum(t - 1, 0)
            hit = (lane & (SCAN_SUB - 1)) == (i - 1)
            for g in range(groups):
                aa = a_ref[g, pl.ds(t, 1), :]
                ww = w_ref[g, pl.ds(t, 1), :]
                kr = k_ref[g, pl.ds(t, 1), :]
                vv = v_ref[g, pl.ds(t, 1), :]
                bb = b_ref[g, pl.ds(t, 1), :]
                rp = r_ref[g, pl.ds(tp, 1), :]
                lhs_a, lhs_q = [], []
                for p in range(n_pairs):
                    S = S_s[g * n_pairs + p]
                    hi, lo = _split2(S * aa[:, lanes(p)])
                    lhs_a.append(jnp.concatenate([hi, lo], axis=1))
                    q = (S * rp[:, lanes(p)]).astype(BF16)
                    x = jnp.where(eye2, vv[:, lanes(p)], 0.0).astype(BF16)
                    lhs_q.append(jnp.concatenate([q, x], axis=1))
                res = _dot(jnp.concatenate(lhs_a + lhs_q, axis=0), ones_bd)
                for p in range(n_pairs):
                    idx = g * n_pairs + p
                    ra = res[p * H:(p + 1) * H]
                    rq = res[(n_pairs + p) * H:(n_pairs + p + 1) * H]
                    sa = ra[:, :LANES] + ra[:, LANES:]
                    S_s[idx] = S_s[idx] * ww[:, lanes(p)] + sa * bb[:, lanes(p)] + rq[:, LANES:] * kr[:, lanes(p)]
                    Z_s[idx] = jnp.where(hit, rq[:, :LANES], Z_s[idx])
            return carry

        lax.fori_loop(0, n, step, 0)
        hit_last = (lane & (SCAN_SUB - 1)) == (n - 1)
        for g in range(groups):
            rl = r_ref[g, pl.ds(t0 + n - 1, 1), :]
            lhs = [jnp.concatenate([(S_s[g * n_pairs + p] * rl[:, lanes(p)]).astype(BF16), zero_half], axis=1)
                   for p in range(n_pairs)]
            res = _dot(jnp.concatenate(lhs, axis=0), ones_bd)
            for p in range(n_pairs):
                idx = g * n_pairs + p
                zt = jnp.where(hit_last, res[p * H:(p + 1) * H, :LANES], Z_s[idx]).T
                y_ref[g, pl.ds(t0, n), p * LANES:p * LANES + H] = zt[0:n, :]
                y_ref[g, pl.ds(t0, n), p * LANES + H:(p + 1) * LANES] = zt[SCAN_SUB:SCAN_SUB + n, :]

    if tb_len <= SCAN_SUB:
        run(0, tb_len)
    else:
        def outer(blk, carry):
            run(pl.multiple_of(blk * SCAN_SUB, SCAN_SUB), SCAN_SUB)
            return carry
        lax.fori_loop(0, tb_len // SCAN_SUB, outer, 0)

    @pl.when(tb == pl.num_programs(1) - 1)
    def _():
        for g in range(groups):
            for p in range(n_pairs):
                S = S_s[g * n_pairs + p]
                sT_ref[g, 2 * p] = S[:, 0:H]
                sT_ref[g, 2 * p + 1] = S[:, H:LANES]


def _wkv_scan(r, w, k, v, a, b, s0, n_seq, seq_len):
    nh = WIDTH // HEAD_DIM
    groups = 2 if n_seq % 2 == 0 else 1
    tb_len = min(seq_len, 128)
    nt = seq_len // tb_len
    blk = pl.BlockSpec((groups, tb_len, WIDTH), lambda s, t: (s, t, 0))
    st = pl.BlockSpec((groups, nh, HEAD_DIM, HEAD_DIM), lambda s, t: (s, 0, 0, 0))
    n_pairs = WIDTH // LANES
    seq3 = lambda x: x.reshape(n_seq, seq_len, WIDTH)
    y, s_new = pl.pallas_call(
        functools.partial(_wkv_kernel, groups=groups, n_pairs=n_pairs, tb_len=tb_len),
        grid=(n_seq // groups, nt),
        in_specs=[blk] * 6 + [st],
        out_specs=[blk, st],
        out_shape=[jax.ShapeDtypeStruct((n_seq, seq_len, WIDTH), F32),
                   jax.ShapeDtypeStruct((n_seq, nh, HEAD_DIM, HEAD_DIM), F32)],
        scratch_shapes=[pltpu.VMEM((groups * n_pairs, HEAD_DIM, LANES), F32),
                        pltpu.VMEM((groups * n_pairs, HEAD_DIM, LANES), F32)],
        compiler_params=_cp(("arbitrary", "arbitrary")),
        name=f"wkv_scan_{seq_len}",
    )(seq3(r), seq3(w), seq3(k), seq3(v), seq3(a), seq3(b), s0)
    return y.reshape(n_seq * seq_len, WIDTH), s_new


def _outproj_kernel(x_ref, y_ref, bonus_ref, g_ref, of_ref, lw_ref, lb_ref, wo_ref, gf_ref, wrh_ref, wrl_ref, br_ref,
                    x1_ref, h_ref, route_ref):
    bd = _block_diag_ones()
    y = y_ref[...]
    mu = _segsum(y, bd) * (1.0 / HEAD_DIM)
    d = y - mu
    var = _segsum(d * d, bd) * (1.0 / HEAD_DIM)
    yn = d * lax.rsqrt(var + LNX_EPS) * lw_ref[...] + lb_ref[...]
    o_r = ((yn + bonus_ref[...]) * g_ref[...]).astype(BF16)
    o_f = of_ref[...].astype(BF16)
    x1 = x_ref[...] + _dot(o_r, wo_ref[0:WIDTH, :]) + _dot(o_f, wo_ref[WIDTH:2 * WIDTH, :])
    x1_ref[...] = x1
    h = _rms_rows(x1, gf_ref[...])
    h_ref[...] = h
    hi, lo = _split2(h)
    logits = _dot(hi, wrh_ref[...]) + _dot(lo, wrh_ref[...]) + _dot(hi, wrl_ref[...]) + br_ref[...]
    tm = logits.shape[0]
    lane = lax.broadcasted_iota(jnp.int32, (tm, LANES), 1)
    neg = -jnp.inf
    lg = jnp.where(lane < N_GROUPS, logits, neg)
    mg = jnp.max(lg, axis=-1, keepdims=True)
    pg_top = 1.0 / jnp.sum(jnp.exp(lg - mg), axis=-1, keepdims=True)
    g_sel = jnp.min(jnp.where(lg == mg, lane, LANES), axis=-1, keepdims=True)
    in_grp = (lane >= N_GROUPS) & (lane < N_GROUPS + N_EXPERTS) & (((lane - N_GROUPS) >> 3) == g_sel)
    le = jnp.where(in_grp, logits, neg)
    m1 = jnp.max(le, axis=-1, keepdims=True)
    i1 = jnp.min(jnp.where(le == m1, lane, LANES), axis=-1, keepdims=True)
    le2 = jnp.where(lane == i1, neg, le)
    m2 = jnp.max(le2, axis=-1, keepdims=True)
    i2 = jnp.min(jnp.where(le2 == m2, lane, LANES), axis=-1, keepdims=True)
    e2 = jnp.exp(m2 - m1)
    gate1 = pg_top / (1.0 + e2)
    gate2 = pg_top * e2 / (1.0 + e2)
    route = jnp.where(lane == 0, (i1 - N_GROUPS).astype(F32),
                      jnp.where(lane == 1, (i2 - N_GROUPS).astype(F32),
                                jnp.where(lane == 2, gate1, jnp.where(lane == 3, gate2, 0.0))))
    route_ref[...] = route


def _outproj(x, y, bonus, g, o_f, lnx_w, lnx_b, w_out, g_ffn, wr_hi, wr_lo, b_r):
    m, d = x.shape
    tm = _tile(m, 256)
    row = lambda n: pl.BlockSpec((tm, n), lambda i: (i, 0))
    vec = lambda n: pl.BlockSpec((1, n), lambda i: (0, 0))
    full = lambda a: pl.BlockSpec(a.shape, lambda i: (0, 0))
    return pl.pallas_call(
        _outproj_kernel,
        grid=(m // tm,),
        in_specs=[row(d), row(WIDTH), row(WIDTH), row(WIDTH), row(WIDTH), vec(WIDTH), vec(WIDTH),
                  full(w_out), vec(d), full(wr_hi), full(wr_lo), vec(LANES)],
        out_specs=[row(d), row(d), row(LANES)],
        out_shape=[jax.ShapeDtypeStruct((m, d), F32), jax.ShapeDtypeStruct((m, d), F32),
                   jax.ShapeDtypeStruct((m, LANES), F32)],
        compiler_params=_cp(("arbitrary",)),
        name="outproj_router",
    )(x, y, bonus, g, o_f, lnx_w, lnx_b, w_out, g_ffn, wr_hi, wr_lo, b_r)


def _expert_kernel(be_ref, tok_ref, nu_ref, h_hbm, wg_ref, wu_ref, wd_ref, o_ref, xbuf, sem):
    i = pl.program_id(0)

    def row_copy(r, tok):
        return pltpu.make_async_copy(h_hbm.at[pl.ds(tok, 1), :], xbuf.at[pl.ds(r, 1), :], sem)

    @pl.when(i < nu_ref[0])
    def _():
        def start(r, c):
            row_copy(r, tok_ref[i * MOE_BLOCK + r]).start()
            return c

        def wait(r, c):
            row_copy(r, 0).wait()
            return c

        lax.fori_loop(0, MOE_BLOCK, start, 0)
        lax.fori_loop(0, MOE_BLOCK, wait, 0)
        x = xbuf[...].astype(BF16)
        gate = _dot(x, wg_ref[0].astype(BF16))
        up = _dot(x, wu_ref[0].astype(BF16))
        act = (gate * jax.nn.sigmoid(gate) * up).astype(BF16)
        o_ref[...] = _dot(act, wd_ref[0].astype(BF16))

    @pl.when(i >= nu_ref[0])
    def _():
        o_ref[...] = jnp.zeros_like(o_ref)


def _experts(block_e, slot_tok, n_used, h, w_gate, w_up, w_down):
    n_blocks = block_e.shape[0]
    d = h.shape[1]
    de = w_gate.shape[2]
    gs = pltpu.PrefetchScalarGridSpec(
        num_scalar_prefetch=3,
        grid=(n_blocks,),
        in_specs=[pl.BlockSpec(memory_space=pl.ANY),
                  pl.BlockSpec((1, d, de), lambda i, be, tok, nu: (be[i], 0, 0)),
                  pl.BlockSpec((1, d, de), lambda i, be, tok, nu: (be[i], 0, 0)),
                  pl.BlockSpec((1, de, d), lambda i, be, tok, nu: (be[i], 0, 0))],
        out_specs=pl.BlockSpec((MOE_BLOCK, d), lambda i, be, tok, nu: (i, 0)),
        scratch_shapes=[pltpu.VMEM((MOE_BLOCK, d), F32), pltpu.SemaphoreType.DMA(())],
    )
    return pl.pallas_call(
        _expert_kernel,
        grid_spec=gs,
        out_shape=jax.ShapeDtypeStruct((n_blocks * MOE_BLOCK, d), F32),
        compiler_params=_cp(("arbitrary",)),
        name="experts",
    )(block_e, slot_tok, n_used, h, w_gate, w_up, w_down)


def _ple_kernel(dest_ref, x1_ref, route_ref, p_ref, yb_hbm, gp_ref, wpg_ref, bpg_ref, wpp_ref, gfin_ref, o_ref, buf, sem):
    i = pl.program_id(0)
    tm = x1_ref.shape[0]

    def row_copy(r, k, slot):
        return pltpu.make_async_copy(yb_hbm.at[pl.ds(slot, 1), :], buf.at[k, pl.ds(r, 1), :], sem)

    def start(r, c):
        for k in range(2):
            row_copy(r, k, dest_ref[(i * tm + r) * 2 + k]).start()
        return c

    def wait(r, c):
        for k in range(2):
            row_copy(r, k, 0).wait()
        return c

    lax.fori_loop(0, tm, start, 0)
    lax.fori_loop(0, tm, wait, 0)
    route = route_ref[...]
    moe = buf[0] * route[:, 2:3] + buf[1] * route[:, 3:4]
    x2 = x1_ref[...] + moe
    hn = _rms_rows(x2, gp_ref[...]).astype(BF16)
    gate = jax.nn.sigmoid(_dot(hn, wpg_ref[...]) + bpg_ref[...])
    pe = _dot(p_ref[...].astype(BF16), wpp_ref[...])
    x3 = x2 + gate * pe
    o_ref[...] = _rms_rows(x3, gfin_ref[...])


def _ple(dest, x1, route, p, yb, g_ple, w_pg, b_pg, w_pp, g_final):
    m, d = x1.shape
    tm = _tile(m, 256)
    pd = p.shape[1]
    gs = pltpu.PrefetchScalarGridSpec(
        num_scalar_prefetch=1,
        grid=(m // tm,),
        in_specs=[pl.BlockSpec((tm, d), lambda i, ds: (i, 0)),
                  pl.BlockSpec((tm, LANES), lambda i, ds: (i, 0)),
                  pl.BlockSpec((tm, pd), lambda i, ds: (i, 0)),
                  pl.BlockSpec(memory_space=pl.ANY),
                  pl.BlockSpec((1, d), lambda i, ds: (0, 0)),
                  pl.BlockSpec((d, d), lambda i, ds: (0, 0)),
                  pl.BlockSpec((1, d), lambda i, ds: (0, 0)),
                  pl.BlockSpec((pd, d), lambda i, ds: (0, 0)),
                  pl.BlockSpec((1, d), lambda i, ds: (0, 0))],
        out_specs=pl.BlockSpec((tm, d), lambda i, ds: (i, 0)),
        scratch_shapes=[pltpu.VMEM((2, tm, d), F32), pltpu.SemaphoreType.DMA(())],
    )
    return pl.pallas_call(
        _ple_kernel,
        grid_spec=gs,
        out_shape=jax.ShapeDtypeStruct((m, d), F32),
        compiler_params=_cp(("arbitrary",)),
        name="moe_combine_ple",
    )(dest, x1, route, p, yb, g_ple, w_pg, b_pg, w_pp, g_final)


def _dispatch(route, n_tok):
    expert = route[:, 0:2].astype(jnp.int32)
    flat_e = expert.reshape(-1)
    n_assign = flat_e.shape[0]
    onehot = (flat_e[:, None] == jnp.arange(N_EXPERTS, dtype=jnp.int32)[None, :]).astype(jnp.int32)
    csum = jnp.cumsum(onehot, axis=0)
    rank = jnp.sum(csum * onehot, axis=1) - 1
    counts = csum[-1]
    pcounts = (counts + MOE_BLOCK - 1) // MOE_BLOCK * MOE_BLOCK
    pend = jnp.cumsum(pcounts)
    pstart = pend - pcounts
    dest = (pstart[flat_e] + rank).astype(jnp.int32)
    n_blocks = -(-n_assign // MOE_BLOCK) + N_EXPERTS
    slot_tok = jnp.zeros((n_blocks * MOE_BLOCK,), jnp.int32).at[dest].set(jnp.arange(n_assign, dtype=jnp.int32) // 2)
    n_used = (pend[-1] // MOE_BLOCK).astype(jnp.int32)
    blk = jnp.arange(n_blocks, dtype=jnp.int32)
    block_e = jnp.searchsorted(pend, jnp.minimum(blk, n_used - 1) * MOE_BLOCK, side='right').astype(jnp.int32)
    block_e = jnp.minimum(block_e, N_EXPERTS - 1)
    return block_e, slot_tok, n_used.reshape(1), dest


def kernel(x_prompt, x_sample, cache_k, cache_v, cache_logf, state_wkv, state_shift, page_table, p_prompt, p_sample, g_attn, w_in, mu_shift, w0, w_up, a0, a_up, g_up, k_k, k_a, r_k, lnx_w, lnx_b, b_f, q_norm, k_norm, g_fox_o, w_out, g_ffn, w_rg, b_rg, w_re, b_re, w_e_gate, w_e_up, w_e_down, g_ple, w_pg, b_pg, w_pp, g_final):
    depth = g_attn.shape[0]
    assert depth == 1
    batch, seq, d = x_prompt.shape
    db, dec_seq, _ = x_sample.shape
    assert dec_seq == DEC_SEQ
    nh = WIDTH // HEAD_DIM
    n_p = batch * seq
    n_s = db * dec_seq
    m = n_p + n_s
    rwkv_in = mu_shift.shape[1]
    n_pool = cache_k.shape[1]
    n_pages = page_table.shape[1]
    row = lambda a: a.reshape(1, -1)

    x_all = jnp.concatenate([x_prompt.reshape(n_p, d), x_sample.reshape(n_s, d)], axis=0)
    p_all = jnp.concatenate([p_prompt[0].reshape(n_p, -1), p_sample[0].reshape(n_s, -1)], axis=0)

    wi = w_in[0]
    zc = lambda n: jnp.zeros((d, n), F32)
    w_z = jnp.concatenate([wi[:, :rwkv_in], zc(C_F - rwkv_in), wi[:, rwkv_in + 4 * WIDTH:], zc(C_Q - C_F - nh),
                           wi[:, rwkv_in:rwkv_in + 4 * WIDTH]], axis=1).astype(BF16)
    w12 = jnp.zeros((128, 2 * WIDTH), F32).at[0:64, 0:WIDTH].set(w_up[0]).at[64:128, WIDTH:].set(a_up[0]).astype(BF16)
    w3 = jnp.zeros((256, WIDTH), F32).at[0:g_up.shape[1]].set(g_up[0]).astype(BF16)
    w_r = jnp.zeros((d, LANES), F32).at[:, 0:N_GROUPS].set(w_rg[0]).at[:, N_GROUPS:N_GROUPS + N_EXPERTS].set(w_re[0])
    wr_hi = w_r.astype(BF16)
    wr_lo = (w_r - wr_hi.astype(F32)).astype(BF16)
    b_r = jnp.zeros((1, LANES), F32).at[0, 0:N_GROUPS].set(b_rg[0]).at[0, N_GROUPS:N_GROUPS + N_EXPERTS].set(b_re[0])
    pad_cols = lambda a, n: jnp.pad(a, ((0, 0), (0, n - a.shape[1])))
    mu_pad = pad_cols(mu_shift, RWKV_COLS)
    bf_pad = pad_cols(b_f, LANES)
    qn_t = jnp.tile(q_norm, (1, nh))
    kn_t = jnp.tile(k_norm, (1, nh))

    z = _inproj(x_all, g_attn, w_z)

    qn, kn, vv, logf, qa, ka, vt = _foxpost(z, qn_t, kn_t, bf_pad, seq)

    of_p = _flash_prompt(qa, ka, vt, z, g_fox_o, batch, seq)

    lf_pages_t = cache_logf[0].transpose(0, 2, 1)
    lf_new_t = pad_cols(logf[n_p:, :nh].reshape(db, dec_seq, nh).transpose(0, 2, 1).reshape(db * nh, dec_seq), PAGE)
    of_s = _paged_sample(page_table, qn, kn, vv, lf_new_t.reshape(db, nh, PAGE),
                         cache_k[0].reshape(n_pool, PAGE * nh, HEAD_DIM), cache_v[0].reshape(n_pool, PAGE * nh, HEAD_DIM),
                         lf_pages_t, z, g_fox_o, n_p)
    o_f = jnp.concatenate([of_p, of_s], axis=0)

    first_p = jnp.zeros((batch, 1, RWKV_COLS), F32)
    first_s = pad_cols(state_shift[0], RWKV_COLS).reshape(db, 1, RWKV_COLS)
    prep_args = (mu_pad, w12, w3, w0, a0, k_k, k_a, row(r_k))
    prep_p = _rwkv_prep(z, first_p, *prep_args, 0, n_p, seq)
    prep_s = _rwkv_prep(z, first_s, *prep_args, n_p, n_s, dec_seq)
    y_p, wkv_p = _wkv_scan(*prep_p[:6], jnp.zeros((batch, nh, HEAD_DIM, HEAD_DIM), F32), batch, seq)
    y_s, wkv_s = _wkv_scan(*prep_s[:6], state_wkv[0], db, dec_seq)
    y = jnp.concatenate([y_p, y_s], axis=0)
    g_ = jnp.concatenate([prep_p[6], prep_s[6]], axis=0)
    bonus = jnp.concatenate([prep_p[7], prep_s[7]], axis=0)

    x1, h2, route = _outproj(x_all, y, bonus, g_, o_f, lnx_w, lnx_b, w_out[0].astype(BF16), g_ffn, wr_hi, wr_lo, b_r)

    block_e, slot_tok, n_used, dest = _dispatch(route, m)
    yb = _experts(block_e, slot_tok, n_used, h2, w_e_gate[0], w_e_up[0], w_e_down[0])

    y_all = _ple(dest, x1, route, p_all, yb, g_ple, w_pg[0].astype(BF16), b_pg, w_pp[0].astype(BF16), row(g_final))

    shift_p = z[seq - 1:n_p:seq, :rwkv_in]
    shift_s = z[n_p + dec_seq - 1::dec_seq, :rwkv_in]
    heads = lambda a, b_, t: a.reshape(1, b_, t, nh, HEAD_DIM)
    return (y_all[:n_p].reshape(batch, seq, d), y_all[n_p:].reshape(db, dec_seq, d),
            heads(kn[:n_p], batch, seq), heads(vv[:n_p], batch, seq), logf[:n_p, :nh].reshape(1, batch, seq, nh),
            wkv_p[None], shift_p[None],
            heads(kn[n_p:], db, dec_seq), heads(vv[n_p:], db, dec_seq), logf[n_p:, :nh].reshape(1, db, dec_seq, nh),
            wkv_s[None], shift_s[None])
```

```python
import functools

import jax
import jax.numpy as jnp
import numpy as np
from jax import lax
from jax.experimental import pallas as pl
from jax.experimental.pallas import tpu as pltpu

F32 = jnp.float32
BF16 = jnp.bfloat16

HEAD_DIM = 64
LANES = 128
RMS_EPS = 1e-6
LNX_EPS = 64e-5
PAGE = 128
MOE_BLOCK = 128
N_GROUPS = 8
N_EXPERTS = 64
DEC_SEQ = 8

C_LORA = 3072
C_F = 3456
RWKV_COLS = 3584
C_Q, C_FK, C_FV, C_OG = 4096, 5120, 6144, 7168
Z_COLS = 8192
WIDTH = 1024

VMEM_LIMIT = 52 * 1024 * 1024


def _cp(sem, vmem=VMEM_LIMIT):
    return pltpu.CompilerParams(dimension_semantics=sem, vmem_limit_bytes=vmem)


def _tile(n, pref):
    for t in (1024, 512, 256, 128, 64, 32, 16, 8):
        if t <= pref and n % t == 0:
            return t
    raise ValueError(f"no tile for {n}")


def _split2(x):
    hi = x.astype(BF16)
    lo = (x - hi.astype(F32)).astype(BF16)
    return hi, lo


def _split3(x):
    hi = x.astype(BF16)
    r1 = x - hi.astype(F32)
    mid = r1.astype(BF16)
    lo = (r1 - mid.astype(F32)).astype(BF16)
    return hi, mid, lo


def _dot(a, b):
    return jnp.dot(a, b, preferred_element_type=F32)


def _dot_nt(a, b):
    return lax.dot_general(a, b, (((1,), (1,)), ((), ())), preferred_element_type=F32)


def _block_diag_ones():
    r = lax.broadcasted_iota(jnp.int32, (LANES, LANES), 0)
    c = lax.broadcasted_iota(jnp.int32, (LANES, LANES), 1)
    return jnp.where((r >> 6) == (c >> 6), 1.0, 0.0).astype(BF16)


def _segsum(x, bd):
    outs = []
    for j in range(x.shape[1] // LANES):
        hi, lo = _split2(x[:, j * LANES:(j + 1) * LANES])
        outs.append(_dot(hi, bd) + _dot(lo, bd))
    return outs[0] if len(outs) == 1 else jnp.concatenate(outs, axis=1)


def _softplus(x):
    return jnp.maximum(x, 0.0) + jnp.log1p(jnp.exp(-jnp.abs(x)))


def _rms_rows(x, g):
    return x * lax.rsqrt(jnp.mean(x * x, axis=-1, keepdims=True) + RMS_EPS) * g


def _inproj_kernel(x_ref, g_ref, w_ref, o_ref, h_ref):
    @pl.when(pl.program_id(1) == 0)
    def _():
        h_ref[...] = _rms_rows(x_ref[...], g_ref[...]).astype(BF16)

    o_ref[...] = _dot(h_ref[...], w_ref[...])


def _inproj(x, g, w):
    m, d = x.shape
    n = w.shape[1]
    tm, tn = _tile(m, 1024), _tile(n, 512)
    return pl.pallas_call(
        _inproj_kernel,
        grid=(m // tm, n // tn),
        in_specs=[pl.BlockSpec((tm, d), lambda i, j: (i, 0)),
                  pl.BlockSpec((1, d), lambda i, j: (0, 0)),
                  pl.BlockSpec((d, tn), lambda i, j: (0, j))],
        out_specs=pl.BlockSpec((tm, tn), lambda i, j: (i, j)),
        out_shape=jax.ShapeDtypeStruct((m, n), F32),
        scratch_shapes=[pltpu.VMEM((tm, d), BF16)],
        compiler_params=_cp(("arbitrary", "arbitrary")),
        name="inproj",
    )(x, g, w)


AUG = 2 * HEAD_DIM


def _aug_constants():
    nh = WIDTH // HEAD_DIM
    pq = np.zeros((3 * LANES, WIDTH), np.float32)
    pk = np.zeros((3 * LANES, WIDTH), np.float32)
    one_q = np.zeros((1, WIDTH), np.float32)
    one_k = np.zeros((1, WIDTH), np.float32)
    for h in range(nh):
        for comp in range(3):
            pq[comp * LANES + h, h * HEAD_DIM + comp] = 1.0
            pk[comp * LANES + h, h * HEAD_DIM + 3 + comp] = -1.0
            one_q[0, h * HEAD_DIM + 3 + comp] = 1.0
            one_k[0, h * HEAD_DIM + comp] = 1.0
    return jnp.asarray(pq, BF16), jnp.asarray(pk, BF16), jnp.asarray(one_q), jnp.asarray(one_k)


def _foxpost_kernel(q_ref, k_ref, v_ref, f_ref, qn_ref, kn_ref, bf_ref, pq_ref, pk_ref, oneq_ref, onek_ref,
                    oq, ok, ov, olf, oqa, oka, ovt, carry, *, tiles_per_seq):
    i = pl.program_id(0)
    bd = _block_diag_ones()
    nh = WIDTH // HEAD_DIM

    def head_norm(x, g):
        ms = _segsum(x * x, bd) * (1.0 / HEAD_DIM)
        return x * lax.rsqrt(ms + RMS_EPS) * g

    qn = head_norm(q_ref[...], qn_ref[...])
    kn = head_norm(k_ref[...], kn_ref[...])
    v = v_ref[...]
    oq[...] = qn
    ok[...] = kn
    ov[...] = v
    lf = -_softplus(-(f_ref[...] + bf_ref[...]))
    olf[...] = lf

    @pl.when(i % tiles_per_seq == 0)
    def _():
        carry[...] = jnp.zeros_like(carry)

    tm = lf.shape[0]
    r = lax.broadcasted_iota(jnp.int32, (tm, tm), 0)
    c = lax.broadcasted_iota(jnp.int32, (tm, tm), 1)
    tri = jnp.where(c <= r, 1.0, 0.0).astype(BF16)
    hi, mid, lo = _split3(lf)
    cum = _dot(tri, hi) + _dot(tri, mid) + _dot(tri, lo) + carry[...]
    carry[...] = cum[tm - 1:tm, :]

    c3 = jnp.concatenate(_split3(cum), axis=1)
    aug_q = _dot(c3, pq_ref[...]) + oneq_ref[...]
    aug_k = _dot(c3, pk_ref[...]) + onek_ref[...]
    qs = qn * (HEAD_DIM ** -0.5)

    def interleave(x, aug):
        pieces = []
        for h in range(nh):
            sl = slice(h * HEAD_DIM, (h + 1) * HEAD_DIM)
            pieces += [x[:, sl], aug[:, sl]]
        return jnp.concatenate(pieces, axis=1).astype(BF16)

    oqa[...] = interleave(qs, aug_q)
    oka[...] = interleave(kn, aug_k)
    ovt[...] = v.T.astype(BF16)


def _foxpost(z, q_norm_t, k_norm_t, bf_pad, seq_len):
    m = z.shape[0]
    nh = WIDTH // HEAD_DIM
    tm = _tile(m, 512)
    tm = min(tm, seq_len)
    row = lambda c: pl.BlockSpec((tm, WIDTH), lambda i, c=c: (i, c // WIDTH))
    vec = pl.BlockSpec((1, WIDTH), lambda i: (0, 0))
    small = pl.BlockSpec((tm, LANES), lambda i: (i, 0))
    wide = pl.BlockSpec((tm, nh * AUG), lambda i: (i, 0))
    place = pl.BlockSpec((3 * LANES, WIDTH), lambda i: (0, 0))
    return pl.pallas_call(
        functools.partial(_foxpost_kernel, tiles_per_seq=seq_len // tm),
        grid=(m // tm,),
        in_specs=[row(C_Q), row(C_FK), row(C_FV),
                  pl.BlockSpec((tm, LANES), lambda i: (i, C_F // LANES)),
                  vec, vec, pl.BlockSpec((1, LANES), lambda i: (0, 0)), place, place, vec, vec],
        out_specs=[pl.BlockSpec((tm, WIDTH), lambda i: (i, 0))] * 3 + [small, wide, wide,
                                                                       pl.BlockSpec((WIDTH, tm), lambda i: (0, i))],
        out_shape=[jax.ShapeDtypeStruct((m, WIDTH), F32)] * 3 + [jax.ShapeDtypeStruct((m, LANES), F32),
                                                                  jax.ShapeDtypeStruct((m, nh * AUG), BF16),
                                                                  jax.ShapeDtypeStruct((m, nh * AUG), BF16),
                                                                  jax.ShapeDtypeStruct((WIDTH, m), BF16)],
        scratch_shapes=[pltpu.VMEM((1, LANES), F32)],
        compiler_params=_cp(("arbitrary",)),
        name="foxpost",
    )(z, z, z, z, q_norm_t, k_norm_t, bf_pad, *_aug_constants())


def _flash_kernel(qi_tab, ki_tab, qa_ref, ka_ref, vt_ref, og_ref, g_ref, o_ref, m_s, l_s, acc_s):
    s_idx = pl.program_id(2)
    qi = qi_tab[s_idx]
    ki = ki_tab[s_idx]
    tq = qa_ref.shape[0]
    tk = ka_ref.shape[0]

    @pl.when(ki == 0)
    def _():
        m_s[...] = jnp.full_like(m_s, -jnp.inf)
        l_s[...] = jnp.zeros_like(l_s)
        acc_s[...] = jnp.zeros_like(acc_s)

    def update(masked):
        for hh in range(2):
            st = _dot_nt(ka_ref[:, hh * AUG:(hh + 1) * AUG], qa_ref[:, hh * AUG:(hh + 1) * AUG])
            if masked:
                r = lax.broadcasted_iota(jnp.int32, (tk, tq), 0)
                c = lax.broadcasted_iota(jnp.int32, (tk, tq), 1)
                st = jnp.where(r <= c, st, -jnp.inf)
            m_prev = m_s[hh]
            m_new = jnp.maximum(m_prev, jnp.max(st, axis=0, keepdims=True))
            alpha = jnp.exp(m_prev - m_new)
            p = jnp.exp(st - m_new)
            l_s[hh] = alpha * l_s[hh] + jnp.sum(p, axis=0, keepdims=True)
            acc_s[hh] = alpha * acc_s[hh] + _dot(vt_ref[hh * HEAD_DIM:(hh + 1) * HEAD_DIM, :], p.astype(BF16))
            m_s[hh] = m_new

    @pl.when(ki < qi)
    def _():
        update(False)

    @pl.when(ki == qi)
    def _():
        update(True)
        outs = []
        for hh in range(2):
            sl = slice(hh * HEAD_DIM, (hh + 1) * HEAD_DIM)
            ot = acc_s[hh] / l_s[hh]
            ont = ot * lax.rsqrt(jnp.mean(ot * ot, axis=0, keepdims=True) + RMS_EPS)
            outs.append(ont.T * g_ref[:, sl] * jax.nn.sigmoid(og_ref[:, sl]))
        o_ref[...] = jnp.concatenate(outs, axis=1)


def _flash_prompt(qa, ka, vt, z, g_fox_o, batch, seq):
    tq = _tile(seq, 512)
    nq = seq // tq
    pairs = [(a, b) for a in range(nq) for b in range(a + 1)]
    qi_tab = jnp.array([a for a, _ in pairs], jnp.int32)
    ki_tab = jnp.array([b for _, b in pairs], jnp.int32)
    npairs = WIDTH // LANES
    qmap = lambda b, p, s, qt, kt: (b * nq + qt[s], p)
    kmap = lambda b, p, s, qt, kt: (b * nq + kt[s], p)
    gs = pltpu.PrefetchScalarGridSpec(
        num_scalar_prefetch=2,
        grid=(batch, npairs, len(pairs)),
        in_specs=[pl.BlockSpec((tq, 2 * AUG), qmap),
                  pl.BlockSpec((tq, 2 * AUG), kmap),
                  pl.BlockSpec((LANES, tq), lambda b, p, s, qt, kt: (p, b * nq + kt[s])),
                  pl.BlockSpec((tq, LANES), lambda b, p, s, qt, kt: (b * nq + qt[s], C_OG // LANES + p)),
                  pl.BlockSpec((1, LANES), lambda b, p, s, qt, kt: (0, p))],
        out_specs=pl.BlockSpec((tq, LANES), qmap),
        scratch_shapes=[pltpu.VMEM((2, 1, tq), F32), pltpu.VMEM((2, 1, tq), F32), pltpu.VMEM((2, HEAD_DIM, tq), F32)],
    )
    return pl.pallas_call(
        _flash_kernel,
        grid_spec=gs,
        out_shape=jax.ShapeDtypeStruct((batch * seq, WIDTH), F32),
        compiler_params=_cp(("arbitrary", "arbitrary", "arbitrary")),
        name="flash_prompt",
    )(qi_tab, ki_tab, qa, ka, vt, z, g_fox_o)


PAGE_GROUP = 4
PAGE_BUFS = 8


def _paged_kernel(pt_ref, q_ref, kn_ref, vn_ref, lfn_ref, og_ref, g_ref, ck_hbm, cv_hbm, lf_hbm, o_ref,
                  kbuf, vbuf, lbuf, sem, *, n_pages):
    b = pl.program_id(0)
    total = pl.num_programs(0) * n_pages
    depth = PAGE_BUFS - PAGE_GROUP
    nh = WIDTH // HEAD_DIM
    rows = nh * DEC_SEQ

    def copies(page, slot):
        return (pltpu.make_async_copy(ck_hbm.at[page], kbuf.at[slot], sem.at[0, slot]),
                pltpu.make_async_copy(cv_hbm.at[page], vbuf.at[slot], sem.at[1, slot]),
                pltpu.make_async_copy(lf_hbm.at[page], lbuf.at[slot], sem.at[2, slot]))

    def fetch(g):
        bb = g // n_pages
        for c in copies(pt_ref[bb, g - bb * n_pages], g % PAGE_BUFS):
            c.start()

    @pl.when(b == 0)
    def _():
        for g in range(depth):
            fetch(g)

    q = (q_ref[...] * (HEAD_DIM ** -0.5)).astype(BF16)
    qh = [q[:, h * HEAD_DIM:(h + 1) * HEAD_DIM] for h in range(nh)]
    def cumulate(lf, carry):
        n = lf.shape[1]
        r = lax.broadcasted_iota(jnp.int32, (n, n), 0)
        c = lax.broadcasted_iota(jnp.int32, (n, n), 1)
        upper = jnp.where(r <= c, 1.0, 0.0).astype(BF16)
        hi, mid, lo = _split3(lf)
        cum = _dot(hi, upper) + _dot(mid, upper) + _dot(lo, upper) + carry
        return cum, cum[:, n - 1:n]

    def attend(pages, feature_major, ck, valid, m, l, acc):
        qk = _dot if feature_major else _dot_nt
        pv_dot = _dot_nt if feature_major else _dot
        s = jnp.concatenate([jnp.concatenate([qk(qh[h], kh[h]) for kh, _ in pages], axis=1) for h in range(nh)], axis=0)
        s = s - jnp.concatenate([jnp.broadcast_to(ck[h:h + 1, :], (DEC_SEQ, ck.shape[1])) for h in range(nh)], axis=0)
        if valid is not None:
            s = jnp.where(valid, s, -jnp.inf)
        m_new = jnp.maximum(m, jnp.max(s, axis=-1, keepdims=True))
        alpha = jnp.exp(m - m_new)
        p = jnp.exp(s - m_new)
        l = alpha * l + jnp.sum(p, axis=-1, keepdims=True)
        pb = p.astype(BF16)
        pv = []
        for h in range(nh):
            rows_h = slice(h * DEC_SEQ, (h + 1) * DEC_SEQ)
            pv.append(sum(pv_dot(pb[rows_h, i * PAGE:(i + 1) * PAGE], vh[h]) for i, (_, vh) in enumerate(pages)))
        return m_new, l, alpha * acc + jnp.concatenate(pv, axis=0)

    def pair_step(jj, carry):
        m, l, acc, ccar = carry
        g0 = b * n_pages + PAGE_GROUP * jj
        slots = [(g0 + u) % PAGE_BUFS for u in range(PAGE_GROUP)]
        for slot in slots:
            for cp in copies(0, slot):
                cp.wait()
        for u in range(PAGE_GROUP):
            @pl.when(g0 + depth + u < total)
            def _():
                fetch(g0 + depth + u)

        ck, ccar = cumulate(jnp.concatenate([lbuf[slot] for slot in slots], axis=1), ccar)
        pages = [([kbuf[slot, h].astype(BF16) for h in range(nh)],
                  [vbuf[slot, h].astype(BF16) for h in range(nh)]) for slot in slots]
        m, l, acc = attend(pages, True, ck, None, m, l, acc)
        return m, l, acc, ccar

    init = (jnp.full((rows, 1), -jnp.inf, F32), jnp.zeros((rows, 1), F32), jnp.zeros((rows, HEAD_DIM), F32),
            jnp.zeros((nh, 1), F32))
    m, l, acc, ccar = lax.fori_loop(0, n_pages // PAGE_GROUP, pair_step, init)

    pad = jnp.zeros((PAGE - DEC_SEQ, HEAD_DIM), BF16)
    kn = kn_ref[...].astype(BF16)
    vn = vn_ref[...].astype(BF16)
    k_heads = [jnp.concatenate([kn[:, h * HEAD_DIM:(h + 1) * HEAD_DIM], pad], axis=0) for h in range(nh)]
    v_heads = [jnp.concatenate([vn[:, h * HEAD_DIM:(h + 1) * HEAD_DIM], pad], axis=0) for h in range(nh)]
    ck, _ = cumulate(lfn_ref[0], ccar)
    rr = lax.broadcasted_iota(jnp.int32, (rows, PAGE), 0)
    cc = lax.broadcasted_iota(jnp.int32, (rows, PAGE), 1)
    m, l, acc = attend([(k_heads, v_heads)], False, ck, cc <= (rr & (DEC_SEQ - 1)), m, l, acc)
    o = acc / l
    on = o * lax.rsqrt(jnp.mean(o * o, axis=-1, keepdims=True) + RMS_EPS)
    out = jnp.concatenate([on[h * DEC_SEQ:(h + 1) * DEC_SEQ, :] for h in range(nh)], axis=1)
    o_ref[...] = out * g_ref[...] * jax.nn.sigmoid(og_ref[...])


def _paged_sample(page_table, qn, kn, v, lf_new_t, cache_k, cache_v, lf_pages_t, z, g_fox_o, row0):
    db, n_pages = page_table.shape
    assert n_pages % PAGE_GROUP == 0 and n_pages >= PAGE_BUFS
    nh = WIDTH // HEAD_DIM
    rb = row0 // DEC_SEQ
    newmap = lambda b, pt: (rb + b, 0)
    hbm = pl.BlockSpec(memory_space=pl.ANY)
    gs = pltpu.PrefetchScalarGridSpec(
        num_scalar_prefetch=1,
        grid=(db,),
        in_specs=[pl.BlockSpec((DEC_SEQ, WIDTH), newmap),
                  pl.BlockSpec((DEC_SEQ, WIDTH), newmap),
                  pl.BlockSpec((DEC_SEQ, WIDTH), newmap),
                  pl.BlockSpec((1, nh, PAGE), lambda b, pt: (b, 0, 0)),
                  pl.BlockSpec((DEC_SEQ, WIDTH), lambda b, pt: (rb + b, C_OG // WIDTH)),
                  pl.BlockSpec((1, WIDTH), lambda b, pt: (0, 0)),
                  hbm, hbm, hbm],
        out_specs=pl.BlockSpec((DEC_SEQ, WIDTH), lambda b, pt: (b, 0)),
        scratch_shapes=[pltpu.VMEM((PAGE_BUFS, nh, HEAD_DIM, PAGE), F32), pltpu.VMEM((PAGE_BUFS, nh, HEAD_DIM, PAGE), F32),
                        pltpu.VMEM((PAGE_BUFS, nh, PAGE), F32), pltpu.SemaphoreType.DMA((3, PAGE_BUFS))],
    )
    return pl.pallas_call(
        functools.partial(_paged_kernel, n_pages=n_pages),
        grid_spec=gs,
        out_shape=jax.ShapeDtypeStruct((db * DEC_SEQ, WIDTH), F32),
        compiler_params=_cp(("arbitrary",)),
        name="paged_sample",
    )(page_table, qn, kn, v, lf_new_t, z, g_fox_o, cache_k, cache_v, lf_pages_t)


def _prep_kernel(z_ref, first_ref, mu_ref, w12_ref, w3_ref, w0_ref, a0_ref, kk_ref, ka_ref, rk_ref,
                 o_r, o_w, o_k, o_v, o_a, o_b, o_g, o_bonus, carry, *, seq_len, tiles_per_seq):
    z = z_ref[...]
    tm = z.shape[0]
    row = lax.broadcasted_iota(jnp.int32, (tm, 1), 0)
    rolled = pltpu.roll(z, 1, 0)
    if tiles_per_seq >= 1 and seq_len >= tm:
        li = pl.program_id(0) % tiles_per_seq

        @pl.when(li == 0)
        def _():
            carry[...] = first_ref[0]

        zp = jnp.where(row == 0, carry[...], rolled)
        carry[...] = z[tm - 1:tm, :]
    else:
        nseq = tm // seq_len
        first = first_ref[...]
        exp = jnp.broadcast_to(first, (nseq, seq_len, first.shape[-1])).reshape(tm, first.shape[-1])
        zp = jnp.where((row & (seq_len - 1)) == 0, exp, rolled)
    zm = z + (zp - z) * mu_ref[...]
    r = zm[:, 0:WIDTH]
    k = zm[:, WIDTH:2 * WIDTH]
    v = zm[:, 2 * WIDTH:3 * WIDTH]
    lo = zm[:, C_LORA:C_LORA + 384]
    lane = lax.broadcasted_iota(jnp.int32, (tm, 384), 1)
    act = jnp.where(lane < 64, jnp.tanh(lo), jnp.where(lane < 128, lo, jax.nn.sigmoid(lo))).astype(BF16)
    l12 = _dot(act[:, 0:128], w12_ref[...])
    g = _dot(act[:, 128:384], w3_ref[...])
    log_w = -_softplus(-(w0_ref[...] + l12[:, 0:WIDTH])) - 0.5
    decay = jnp.exp(-jnp.exp(log_w))
    asig = jax.nn.sigmoid(a0_ref[...] + l12[:, WIDTH:2 * WIDTH])
    bd = _block_diag_ones()
    kk = k * kk_ref[...]
    kk = kk / jnp.maximum(jnp.sqrt(_segsum(kk * kk, bd)), 1e-12)
    kf = k * (1.0 + (asig - 1.0) * ka_ref[...])
    o_r[...] = r
    o_w[...] = decay
    o_k[...] = kf
    o_v[...] = v
    o_a[...] = -kk
    o_b[...] = kk * asig
    o_g[...] = g
    o_bonus[...] = _segsum(r * kf * rk_ref[...], bd) * v


def _rwkv_prep(z, first, mu, w12, w3, w0, a0, k_k, k_a, r_k, row0, n_rows, seq_len):
    if seq_len >= 256:
        tm = _tile(seq_len, 256)
        first_spec = pl.BlockSpec((1, 1, RWKV_COLS), lambda i: (i // (seq_len // tm), 0, 0))
    else:
        tm = _tile(n_rows, 128)
        nseq = tm // seq_len
        first_spec = pl.BlockSpec((nseq, 1, RWKV_COLS), lambda i: (i, 0, 0))
    rb = row0 // tm
    vec = lambda n: pl.BlockSpec((1, n), lambda i: (0, 0))
    out = pl.BlockSpec((tm, WIDTH), lambda i: (i, 0))
    return pl.pallas_call(
        functools.partial(_prep_kernel, seq_len=seq_len, tiles_per_seq=max(seq_len // tm, 1)),
        grid=(n_rows // tm,),
        in_specs=[pl.BlockSpec((tm, RWKV_COLS), lambda i: (rb + i, 0)), first_spec, vec(RWKV_COLS),
                  pl.BlockSpec((128, 2 * WIDTH), lambda i: (0, 0)), pl.BlockSpec((256, WIDTH), lambda i: (0, 0)),
                  vec(WIDTH), vec(WIDTH), vec(WIDTH), vec(WIDTH), vec(WIDTH)],
        out_specs=[out] * 8,
        out_shape=[jax.ShapeDtypeStruct((n_rows, WIDTH), F32)] * 8,
        scratch_shapes=[pltpu.VMEM((1, RWKV_COLS), F32)],
        compiler_params=_cp(("arbitrary",)),
        name=f"rwkv_prep_{seq_len}",
    )(z, first, mu, w12, w3, w0, a0, k_k, k_a, r_k)


SCAN_SUB = 64


def _wkv_kernel(r_ref, w_ref, k_ref, v_ref, a_ref, b_ref, s0_ref, y_ref, sT_ref, S_s, Z_s, *, groups, n_pairs, tb_len):
    tb = pl.program_id(1)
    r2 = lax.broadcasted_iota(jnp.int32, (2 * LANES, 2 * LANES), 0)
    c2 = lax.broadcasted_iota(jnp.int32, (2 * LANES, 2 * LANES), 1)
    ones_bd = jnp.where((r2 >> 6) == (c2 >> 6), 1.0, 0.0).astype(BF16)
    vrow = lax.broadcasted_iota(jnp.int32, (HEAD_DIM, LANES), 0)
    lane = lax.broadcasted_iota(jnp.int32, (HEAD_DIM, LANES), 1)
    eye2 = (lane & (HEAD_DIM - 1)) == vrow
    zero_half = jnp.zeros((HEAD_DIM, LANES), BF16)
    H = HEAD_DIM

    @pl.when(tb == 0)
    def _():
        for g in range(groups):
            for p in range(n_pairs):
                S_s[g * n_pairs + p] = jnp.concatenate([s0_ref[g, 2 * p], s0_ref[g, 2 * p + 1]], axis=1)
        Z_s[...] = jnp.zeros_like(Z_s)

    def lanes(p):
        return slice(p * LANES, (p + 1) * LANES)

    def run(t0, n):
        def step(i, carry):
            t = t0 + i
            tp = jnp.maximum(t - 1, 0)
            hit = (lane & (SCAN_SUB - 1)) == (i - 1)
            for g in range(groups):
                aa = a_ref[g, pl.ds(t, 1), :]
                ww = w_ref[g, pl.ds(t, 1), :]
                kr = k_ref[g, pl.ds(t, 1), :]
                vv = v_ref[g, pl.ds(t, 1), :]
                bb = b_ref[g, pl.ds(t, 1), :]
                rp = r_ref[g, pl.ds(tp, 1), :]
                lhs_s, xs = [], []
                for p in range(n_pairs):
                    S = S_s[g * n_pairs + p]
                    lhs_s.append(jnp.concatenate([(S * aa[:, lanes(p)]).astype(BF16),
                                                  (S * rp[:, lanes(p)]).astype(BF16)], axis=1))
                    xs.append(jnp.where(eye2, vv[:, lanes(p)], 0.0).astype(BF16))
                lhs_x = [jnp.concatenate(xs[p:p + 2], axis=1) for p in range(0, n_pairs, 2)]
                res = _dot(jnp.concatenate(lhs_s + lhs_x, axis=0), ones_bd)
                for p in range(n_pairs):
                    idx = g * n_pairs + p
                    rs = res[p * H:(p + 1) * H]
                    vb = res[(n_pairs + p // 2) * H:(n_pairs + p // 2 + 1) * H, (p % 2) * LANES:(p % 2 + 1) * LANES]
                    S_s[idx] = S_s[idx] * ww[:, lanes(p)] + rs[:, :LANES] * bb[:, lanes(p)] + vb * kr[:, lanes(p)]
                    Z_s[idx] = jnp.where(hit, rs[:, LANES:], Z_s[idx])
            return carry

        lax.fori_loop(0, n, step, 0)
        hit_last = (lane & (SCAN_SUB - 1)) == (n - 1)
        for g in range(groups):
            rl = r_ref[g, pl.ds(t0 + n - 1, 1), :]
            lhs = [jnp.concatenate([(S_s[g * n_pairs + p] * rl[:, lanes(p)]).astype(BF16), zero_half], axis=1)
                   for p in range(n_pairs)]
            res = _dot(jnp.concatenate(lhs, axis=0), ones_bd)
            for p in range(n_pairs):
                idx = g * n_pairs + p
                zt = jnp.where(hit_last, res[p * H:(p + 1) * H, :LANES], Z_s[idx]).T
                y_ref[g, pl.ds(t0, n), p * LANES:p * LANES + H] = zt[0:n, :]
                y_ref[g, pl.ds(t0, n), p * LANES + H:(p + 1) * LANES] = zt[SCAN_SUB:SCAN_SUB + n, :]

    if tb_len <= SCAN_SUB:
        run(0, tb_len)
    else:
        def outer(blk, carry):
            run(pl.multiple_of(blk * SCAN_SUB, SCAN_SUB), SCAN_SUB)
            return carry
        lax.fori_loop(0, tb_len // SCAN_SUB, outer, 0)

    @pl.when(tb == pl.num_programs(1) - 1)
    def _():
        for g in range(groups):
            for p in range(n_pairs):
                S = S_s[g * n_pairs + p]
                sT_ref[g, 2 * p] = S[:, 0:H]
                sT_ref[g, 2 * p + 1] = S[:, H:LANES]


def _wkv_scan(r, w, k, v, a, b, s0, n_seq, seq_len):
    nh = WIDTH // HEAD_DIM
    groups = 2 if n_seq % 2 == 0 else 1
    tb_len = min(seq_len, 128)
    nt = seq_len // tb_len
    blk = pl.BlockSpec((groups, tb_len, WIDTH), lambda s, t: (s, t, 0))
    st = pl.BlockSpec((groups, nh, HEAD_DIM, HEAD_DIM), lambda s, t: (s, 0, 0, 0))
    n_pairs = WIDTH // LANES
    seq3 = lambda x: x.reshape(n_seq, seq_len, WIDTH)
    y, s_new = pl.pallas_call(
        functools.partial(_wkv_kernel, groups=groups, n_pairs=n_pairs, tb_len=tb_len),
        grid=(n_seq // groups, nt),
        in_specs=[blk] * 6 + [st],
        out_specs=[blk, st],
        out_shape=[jax.ShapeDtypeStruct((n_seq, seq_len, WIDTH), F32),
                   jax.ShapeDtypeStruct((n_seq, nh, HEAD_DIM, HEAD_DIM), F32)],
        scratch_shapes=[pltpu.VMEM((groups * n_pairs, HEAD_DIM, LANES), F32),
                        pltpu.VMEM((groups * n_pairs, HEAD_DIM, LANES), F32)],
        compiler_params=_cp(("arbitrary", "arbitrary")),
        name=f"wkv_scan_{seq_len}",
    )(seq3(r), seq3(w), seq3(k), seq3(v), seq3(a), seq3(b), s0)
    return y.reshape(n_seq * seq_len, WIDTH), s_new


def _outproj_kernel(x_ref, y_ref, bonus_ref, g_ref, of_ref, lw_ref, lb_ref, wo_ref, gf_ref, wrh_ref, wrl_ref, br_ref,
                    x1_ref, h_ref, route_ref):
    bd = _block_diag_ones()
    y = y_ref[...]
    mu = _segsum(y, bd) * (1.0 / HEAD_DIM)
    d = y - mu
    var = _segsum(d * d, bd) * (1.0 / HEAD_DIM)
    yn = d * lax.rsqrt(var + LNX_EPS) * lw_ref[...] + lb_ref[...]
    o_r = ((yn + bonus_ref[...]) * g_ref[...]).astype(BF16)
    o_f = of_ref[...].astype(BF16)
    x1 = x_ref[...] + _dot(o_r, wo_ref[0:WIDTH, :]) + _dot(o_f, wo_ref[WIDTH:2 * WIDTH, :])
    x1_ref[...] = x1
    h = _rms_rows(x1, gf_ref[...])
    h_ref[...] = h
    hi, lo = _split2(h)
    logits = _dot(hi, wrh_ref[...]) + _dot(lo, wrh_ref[...]) + _dot(hi, wrl_ref[...]) + br_ref[...]
    tm = logits.shape[0]
    lane = lax.broadcasted_iota(jnp.int32, (tm, LANES), 1)
    neg = -jnp.inf
    lg = jnp.where(lane < N_GROUPS, logits, neg)
    mg = jnp.max(lg, axis=-1, keepdims=True)
    pg_top = 1.0 / jnp.sum(jnp.exp(lg - mg), axis=-1, keepdims=True)
    g_sel = jnp.min(jnp.where(lg == mg, lane, LANES), axis=-1, keepdims=True)
    in_grp = (lane >= N_GROUPS) & (lane < N_GROUPS + N_EXPERTS) & (((lane - N_GROUPS) >> 3) == g_sel)
    le = jnp.where(in_grp, logits, neg)
    m1 = jnp.max(le, axis=-1, keepdims=True)
    i1 = jnp.min(jnp.where(le == m1, lane, LANES), axis=-1, keepdims=True)
    le2 = jnp.where(lane == i1, neg, le)
    m2 = jnp.max(le2, axis=-1, keepdims=True)
    i2 = jnp.min(jnp.where(le2 == m2, lane, LANES), axis=-1, keepdims=True)
    e2 = jnp.exp(m2 - m1)
    gate1 = pg_top / (1.0 + e2)
    gate2 = pg_top * e2 / (1.0 + e2)
    route = jnp.where(lane == 0, (i1 - N_GROUPS).astype(F32),
                      jnp.where(lane == 1, (i2 - N_GROUPS).astype(F32),
                                jnp.where(lane == 2, gate1, jnp.where(lane == 3, gate2, 0.0))))
    route_ref[...] = route


def _outproj(x, y, bonus, g, o_f, lnx_w, lnx_b, w_out, g_ffn, wr_hi, wr_lo, b_r):
    m, d = x.shape
    tm = _tile(m, 256)
    row = lambda n: pl.BlockSpec((tm, n), lambda i: (i, 0))
    vec = lambda n: pl.BlockSpec((1, n), lambda i: (0, 0))
    full = lambda a: pl.BlockSpec(a.shape, lambda i: (0, 0))
    return pl.pallas_call(
        _outproj_kernel,
        grid=(m // tm,),
        in_specs=[row(d), row(WIDTH), row(WIDTH), row(WIDTH), row(WIDTH), vec(WIDTH), vec(WIDTH),
                  full(w_out), vec(d), full(wr_hi), full(wr_lo), vec(LANES)],
        out_specs=[row(d), row(d), row(LANES)],
        out_shape=[jax.ShapeDtypeStruct((m, d), F32), jax.ShapeDtypeStruct((m, d), F32),
                   jax.ShapeDtypeStruct((m, LANES), F32)],
        compiler_params=_cp(("arbitrary",)),
        name="outproj_router",
    )(x, y, bonus, g, o_f, lnx_w, lnx_b, w_out, g_ffn, wr_hi, wr_lo, b_r)


def _expert_kernel(be_ref, tok_ref, nu_ref, h_hbm, wg_ref, wu_ref, wd_ref, o_ref, xbuf, sem):
    i = pl.program_id(0)
    n_used = nu_ref[0]

    def row_copy(tok, r, slot):
        return pltpu.make_async_copy(h_hbm.at[pl.ds(tok, 1), :], xbuf.at[slot, pl.ds(r, 1), :], sem.at[slot])

    def gather(blk, slot):
        def start(r, c):
            row_copy(tok_ref[blk * MOE_BLOCK + r], r, slot).start()
            return c
        lax.fori_loop(0, MOE_BLOCK, start, 0, unroll=8)

    @pl.when(i == 0)
    def _():
        gather(0, 0)

    @pl.when(i < n_used)
    def _():
        slot = i & 1

        def wait(r, c):
            row_copy(0, r, slot).wait()
            return c
        lax.fori_loop(0, MOE_BLOCK, wait, 0, unroll=8)

        @pl.when(i + 1 < n_used)
        def _():
            gather(i + 1, 1 - slot)

        x = xbuf[slot].astype(BF16)
        gate = _dot(x, wg_ref[0].astype(BF16))
        up = _dot(x, wu_ref[0].astype(BF16))
        act = (gate * jax.nn.sigmoid(gate) * up).astype(BF16)
        o_ref[...] = _dot(act, wd_ref[0].astype(BF16))

    @pl.when(i >= n_used)
    def _():
        o_ref[...] = jnp.zeros_like(o_ref)


def _experts(block_e, slot_tok, n_used, h, w_gate, w_up, w_down):
    n_blocks = block_e.shape[0]
    d = h.shape[1]
    de = w_gate.shape[2]
    gs = pltpu.PrefetchScalarGridSpec(
        num_scalar_prefetch=3,
        grid=(n_blocks,),
        in_specs=[pl.BlockSpec(memory_space=pl.ANY),
                  pl.BlockSpec((1, d, de), lambda i, be, tok, nu: (be[i], 0, 0)),
                  pl.BlockSpec((1, d, de), lambda i, be, tok, nu: (be[i], 0, 0)),
                  pl.BlockSpec((1, de, d), lambda i, be, tok, nu: (be[i], 0, 0))],
        out_specs=pl.BlockSpec((MOE_BLOCK, d), lambda i, be, tok, nu: (i, 0)),
        scratch_shapes=[pltpu.VMEM((2, MOE_BLOCK, d), F32), pltpu.SemaphoreType.DMA((2,))],
    )
    return pl.pallas_call(
        _expert_kernel,
        grid_spec=gs,
        out_shape=jax.ShapeDtypeStruct((n_blocks * MOE_BLOCK, d), F32),
        compiler_params=_cp(("arbitrary",)),
        name="experts",
    )(block_e, slot_tok, n_used, h, w_gate, w_up, w_down)


def _ple_kernel(dest_ref, x1_ref, route_ref, p_ref, yb_hbm, gp_ref, wpg_ref, bpg_ref, wpp_ref, gfin_ref, o_ref, buf, sem):
    i = pl.program_id(0)
    tm = x1_ref.shape[0]

    def row_copy(r, k, slot):
        return pltpu.make_async_copy(yb_hbm.at[pl.ds(slot, 1), :], buf.at[k, pl.ds(r, 1), :], sem)

    def start(r, c):
        for k in range(2):
            row_copy(r, k, dest_ref[(i * tm + r) * 2 + k]).start()
        return c

    def wait(r, c):
        for k in range(2):
            row_copy(r, k, 0).wait()
        return c

    lax.fori_loop(0, tm, start, 0)
    lax.fori_loop(0, tm, wait, 0)
    route = route_ref[...]
    moe = buf[0] * route[:, 2:3] + buf[1] * route[:, 3:4]
    x2 = x1_ref[...] + moe
    hn = _rms_rows(x2, gp_ref[...]).astype(BF16)
    gate = jax.nn.sigmoid(_dot(hn, wpg_ref[...]) + bpg_ref[...])
    pe = _dot(p_ref[...].astype(BF16), wpp_ref[...])
    x3 = x2 + gate * pe
    o_ref[...] = _rms_rows(x3, gfin_ref[...])


def _ple(dest, x1, route, p, yb, g_ple, w_pg, b_pg, w_pp, g_final):
    m, d = x1.shape
    tm = _tile(m, 256)
    pd = p.shape[1]
    gs = pltpu.PrefetchScalarGridSpec(
        num_scalar_prefetch=1,
        grid=(m // tm,),
        in_specs=[pl.BlockSpec((tm, d), lambda i, ds: (i, 0)),
                  pl.BlockSpec((tm, LANES), lambda i, ds: (i, 0)),
                  pl.BlockSpec((tm, pd), lambda i, ds: (i, 0)),
                  pl.BlockSpec(memory_space=pl.ANY),
                  pl.BlockSpec((1, d), lambda i, ds: (0, 0)),
                  pl.BlockSpec((d, d), lambda i, ds: (0, 0)),
                  pl.BlockSpec((1, d), lambda i, ds: (0, 0)),
                  pl.BlockSpec((pd, d), lambda i, ds: (0, 0)),
                  pl.BlockSpec((1, d), lambda i, ds: (0, 0))],
        out_specs=pl.BlockSpec((tm, d), lambda i, ds: (i, 0)),
        scratch_shapes=[pltpu.VMEM((2, tm, d), F32), pltpu.SemaphoreType.DMA(())],
    )
    return pl.pallas_call(
        _ple_kernel,
        grid_spec=gs,
        out_shape=jax.ShapeDtypeStruct((m, d), F32),
        compiler_params=_cp(("arbitrary",)),
        name="moe_combine_ple",
    )(dest, x1, route, p, yb, g_ple, w_pg, b_pg, w_pp, g_final)


def _dispatch(route, n_tok):
    expert = route[:, 0:2].astype(jnp.int32)
    flat_e = expert.reshape(-1)
    n_assign = flat_e.shape[0]
    onehot = (flat_e[:, None] == jnp.arange(N_EXPERTS, dtype=jnp.int32)[None, :]).astype(jnp.int32)
    csum = jnp.cumsum(onehot, axis=0)
    rank = jnp.sum(csum * onehot, axis=1) - 1
    counts = csum[-1]
    pcounts = (counts + MOE_BLOCK - 1) // MOE_BLOCK * MOE_BLOCK
    pend = jnp.cumsum(pcounts)
    pstart = pend - pcounts
    dest = (pstart[flat_e] + rank).astype(jnp.int32)
    n_blocks = -(-n_assign // MOE_BLOCK) + N_EXPERTS
    slot_tok = jnp.zeros((n_blocks * MOE_BLOCK,), jnp.int32).at[dest].set(jnp.arange(n_assign, dtype=jnp.int32) // 2)
    n_used = (pend[-1] // MOE_BLOCK).astype(jnp.int32)
    blk = jnp.arange(n_blocks, dtype=jnp.int32)
    block_e = jnp.searchsorted(pend, jnp.minimum(blk, n_used - 1) * MOE_BLOCK, side='right').astype(jnp.int32)
    block_e = jnp.minimum(block_e, N_EXPERTS - 1)
    return block_e, slot_tok, n_used.reshape(1), dest


def kernel(x_prompt, x_sample, cache_k, cache_v, cache_logf, state_wkv, state_shift, page_table, p_prompt, p_sample, g_attn, w_in, mu_shift, w0, w_up, a0, a_up, g_up, k_k, k_a, r_k, lnx_w, lnx_b, b_f, q_norm, k_norm, g_fox_o, w_out, g_ffn, w_rg, b_rg, w_re, b_re, w_e_gate, w_e_up, w_e_down, g_ple, w_pg, b_pg, w_pp, g_final):
    depth = g_attn.shape[0]
    assert depth == 1
    batch, seq, d = x_prompt.shape
    db, dec_seq, _ = x_sample.shape
    assert dec_seq == DEC_SEQ
    nh = WIDTH // HEAD_DIM
    n_p = batch * seq
    n_s = db * dec_seq
    m = n_p + n_s
    rwkv_in = mu_shift.shape[1]
    n_pool = cache_k.shape[1]
    n_pages = page_table.shape[1]
    row = lambda a: a.reshape(1, -1)

    x_all = jnp.concatenate([x_prompt.reshape(n_p, d), x_sample.reshape(n_s, d)], axis=0)
    p_all = jnp.concatenate([p_prompt[0].reshape(n_p, -1), p_sample[0].reshape(n_s, -1)], axis=0)

    wi = w_in[0]
    zc = lambda n: jnp.zeros((d, n), F32)
    w_z = jnp.concatenate([wi[:, :rwkv_in], zc(C_F - rwkv_in), wi[:, rwkv_in + 4 * WIDTH:], zc(C_Q - C_F - nh),
                           wi[:, rwkv_in:rwkv_in + 4 * WIDTH]], axis=1).astype(BF16)
    w12 = jnp.zeros((128, 2 * WIDTH), F32).at[0:64, 0:WIDTH].set(w_up[0]).at[64:128, WIDTH:].set(a_up[0]).astype(BF16)
    w3 = jnp.zeros((256, WIDTH), F32).at[0:g_up.shape[1]].set(g_up[0]).astype(BF16)
    w_r = jnp.zeros((d, LANES), F32).at[:, 0:N_GROUPS].set(w_rg[0]).at[:, N_GROUPS:N_GROUPS + N_EXPERTS].set(w_re[0])
    wr_hi = w_r.astype(BF16)
    wr_lo = (w_r - wr_hi.astype(F32)).astype(BF16)
    b_r = jnp.zeros((1, LANES), F32).at[0, 0:N_GROUPS].set(b_rg[0]).at[0, N_GROUPS:N_GROUPS + N_EXPERTS].set(b_re[0])
    pad_cols = lambda a, n: jnp.pad(a, ((0, 0), (0, n - a.shape[1])))
    mu_pad = pad_cols(mu_shift, RWKV_COLS)
    bf_pad = pad_cols(b_f, LANES)
    qn_t = jnp.tile(q_norm, (1, nh))
    kn_t = jnp.tile(k_norm, (1, nh))

    z = _inproj(x_all, g_attn, w_z)

    qn, kn, vv, logf, qa, ka, vt = _foxpost(z, qn_t, kn_t, bf_pad, seq)

    of_p = _flash_prompt(qa, ka, vt, z, g_fox_o, batch, seq)

    lf_pages_t = cache_logf[0].transpose(0, 2, 1)
    lf_new_t = pad_cols(logf[n_p:, :nh].reshape(db, dec_seq, nh).transpose(0, 2, 1).reshape(db * nh, dec_seq), PAGE)
    of_s = _paged_sample(page_table, qn, kn, vv, lf_new_t.reshape(db, nh, PAGE),
                         cache_k[0].transpose(0, 2, 3, 1), cache_v[0].transpose(0, 2, 3, 1), lf_pages_t, z, g_fox_o, n_p)
    o_f = jnp.concatenate([of_p, of_s], axis=0)

    first_p = jnp.zeros((batch, 1, RWKV_COLS), F32)
    first_s = pad_cols(state_shift[0], RWKV_COLS).reshape(db, 1, RWKV_COLS)
    prep_args = (mu_pad, w12, w3, w0, a0, k_k, k_a, row(r_k))
    prep_p = _rwkv_prep(z, first_p, *prep_args, 0, n_p, seq)
    prep_s = _rwkv_prep(z, first_s, *prep_args, n_p, n_s, dec_seq)
    y_p, wkv_p = _wkv_scan(*prep_p[:6], jnp.zeros((batch, nh, HEAD_DIM, HEAD_DIM), F32), batch, seq)
    y_s, wkv_s = _wkv_scan(*prep_s[:6], state_wkv[0], db, dec_seq)
    y = jnp.concatenate([y_p, y_s], axis=0)
    g_ = jnp.concatenate([prep_p[6], prep_s[6]], axis=0)
    bonus = jnp.concatenate([prep_p[7], prep_s[7]], axis=0)

    x1, h2, route = _outproj(x_all, y, bonus, g_, o_f, lnx_w, lnx_b, w_out[0].astype(BF16), g_ffn, wr_hi, wr_lo, b_r)

    block_e, slot_tok, n_used, dest = _dispatch(route, m)
    yb = _experts(block_e, slot_tok, n_used, h2, w_e_gate[0], w_e_up[0], w_e_down[0])

    y_all = _ple(dest, x1, route, p_all, yb, g_ple, w_pg[0].astype(BF16), b_pg, w_pp[0].astype(BF16), row(g_final))

    shift_p = z[seq - 1:n_p:seq, :rwkv_in]
    shift_s = z[n_p + dec_seq - 1::dec_seq, :rwkv_in]
    heads = lambda a, b_, t: a.reshape(1, b_, t, nh, HEAD_DIM)
    return (y_all[:n_p].reshape(batch, seq, d), y_all[n_p:].reshape(db, dec_seq, d),
            heads(kn[:n_p], batch, seq), heads(vv[:n_p], batch, seq), logf[:n_p, :nh].reshape(1, batch, seq, nh),
            wkv_p[None], shift_p[None],
            heads(kn[n_p:], db, dec_seq), heads(vv[n_p:], db, dec_seq), logf[n_p:, :nh].reshape(1, db, dec_seq, nh),
            wkv_s[None], shift_s[None])
```

```python
import functools
import math

import jax
import jax.numpy as jnp
import numpy as np
from jax import lax
from jax.experimental import pallas as pl
from jax.experimental.pallas import tpu as pltpu

F32 = jnp.float32
BF16 = jnp.bfloat16

HEAD_DIM = 64
LANES = 128
RMS_EPS = 1e-6
LNX_EPS = 64e-5
PAGE = 128
MOE_BLOCK = 128
N_GROUPS = 8
N_EXPERTS = 64
DEC_SEQ = 8

C_LORA = 3072
C_F = 3456
RWKV_COLS = 3584
C_Q, C_FK, C_FV, C_OG = 4096, 5120, 6144, 7168
Z_COLS = 8192
WIDTH = 1024

VMEM_LIMIT = 52 * 1024 * 1024


def _cp(sem, vmem=VMEM_LIMIT):
    return pltpu.CompilerParams(dimension_semantics=sem, vmem_limit_bytes=vmem)


def _tile(n, pref):
    for t in (1024, 512, 256, 128, 64, 32, 16, 8):
        if t <= pref and n % t == 0:
            return t
    raise ValueError(f"no tile for {n}")


def _split2(x):
    hi = x.astype(BF16)
    lo = (x - hi.astype(F32)).astype(BF16)
    return hi, lo


def _split3(x):
    hi = x.astype(BF16)
    r1 = x - hi.astype(F32)
    mid = r1.astype(BF16)
    lo = (r1 - mid.astype(F32)).astype(BF16)
    return hi, mid, lo


def _dot(a, b):
    return jnp.dot(a, b, preferred_element_type=F32)


def _dot_nt(a, b):
    return lax.dot_general(a, b, (((1,), (1,)), ((), ())), preferred_element_type=F32)


def _block_diag_ones():
    r = lax.broadcasted_iota(jnp.int32, (LANES, LANES), 0)
    c = lax.broadcasted_iota(jnp.int32, (LANES, LANES), 1)
    return jnp.where((r >> 6) == (c >> 6), 1.0, 0.0).astype(BF16)


def _segsum(x, bd):
    outs = []
    for j in range(x.shape[1] // LANES):
        hi, lo = _split2(x[:, j * LANES:(j + 1) * LANES])
        outs.append(_dot(hi, bd) + _dot(lo, bd))
    return outs[0] if len(outs) == 1 else jnp.concatenate(outs, axis=1)


def _softplus(x):
    return jnp.maximum(x, 0.0) + jnp.log1p(jnp.exp(-jnp.abs(x)))


def _rms_rows(x, g):
    return x * lax.rsqrt(jnp.mean(x * x, axis=-1, keepdims=True) + RMS_EPS) * g


def _inproj_kernel(x_ref, g_ref, w_ref, o_ref, h_ref):
    @pl.when(pl.program_id(1) == 0)
    def _():
        h_ref[...] = _rms_rows(x_ref[...], g_ref[...]).astype(BF16)

    o_ref[...] = _dot(h_ref[...], w_ref[...])


def _inproj(x, g, w):
    m, d = x.shape
    n = w.shape[1]
    tm, tn = _tile(m, 1024), _tile(n, 512)
    return pl.pallas_call(
        _inproj_kernel,
        grid=(m // tm, n // tn),
        in_specs=[pl.BlockSpec((tm, d), lambda i, j: (i, 0)),
                  pl.BlockSpec((1, d), lambda i, j: (0, 0)),
                  pl.BlockSpec((d, tn), lambda i, j: (0, j))],
        out_specs=pl.BlockSpec((tm, tn), lambda i, j: (i, j)),
        out_shape=jax.ShapeDtypeStruct((m, n), F32),
        scratch_shapes=[pltpu.VMEM((tm, d), BF16)],
        compiler_params=_cp(("arbitrary", "arbitrary")),
        name="inproj",
    )(x, g, w)


AUG = 2 * HEAD_DIM


def _aug_constants():
    nh = WIDTH // HEAD_DIM
    pq = np.zeros((3 * LANES, WIDTH), np.float32)
    pk = np.zeros((3 * LANES, WIDTH), np.float32)
    one_q = np.zeros((1, WIDTH), np.float32)
    one_k = np.zeros((1, WIDTH), np.float32)
    for h in range(nh):
        for comp in range(3):
            pq[comp * LANES + h, h * HEAD_DIM + comp] = 1.0
            pk[comp * LANES + h, h * HEAD_DIM + 3 + comp] = -1.0
            one_q[0, h * HEAD_DIM + 3 + comp] = 1.0
            one_k[0, h * HEAD_DIM + comp] = 1.0
    return jnp.asarray(pq, BF16), jnp.asarray(pk, BF16), jnp.asarray(one_q), jnp.asarray(one_k)


def _foxpost_kernel(q_ref, k_ref, v_ref, f_ref, qn_ref, kn_ref, bf_ref, pq_ref, pk_ref, oneq_ref, onek_ref,
                    oq, ok, ov, olf, oqa, oka, ovt, okt32, ovt32, carry, *, tiles_per_seq, prompt_tiles):
    i = pl.program_id(0)
    bd = _block_diag_ones()
    nh = WIDTH // HEAD_DIM

    def head_norm(x, g):
        ms = _segsum(x * x, bd) * (1.0 / HEAD_DIM)
        return x * lax.rsqrt(ms + RMS_EPS) * g

    qn = head_norm(q_ref[...], qn_ref[...])
    kn = head_norm(k_ref[...], kn_ref[...])
    v = v_ref[...]
    vt = v.T
    ovt[...] = vt.astype(BF16)

    @pl.when(i < prompt_tiles)
    def _():
        okt32[0] = kn.T
        ovt32[0] = vt

    @pl.when(i >= prompt_tiles)
    def _():
        oq[...] = qn
        ok[...] = kn
        ov[...] = v

    lf = -_softplus(-(f_ref[...] + bf_ref[...]))
    olf[...] = lf

    @pl.when(i % tiles_per_seq == 0)
    def _():
        carry[...] = jnp.zeros_like(carry)

    tm = lf.shape[0]
    r = lax.broadcasted_iota(jnp.int32, (tm, tm), 0)
    c = lax.broadcasted_iota(jnp.int32, (tm, tm), 1)
    tri = jnp.where(c <= r, 1.0, 0.0).astype(BF16)
    hi, mid, lo = _split3(lf)
    cum = _dot(tri, hi) + _dot(tri, mid) + _dot(tri, lo) + carry[...]
    carry[...] = cum[tm - 1:tm, :]

    c3 = jnp.concatenate(_split3(cum), axis=1)
    aug_q = _dot(c3, pq_ref[...]) + oneq_ref[...]
    aug_k = _dot(c3, pk_ref[...]) + onek_ref[...]
    qs = qn * (HEAD_DIM ** -0.5)

    def interleave(x, aug):
        pieces = []
        for h in range(nh):
            sl = slice(h * HEAD_DIM, (h + 1) * HEAD_DIM)
            pieces += [x[:, sl], aug[:, sl]]
        return jnp.concatenate(pieces, axis=1).astype(BF16)

    oqa[...] = interleave(qs, aug_q)
    oka[...] = interleave(kn, aug_k)


def _foxpost(z, q_norm_t, k_norm_t, bf_pad, batch, seq_len):
    m = z.shape[0]
    nh = WIDTH // HEAD_DIM
    n_p = batch * seq_len
    tm = min(_tile(math.gcd(n_p, m - n_p), 512), seq_len)
    npt, tps = n_p // tm, seq_len // tm
    sample = pl.BlockSpec((tm, WIDTH), lambda i: (jnp.maximum(i - npt, 0), 0))
    prompt_t = pl.BlockSpec((1, WIDTH, tm), lambda i: (jnp.minimum(i, npt - 1) // tps, 0, jnp.minimum(i, npt - 1) % tps))
    row = lambda c: pl.BlockSpec((tm, WIDTH), lambda i, c=c: (i, c // WIDTH))
    vec = pl.BlockSpec((1, WIDTH), lambda i: (0, 0))
    small = pl.BlockSpec((tm, LANES), lambda i: (i, 0))
    wide = pl.BlockSpec((tm, nh * AUG), lambda i: (i, 0))
    place = pl.BlockSpec((3 * LANES, WIDTH), lambda i: (0, 0))
    return pl.pallas_call(
        functools.partial(_foxpost_kernel, tiles_per_seq=tps, prompt_tiles=npt),
        grid=(m // tm,),
        in_specs=[row(C_Q), row(C_FK), row(C_FV),
                  pl.BlockSpec((tm, LANES), lambda i: (i, C_F // LANES)),
                  vec, vec, pl.BlockSpec((1, LANES), lambda i: (0, 0)), place, place, vec, vec],
        out_specs=[sample] * 3 + [small, wide, wide, pl.BlockSpec((WIDTH, tm), lambda i: (0, i)), prompt_t, prompt_t],
        out_shape=[jax.ShapeDtypeStruct((m - n_p, WIDTH), F32)] * 3 + [jax.ShapeDtypeStruct((m, LANES), F32),
                                                                        jax.ShapeDtypeStruct((m, nh * AUG), BF16),
                                                                        jax.ShapeDtypeStruct((m, nh * AUG), BF16),
                                                                        jax.ShapeDtypeStruct((WIDTH, m), BF16),
                                                                        jax.ShapeDtypeStruct((batch, WIDTH, seq_len), F32),
                                                                        jax.ShapeDtypeStruct((batch, WIDTH, seq_len), F32)],
        scratch_shapes=[pltpu.VMEM((1, LANES), F32)],
        compiler_params=_cp(("arbitrary",)),
        name="foxpost",
    )(z, z, z, z, q_norm_t, k_norm_t, bf_pad, *_aug_constants())


def _flash_kernel(qi_tab, ki_tab, qa_ref, ka_ref, vt_ref, og_ref, g_ref, init_hbm, o_ref, m_s, l_s, acc_s):
    s_idx = pl.program_id(2)
    qi = qi_tab[s_idx]
    ki = ki_tab[s_idx]
    tq = qa_ref.shape[0]
    tk = ka_ref.shape[0]

    @pl.when(ki == 0)
    def _():
        m_s[...] = jnp.full_like(m_s, -jnp.inf)
        l_s[...] = jnp.zeros_like(l_s)
        acc_s[...] = jnp.zeros_like(acc_s)

    def update(masked):
        for hh in range(2):
            st = _dot_nt(ka_ref[:, hh * AUG:(hh + 1) * AUG], qa_ref[:, hh * AUG:(hh + 1) * AUG])
            if masked:
                r = lax.broadcasted_iota(jnp.int32, (tk, tq), 0)
                c = lax.broadcasted_iota(jnp.int32, (tk, tq), 1)
                st = jnp.where(r <= c, st, -jnp.inf)
            m_prev = m_s[hh]
            m_new = jnp.maximum(m_prev, jnp.max(st, axis=0, keepdims=True))
            alpha = jnp.exp(m_prev - m_new)
            p = jnp.exp(st - m_new)
            l_s[hh] = alpha * l_s[hh] + jnp.sum(p, axis=0, keepdims=True)
            acc_s[hh] = alpha * acc_s[hh] + _dot(vt_ref[hh * HEAD_DIM:(hh + 1) * HEAD_DIM, :], p.astype(BF16))
            m_s[hh] = m_new

    @pl.when(ki < qi)
    def _():
        update(False)

    @pl.when(ki == qi)
    def _():
        update(True)
        outs = []
        for hh in range(2):
            sl = slice(hh * HEAD_DIM, (hh + 1) * HEAD_DIM)
            ot = acc_s[hh] / l_s[hh]
            ont = ot * lax.rsqrt(jnp.mean(ot * ot, axis=0, keepdims=True) + RMS_EPS)
            outs.append(ont.T * g_ref[:, sl] * jax.nn.sigmoid(og_ref[:, sl]))
        o_ref[...] = jnp.concatenate(outs, axis=1)


def _flash_prompt(qa, ka, vt, z, g_fox_o, batch, seq):
    tq = _tile(seq, 512)
    nq = seq // tq
    pairs = [(a, b) for a in range(nq) for b in range(a + 1)]
    qi_tab = jnp.array([a for a, _ in pairs], jnp.int32)
    ki_tab = jnp.array([b for _, b in pairs], jnp.int32)
    npairs = WIDTH // LANES
    qmap = lambda b, p, s, qt, kt: (b * nq + qt[s], p)
    kmap = lambda b, p, s, qt, kt: (b * nq + kt[s], p)
    gs = pltpu.PrefetchScalarGridSpec(
        num_scalar_prefetch=2,
        grid=(batch, npairs, len(pairs)),
        in_specs=[pl.BlockSpec((tq, 2 * AUG), qmap),
                  pl.BlockSpec((tq, 2 * AUG), kmap),
                  pl.BlockSpec((LANES, tq), lambda b, p, s, qt, kt: (p, b * nq + kt[s])),
                  pl.BlockSpec((tq, LANES), lambda b, p, s, qt, kt: (b * nq + qt[s], C_OG // LANES + p)),
                  pl.BlockSpec((1, LANES), lambda b, p, s, qt, kt: (0, p)),
                  pl.BlockSpec(memory_space=pl.ANY)],
        out_specs=pl.BlockSpec((tq, LANES), qmap),
        scratch_shapes=[pltpu.VMEM((2, 1, tq), F32), pltpu.VMEM((2, 1, tq), F32), pltpu.VMEM((2, HEAD_DIM, tq), F32)],
    )
    return pl.pallas_call(
        _flash_kernel,
        grid_spec=gs,
        out_shape=jax.ShapeDtypeStruct((qa.shape[0], WIDTH), F32),
        input_output_aliases={7: 0},
        compiler_params=_cp(("arbitrary", "arbitrary", "arbitrary")),
        name="flash_prompt",
    )(qi_tab, ki_tab, qa, ka, vt, z, g_fox_o, jnp.zeros((qa.shape[0], WIDTH), F32))


PAGE_GROUP = 4
PAGE_BUFS = 8


def _paged_kernel(pt_ref, q_ref, kn_ref, vn_ref, lfn_ref, og_ref, g_ref, ck_hbm, cv_hbm, lf_hbm, of_hbm, o_ref,
                  kbuf, vbuf, lbuf, sem, *, n_pages):
    b = pl.program_id(0)
    total = pl.num_programs(0) * n_pages
    depth = PAGE_BUFS - PAGE_GROUP
    nh = WIDTH // HEAD_DIM
    rows = nh * DEC_SEQ

    def copies(page, slot):
        return (pltpu.make_async_copy(ck_hbm.at[page], kbuf.at[slot], sem.at[0, slot]),
                pltpu.make_async_copy(cv_hbm.at[page], vbuf.at[slot], sem.at[1, slot]),
                pltpu.make_async_copy(lf_hbm.at[page], lbuf.at[slot], sem.at[2, slot]))

    def fetch(g):
        bb = g // n_pages
        for c in copies(pt_ref[bb, g - bb * n_pages], g % PAGE_BUFS):
            c.start()

    @pl.when(b == 0)
    def _():
        for g in range(depth):
            fetch(g)

    q = (q_ref[...] * (HEAD_DIM ** -0.5)).astype(BF16)
    qh = [q[:, h * HEAD_DIM:(h + 1) * HEAD_DIM] for h in range(nh)]
    def cumulate(lf, carry):
        n = lf.shape[1]
        r = lax.broadcasted_iota(jnp.int32, (n, n), 0)
        c = lax.broadcasted_iota(jnp.int32, (n, n), 1)
        upper = jnp.where(r <= c, 1.0, 0.0).astype(BF16)
        hi, mid, lo = _split3(lf)
        cum = _dot(hi, upper) + _dot(mid, upper) + _dot(lo, upper) + carry
        return cum, cum[:, n - 1:n]

    def attend(pages, feature_major, ck, valid, m, l, acc):
        qk = _dot if feature_major else _dot_nt
        pv_dot = _dot_nt if feature_major else _dot
        s = jnp.concatenate([jnp.concatenate([qk(qh[h], kh[h]) for kh, _ in pages], axis=1) for h in range(nh)], axis=0)
        s = s - jnp.concatenate([jnp.broadcast_to(ck[h:h + 1, :], (DEC_SEQ, ck.shape[1])) for h in range(nh)], axis=0)
        if valid is not None:
            s = jnp.where(valid, s, -jnp.inf)
        m_new = jnp.maximum(m, jnp.max(s, axis=-1, keepdims=True))
        alpha = jnp.exp(m - m_new)
        p = jnp.exp(s - m_new)
        l = alpha * l + jnp.sum(p, axis=-1, keepdims=True)
        pb = p.astype(BF16)
        pv = []
        for h in range(nh):
            rows_h = slice(h * DEC_SEQ, (h + 1) * DEC_SEQ)
            pv.append(sum(pv_dot(pb[rows_h, i * PAGE:(i + 1) * PAGE], vh[h]) for i, (_, vh) in enumerate(pages)))
        return m_new, l, alpha * acc + jnp.concatenate(pv, axis=0)

    def pair_step(jj, carry):
        m, l, acc, ccar = carry
        g0 = b * n_pages + PAGE_GROUP * jj
        slots = [(g0 + u) % PAGE_BUFS for u in range(PAGE_GROUP)]
        for slot in slots:
            for cp in copies(0, slot):
                cp.wait()
        for u in range(PAGE_GROUP):
            @pl.when(g0 + depth + u < total)
            def _():
                fetch(g0 + depth + u)

        ck, ccar = cumulate(jnp.concatenate([lbuf[slot] for slot in slots], axis=1), ccar)
        pages = [([kbuf[slot, h].astype(BF16) for h in range(nh)],
                  [vbuf[slot, h].astype(BF16) for h in range(nh)]) for slot in slots]
        m, l, acc = attend(pages, True, ck, None, m, l, acc)
        return m, l, acc, ccar

    init = (jnp.full((rows, 1), -jnp.inf, F32), jnp.zeros((rows, 1), F32), jnp.zeros((rows, HEAD_DIM), F32),
            jnp.zeros((nh, 1), F32))
    m, l, acc, ccar = lax.fori_loop(0, n_pages // PAGE_GROUP, pair_step, init)

    pad = jnp.zeros((PAGE - DEC_SEQ, HEAD_DIM), BF16)
    kn = kn_ref[...].astype(BF16)
    vn = vn_ref[...].astype(BF16)
    k_heads = [jnp.concatenate([kn[:, h * HEAD_DIM:(h + 1) * HEAD_DIM], pad], axis=0) for h in range(nh)]
    v_heads = [jnp.concatenate([vn[:, h * HEAD_DIM:(h + 1) * HEAD_DIM], pad], axis=0) for h in range(nh)]
    ck, _ = cumulate(lfn_ref[0], ccar)
    rr = lax.broadcasted_iota(jnp.int32, (rows, PAGE), 0)
    cc = lax.broadcasted_iota(jnp.int32, (rows, PAGE), 1)
    m, l, acc = attend([(k_heads, v_heads)], False, ck, cc <= (rr & (DEC_SEQ - 1)), m, l, acc)
    o = acc / l
    on = o * lax.rsqrt(jnp.mean(o * o, axis=-1, keepdims=True) + RMS_EPS)
    out = jnp.concatenate([on[h * DEC_SEQ:(h + 1) * DEC_SEQ, :] for h in range(nh)], axis=1)
    o_ref[...] = out * g_ref[...] * jax.nn.sigmoid(og_ref[...])


def _paged_sample(page_table, qn, kn, v, lf_new_t, cache_k, cache_v, lf_pages_t, z, g_fox_o, o_f, row0):
    db, n_pages = page_table.shape
    assert n_pages % PAGE_GROUP == 0 and n_pages >= PAGE_BUFS
    nh = WIDTH // HEAD_DIM
    rb = row0 // DEC_SEQ
    newmap = lambda b, pt: (b, 0)
    hbm = pl.BlockSpec(memory_space=pl.ANY)
    gs = pltpu.PrefetchScalarGridSpec(
        num_scalar_prefetch=1,
        grid=(db,),
        in_specs=[pl.BlockSpec((DEC_SEQ, WIDTH), newmap),
                  pl.BlockSpec((DEC_SEQ, WIDTH), newmap),
                  pl.BlockSpec((DEC_SEQ, WIDTH), newmap),
                  pl.BlockSpec((1, nh, PAGE), lambda b, pt: (b, 0, 0)),
                  pl.BlockSpec((DEC_SEQ, WIDTH), lambda b, pt: (rb + b, C_OG // WIDTH)),
                  pl.BlockSpec((1, WIDTH), lambda b, pt: (0, 0)),
                  hbm, hbm, hbm, hbm],
        out_specs=pl.BlockSpec((DEC_SEQ, WIDTH), lambda b, pt: (rb + b, 0)),
        scratch_shapes=[pltpu.VMEM((PAGE_BUFS, nh, HEAD_DIM, PAGE), F32), pltpu.VMEM((PAGE_BUFS, nh, HEAD_DIM, PAGE), F32),
                        pltpu.VMEM((PAGE_BUFS, nh, PAGE), F32), pltpu.SemaphoreType.DMA((3, PAGE_BUFS))],
    )
    return pl.pallas_call(
        functools.partial(_paged_kernel, n_pages=n_pages),
        grid_spec=gs,
        out_shape=jax.ShapeDtypeStruct(o_f.shape, F32),
        input_output_aliases={10: 0},
        compiler_params=_cp(("arbitrary",)),
        name="paged_sample",
    )(page_table, qn, kn, v, lf_new_t, z, g_fox_o, cache_k, cache_v, lf_pages_t, o_f)


def _prep_kernel(z_ref, first_ref, mu_ref, w12_ref, w3_ref, w0_ref, a0_ref, kk_ref, ka_ref, rk_ref,
                 g_hbm, bonus_hbm, o_r, o_w, o_k, o_v, o_a, o_b, o_g, o_bonus, carry, *, seq_len, tiles_per_seq):
    z = z_ref[...]
    tm = z.shape[0]
    row = lax.broadcasted_iota(jnp.int32, (tm, 1), 0)
    rolled = pltpu.roll(z, 1, 0)
    if tiles_per_seq >= 1 and seq_len >= tm:
        li = pl.program_id(0) % tiles_per_seq

        @pl.when(li == 0)
        def _():
            carry[...] = first_ref[0]

        zp = jnp.where(row == 0, carry[...], rolled)
        carry[...] = z[tm - 1:tm, :]
    else:
        nseq = tm // seq_len
        first = first_ref[...]
        exp = jnp.broadcast_to(first, (nseq, seq_len, first.shape[-1])).reshape(tm, first.shape[-1])
        zp = jnp.where((row & (seq_len - 1)) == 0, exp, rolled)
    zm = z + (zp - z) * mu_ref[...]
    r = zm[:, 0:WIDTH]
    k = zm[:, WIDTH:2 * WIDTH]
    v = zm[:, 2 * WIDTH:3 * WIDTH]
    lo = zm[:, C_LORA:C_LORA + 384]
    lane = lax.broadcasted_iota(jnp.int32, (tm, 384), 1)
    act = jnp.where(lane < 64, jnp.tanh(lo), jnp.where(lane < 128, lo, jax.nn.sigmoid(lo))).astype(BF16)
    l12 = _dot(act[:, 0:128], w12_ref[...])
    g = _dot(act[:, 128:384], w3_ref[...])
    log_w = -_softplus(-(w0_ref[...] + l12[:, 0:WIDTH])) - 0.5
    decay = jnp.exp(-jnp.exp(log_w))
    asig = jax.nn.sigmoid(a0_ref[...] + l12[:, WIDTH:2 * WIDTH])
    bd = _block_diag_ones()
    kk = k * kk_ref[...]
    kk = kk / jnp.maximum(jnp.sqrt(_segsum(kk * kk, bd)), 1e-12)
    kf = k * (1.0 + (asig - 1.0) * ka_ref[...])
    o_r[...] = r
    o_w[...] = decay
    o_k[...] = kf
    o_v[...] = v
    o_a[...] = -kk
    o_b[...] = kk * asig
    o_g[...] = g
    o_bonus[...] = _segsum(r * kf * rk_ref[...], bd) * v


def _rwkv_prep(z, first, mu, w12, w3, w0, a0, k_k, k_a, r_k, row0, n_rows, seq_len, shared):
    if seq_len >= 256:
        tm = _tile(seq_len, 256)
        first_spec = pl.BlockSpec((1, 1, RWKV_COLS), lambda i: (i // (seq_len // tm), 0, 0))
    else:
        tm = _tile(n_rows, 128)
        nseq = tm // seq_len
        first_spec = pl.BlockSpec((nseq, 1, RWKV_COLS), lambda i: (i, 0, 0))
    rb = row0 // tm
    vec = lambda n: pl.BlockSpec((1, n), lambda i: (0, 0))
    out = pl.BlockSpec((tm, WIDTH), lambda i: (i, 0))
    out_all = pl.BlockSpec((tm, WIDTH), lambda i: (rb + i, 0))
    hbm = pl.BlockSpec(memory_space=pl.ANY)
    extra = tuple(shared)
    return pl.pallas_call(
        functools.partial(_prep_kernel, seq_len=seq_len, tiles_per_seq=max(seq_len // tm, 1)),
        grid=(n_rows // tm,),
        in_specs=[pl.BlockSpec((tm, RWKV_COLS), lambda i: (rb + i, 0)), first_spec, vec(RWKV_COLS),
                  pl.BlockSpec((128, 2 * WIDTH), lambda i: (0, 0)), pl.BlockSpec((256, WIDTH), lambda i: (0, 0)),
                  vec(WIDTH), vec(WIDTH), vec(WIDTH), vec(WIDTH), vec(WIDTH)] + [hbm] * len(extra),
        out_specs=[out] * 6 + [out_all] * 2,
        out_shape=[jax.ShapeDtypeStruct((n_rows, WIDTH), F32)] * 6 + [jax.ShapeDtypeStruct((z.shape[0], WIDTH), F32)] * 2,
        input_output_aliases={10: 6, 11: 7},
        scratch_shapes=[pltpu.VMEM((1, RWKV_COLS), F32)],
        compiler_params=_cp(("arbitrary",)),
        name=f"rwkv_prep_{seq_len}",
    )(z, first, mu, w12, w3, w0, a0, k_k, k_a, r_k, *extra)


SCAN_SUB = 64
SCAN_UNROLL = 4


def _wkv_kernel(r_ref, w_ref, k_ref, v_ref, a_ref, b_ref, s0_ref, y_ref, sT_ref, S_s, Z_s, *, groups, n_pairs, tb_len):
    tb = pl.program_id(1)
    r2 = lax.broadcasted_iota(jnp.int32, (2 * LANES, 2 * LANES), 0)
    c2 = lax.broadcasted_iota(jnp.int32, (2 * LANES, 2 * LANES), 1)
    ones_bd = jnp.where((r2 >> 6) == (c2 >> 6), 1.0, 0.0).astype(BF16)
    vrow = lax.broadcasted_iota(jnp.int32, (HEAD_DIM, LANES), 0)
    lane = lax.broadcasted_iota(jnp.int32, (HEAD_DIM, LANES), 1)
    eye2 = (lane & (HEAD_DIM - 1)) == vrow
    zero_half = jnp.zeros((HEAD_DIM, LANES), BF16)
    H = HEAD_DIM

    @pl.when(tb == 0)
    def _():
        for g in range(groups):
            for p in range(n_pairs):
                S_s[g * n_pairs + p] = jnp.concatenate([s0_ref[g, 2 * p], s0_ref[g, 2 * p + 1]], axis=1)
        Z_s[...] = jnp.zeros_like(Z_s)

    def lanes(p):
        return slice(p * LANES, (p + 1) * LANES)

    def run(t0, n):
        def step(i, carry):
            t = t0 + i
            tp = jnp.maximum(t - 1, 0)
            hit = (lane & (SCAN_SUB - 1)) == (i - 1)
            for g in range(groups):
                aa = a_ref[g, pl.ds(t, 1), :]
                ww = w_ref[g, pl.ds(t, 1), :]
                kr = k_ref[g, pl.ds(t, 1), :]
                vv = v_ref[g, pl.ds(t, 1), :]
                bb = b_ref[g, pl.ds(t, 1), :]
                rp = r_ref[g, pl.ds(tp, 1), :]
                lhs_s, xs = [], []
                for p in range(n_pairs):
                    S = S_s[g * n_pairs + p]
                    lhs_s.append(jnp.concatenate([(S * aa[:, lanes(p)]).astype(BF16),
                                                  (S * rp[:, lanes(p)]).astype(BF16)], axis=1))
                    xs.append(jnp.where(eye2, vv[:, lanes(p)], 0.0).astype(BF16))
                lhs_x = [jnp.concatenate(xs[p:p + 2], axis=1) for p in range(0, n_pairs, 2)]
                res = _dot(jnp.concatenate(lhs_s + lhs_x, axis=0), ones_bd)
                for p in range(n_pairs):
                    idx = g * n_pairs + p
                    rs = res[p * H:(p + 1) * H]
                    vb = res[(n_pairs + p // 2) * H:(n_pairs + p // 2 + 1) * H, (p % 2) * LANES:(p % 2 + 1) * LANES]
                    S_s[idx] = S_s[idx] * ww[:, lanes(p)] + rs[:, :LANES] * bb[:, lanes(p)] + vb * kr[:, lanes(p)]
                    Z_s[idx] = jnp.where(hit, rs[:, LANES:], Z_s[idx])
            return carry

        lax.fori_loop(0, n, step, 0, unroll=SCAN_UNROLL)
        hit_last = (lane & (SCAN_SUB - 1)) == (n - 1)
        for g in range(groups):
            rl = r_ref[g, pl.ds(t0 + n - 1, 1), :]
            lhs = [jnp.concatenate([(S_s[g * n_pairs + p] * rl[:, lanes(p)]).astype(BF16), zero_half], axis=1)
                   for p in range(n_pairs)]
            res = _dot(jnp.concatenate(lhs, axis=0), ones_bd)
            for p in range(n_pairs):
                idx = g * n_pairs + p
                zt = jnp.where(hit_last, res[p * H:(p + 1) * H, :LANES], Z_s[idx]).T
                y_ref[g, pl.ds(t0, n), p * LANES:p * LANES + H] = zt[0:n, :]
                y_ref[g, pl.ds(t0, n), p * LANES + H:(p + 1) * LANES] = zt[SCAN_SUB:SCAN_SUB + n, :]

    if tb_len <= SCAN_SUB:
        run(0, tb_len)
    else:
        def outer(blk, carry):
            run(pl.multiple_of(blk * SCAN_SUB, SCAN_SUB), SCAN_SUB)
            return carry
        lax.fori_loop(0, tb_len // SCAN_SUB, outer, 0)

    @pl.when(tb == pl.num_programs(1) - 1)
    def _():
        for g in range(groups):
            for p in range(n_pairs):
                S = S_s[g * n_pairs + p]
                sT_ref[g, 2 * p] = S[:, 0:H]
                sT_ref[g, 2 * p + 1] = S[:, H:LANES]


def _wkv_scan(r, w, k, v, a, b, s0, n_seq, seq_len):
    nh = WIDTH // HEAD_DIM
    groups = 2 if n_seq % 2 == 0 else 1
    tb_len = min(seq_len, 128)
    nt = seq_len // tb_len
    blk = pl.BlockSpec((groups, tb_len, WIDTH), lambda s, t: (s, t, 0))
    st = pl.BlockSpec((groups, nh, HEAD_DIM, HEAD_DIM), lambda s, t: (s, 0, 0, 0))
    n_pairs = WIDTH // LANES
    seq3 = lambda x: x.reshape(n_seq, seq_len, WIDTH)
    y, s_new = pl.pallas_call(
        functools.partial(_wkv_kernel, groups=groups, n_pairs=n_pairs, tb_len=tb_len),
        grid=(n_seq // groups, nt),
        in_specs=[blk] * 6 + [st],
        out_specs=[blk, st],
        out_shape=[jax.ShapeDtypeStruct((n_seq, seq_len, WIDTH), F32),
                   jax.ShapeDtypeStruct((n_seq, nh, HEAD_DIM, HEAD_DIM), F32)],
        scratch_shapes=[pltpu.VMEM((groups * n_pairs, HEAD_DIM, LANES), F32),
                        pltpu.VMEM((groups * n_pairs, HEAD_DIM, LANES), F32)],
        compiler_params=_cp(("arbitrary", "arbitrary")),
        name=f"wkv_scan_{seq_len}",
    )(seq3(r), seq3(w), seq3(k), seq3(v), seq3(a), seq3(b), s0)
    return y.reshape(n_seq * seq_len, WIDTH), s_new


def _outproj_kernel(x_ref, y_ref, bonus_ref, g_ref, of_ref, lw_ref, lb_ref, wo_ref, gf_ref, wrh_ref, wrl_ref, br_ref,
                    x1_ref, h_ref, route_ref):
    bd = _block_diag_ones()
    y = y_ref[...]
    mu = _segsum(y, bd) * (1.0 / HEAD_DIM)
    d = y - mu
    var = _segsum(d * d, bd) * (1.0 / HEAD_DIM)
    yn = d * lax.rsqrt(var + LNX_EPS) * lw_ref[...] + lb_ref[...]
    o_r = ((yn + bonus_ref[...]) * g_ref[...]).astype(BF16)
    o_f = of_ref[...].astype(BF16)
    x1 = x_ref[...] + _dot(o_r, wo_ref[0:WIDTH, :]) + _dot(o_f, wo_ref[WIDTH:2 * WIDTH, :])
    x1_ref[...] = x1
    h = _rms_rows(x1, gf_ref[...])
    h_ref[...] = h
    hi, lo = _split2(h)
    logits = _dot(hi, wrh_ref[...]) + _dot(lo, wrh_ref[...]) + _dot(hi, wrl_ref[...]) + br_ref[...]
    tm = logits.shape[0]
    lane = lax.broadcasted_iota(jnp.int32, (tm, LANES), 1)
    neg = -jnp.inf
    lg = jnp.where(lane < N_GROUPS, logits, neg)
    mg = jnp.max(lg, axis=-1, keepdims=True)
    pg_top = 1.0 / jnp.sum(jnp.exp(lg - mg), axis=-1, keepdims=True)
    g_sel = jnp.min(jnp.where(lg == mg, lane, LANES), axis=-1, keepdims=True)
    in_grp = (lane >= N_GROUPS) & (lane < N_GROUPS + N_EXPERTS) & (((lane - N_GROUPS) >> 3) == g_sel)
    le = jnp.where(in_grp, logits, neg)
    m1 = jnp.max(le, axis=-1, keepdims=True)
    i1 = jnp.min(jnp.where(le == m1, lane, LANES), axis=-1, keepdims=True)
    le2 = jnp.where(lane == i1, neg, le)
    m2 = jnp.max(le2, axis=-1, keepdims=True)
    i2 = jnp.min(jnp.where(le2 == m2, lane, LANES), axis=-1, keepdims=True)
    e2 = jnp.exp(m2 - m1)
    gate1 = pg_top / (1.0 + e2)
    gate2 = pg_top * e2 / (1.0 + e2)
    route = jnp.where(lane == 0, (i1 - N_GROUPS).astype(F32),
                      jnp.where(lane == 1, (i2 - N_GROUPS).astype(F32),
                                jnp.where(lane == 2, gate1, jnp.where(lane == 3, gate2, 0.0))))
    route_ref[...] = route


def _outproj(x, y, bonus, g, o_f, lnx_w, lnx_b, w_out, g_ffn, wr_hi, wr_lo, b_r):
    m, d = x.shape
    tm = _tile(m, 256)
    row = lambda n: pl.BlockSpec((tm, n), lambda i: (i, 0))
    vec = lambda n: pl.BlockSpec((1, n), lambda i: (0, 0))
    full = lambda a: pl.BlockSpec(a.shape, lambda i: (0, 0))
    return pl.pallas_call(
        _outproj_kernel,
        grid=(m // tm,),
        in_specs=[row(d), row(WIDTH), row(WIDTH), row(WIDTH), row(WIDTH), vec(WIDTH), vec(WIDTH),
                  full(w_out), vec(d), full(wr_hi), full(wr_lo), vec(LANES)],
        out_specs=[row(d), row(d), row(LANES)],
        out_shape=[jax.ShapeDtypeStruct((m, d), F32), jax.ShapeDtypeStruct((m, d), F32),
                   jax.ShapeDtypeStruct((m, LANES), F32)],
        compiler_params=_cp(("arbitrary",)),
        name="outproj_router",
    )(x, y, bonus, g, o_f, lnx_w, lnx_b, w_out, g_ffn, wr_hi, wr_lo, b_r)


def _expert_kernel(be_ref, tok_ref, nu_ref, h_hbm, wg_ref, wu_ref, wd_ref, o_ref, xbuf, sem):
    i = pl.program_id(0)
    n_used = nu_ref[0]

    def row_copy(tok, r, slot):
        return pltpu.make_async_copy(h_hbm.at[pl.ds(tok, 1), :], xbuf.at[slot, pl.ds(r, 1), :], sem.at[slot])

    def gather(blk, slot):
        def start(r, c):
            row_copy(tok_ref[blk * MOE_BLOCK + r], r, slot).start()
            return c
        lax.fori_loop(0, MOE_BLOCK, start, 0, unroll=8)

    @pl.when(i == 0)
    def _():
        gather(0, 0)

    @pl.when(i < n_used)
    def _():
        slot = i & 1

        def wait(r, c):
            row_copy(0, r, slot).wait()
            return c
        lax.fori_loop(0, MOE_BLOCK, wait, 0, unroll=8)

        @pl.when(i + 1 < n_used)
        def _():
            gather(i + 1, 1 - slot)

        x = xbuf[slot].astype(BF16)
        gate = _dot(x, wg_ref[0].astype(BF16))
        up = _dot(x, wu_ref[0].astype(BF16))
        act = (gate * jax.nn.sigmoid(gate) * up).astype(BF16)
        o_ref[...] = _dot(act, wd_ref[0].astype(BF16))

    @pl.when(i >= n_used)
    def _():
        o_ref[...] = jnp.zeros_like(o_ref)


def _experts(block_e, slot_tok, n_used, h, w_gate, w_up, w_down):
    n_blocks = block_e.shape[0]
    d = h.shape[1]
    de = w_gate.shape[2]
    gs = pltpu.PrefetchScalarGridSpec(
        num_scalar_prefetch=3,
        grid=(n_blocks,),
        in_specs=[pl.BlockSpec(memory_space=pl.ANY),
                  pl.BlockSpec((1, d, de), lambda i, be, tok, nu: (be[i], 0, 0)),
                  pl.BlockSpec((1, d, de), lambda i, be, tok, nu: (be[i], 0, 0)),
                  pl.BlockSpec((1, de, d), lambda i, be, tok, nu: (be[i], 0, 0))],
        out_specs=pl.BlockSpec((MOE_BLOCK, d), lambda i, be, tok, nu: (i, 0)),
        scratch_shapes=[pltpu.VMEM((2, MOE_BLOCK, d), F32), pltpu.SemaphoreType.DMA((2,))],
    )
    return pl.pallas_call(
        _expert_kernel,
        grid_spec=gs,
        out_shape=jax.ShapeDtypeStruct((n_blocks * MOE_BLOCK, d), F32),
        compiler_params=_cp(("arbitrary",)),
        name="experts",
    )(block_e, slot_tok, n_used, h, w_gate, w_up, w_down)


def _ple_kernel(dest_ref, x1_ref, route_ref, p_ref, yb_hbm, gp_ref, wpg_ref, bpg_ref, wpp_ref, gfin_ref, op_ref, os_ref,
                buf, sem, *, prompt_tiles):
    i = pl.program_id(0)
    tm = x1_ref.shape[0]

    def row_copy(slot_row, r, k, half):
        return pltpu.make_async_copy(yb_hbm.at[pl.ds(slot_row, 1), :], buf.at[half, k, pl.ds(r, 1), :], sem.at[half])

    def gather(tile, half):
        def start(r, c):
            for k in range(2):
                row_copy(dest_ref[(tile * tm + r) * 2 + k], r, k, half).start()
            return c
        lax.fori_loop(0, tm, start, 0, unroll=8)

    @pl.when(i == 0)
    def _():
        gather(0, 0)

    half = i & 1

    def wait(r, c):
        for k in range(2):
            row_copy(0, r, k, half).wait()
        return c
    lax.fori_loop(0, tm, wait, 0, unroll=8)

    @pl.when(i + 1 < pl.num_programs(0))
    def _():
        gather(i + 1, 1 - half)

    route = route_ref[...]
    moe = buf[half, 0] * route[:, 2:3] + buf[half, 1] * route[:, 3:4]
    x2 = x1_ref[...] + moe
    hn = _rms_rows(x2, gp_ref[...]).astype(BF16)
    gate = jax.nn.sigmoid(_dot(hn, wpg_ref[...]) + bpg_ref[...])
    pe = _dot(p_ref[...].astype(BF16), wpp_ref[...])
    x3 = x2 + gate * pe
    y = _rms_rows(x3, gfin_ref[...])

    @pl.when(i < prompt_tiles)
    def _():
        op_ref[...] = y

    @pl.when(i >= prompt_tiles)
    def _():
        os_ref[...] = y


def _ple(dest, x1, route, p, yb, g_ple, w_pg, b_pg, w_pp, g_final, n_p):
    m, d = x1.shape
    tm = _tile(math.gcd(n_p, m - n_p), 256)
    npt = n_p // tm
    pd = p.shape[1]
    gs = pltpu.PrefetchScalarGridSpec(
        num_scalar_prefetch=1,
        grid=(m // tm,),
        in_specs=[pl.BlockSpec((tm, d), lambda i, ds: (i, 0)),
                  pl.BlockSpec((tm, LANES), lambda i, ds: (i, 0)),
                  pl.BlockSpec((tm, pd), lambda i, ds: (i, 0)),
                  pl.BlockSpec(memory_space=pl.ANY),
                  pl.BlockSpec((1, d), lambda i, ds: (0, 0)),
                  pl.BlockSpec((d, d), lambda i, ds: (0, 0)),
                  pl.BlockSpec((1, d), lambda i, ds: (0, 0)),
                  pl.BlockSpec((pd, d), lambda i, ds: (0, 0)),
                  pl.BlockSpec((1, d), lambda i, ds: (0, 0))],
        out_specs=[pl.BlockSpec((tm, d), lambda i, ds: (jnp.minimum(i, npt - 1), 0)),
                   pl.BlockSpec((tm, d), lambda i, ds: (jnp.maximum(i - npt, 0), 0))],
        scratch_shapes=[pltpu.VMEM((2, 2, tm, d), F32), pltpu.SemaphoreType.DMA((2,))],
    )
    return pl.pallas_call(
        functools.partial(_ple_kernel, prompt_tiles=npt),
        grid_spec=gs,
        out_shape=[jax.ShapeDtypeStruct((n_p, d), F32), jax.ShapeDtypeStruct((m - n_p, d), F32)],
        compiler_params=_cp(("arbitrary",)),
        name="moe_combine_ple",
    )(dest, x1, route, p, yb, g_ple, w_pg, b_pg, w_pp, g_final)


def _dispatch(route, n_tok):
    expert = route[:, 0:2].astype(jnp.int32)
    flat_e = expert.reshape(-1)
    n_assign = flat_e.shape[0]
    onehot = (flat_e[:, None] == jnp.arange(N_EXPERTS, dtype=jnp.int32)[None, :]).astype(jnp.int32)
    csum = jnp.cumsum(onehot, axis=0)
    rank = jnp.sum(csum * onehot, axis=1) - 1
    counts = csum[-1]
    pcounts = (counts + MOE_BLOCK - 1) // MOE_BLOCK * MOE_BLOCK
    pend = jnp.cumsum(pcounts)
    pstart = pend - pcounts
    dest = (pstart[flat_e] + rank).astype(jnp.int32)
    n_blocks = -(-n_assign // MOE_BLOCK) + N_EXPERTS
    slot_tok = jnp.zeros((n_blocks * MOE_BLOCK,), jnp.int32).at[dest].set(jnp.arange(n_assign, dtype=jnp.int32) // 2)
    n_used = (pend[-1] // MOE_BLOCK).astype(jnp.int32)
    blk = jnp.arange(n_blocks, dtype=jnp.int32)
    block_e = jnp.searchsorted(pend, jnp.minimum(blk, n_used - 1) * MOE_BLOCK, side='right').astype(jnp.int32)
    block_e = jnp.minimum(block_e, N_EXPERTS - 1)
    return block_e, slot_tok, n_used.reshape(1), dest


def kernel(x_prompt, x_sample, cache_k, cache_v, cache_logf, state_wkv, state_shift, page_table, p_prompt, p_sample, g_attn, w_in, mu_shift, w0, w_up, a0, a_up, g_up, k_k, k_a, r_k, lnx_w, lnx_b, b_f, q_norm, k_norm, g_fox_o, w_out, g_ffn, w_rg, b_rg, w_re, b_re, w_e_gate, w_e_up, w_e_down, g_ple, w_pg, b_pg, w_pp, g_final):
    depth = g_attn.shape[0]
    assert depth == 1
    batch, seq, d = x_prompt.shape
    db, dec_seq, _ = x_sample.shape
    assert dec_seq == DEC_SEQ
    nh = WIDTH // HEAD_DIM
    n_p = batch * seq
    n_s = db * dec_seq
    m = n_p + n_s
    rwkv_in = mu_shift.shape[1]
    n_pool = cache_k.shape[1]
    n_pages = page_table.shape[1]
    row = lambda a: a.reshape(1, -1)

    x_all = jnp.concatenate([x_prompt.reshape(n_p, d), x_sample.reshape(n_s, d)], axis=0)
    p_all = jnp.concatenate([p_prompt[0].reshape(n_p, -1), p_sample[0].reshape(n_s, -1)], axis=0)

    wi = w_in[0]
    zc = lambda n: jnp.zeros((d, n), F32)
    w_z = jnp.concatenate([wi[:, :rwkv_in], zc(C_F - rwkv_in), wi[:, rwkv_in + 4 * WIDTH:], zc(C_Q - C_F - nh),
                           wi[:, rwkv_in:rwkv_in + 4 * WIDTH]], axis=1).astype(BF16)
    w12 = jnp.zeros((128, 2 * WIDTH), F32).at[0:64, 0:WIDTH].set(w_up[0]).at[64:128, WIDTH:].set(a_up[0]).astype(BF16)
    w3 = jnp.zeros((256, WIDTH), F32).at[0:g_up.shape[1]].set(g_up[0]).astype(BF16)
    w_r = jnp.zeros((d, LANES), F32).at[:, 0:N_GROUPS].set(w_rg[0]).at[:, N_GROUPS:N_GROUPS + N_EXPERTS].set(w_re[0])
    wr_hi = w_r.astype(BF16)
    wr_lo = (w_r - wr_hi.astype(F32)).astype(BF16)
    b_r = jnp.zeros((1, LANES), F32).at[0, 0:N_GROUPS].set(b_rg[0]).at[0, N_GROUPS:N_GROUPS + N_EXPERTS].set(b_re[0])
    pad_cols = lambda a, n: jnp.pad(a, ((0, 0), (0, n - a.shape[1])))
    mu_pad = pad_cols(mu_shift, RWKV_COLS)
    bf_pad = pad_cols(b_f, LANES)
    qn_t = jnp.tile(q_norm, (1, nh))
    kn_t = jnp.tile(k_norm, (1, nh))

    z = _inproj(x_all, g_attn, w_z)

    qn_s, kn_s, vv_s, logf, qa, ka, vt, kt_p, vt_p = _foxpost(z, qn_t, kn_t, bf_pad, batch, seq)

    o_f = _flash_prompt(qa, ka, vt, z, g_fox_o, batch, seq)
    lf_pages_t = cache_logf[0].transpose(0, 2, 1)
    lf_new_t = pad_cols(logf[n_p:, :nh].reshape(db, dec_seq, nh).transpose(0, 2, 1).reshape(db * nh, dec_seq), PAGE)
    o_f = _paged_sample(page_table, qn_s, kn_s, vv_s, lf_new_t.reshape(db, nh, PAGE),
                        cache_k[0].transpose(0, 2, 3, 1), cache_v[0].transpose(0, 2, 3, 1), lf_pages_t, z, g_fox_o, o_f, n_p)

    first_p = jnp.zeros((batch, 1, RWKV_COLS), F32)
    first_s = pad_cols(state_shift[0], RWKV_COLS).reshape(db, 1, RWKV_COLS)
    prep_args = (mu_pad, w12, w3, w0, a0, k_k, k_a, row(r_k))
    prep_p = _rwkv_prep(z, first_p, *prep_args, 0, n_p, seq, shared=(jnp.zeros((m, WIDTH), F32),) * 2)
    prep_s = _rwkv_prep(z, first_s, *prep_args, n_p, n_s, dec_seq, shared=prep_p[6:8])
    g_, bonus = prep_s[6:8]
    y_p, wkv_p = _wkv_scan(*prep_p[:6], jnp.zeros((batch, nh, HEAD_DIM, HEAD_DIM), F32), batch, seq)
    y_s, wkv_s = _wkv_scan(*prep_s[:6], state_wkv[0], db, dec_seq)
    y = jnp.concatenate([y_p, y_s], axis=0)

    x1, h2, route = _outproj(x_all, y, bonus, g_, o_f, lnx_w, lnx_b, w_out[0].astype(BF16), g_ffn, wr_hi, wr_lo, b_r)

    block_e, slot_tok, n_used, dest = _dispatch(route, m)
    yb = _experts(block_e, slot_tok, n_used, h2, w_e_gate[0], w_e_up[0], w_e_down[0])

    y_out_p, y_out_s = _ple(dest, x1, route, p_all, yb, g_ple, w_pg[0].astype(BF16), b_pg, w_pp[0].astype(BF16),
                            row(g_final), n_p)

    shift_p = jnp.concatenate([z[(b + 1) * seq - 1:(b + 1) * seq, :rwkv_in] for b in range(batch)], axis=0)
    shift_s = z[n_p + dec_seq - 1::dec_seq, :rwkv_in]
    heads_t = lambda a: a.reshape(batch, nh, HEAD_DIM, seq).transpose(0, 3, 1, 2)[None]
    heads = lambda a: a.reshape(1, db, dec_seq, nh, HEAD_DIM)
    return (y_out_p.reshape(batch, seq, d), y_out_s.reshape(db, dec_seq, d),
            heads_t(kt_p), heads_t(vt_p), logf[:n_p, :nh].reshape(1, batch, seq, nh),
            wkv_p[None], shift_p[None],
            heads(kn_s), heads(vv_s), logf[n_p:, :nh].reshape(1, db, dec_seq, nh),
            wkv_s[None], shift_s[None])
```

```python
import functools
import math

import jax
import jax.numpy as jnp
import numpy as np
from jax import lax
from jax.experimental import pallas as pl
from jax.experimental.pallas import tpu as pltpu

F32 = jnp.float32
BF16 = jnp.bfloat16

HEAD_DIM = 64
LANES = 128
RMS_EPS = 1e-6
LNX_EPS = 64e-5
PAGE = 128
MOE_BLOCK = 128
N_GROUPS = 8
N_EXPERTS = 64
DEC_SEQ = 8

C_LORA = 3072
C_F = 3456
RWKV_COLS = 3584
C_Q, C_FK, C_FV, C_OG = 4096, 5120, 6144, 7168
Z_COLS = 8192
WIDTH = 1024

VMEM_LIMIT = 52 * 1024 * 1024


def _cp(sem, vmem=VMEM_LIMIT):
    return pltpu.CompilerParams(dimension_semantics=sem, vmem_limit_bytes=vmem)


def _tile(n, pref):
    for t in (1024, 512, 256, 128, 64, 32, 16, 8):
        if t <= pref and n % t == 0:
            return t
    raise ValueError(f"no tile for {n}")


def _split2(x):
    hi = x.astype(BF16)
    lo = (x - hi.astype(F32)).astype(BF16)
    return hi, lo


def _split3(x):
    hi = x.astype(BF16)
    r1 = x - hi.astype(F32)
    mid = r1.astype(BF16)
    lo = (r1 - mid.astype(F32)).astype(BF16)
    return hi, mid, lo


def _dot(a, b):
    return jnp.dot(a, b, preferred_element_type=F32)


def _dot_nt(a, b):
    return lax.dot_general(a, b, (((1,), (1,)), ((), ())), preferred_element_type=F32)


def _block_diag_ones():
    r = lax.broadcasted_iota(jnp.int32, (LANES, LANES), 0)
    c = lax.broadcasted_iota(jnp.int32, (LANES, LANES), 1)
    return jnp.where((r >> 6) == (c >> 6), 1.0, 0.0).astype(BF16)


def _segsum(x, bd):
    outs = []
    for j in range(x.shape[1] // LANES):
        hi, lo = _split2(x[:, j * LANES:(j + 1) * LANES])
        outs.append(_dot(hi, bd) + _dot(lo, bd))
    return outs[0] if len(outs) == 1 else jnp.concatenate(outs, axis=1)


def _softplus(x):
    return jnp.maximum(x, 0.0) + jnp.log1p(jnp.exp(-jnp.abs(x)))


def _rms_rows(x, g):
    return x * lax.rsqrt(jnp.mean(x * x, axis=-1, keepdims=True) + RMS_EPS) * g


def _inproj_kernel(x_ref, g_ref, w_ref, o_ref, h_ref):
    @pl.when(pl.program_id(1) == 0)
    def _():
        h_ref[...] = _rms_rows(x_ref[...], g_ref[...]).astype(BF16)

    o_ref[...] = _dot(h_ref[...], w_ref[...])


def _inproj(x, g, w):
    m, d = x.shape
    n = w.shape[1]
    tm, tn = _tile(m, 1024), _tile(n, 512)
    return pl.pallas_call(
        _inproj_kernel,
        grid=(m // tm, n // tn),
        in_specs=[pl.BlockSpec((tm, d), lambda i, j: (i, 0)),
                  pl.BlockSpec((1, d), lambda i, j: (0, 0)),
                  pl.BlockSpec((d, tn), lambda i, j: (0, j))],
        out_specs=pl.BlockSpec((tm, tn), lambda i, j: (i, j)),
        out_shape=jax.ShapeDtypeStruct((m, n), F32),
        scratch_shapes=[pltpu.VMEM((tm, d), BF16)],
        compiler_params=_cp(("arbitrary", "arbitrary")),
        name="inproj",
    )(x, g, w)


AUG = 2 * HEAD_DIM


def _aug_constants():
    nh = WIDTH // HEAD_DIM
    pq = np.zeros((3 * LANES, WIDTH), np.float32)
    pk = np.zeros((3 * LANES, WIDTH), np.float32)
    one_q = np.zeros((1, WIDTH), np.float32)
    one_k = np.zeros((1, WIDTH), np.float32)
    for h in range(nh):
        for comp in range(3):
            pq[comp * LANES + h, h * HEAD_DIM + comp] = 1.0
            pk[comp * LANES + h, h * HEAD_DIM + 3 + comp] = -1.0
            one_q[0, h * HEAD_DIM + 3 + comp] = 1.0
            one_k[0, h * HEAD_DIM + comp] = 1.0
    return jnp.asarray(pq, BF16), jnp.asarray(pk, BF16), jnp.asarray(one_q), jnp.asarray(one_k)


def _foxpost_kernel(q_ref, k_ref, v_ref, f_ref, qn_ref, kn_ref, bf_ref, pq_ref, pk_ref, oneq_ref, onek_ref,
                    oq, ok, ov, olf, oqa, oka, ovt, okt32, ovt32, carry, *, tiles_per_seq, prompt_tiles):
    i = pl.program_id(0)
    bd = _block_diag_ones()
    nh = WIDTH // HEAD_DIM

    def head_norm(x, g):
        ms = _segsum(x * x, bd) * (1.0 / HEAD_DIM)
        return x * lax.rsqrt(ms + RMS_EPS) * g

    qn = head_norm(q_ref[...], qn_ref[...])
    kn = head_norm(k_ref[...], kn_ref[...])
    v = v_ref[...]
    vt = v.T
    ovt[...] = vt.astype(BF16)

    @pl.when(i < prompt_tiles)
    def _():
        okt32[0] = kn.T
        ovt32[0] = vt

    @pl.when(i >= prompt_tiles)
    def _():
        oq[...] = qn
        ok[...] = kn
        ov[...] = v

    lf = -_softplus(-(f_ref[...] + bf_ref[...]))
    olf[...] = lf

    @pl.when(i % tiles_per_seq == 0)
    def _():
        carry[...] = jnp.zeros_like(carry)

    tm = lf.shape[0]
    r = lax.broadcasted_iota(jnp.int32, (tm, tm), 0)
    c = lax.broadcasted_iota(jnp.int32, (tm, tm), 1)
    tri = jnp.where(c <= r, 1.0, 0.0).astype(BF16)
    hi, mid, lo = _split3(lf)
    cum = _dot(tri, hi) + _dot(tri, mid) + _dot(tri, lo) + carry[...]
    carry[...] = cum[tm - 1:tm, :]

    c3 = jnp.concatenate(_split3(cum), axis=1)
    aug_q = _dot(c3, pq_ref[...]) + oneq_ref[...]
    aug_k = _dot(c3, pk_ref[...]) + onek_ref[...]
    qs = qn * (HEAD_DIM ** -0.5)

    def interleave(x, aug):
        pieces = []
        for h in range(nh):
            sl = slice(h * HEAD_DIM, (h + 1) * HEAD_DIM)
            pieces += [x[:, sl], aug[:, sl]]
        return jnp.concatenate(pieces, axis=1).astype(BF16)

    oqa[...] = interleave(qs, aug_q)
    oka[...] = interleave(kn, aug_k)


def _foxpost(z, q_norm_t, k_norm_t, bf_pad, batch, seq_len):
    m = z.shape[0]
    nh = WIDTH // HEAD_DIM
    n_p = batch * seq_len
    tm = min(_tile(math.gcd(n_p, m - n_p), 512), seq_len)
    npt, tps = n_p // tm, seq_len // tm
    sample = pl.BlockSpec((tm, WIDTH), lambda i: (jnp.maximum(i - npt, 0), 0))
    prompt_t = pl.BlockSpec((1, WIDTH, tm), lambda i: (jnp.minimum(i, npt - 1) // tps, 0, jnp.minimum(i, npt - 1) % tps))
    row = lambda c: pl.BlockSpec((tm, WIDTH), lambda i, c=c: (i, c // WIDTH))
    vec = pl.BlockSpec((1, WIDTH), lambda i: (0, 0))
    small = pl.BlockSpec((tm, LANES), lambda i: (i, 0))
    wide = pl.BlockSpec((tm, nh * AUG), lambda i: (i, 0))
    place = pl.BlockSpec((3 * LANES, WIDTH), lambda i: (0, 0))
    return pl.pallas_call(
        functools.partial(_foxpost_kernel, tiles_per_seq=tps, prompt_tiles=npt),
        grid=(m // tm,),
        in_specs=[row(C_Q), row(C_FK), row(C_FV),
                  pl.BlockSpec((tm, LANES), lambda i: (i, C_F // LANES)),
                  vec, vec, pl.BlockSpec((1, LANES), lambda i: (0, 0)), place, place, vec, vec],
        out_specs=[sample] * 3 + [small, wide, wide, pl.BlockSpec((WIDTH, tm), lambda i: (0, i)), prompt_t, prompt_t],
        out_shape=[jax.ShapeDtypeStruct((m - n_p, WIDTH), F32)] * 3 + [jax.ShapeDtypeStruct((m, LANES), F32),
                                                                        jax.ShapeDtypeStruct((m, nh * AUG), BF16),
                                                                        jax.ShapeDtypeStruct((m, nh * AUG), BF16),
                                                                        jax.ShapeDtypeStruct((WIDTH, m), BF16),
                                                                        jax.ShapeDtypeStruct((batch, WIDTH, seq_len), F32),
                                                                        jax.ShapeDtypeStruct((batch, WIDTH, seq_len), F32)],
        scratch_shapes=[pltpu.VMEM((1, LANES), F32)],
        compiler_params=_cp(("arbitrary",)),
        name="foxpost",
    )(z, z, z, z, q_norm_t, k_norm_t, bf_pad, *_aug_constants())


def _flash_kernel(qi_tab, ki_tab, qa_ref, ka_ref, vt_ref, og_ref, g_ref, init_hbm, o_ref, m_s, l_s, acc_s):
    s_idx = pl.program_id(2)
    qi = qi_tab[s_idx]
    ki = ki_tab[s_idx]
    tq = qa_ref.shape[0]
    tk = ka_ref.shape[0]

    @pl.when(ki == 0)
    def _():
        m_s[...] = jnp.full_like(m_s, -jnp.inf)
        l_s[...] = jnp.zeros_like(l_s)
        acc_s[...] = jnp.zeros_like(acc_s)

    def update(masked):
        for hh in range(2):
            st = _dot_nt(ka_ref[:, hh * AUG:(hh + 1) * AUG], qa_ref[:, hh * AUG:(hh + 1) * AUG])
            if masked:
                r = lax.broadcasted_iota(jnp.int32, (tk, tq), 0)
                c = lax.broadcasted_iota(jnp.int32, (tk, tq), 1)
                st = jnp.where(r <= c, st, -jnp.inf)
            m_prev = m_s[hh]
            m_new = jnp.maximum(m_prev, jnp.max(st, axis=0, keepdims=True))
            alpha = jnp.exp(m_prev - m_new)
            p = jnp.exp(st - m_new)
            l_s[hh] = alpha * l_s[hh] + jnp.sum(p, axis=0, keepdims=True)
            acc_s[hh] = alpha * acc_s[hh] + _dot(vt_ref[hh * HEAD_DIM:(hh + 1) * HEAD_DIM, :], p.astype(BF16))
            m_s[hh] = m_new

    @pl.when(ki < qi)
    def _():
        update(False)

    @pl.when(ki == qi)
    def _():
        update(True)
        outs = []
        for hh in range(2):
            sl = slice(hh * HEAD_DIM, (hh + 1) * HEAD_DIM)
            ot = acc_s[hh] / l_s[hh]
            ont = ot * lax.rsqrt(jnp.mean(ot * ot, axis=0, keepdims=True) + RMS_EPS)
            outs.append(ont.T * g_ref[:, sl] * jax.nn.sigmoid(og_ref[:, sl]))
        o_ref[...] = jnp.concatenate(outs, axis=1)


def _flash_prompt(qa, ka, vt, z, g_fox_o, batch, seq):
    tq = _tile(seq, 512)
    nq = seq // tq
    pairs = [(a, b) for a in range(nq) for b in range(a + 1)]
    qi_tab = jnp.array([a for a, _ in pairs], jnp.int32)
    ki_tab = jnp.array([b for _, b in pairs], jnp.int32)
    npairs = WIDTH // LANES
    qmap = lambda b, p, s, qt, kt: (b * nq + qt[s], p)
    kmap = lambda b, p, s, qt, kt: (b * nq + kt[s], p)
    gs = pltpu.PrefetchScalarGridSpec(
        num_scalar_prefetch=2,
        grid=(batch, npairs, len(pairs)),
        in_specs=[pl.BlockSpec((tq, 2 * AUG), qmap),
                  pl.BlockSpec((tq, 2 * AUG), kmap),
                  pl.BlockSpec((LANES, tq), lambda b, p, s, qt, kt: (p, b * nq + kt[s])),
                  pl.BlockSpec((tq, LANES), lambda b, p, s, qt, kt: (b * nq + qt[s], C_OG // LANES + p)),
                  pl.BlockSpec((1, LANES), lambda b, p, s, qt, kt: (0, p)),
                  pl.BlockSpec(memory_space=pl.ANY)],
        out_specs=pl.BlockSpec((tq, LANES), qmap),
        scratch_shapes=[pltpu.VMEM((2, 1, tq), F32), pltpu.VMEM((2, 1, tq), F32), pltpu.VMEM((2, HEAD_DIM, tq), F32)],
    )
    return pl.pallas_call(
        _flash_kernel,
        grid_spec=gs,
        out_shape=jax.ShapeDtypeStruct((qa.shape[0], WIDTH), F32),
        input_output_aliases={7: 0},
        compiler_params=_cp(("arbitrary", "arbitrary", "arbitrary")),
        name="flash_prompt",
    )(qi_tab, ki_tab, qa, ka, vt, z, g_fox_o, jnp.zeros((qa.shape[0], WIDTH), F32))


PAGE_GROUP = 4
PAGE_BUFS = 12


def _paged_kernel(pt_ref, q_ref, kn_ref, vn_ref, lfn_ref, og_ref, g_ref, ck_hbm, cv_hbm, lf_hbm, of_hbm, o_ref,
                  kbuf, vbuf, lbuf, sem, *, n_pages):
    b = pl.program_id(0)
    total = pl.num_programs(0) * n_pages
    depth = PAGE_BUFS - PAGE_GROUP
    nh = WIDTH // HEAD_DIM
    rows = nh * DEC_SEQ

    def copies(page, slot):
        return (pltpu.make_async_copy(ck_hbm.at[page], kbuf.at[slot], sem.at[0, slot]),
                pltpu.make_async_copy(cv_hbm.at[page], vbuf.at[slot], sem.at[1, slot]),
                pltpu.make_async_copy(lf_hbm.at[page], lbuf.at[slot], sem.at[2, slot]))

    def fetch(g):
        bb = g // n_pages
        for c in copies(pt_ref[bb, g - bb * n_pages], g % PAGE_BUFS):
            c.start()

    @pl.when(b == 0)
    def _():
        for g in range(depth):
            fetch(g)

    q = (q_ref[...] * (HEAD_DIM ** -0.5)).astype(BF16)
    qh = [q[:, h * HEAD_DIM:(h + 1) * HEAD_DIM] for h in range(nh)]
    def cumulate(lf, carry):
        n = lf.shape[1]
        r = lax.broadcasted_iota(jnp.int32, (n, n), 0)
        c = lax.broadcasted_iota(jnp.int32, (n, n), 1)
        upper = jnp.where(r <= c, 1.0, 0.0).astype(BF16)
        hi, mid, lo = _split3(lf)
        cum = _dot(hi, upper) + _dot(mid, upper) + _dot(lo, upper) + carry
        return cum, cum[:, n - 1:n]

    def attend(pages, feature_major, ck, valid, m, l, acc):
        qk = _dot if feature_major else _dot_nt
        pv_dot = _dot_nt if feature_major else _dot
        s = jnp.concatenate([jnp.concatenate([qk(qh[h], kh[h]) for kh, _ in pages], axis=1) for h in range(nh)], axis=0)
        s = s - jnp.concatenate([jnp.broadcast_to(ck[h:h + 1, :], (DEC_SEQ, ck.shape[1])) for h in range(nh)], axis=0)
        if valid is not None:
            s = jnp.where(valid, s, -jnp.inf)
        m_new = jnp.maximum(m, jnp.max(s, axis=-1, keepdims=True))
        alpha = jnp.exp(m - m_new)
        p = jnp.exp(s - m_new)
        l = alpha * l + jnp.sum(p, axis=-1, keepdims=True)
        pb = p.astype(BF16)
        pv = []
        for h in range(nh):
            rows_h = slice(h * DEC_SEQ, (h + 1) * DEC_SEQ)
            pv.append(sum(pv_dot(pb[rows_h, i * PAGE:(i + 1) * PAGE], vh[h]) for i, (_, vh) in enumerate(pages)))
        return m_new, l, alpha * acc + jnp.concatenate(pv, axis=0)

    def pair_step(jj, carry):
        m, l, acc, ccar = carry
        g0 = b * n_pages + PAGE_GROUP * jj
        slots = [(g0 + u) % PAGE_BUFS for u in range(PAGE_GROUP)]
        for slot in slots:
            for cp in copies(0, slot):
                cp.wait()
        for u in range(PAGE_GROUP):
            @pl.when(g0 + depth + u < total)
            def _():
                fetch(g0 + depth + u)

        ck, ccar = cumulate(jnp.concatenate([lbuf[slot] for slot in slots], axis=1), ccar)
        pages = [([kbuf[slot, h].astype(BF16) for h in range(nh)],
                  [vbuf[slot, h].astype(BF16) for h in range(nh)]) for slot in slots]
        m, l, acc = attend(pages, True, ck, None, m, l, acc)
        return m, l, acc, ccar

    init = (jnp.full((rows, 1), -jnp.inf, F32), jnp.zeros((rows, 1), F32), jnp.zeros((rows, HEAD_DIM), F32),
            jnp.zeros((nh, 1), F32))
    m, l, acc, ccar = lax.fori_loop(0, n_pages // PAGE_GROUP, pair_step, init)

    pad = jnp.zeros((PAGE - DEC_SEQ, HEAD_DIM), BF16)
    kn = kn_ref[...].astype(BF16)
    vn = vn_ref[...].astype(BF16)
    k_heads = [jnp.concatenate([kn[:, h * HEAD_DIM:(h + 1) * HEAD_DIM], pad], axis=0) for h in range(nh)]
    v_heads = [jnp.concatenate([vn[:, h * HEAD_DIM:(h + 1) * HEAD_DIM], pad], axis=0) for h in range(nh)]
    ck, _ = cumulate(lfn_ref[0], ccar)
    rr = lax.broadcasted_iota(jnp.int32, (rows, PAGE), 0)
    cc = lax.broadcasted_iota(jnp.int32, (rows, PAGE), 1)
    m, l, acc = attend([(k_heads, v_heads)], False, ck, cc <= (rr & (DEC_SEQ - 1)), m, l, acc)
    o = acc / l
    on = o * lax.rsqrt(jnp.mean(o * o, axis=-1, keepdims=True) + RMS_EPS)
    out = jnp.concatenate([on[h * DEC_SEQ:(h + 1) * DEC_SEQ, :] for h in range(nh)], axis=1)
    o_ref[...] = out * g_ref[...] * jax.nn.sigmoid(og_ref[...])


def _paged_sample(page_table, qn, kn, v, lf_new_t, cache_k, cache_v, lf_pages_t, z, g_fox_o, o_f, row0):
    db, n_pages = page_table.shape
    assert n_pages % PAGE_GROUP == 0 and n_pages >= PAGE_BUFS
    nh = WIDTH // HEAD_DIM
    rb = row0 // DEC_SEQ
    newmap = lambda b, pt: (b, 0)
    hbm = pl.BlockSpec(memory_space=pl.ANY)
    gs = pltpu.PrefetchScalarGridSpec(
        num_scalar_prefetch=1,
        grid=(db,),
        in_specs=[pl.BlockSpec((DEC_SEQ, WIDTH), newmap),
                  pl.BlockSpec((DEC_SEQ, WIDTH), newmap),
                  pl.BlockSpec((DEC_SEQ, WIDTH), newmap),
                  pl.BlockSpec((1, nh, PAGE), lambda b, pt: (b, 0, 0)),
                  pl.BlockSpec((DEC_SEQ, WIDTH), lambda b, pt: (rb + b, C_OG // WIDTH)),
                  pl.BlockSpec((1, WIDTH), lambda b, pt: (0, 0)),
                  hbm, hbm, hbm, hbm],
        out_specs=pl.BlockSpec((DEC_SEQ, WIDTH), lambda b, pt: (rb + b, 0)),
        scratch_shapes=[pltpu.VMEM((PAGE_BUFS, nh, HEAD_DIM, PAGE), F32), pltpu.VMEM((PAGE_BUFS, nh, HEAD_DIM, PAGE), F32),
                        pltpu.VMEM((PAGE_BUFS, nh, PAGE), F32), pltpu.SemaphoreType.DMA((3, PAGE_BUFS))],
    )
    return pl.pallas_call(
        functools.partial(_paged_kernel, n_pages=n_pages),
        grid_spec=gs,
        out_shape=jax.ShapeDtypeStruct(o_f.shape, F32),
        input_output_aliases={10: 0},
        compiler_params=_cp(("arbitrary",)),
        name="paged_sample",
    )(page_table, qn, kn, v, lf_new_t, z, g_fox_o, cache_k, cache_v, lf_pages_t, o_f)


def _prep_kernel(z_ref, first_ref, mu_ref, w12_ref, w3_ref, w0_ref, a0_ref, kk_ref, ka_ref, rk_ref,
                 g_hbm, bonus_hbm, o_r, o_w, o_k, o_v, o_a, o_b, o_g, o_bonus, carry, *, seq_len, tiles_per_seq):
    z = z_ref[...]
    tm = z.shape[0]
    row = lax.broadcasted_iota(jnp.int32, (tm, 1), 0)
    rolled = pltpu.roll(z, 1, 0)
    if tiles_per_seq >= 1 and seq_len >= tm:
        li = pl.program_id(0) % tiles_per_seq

        @pl.when(li == 0)
        def _():
            carry[...] = first_ref[0]

        zp = jnp.where(row == 0, carry[...], rolled)
        carry[...] = z[tm - 1:tm, :]
    else:
        nseq = tm // seq_len
        first = first_ref[...]
        exp = jnp.broadcast_to(first, (nseq, seq_len, first.shape[-1])).reshape(tm, first.shape[-1])
        zp = jnp.where((row & (seq_len - 1)) == 0, exp, rolled)
    zm = z + (zp - z) * mu_ref[...]
    r = zm[:, 0:WIDTH]
    k = zm[:, WIDTH:2 * WIDTH]
    v = zm[:, 2 * WIDTH:3 * WIDTH]
    lo = zm[:, C_LORA:C_LORA + 384]
    lane = lax.broadcasted_iota(jnp.int32, (tm, 384), 1)
    act = jnp.where(lane < 64, jnp.tanh(lo), jnp.where(lane < 128, lo, jax.nn.sigmoid(lo))).astype(BF16)
    l12 = _dot(act[:, 0:128], w12_ref[...])
    g = _dot(act[:, 128:384], w3_ref[...])
    log_w = -_softplus(-(w0_ref[...] + l12[:, 0:WIDTH])) - 0.5
    decay = jnp.exp(-jnp.exp(log_w))
    asig = jax.nn.sigmoid(a0_ref[...] + l12[:, WIDTH:2 * WIDTH])
    bd = _block_diag_ones()
    kk = k * kk_ref[...]
    kk = kk / jnp.maximum(jnp.sqrt(_segsum(kk * kk, bd)), 1e-12)
    kf = k * (1.0 + (asig - 1.0) * ka_ref[...])
    o_r[...] = r
    o_w[...] = decay
    o_k[...] = kf
    o_v[...] = v
    o_a[...] = -kk
    o_b[...] = kk * asig
    o_g[...] = g
    o_bonus[...] = _segsum(r * kf * rk_ref[...], bd) * v


def _rwkv_prep(z, first, mu, w12, w3, w0, a0, k_k, k_a, r_k, row0, n_rows, seq_len, shared):
    if seq_len >= 256:
        tm = _tile(seq_len, 256)
        first_spec = pl.BlockSpec((1, 1, RWKV_COLS), lambda i: (i // (seq_len // tm), 0, 0))
    else:
        tm = _tile(n_rows, 128)
        nseq = tm // seq_len
        first_spec = pl.BlockSpec((nseq, 1, RWKV_COLS), lambda i: (i, 0, 0))
    rb = row0 // tm
    vec = lambda n: pl.BlockSpec((1, n), lambda i: (0, 0))
    out = pl.BlockSpec((tm, WIDTH), lambda i: (i, 0))
    out_all = pl.BlockSpec((tm, WIDTH), lambda i: (rb + i, 0))
    hbm = pl.BlockSpec(memory_space=pl.ANY)
    extra = tuple(shared)
    return pl.pallas_call(
        functools.partial(_prep_kernel, seq_len=seq_len, tiles_per_seq=max(seq_len // tm, 1)),
        grid=(n_rows // tm,),
        in_specs=[pl.BlockSpec((tm, RWKV_COLS), lambda i: (rb + i, 0)), first_spec, vec(RWKV_COLS),
                  pl.BlockSpec((128, 2 * WIDTH), lambda i: (0, 0)), pl.BlockSpec((256, WIDTH), lambda i: (0, 0)),
                  vec(WIDTH), vec(WIDTH), vec(WIDTH), vec(WIDTH), vec(WIDTH)] + [hbm] * len(extra),
        out_specs=[out] * 6 + [out_all] * 2,
        out_shape=[jax.ShapeDtypeStruct((n_rows, WIDTH), F32)] * 6 + [jax.ShapeDtypeStruct((z.shape[0], WIDTH), F32)] * 2,
        input_output_aliases={10: 6, 11: 7},
        scratch_shapes=[pltpu.VMEM((1, RWKV_COLS), F32)],
        compiler_params=_cp(("arbitrary",)),
        name=f"rwkv_prep_{seq_len}",
    )(z, first, mu, w12, w3, w0, a0, k_k, k_a, r_k, *extra)


SCAN_SUB = 64
SCAN_UNROLL = 4


def _wkv_kernel(r_ref, w_ref, k_ref, v_ref, a_ref, b_ref, s0_ref, y_ref, sT_ref, S_s, Z_s, *, groups, n_pairs, tb_len):
    tb = pl.program_id(1)
    r2 = lax.broadcasted_iota(jnp.int32, (2 * LANES, 2 * LANES), 0)
    c2 = lax.broadcasted_iota(jnp.int32, (2 * LANES, 2 * LANES), 1)
    ones_bd = jnp.where((r2 >> 6) == (c2 >> 6), 1.0, 0.0).astype(BF16)
    vrow = lax.broadcasted_iota(jnp.int32, (HEAD_DIM, LANES), 0)
    lane = lax.broadcasted_iota(jnp.int32, (HEAD_DIM, LANES), 1)
    eye2 = (lane & (HEAD_DIM - 1)) == vrow
    zero_half = jnp.zeros((HEAD_DIM, LANES), BF16)
    H = HEAD_DIM

    @pl.when(tb == 0)
    def _():
        for g in range(groups):
            for p in range(n_pairs):
                S_s[g * n_pairs + p] = jnp.concatenate([s0_ref[g, 2 * p], s0_ref[g, 2 * p + 1]], axis=1)
        Z_s[...] = jnp.zeros_like(Z_s)

    def lanes(p):
        return slice(p * LANES, (p + 1) * LANES)

    def run(t0, n):
        def step(i, carry):
            t = t0 + i
            tp = jnp.maximum(t - 1, 0)
            hit = (lane & (SCAN_SUB - 1)) == (i - 1)
            for g in range(groups):
                aa = a_ref[g, pl.ds(t, 1), :]
                ww = w_ref[g, pl.ds(t, 1), :]
                kr = k_ref[g, pl.ds(t, 1), :]
                vv = v_ref[g, pl.ds(t, 1), :]
                bb = b_ref[g, pl.ds(t, 1), :]
                rp = r_ref[g, pl.ds(tp, 1), :]
                lhs_s, xs = [], []
                for p in range(n_pairs):
                    S = S_s[g * n_pairs + p]
                    lhs_s.append(jnp.concatenate([(S * aa[:, lanes(p)]).astype(BF16),
                                                  (S * rp[:, lanes(p)]).astype(BF16)], axis=1))
                    xs.append(jnp.where(eye2, vv[:, lanes(p)], 0.0).astype(BF16))
                lhs_x = [jnp.concatenate(xs[p:p + 2], axis=1) for p in range(0, n_pairs, 2)]
                res = _dot(jnp.concatenate(lhs_s + lhs_x, axis=0), ones_bd)
                for p in range(n_pairs):
                    idx = g * n_pairs + p
                    rs = res[p * H:(p + 1) * H]
                    vb = res[(n_pairs + p // 2) * H:(n_pairs + p // 2 + 1) * H, (p % 2) * LANES:(p % 2 + 1) * LANES]
                    S_s[idx] = S_s[idx] * ww[:, lanes(p)] + rs[:, :LANES] * bb[:, lanes(p)] + vb * kr[:, lanes(p)]
                    Z_s[idx] = jnp.where(hit, rs[:, LANES:], Z_s[idx])
            return carry

        lax.fori_loop(0, n, step, 0, unroll=SCAN_UNROLL)
        hit_last = (lane & (SCAN_SUB - 1)) == (n - 1)
        for g in range(groups):
            rl = r_ref[g, pl.ds(t0 + n - 1, 1), :]
            lhs = [jnp.concatenate([(S_s[g * n_pairs + p] * rl[:, lanes(p)]).astype(BF16), zero_half], axis=1)
                   for p in range(n_pairs)]
            res = _dot(jnp.concatenate(lhs, axis=0), ones_bd)
            for p in range(n_pairs):
                idx = g * n_pairs + p
                zt = jnp.where(hit_last, res[p * H:(p + 1) * H, :LANES], Z_s[idx]).T
                y_ref[g, pl.ds(t0, n), p * LANES:p * LANES + H] = zt[0:n, :]
                y_ref[g, pl.ds(t0, n), p * LANES + H:(p + 1) * LANES] = zt[SCAN_SUB:SCAN_SUB + n, :]

    if tb_len <= SCAN_SUB:
        run(0, tb_len)
    else:
        def outer(blk, carry):
            run(pl.multiple_of(blk * SCAN_SUB, SCAN_SUB), SCAN_SUB)
            return carry
        lax.fori_loop(0, tb_len // SCAN_SUB, outer, 0)

    @pl.when(tb == pl.num_programs(1) - 1)
    def _():
        for g in range(groups):
            for p in range(n_pairs):
                S = S_s[g * n_pairs + p]
                sT_ref[g, 2 * p] = S[:, 0:H]
                sT_ref[g, 2 * p + 1] = S[:, H:LANES]


def _wkv_scan(r, w, k, v, a, b, s0, n_seq, seq_len):
    nh = WIDTH // HEAD_DIM
    groups = 2 if n_seq % 2 == 0 else 1
    tb_len = min(seq_len, 128)
    nt = seq_len // tb_len
    blk = pl.BlockSpec((groups, tb_len, WIDTH), lambda s, t: (s, t, 0))
    st = pl.BlockSpec((groups, nh, HEAD_DIM, HEAD_DIM), lambda s, t: (s, 0, 0, 0))
    n_pairs = WIDTH // LANES
    seq3 = lambda x: x.reshape(n_seq, seq_len, WIDTH)
    y, s_new = pl.pallas_call(
        functools.partial(_wkv_kernel, groups=groups, n_pairs=n_pairs, tb_len=tb_len),
        grid=(n_seq // groups, nt),
        in_specs=[blk] * 6 + [st],
        out_specs=[blk, st],
        out_shape=[jax.ShapeDtypeStruct((n_seq, seq_len, WIDTH), F32),
                   jax.ShapeDtypeStruct((n_seq, nh, HEAD_DIM, HEAD_DIM), F32)],
        scratch_shapes=[pltpu.VMEM((groups * n_pairs, HEAD_DIM, LANES), F32),
                        pltpu.VMEM((groups * n_pairs, HEAD_DIM, LANES), F32)],
        compiler_params=_cp(("arbitrary", "arbitrary")),
        name=f"wkv_scan_{seq_len}",
    )(seq3(r), seq3(w), seq3(k), seq3(v), seq3(a), seq3(b), s0)
    return y.reshape(n_seq * seq_len, WIDTH), s_new


def _outproj_kernel(x_ref, y_ref, bonus_ref, g_ref, of_ref, lw_ref, lb_ref, wo_ref, gf_ref, wrh_ref, wrl_ref, br_ref,
                    x1_ref, h_ref, route_ref):
    bd = _block_diag_ones()
    y = y_ref[...]
    mu = _segsum(y, bd) * (1.0 / HEAD_DIM)
    d = y - mu
    var = _segsum(d * d, bd) * (1.0 / HEAD_DIM)
    yn = d * lax.rsqrt(var + LNX_EPS) * lw_ref[...] + lb_ref[...]
    o_r = ((yn + bonus_ref[...]) * g_ref[...]).astype(BF16)
    o_f = of_ref[...].astype(BF16)
    x1 = x_ref[...] + _dot(o_r, wo_ref[0:WIDTH, :]) + _dot(o_f, wo_ref[WIDTH:2 * WIDTH, :])
    x1_ref[...] = x1
    h = _rms_rows(x1, gf_ref[...])
    h_ref[...] = h
    hi, lo = _split2(h)
    logits = _dot(hi, wrh_ref[...]) + _dot(lo, wrh_ref[...]) + _dot(hi, wrl_ref[...]) + br_ref[...]
    tm = logits.shape[0]
    lane = lax.broadcasted_iota(jnp.int32, (tm, LANES), 1)
    neg = -jnp.inf
    lg = jnp.where(lane < N_GROUPS, logits, neg)
    mg = jnp.max(lg, axis=-1, keepdims=True)
    pg_top = 1.0 / jnp.sum(jnp.exp(lg - mg), axis=-1, keepdims=True)
    g_sel = jnp.min(jnp.where(lg == mg, lane, LANES), axis=-1, keepdims=True)
    in_grp = (lane >= N_GROUPS) & (lane < N_GROUPS + N_EXPERTS) & (((lane - N_GROUPS) >> 3) == g_sel)
    le = jnp.where(in_grp, logits, neg)
    m1 = jnp.max(le, axis=-1, keepdims=True)
    i1 = jnp.min(jnp.where(le == m1, lane, LANES), axis=-1, keepdims=True)
    le2 = jnp.where(lane == i1, neg, le)
    m2 = jnp.max(le2, axis=-1, keepdims=True)
    i2 = jnp.min(jnp.where(le2 == m2, lane, LANES), axis=-1, keepdims=True)
    e2 = jnp.exp(m2 - m1)
    gate1 = pg_top / (1.0 + e2)
    gate2 = pg_top * e2 / (1.0 + e2)
    route = jnp.where(lane == 0, (i1 - N_GROUPS).astype(F32),
                      jnp.where(lane == 1, (i2 - N_GROUPS).astype(F32),
                                jnp.where(lane == 2, gate1, jnp.where(lane == 3, gate2, 0.0))))
    route_ref[...] = route


def _outproj(x, y, bonus, g, o_f, lnx_w, lnx_b, w_out, g_ffn, wr_hi, wr_lo, b_r):
    m, d = x.shape
    tm = _tile(m, 256)
    row = lambda n: pl.BlockSpec((tm, n), lambda i: (i, 0))
    vec = lambda n: pl.BlockSpec((1, n), lambda i: (0, 0))
    full = lambda a: pl.BlockSpec(a.shape, lambda i: (0, 0))
    return pl.pallas_call(
        _outproj_kernel,
        grid=(m // tm,),
        in_specs=[row(d), row(WIDTH), row(WIDTH), row(WIDTH), row(WIDTH), vec(WIDTH), vec(WIDTH),
                  full(w_out), vec(d), full(wr_hi), full(wr_lo), vec(LANES)],
        out_specs=[row(d), row(d), row(LANES)],
        out_shape=[jax.ShapeDtypeStruct((m, d), F32), jax.ShapeDtypeStruct((m, d), F32),
                   jax.ShapeDtypeStruct((m, LANES), F32)],
        compiler_params=_cp(("arbitrary",)),
        name="outproj_router",
    )(x, y, bonus, g, o_f, lnx_w, lnx_b, w_out, g_ffn, wr_hi, wr_lo, b_r)


def _expert_kernel(plan_ref, tok_ref, nu_ref, h_hbm, wg_hbm, wu_hbm, wd_hbm, o_ref, xbuf, wg, wu, wd, sem, wsem):
    i = pl.program_id(0)
    n_used = nu_ref[0]
    first, nxt, wslot = plan_ref[1, i], plan_ref[2, i], plan_ref[3, i]

    def row_copy(tok, r, slot):
        return pltpu.make_async_copy(h_hbm.at[pl.ds(tok, 1), :], xbuf.at[slot, pl.ds(r, 1), :], sem.at[slot])

    def gather(blk, slot):
        def start(r, c):
            row_copy(tok_ref[blk * MOE_BLOCK + r], r, slot).start()
            return c
        lax.fori_loop(0, MOE_BLOCK, start, 0, unroll=8)

    def weight_copies(expert, slot):
        return (pltpu.make_async_copy(wg_hbm.at[expert], wg.at[slot], wsem.at[0, slot]),
                pltpu.make_async_copy(wu_hbm.at[expert], wu.at[slot], wsem.at[1, slot]),
                pltpu.make_async_copy(wd_hbm.at[expert], wd.at[slot], wsem.at[2, slot]))

    @pl.when(i == 0)
    def _():
        gather(0, 0)
        for c in weight_copies(plan_ref[0, 0], 0):
            c.start()

    @pl.when(i < n_used)
    def _():
        slot = i & 1

        @pl.when(first == 1)
        def _():
            for c in weight_copies(0, wslot):
                c.wait()

            @pl.when(nxt >= 0)
            def _():
                for c in weight_copies(nxt, 1 - wslot):
                    c.start()

        def wait(r, c):
            row_copy(0, r, slot).wait()
            return c
        lax.fori_loop(0, MOE_BLOCK, wait, 0, unroll=8)

        @pl.when(i + 1 < n_used)
        def _():
            gather(i + 1, 1 - slot)

        x = xbuf[slot].astype(BF16)
        gate = _dot(x, wg[wslot].astype(BF16))
        up = _dot(x, wu[wslot].astype(BF16))
        act = (gate * jax.nn.sigmoid(gate) * up).astype(BF16)
        o_ref[...] = _dot(act, wd[wslot].astype(BF16))

    @pl.when(i >= n_used)
    def _():
        o_ref[...] = jnp.zeros_like(o_ref)


def _experts(plan, slot_tok, n_used, h, w_gate, w_up, w_down):
    n_blocks = plan.shape[1]
    d = h.shape[1]
    de = w_gate.shape[2]
    hbm = pl.BlockSpec(memory_space=pl.ANY)
    gs = pltpu.PrefetchScalarGridSpec(
        num_scalar_prefetch=3,
        grid=(n_blocks,),
        in_specs=[hbm, hbm, hbm, hbm],
        out_specs=pl.BlockSpec((MOE_BLOCK, d), lambda i, plan, tok, nu: (i, 0)),
        scratch_shapes=[pltpu.VMEM((2, MOE_BLOCK, d), F32), pltpu.VMEM((2, d, de), F32), pltpu.VMEM((2, d, de), F32),
                        pltpu.VMEM((2, de, d), F32), pltpu.SemaphoreType.DMA((2,)), pltpu.SemaphoreType.DMA((3, 2))],
    )
    return pl.pallas_call(
        _expert_kernel,
        grid_spec=gs,
        out_shape=jax.ShapeDtypeStruct((n_blocks * MOE_BLOCK, d), F32),
        compiler_params=_cp(("arbitrary",)),
        name="experts",
    )(plan, slot_tok, n_used, h, w_gate, w_up, w_down)


def _ple_kernel(dest_ref, x1_ref, route_ref, p_ref, yb_hbm, gp_ref, wpg_ref, bpg_ref, wpp_ref, gfin_ref, op_ref, os_ref,
                buf, sem, *, prompt_tiles):
    i = pl.program_id(0)
    tm = x1_ref.shape[0]

    def row_copy(slot_row, r, k, half):
        return pltpu.make_async_copy(yb_hbm.at[pl.ds(slot_row, 1), :], buf.at[half, k, pl.ds(r, 1), :], sem.at[half])

    def gather(tile, half):
        def start(r, c):
            for k in range(2):
                row_copy(dest_ref[(tile * tm + r) * 2 + k], r, k, half).start()
            return c
        lax.fori_loop(0, tm, start, 0, unroll=8)

    @pl.when(i == 0)
    def _():
        gather(0, 0)

    half = i & 1

    def wait(r, c):
        for k in range(2):
            row_copy(0, r, k, half).wait()
        return c
    lax.fori_loop(0, tm, wait, 0, unroll=8)

    @pl.when(i + 1 < pl.num_programs(0))
    def _():
        gather(i + 1, 1 - half)

    route = route_ref[...]
    moe = buf[half, 0] * route[:, 2:3] + buf[half, 1] * route[:, 3:4]
    x2 = x1_ref[...] + moe
    hn = _rms_rows(x2, gp_ref[...]).astype(BF16)
    gate = jax.nn.sigmoid(_dot(hn, wpg_ref[...]) + bpg_ref[...])
    pe = _dot(p_ref[...].astype(BF16), wpp_ref[...])
    x3 = x2 + gate * pe
    y = _rms_rows(x3, gfin_ref[...])

    @pl.when(i < prompt_tiles)
    def _():
        op_ref[...] = y

    @pl.when(i >= prompt_tiles)
    def _():
        os_ref[...] = y


def _ple(dest, x1, route, p, yb, g_ple, w_pg, b_pg, w_pp, g_final, n_p):
    m, d = x1.shape
    tm = _tile(math.gcd(n_p, m - n_p), 256)
    npt = n_p // tm
    pd = p.shape[1]
    gs = pltpu.PrefetchScalarGridSpec(
        num_scalar_prefetch=1,
        grid=(m // tm,),
        in_specs=[pl.BlockSpec((tm, d), lambda i, ds: (i, 0)),
                  pl.BlockSpec((tm, LANES), lambda i, ds: (i, 0)),
                  pl.BlockSpec((tm, pd), lambda i, ds: (i, 0)),
                  pl.BlockSpec(memory_space=pl.ANY),
                  pl.BlockSpec((1, d), lambda i, ds: (0, 0)),
                  pl.BlockSpec((d, d), lambda i, ds: (0, 0)),
                  pl.BlockSpec((1, d), lambda i, ds: (0, 0)),
                  pl.BlockSpec((pd, d), lambda i, ds: (0, 0)),
                  pl.BlockSpec((1, d), lambda i, ds: (0, 0))],
        out_specs=[pl.BlockSpec((tm, d), lambda i, ds: (jnp.minimum(i, npt - 1), 0)),
                   pl.BlockSpec((tm, d), lambda i, ds: (jnp.maximum(i - npt, 0), 0))],
        scratch_shapes=[pltpu.VMEM((2, 2, tm, d), F32), pltpu.SemaphoreType.DMA((2,))],
    )
    return pl.pallas_call(
        functools.partial(_ple_kernel, prompt_tiles=npt),
        grid_spec=gs,
        out_shape=[jax.ShapeDtypeStruct((n_p, d), F32), jax.ShapeDtypeStruct((m - n_p, d), F32)],
        compiler_params=_cp(("arbitrary",)),
        name="moe_combine_ple",
    )(dest, x1, route, p, yb, g_ple, w_pg, b_pg, w_pp, g_final)


def _dispatch(route, n_tok):
    expert = route[:, 0:2].astype(jnp.int32)
    flat_e = expert.reshape(-1)
    n_assign = flat_e.shape[0]
    onehot = (flat_e[:, None] == jnp.arange(N_EXPERTS, dtype=jnp.int32)[None, :]).astype(jnp.int32)
    csum = jnp.cumsum(onehot, axis=0)
    rank = jnp.sum(csum * onehot, axis=1) - 1
    counts = csum[-1]
    pcounts = (counts + MOE_BLOCK - 1) // MOE_BLOCK * MOE_BLOCK
    pend = jnp.cumsum(pcounts)
    pstart = pend - pcounts
    dest = (pstart[flat_e] + rank).astype(jnp.int32)
    n_blocks = -(-n_assign // MOE_BLOCK) + N_EXPERTS
    slot_tok = jnp.zeros((n_blocks * MOE_BLOCK,), jnp.int32).at[dest].set(jnp.arange(n_assign, dtype=jnp.int32) // 2)
    n_used = (pend[-1] // MOE_BLOCK).astype(jnp.int32)
    blk = jnp.arange(n_blocks, dtype=jnp.int32)
    block_e = jnp.searchsorted(pend, jnp.minimum(blk, n_used - 1) * MOE_BLOCK, side='right').astype(jnp.int32)
    block_e = jnp.minimum(block_e, N_EXPERTS - 1)
    first = jnp.concatenate([jnp.ones((1,), jnp.int32), (block_e[1:] != block_e[:-1]).astype(jnp.int32)])
    first = jnp.where(blk < n_used, first, 0)
    used = counts > 0
    eidx = jnp.arange(N_EXPERTS, dtype=jnp.int32)
    later = jnp.flip(lax.cummin(jnp.flip(jnp.where(used, eidx, N_EXPERTS))))
    next_used = jnp.concatenate([later[1:], jnp.full((1,), N_EXPERTS, jnp.int32)])
    next_used = jnp.where(next_used < N_EXPERTS, next_used, -1)
    ordinal = jnp.cumsum(used.astype(jnp.int32)) - 1
    plan = jnp.stack([block_e, first, next_used[block_e], ordinal[block_e] & 1]).astype(jnp.int32)
    return plan, slot_tok, n_used.reshape(1), dest


def kernel(x_prompt, x_sample, cache_k, cache_v, cache_logf, state_wkv, state_shift, page_table, p_prompt, p_sample, g_attn, w_in, mu_shift, w0, w_up, a0, a_up, g_up, k_k, k_a, r_k, lnx_w, lnx_b, b_f, q_norm, k_norm, g_fox_o, w_out, g_ffn, w_rg, b_rg, w_re, b_re, w_e_gate, w_e_up, w_e_down, g_ple, w_pg, b_pg, w_pp, g_final):
    depth = g_attn.shape[0]
    assert depth == 1
    batch, seq, d = x_prompt.shape
    db, dec_seq, _ = x_sample.shape
    assert dec_seq == DEC_SEQ
    nh = WIDTH // HEAD_DIM
    n_p = batch * seq
    n_s = db * dec_seq
    m = n_p + n_s
    rwkv_in = mu_shift.shape[1]
    n_pool = cache_k.shape[1]
    n_pages = page_table.shape[1]
    row = lambda a: a.reshape(1, -1)

    x_all = jnp.concatenate([x_prompt.reshape(n_p, d), x_sample.reshape(n_s, d)], axis=0)
    p_all = jnp.concatenate([p_prompt[0].reshape(n_p, -1), p_sample[0].reshape(n_s, -1)], axis=0)

    wi = w_in[0]
    zc = lambda n: jnp.zeros((d, n), F32)
    w_z = jnp.concatenate([wi[:, :rwkv_in], zc(C_F - rwkv_in), wi[:, rwkv_in + 4 * WIDTH:], zc(C_Q - C_F - nh),
                           wi[:, rwkv_in:rwkv_in + 4 * WIDTH]], axis=1).astype(BF16)
    w12 = jnp.zeros((128, 2 * WIDTH), F32).at[0:64, 0:WIDTH].set(w_up[0]).at[64:128, WIDTH:].set(a_up[0]).astype(BF16)
    w3 = jnp.zeros((256, WIDTH), F32).at[0:g_up.shape[1]].set(g_up[0]).astype(BF16)
    w_r = jnp.zeros((d, LANES), F32).at[:, 0:N_GROUPS].set(w_rg[0]).at[:, N_GROUPS:N_GROUPS + N_EXPERTS].set(w_re[0])
    wr_hi = w_r.astype(BF16)
    wr_lo = (w_r - wr_hi.astype(F32)).astype(BF16)
    b_r = jnp.zeros((1, LANES), F32).at[0, 0:N_GROUPS].set(b_rg[0]).at[0, N_GROUPS:N_GROUPS + N_EXPERTS].set(b_re[0])
    pad_cols = lambda a, n: jnp.pad(a, ((0, 0), (0, n - a.shape[1])))
    mu_pad = pad_cols(mu_shift, RWKV_COLS)
    bf_pad = pad_cols(b_f, LANES)
    qn_t = jnp.tile(q_norm, (1, nh))
    kn_t = jnp.tile(k_norm, (1, nh))

    z = _inproj(x_all, g_attn, w_z)

    qn_s, kn_s, vv_s, logf, qa, ka, vt, kt_p, vt_p = _foxpost(z, qn_t, kn_t, bf_pad, batch, seq)

    o_f = _flash_prompt(qa, ka, vt, z, g_fox_o, batch, seq)
    lf_pages_t = cache_logf[0].transpose(0, 2, 1)
    lf_new_t = pad_cols(logf[n_p:, :nh].reshape(db, dec_seq, nh).transpose(0, 2, 1).reshape(db * nh, dec_seq), PAGE)
    o_f = _paged_sample(page_table, qn_s, kn_s, vv_s, lf_new_t.reshape(db, nh, PAGE),
                        cache_k[0].transpose(0, 2, 3, 1), cache_v[0].transpose(0, 2, 3, 1), lf_pages_t, z, g_fox_o, o_f, n_p)

    first_p = jnp.zeros((batch, 1, RWKV_COLS), F32)
    first_s = pad_cols(state_shift[0], RWKV_COLS).reshape(db, 1, RWKV_COLS)
    prep_args = (mu_pad, w12, w3, w0, a0, k_k, k_a, row(r_k))
    prep_p = _rwkv_prep(z, first_p, *prep_args, 0, n_p, seq, shared=(jnp.zeros((m, WIDTH), F32),) * 2)
    prep_s = _rwkv_prep(z, first_s, *prep_args, n_p, n_s, dec_seq, shared=prep_p[6:8])
    g_, bonus = prep_s[6:8]
    y_p, wkv_p = _wkv_scan(*prep_p[:6], jnp.zeros((batch, nh, HEAD_DIM, HEAD_DIM), F32), batch, seq)
    y_s, wkv_s = _wkv_scan(*prep_s[:6], state_wkv[0], db, dec_seq)
    y = jnp.concatenate([y_p, y_s], axis=0)

    x1, h2, route = _outproj(x_all, y, bonus, g_, o_f, lnx_w, lnx_b, w_out[0].astype(BF16), g_ffn, wr_hi, wr_lo, b_r)

    plan, slot_tok, n_used, dest = _dispatch(route, m)
    yb = _experts(plan, slot_tok, n_used, h2, w_e_gate[0], w_e_up[0], w_e_down[0])

    y_out_p, y_out_s = _ple(dest, x1, route, p_all, yb, g_ple, w_pg[0].astype(BF16), b_pg, w_pp[0].astype(BF16),
                            row(g_final), n_p)

    shift_p = jnp.concatenate([z[(b + 1) * seq - 1:(b + 1) * seq, :rwkv_in] for b in range(batch)], axis=0)
    shift_s = z[n_p + dec_seq - 1::dec_seq, :rwkv_in]
    heads_t = lambda a: a.reshape(batch, nh, HEAD_DIM, seq).transpose(0, 3, 1, 2)[None]
    heads = lambda a: a.reshape(1, db, dec_seq, nh, HEAD_DIM)
    return (y_out_p.reshape(batch, seq, d), y_out_s.reshape(db, dec_seq, d),
            heads_t(kt_p), heads_t(vt_p), logf[:n_p, :nh].reshape(1, batch, seq, nh),
            wkv_p[None], shift_p[None],
            heads(kn_s), heads(vv_s), logf[n_p:, :nh].reshape(1, db, dec_seq, nh),
            wkv_s[None], shift_s[None])
```

```python
import functools
import math

import jax
import jax.numpy as jnp
import numpy as np
from jax import lax
from jax.experimental import pallas as pl
from jax.experimental.pallas import tpu as pltpu

F32 = jnp.float32
BF16 = jnp.bfloat16

HEAD_DIM = 64
LANES = 128
RMS_EPS = 1e-6
LNX_EPS = 64e-5
PAGE = 128
MOE_BLOCK = 384
N_GROUPS = 8
N_EXPERTS = 64
DEC_SEQ = 8

C_LORA = 3072
C_F = 3456
RWKV_COLS = 3584
C_Q, C_FK, C_FV, C_OG = 4096, 5120, 6144, 7168
Z_COLS = 8192
WIDTH = 1024

VMEM_LIMIT = 52 * 1024 * 1024


def _cp(sem, vmem=VMEM_LIMIT):
    return pltpu.CompilerParams(dimension_semantics=sem, vmem_limit_bytes=vmem)


def _tile(n, pref):
    for t in (1024, 512, 256, 128, 64, 32, 16, 8):
        if t <= pref and n % t == 0:
            return t
    raise ValueError(f"no tile for {n}")


def _split2(x):
    hi = x.astype(BF16)
    lo = (x - hi.astype(F32)).astype(BF16)
    return hi, lo


def _split3(x):
    hi = x.astype(BF16)
    r1 = x - hi.astype(F32)
    mid = r1.astype(BF16)
    lo = (r1 - mid.astype(F32)).astype(BF16)
    return hi, mid, lo


def _dot(a, b):
    return jnp.dot(a, b, preferred_element_type=F32)


def _dot_nt(a, b):
    return lax.dot_general(a, b, (((1,), (1,)), ((), ())), preferred_element_type=F32)


def _block_diag_ones():
    r = lax.broadcasted_iota(jnp.int32, (LANES, LANES), 0)
    c = lax.broadcasted_iota(jnp.int32, (LANES, LANES), 1)
    return jnp.where((r >> 6) == (c >> 6), 1.0, 0.0).astype(BF16)


def _segsum(x, bd):
    outs = []
    for j in range(x.shape[1] // LANES):
        hi, lo = _split2(x[:, j * LANES:(j + 1) * LANES])
        outs.append(_dot(hi, bd) + _dot(lo, bd))
    return outs[0] if len(outs) == 1 else jnp.concatenate(outs, axis=1)


def _softplus(x):
    return jnp.maximum(x, 0.0) + jnp.log1p(jnp.exp(-jnp.abs(x)))


def _rms_rows(x, g):
    return x * lax.rsqrt(jnp.mean(x * x, axis=-1, keepdims=True) + RMS_EPS) * g


def _inproj_kernel(x_ref, g_ref, w_ref, o_ref, h_ref):
    @pl.when(pl.program_id(1) == 0)
    def _():
        h_ref[...] = _rms_rows(x_ref[...], g_ref[...]).astype(BF16)

    o_ref[...] = _dot(h_ref[...], w_ref[...])


def _inproj(x, g, w):
    m, d = x.shape
    n = w.shape[1]
    tm, tn = _tile(m, 1024), _tile(n, 512)
    return pl.pallas_call(
        _inproj_kernel,
        grid=(m // tm, n // tn),
        in_specs=[pl.BlockSpec((tm, d), lambda i, j: (i, 0)),
                  pl.BlockSpec((1, d), lambda i, j: (0, 0)),
                  pl.BlockSpec((d, tn), lambda i, j: (0, j))],
        out_specs=pl.BlockSpec((tm, tn), lambda i, j: (i, j)),
        out_shape=jax.ShapeDtypeStruct((m, n), F32),
        scratch_shapes=[pltpu.VMEM((tm, d), BF16)],
        compiler_params=_cp(("arbitrary", "arbitrary")),
        name="inproj",
    )(x, g, w)


AUG = 2 * HEAD_DIM


def _aug_constants():
    nh = WIDTH // HEAD_DIM
    pq = np.zeros((3 * LANES, WIDTH), np.float32)
    pk = np.zeros((3 * LANES, WIDTH), np.float32)
    one_q = np.zeros((1, WIDTH), np.float32)
    one_k = np.zeros((1, WIDTH), np.float32)
    for h in range(nh):
        for comp in range(3):
            pq[comp * LANES + h, h * HEAD_DIM + comp] = 1.0
            pk[comp * LANES + h, h * HEAD_DIM + 3 + comp] = -1.0
            one_q[0, h * HEAD_DIM + 3 + comp] = 1.0
            one_k[0, h * HEAD_DIM + comp] = 1.0
    return jnp.asarray(pq, BF16), jnp.asarray(pk, BF16), jnp.asarray(one_q), jnp.asarray(one_k)


def _foxpost_kernel(q_ref, k_ref, v_ref, f_ref, qn_ref, kn_ref, bf_ref, pq_ref, pk_ref, oneq_ref, onek_ref,
                    oq, ok, ov, olf, oqa, oka, ovt, okt32, ovt32, carry, *, tiles_per_seq, prompt_tiles):
    i = pl.program_id(0)
    bd = _block_diag_ones()
    nh = WIDTH // HEAD_DIM

    def head_norm(x, g):
        ms = _segsum(x * x, bd) * (1.0 / HEAD_DIM)
        return x * lax.rsqrt(ms + RMS_EPS) * g

    qn = head_norm(q_ref[...], qn_ref[...])
    kn = head_norm(k_ref[...], kn_ref[...])
    v = v_ref[...]
    vt = v.T
    ovt[...] = vt.astype(BF16)

    @pl.when(i < prompt_tiles)
    def _():
        okt32[0] = kn.T
        ovt32[0] = vt

    @pl.when(i >= prompt_tiles)
    def _():
        oq[...] = qn
        ok[...] = kn
        ov[...] = v

    lf = -_softplus(-(f_ref[...] + bf_ref[...]))
    olf[...] = lf

    @pl.when(i % tiles_per_seq == 0)
    def _():
        carry[...] = jnp.zeros_like(carry)

    tm = lf.shape[0]
    r = lax.broadcasted_iota(jnp.int32, (tm, tm), 0)
    c = lax.broadcasted_iota(jnp.int32, (tm, tm), 1)
    tri = jnp.where(c <= r, 1.0, 0.0).astype(BF16)
    hi, mid, lo = _split3(lf)
    cum = _dot(tri, hi) + _dot(tri, mid) + _dot(tri, lo) + carry[...]
    carry[...] = cum[tm - 1:tm, :]

    c3 = jnp.concatenate(_split3(cum), axis=1)
    aug_q = _dot(c3, pq_ref[...]) + oneq_ref[...]
    aug_k = _dot(c3, pk_ref[...]) + onek_ref[...]
    qs = qn * (HEAD_DIM ** -0.5)

    def interleave(x, aug):
        pieces = []
        for h in range(nh):
            sl = slice(h * HEAD_DIM, (h + 1) * HEAD_DIM)
            pieces += [x[:, sl], aug[:, sl]]
        return jnp.concatenate(pieces, axis=1).astype(BF16)

    oqa[...] = interleave(qs, aug_q)
    oka[...] = interleave(kn, aug_k)


def _foxpost(z, q_norm_t, k_norm_t, bf_pad, batch, seq_len):
    m = z.shape[0]
    nh = WIDTH // HEAD_DIM
    n_p = batch * seq_len
    tm = min(_tile(math.gcd(n_p, m - n_p), 512), seq_len)
    npt, tps = n_p // tm, seq_len // tm
    sample = pl.BlockSpec((tm, WIDTH), lambda i: (jnp.maximum(i - npt, 0), 0))
    prompt_t = pl.BlockSpec((1, WIDTH, tm), lambda i: (jnp.minimum(i, npt - 1) // tps, 0, jnp.minimum(i, npt - 1) % tps))
    row = lambda c: pl.BlockSpec((tm, WIDTH), lambda i, c=c: (i, c // WIDTH))
    vec = pl.BlockSpec((1, WIDTH), lambda i: (0, 0))
    small = pl.BlockSpec((tm, LANES), lambda i: (i, 0))
    wide = pl.BlockSpec((tm, nh * AUG), lambda i: (i, 0))
    place = pl.BlockSpec((3 * LANES, WIDTH), lambda i: (0, 0))
    return pl.pallas_call(
        functools.partial(_foxpost_kernel, tiles_per_seq=tps, prompt_tiles=npt),
        grid=(m // tm,),
        in_specs=[row(C_Q), row(C_FK), row(C_FV),
                  pl.BlockSpec((tm, LANES), lambda i: (i, C_F // LANES)),
                  vec, vec, pl.BlockSpec((1, LANES), lambda i: (0, 0)), place, place, vec, vec],
        out_specs=[sample] * 3 + [small, wide, wide, pl.BlockSpec((WIDTH, tm), lambda i: (0, i)), prompt_t, prompt_t],
        out_shape=[jax.ShapeDtypeStruct((m - n_p, WIDTH), F32)] * 3 + [jax.ShapeDtypeStruct((m, LANES), F32),
                                                                        jax.ShapeDtypeStruct((m, nh * AUG), BF16),
                                                                        jax.ShapeDtypeStruct((m, nh * AUG), BF16),
                                                                        jax.ShapeDtypeStruct((WIDTH, m), BF16),
                                                                        jax.ShapeDtypeStruct((batch, WIDTH, seq_len), F32),
                                                                        jax.ShapeDtypeStruct((batch, WIDTH, seq_len), F32)],
        scratch_shapes=[pltpu.VMEM((1, LANES), F32)],
        compiler_params=_cp(("arbitrary",)),
        name="foxpost",
    )(z, z, z, z, q_norm_t, k_norm_t, bf_pad, *_aug_constants())


def _flash_kernel(qi_tab, ki_tab, qa_ref, ka_ref, vt_ref, og_ref, g_ref, init_hbm, o_ref, m_s, l_s, acc_s):
    s_idx = pl.program_id(2)
    qi = qi_tab[s_idx]
    ki = ki_tab[s_idx]
    tq = qa_ref.shape[0]
    tk = ka_ref.shape[0]

    @pl.when(ki == 0)
    def _():
        m_s[...] = jnp.full_like(m_s, -jnp.inf)
        l_s[...] = jnp.zeros_like(l_s)
        acc_s[...] = jnp.zeros_like(acc_s)

    def update(masked):
        for hh in range(2):
            st = _dot_nt(ka_ref[:, hh * AUG:(hh + 1) * AUG], qa_ref[:, hh * AUG:(hh + 1) * AUG])
            if masked:
                r = lax.broadcasted_iota(jnp.int32, (tk, tq), 0)
                c = lax.broadcasted_iota(jnp.int32, (tk, tq), 1)
                st = jnp.where(r <= c, st, -jnp.inf)
            m_prev = m_s[hh]
            m_new = jnp.maximum(m_prev, jnp.max(st, axis=0, keepdims=True))
            alpha = jnp.exp(m_prev - m_new)
            p = jnp.exp(st - m_new)
            l_s[hh] = alpha * l_s[hh] + jnp.sum(p, axis=0, keepdims=True)
            acc_s[hh] = alpha * acc_s[hh] + _dot(vt_ref[hh * HEAD_DIM:(hh + 1) * HEAD_DIM, :], p.astype(BF16))
            m_s[hh] = m_new

    @pl.when(ki < qi)
    def _():
        update(False)

    @pl.when(ki == qi)
    def _():
        update(True)
        outs = []
        for hh in range(2):
            sl = slice(hh * HEAD_DIM, (hh + 1) * HEAD_DIM)
            ot = acc_s[hh] / l_s[hh]
            ont = ot * lax.rsqrt(jnp.mean(ot * ot, axis=0, keepdims=True) + RMS_EPS)
            outs.append(ont.T * g_ref[:, sl] * jax.nn.sigmoid(og_ref[:, sl]))
        o_ref[...] = jnp.concatenate(outs, axis=1)


def _flash_prompt(qa, ka, vt, z, g_fox_o, batch, seq):
    tq = _tile(seq, 512)
    nq = seq // tq
    pairs = [(a, b) for a in range(nq) for b in range(a + 1)]
    qi_tab = jnp.array([a for a, _ in pairs], jnp.int32)
    ki_tab = jnp.array([b for _, b in pairs], jnp.int32)
    npairs = WIDTH // LANES
    qmap = lambda b, p, s, qt, kt: (b * nq + qt[s], p)
    kmap = lambda b, p, s, qt, kt: (b * nq + kt[s], p)
    gs = pltpu.PrefetchScalarGridSpec(
        num_scalar_prefetch=2,
        grid=(batch, npairs, len(pairs)),
        in_specs=[pl.BlockSpec((tq, 2 * AUG), qmap),
                  pl.BlockSpec((tq, 2 * AUG), kmap),
                  pl.BlockSpec((LANES, tq), lambda b, p, s, qt, kt: (p, b * nq + kt[s])),
                  pl.BlockSpec((tq, LANES), lambda b, p, s, qt, kt: (b * nq + qt[s], C_OG // LANES + p)),
                  pl.BlockSpec((1, LANES), lambda b, p, s, qt, kt: (0, p)),
                  pl.BlockSpec(memory_space=pl.ANY)],
        out_specs=pl.BlockSpec((tq, LANES), qmap),
        scratch_shapes=[pltpu.VMEM((2, 1, tq), F32), pltpu.VMEM((2, 1, tq), F32), pltpu.VMEM((2, HEAD_DIM, tq), F32)],
    )
    return pl.pallas_call(
        _flash_kernel,
        grid_spec=gs,
        out_shape=jax.ShapeDtypeStruct((qa.shape[0], WIDTH), F32),
        input_output_aliases={7: 0},
        compiler_params=_cp(("arbitrary", "arbitrary", "arbitrary")),
        name="flash_prompt",
    )(qi_tab, ki_tab, qa, ka, vt, z, g_fox_o, jnp.zeros((qa.shape[0], WIDTH), F32))


PAGE_GROUP = 4
PAGE_BUFS = 12


def _paged_kernel(pt_ref, q_ref, kn_ref, vn_ref, lfn_ref, og_ref, g_ref, ck_hbm, cv_hbm, lf_hbm, of_hbm, o_ref,
                  kbuf, vbuf, lbuf, sem, *, n_pages):
    b = pl.program_id(0)
    total = pl.num_programs(0) * n_pages
    depth = PAGE_BUFS - PAGE_GROUP
    nh = WIDTH // HEAD_DIM
    rows = nh * DEC_SEQ

    def copies(page, slot):
        return (pltpu.make_async_copy(ck_hbm.at[page], kbuf.at[slot], sem.at[0, slot]),
                pltpu.make_async_copy(cv_hbm.at[page], vbuf.at[slot], sem.at[1, slot]),
                pltpu.make_async_copy(lf_hbm.at[page], lbuf.at[slot], sem.at[2, slot]))

    def fetch(g):
        bb = g // n_pages
        for c in copies(pt_ref[bb, g - bb * n_pages], g % PAGE_BUFS):
            c.start()

    @pl.when(b == 0)
    def _():
        for g in range(depth):
            fetch(g)

    q = (q_ref[...] * (HEAD_DIM ** -0.5)).astype(BF16)
    qh = [q[:, h * HEAD_DIM:(h + 1) * HEAD_DIM] for h in range(nh)]
    def cumulate(lf, carry):
        n = lf.shape[1]
        r = lax.broadcasted_iota(jnp.int32, (n, n), 0)
        c = lax.broadcasted_iota(jnp.int32, (n, n), 1)
        upper = jnp.where(r <= c, 1.0, 0.0).astype(BF16)
        hi, mid, lo = _split3(lf)
        cum = _dot(hi, upper) + _dot(mid, upper) + _dot(lo, upper) + carry
        return cum, cum[:, n - 1:n]

    def attend(pages, feature_major, ck, valid, m, l, acc):
        qk = _dot if feature_major else _dot_nt
        pv_dot = _dot_nt if feature_major else _dot
        s = jnp.concatenate([jnp.concatenate([qk(qh[h], kh[h]) for kh, _ in pages], axis=1) for h in range(nh)], axis=0)
        s = s - jnp.concatenate([jnp.broadcast_to(ck[h:h + 1, :], (DEC_SEQ, ck.shape[1])) for h in range(nh)], axis=0)
        if valid is not None:
            s = jnp.where(valid, s, -jnp.inf)
        m_new = jnp.maximum(m, jnp.max(s, axis=-1, keepdims=True))
        alpha = jnp.exp(m - m_new)
        p = jnp.exp(s - m_new)
        l = alpha * l + jnp.sum(p, axis=-1, keepdims=True)
        pb = p.astype(BF16)
        pv = []
        for h in range(nh):
            rows_h = slice(h * DEC_SEQ, (h + 1) * DEC_SEQ)
            pv.append(sum(pv_dot(pb[rows_h, i * PAGE:(i + 1) * PAGE], vh[h]) for i, (_, vh) in enumerate(pages)))
        return m_new, l, alpha * acc + jnp.concatenate(pv, axis=0)

    def pair_step(jj, carry):
        m, l, acc, ccar = carry
        g0 = b * n_pages + PAGE_GROUP * jj
        slots = [(g0 + u) % PAGE_BUFS for u in range(PAGE_GROUP)]
        for slot in slots:
            for cp in copies(0, slot):
                cp.wait()
        for u in range(PAGE_GROUP):
            @pl.when(g0 + depth + u < total)
            def _():
                fetch(g0 + depth + u)

        ck, ccar = cumulate(jnp.concatenate([lbuf[slot] for slot in slots], axis=1), ccar)
        pages = [([kbuf[slot, h].astype(BF16) for h in range(nh)],
                  [vbuf[slot, h].astype(BF16) for h in range(nh)]) for slot in slots]
        m, l, acc = attend(pages, True, ck, None, m, l, acc)
        return m, l, acc, ccar

    init = (jnp.full((rows, 1), -jnp.inf, F32), jnp.zeros((rows, 1), F32), jnp.zeros((rows, HEAD_DIM), F32),
            jnp.zeros((nh, 1), F32))
    m, l, acc, ccar = lax.fori_loop(0, n_pages // PAGE_GROUP, pair_step, init)

    pad = jnp.zeros((PAGE - DEC_SEQ, HEAD_DIM), BF16)
    kn = kn_ref[...].astype(BF16)
    vn = vn_ref[...].astype(BF16)
    k_heads = [jnp.concatenate([kn[:, h * HEAD_DIM:(h + 1) * HEAD_DIM], pad], axis=0) for h in range(nh)]
    v_heads = [jnp.concatenate([vn[:, h * HEAD_DIM:(h + 1) * HEAD_DIM], pad], axis=0) for h in range(nh)]
    ck, _ = cumulate(lfn_ref[0], ccar)
    rr = lax.broadcasted_iota(jnp.int32, (rows, PAGE), 0)
    cc = lax.broadcasted_iota(jnp.int32, (rows, PAGE), 1)
    m, l, acc = attend([(k_heads, v_heads)], False, ck, cc <= (rr & (DEC_SEQ - 1)), m, l, acc)
    o = acc / l
    on = o * lax.rsqrt(jnp.mean(o * o, axis=-1, keepdims=True) + RMS_EPS)
    out = jnp.concatenate([on[h * DEC_SEQ:(h + 1) * DEC_SEQ, :] for h in range(nh)], axis=1)
    o_ref[...] = out * g_ref[...] * jax.nn.sigmoid(og_ref[...])


def _paged_sample(page_table, qn, kn, v, lf_new_t, cache_k, cache_v, lf_pages_t, z, g_fox_o, o_f, row0):
    db, n_pages = page_table.shape
    assert n_pages % PAGE_GROUP == 0 and n_pages >= PAGE_BUFS
    nh = WIDTH // HEAD_DIM
    rb = row0 // DEC_SEQ
    newmap = lambda b, pt: (b, 0)
    hbm = pl.BlockSpec(memory_space=pl.ANY)
    gs = pltpu.PrefetchScalarGridSpec(
        num_scalar_prefetch=1,
        grid=(db,),
        in_specs=[pl.BlockSpec((DEC_SEQ, WIDTH), newmap),
                  pl.BlockSpec((DEC_SEQ, WIDTH), newmap),
                  pl.BlockSpec((DEC_SEQ, WIDTH), newmap),
                  pl.BlockSpec((1, nh, PAGE), lambda b, pt: (b, 0, 0)),
                  pl.BlockSpec((DEC_SEQ, WIDTH), lambda b, pt: (rb + b, C_OG // WIDTH)),
                  pl.BlockSpec((1, WIDTH), lambda b, pt: (0, 0)),
                  hbm, hbm, hbm, hbm],
        out_specs=pl.BlockSpec((DEC_SEQ, WIDTH), lambda b, pt: (rb + b, 0)),
        scratch_shapes=[pltpu.VMEM((PAGE_BUFS, nh, HEAD_DIM, PAGE), F32), pltpu.VMEM((PAGE_BUFS, nh, HEAD_DIM, PAGE), F32),
                        pltpu.VMEM((PAGE_BUFS, nh, PAGE), F32), pltpu.SemaphoreType.DMA((3, PAGE_BUFS))],
    )
    return pl.pallas_call(
        functools.partial(_paged_kernel, n_pages=n_pages),
        grid_spec=gs,
        out_shape=jax.ShapeDtypeStruct(o_f.shape, F32),
        input_output_aliases={10: 0},
        compiler_params=_cp(("arbitrary",)),
        name="paged_sample",
    )(page_table, qn, kn, v, lf_new_t, z, g_fox_o, cache_k, cache_v, lf_pages_t, o_f)


def _prep_kernel(z_ref, first_ref, mu_ref, w12_ref, w3_ref, w0_ref, a0_ref, kk_ref, ka_ref, rk_ref,
                 g_hbm, bonus_hbm, o_r, o_w, o_k, o_v, o_a, o_b, o_g, o_bonus, carry, *, seq_len, tiles_per_seq):
    z = z_ref[...]
    tm = z.shape[0]
    row = lax.broadcasted_iota(jnp.int32, (tm, 1), 0)
    rolled = pltpu.roll(z, 1, 0)
    if tiles_per_seq >= 1 and seq_len >= tm:
        li = pl.program_id(0) % tiles_per_seq

        @pl.when(li == 0)
        def _():
            carry[...] = first_ref[0]

        zp = jnp.where(row == 0, carry[...], rolled)
        carry[...] = z[tm - 1:tm, :]
    else:
        nseq = tm // seq_len
        first = first_ref[...]
        exp = jnp.broadcast_to(first, (nseq, seq_len, first.shape[-1])).reshape(tm, first.shape[-1])
        zp = jnp.where((row & (seq_len - 1)) == 0, exp, rolled)
    zm = z + (zp - z) * mu_ref[...]
    r = zm[:, 0:WIDTH]
    k = zm[:, WIDTH:2 * WIDTH]
    v = zm[:, 2 * WIDTH:3 * WIDTH]
    lo = zm[:, C_LORA:C_LORA + 384]
    lane = lax.broadcasted_iota(jnp.int32, (tm, 384), 1)
    act = jnp.where(lane < 64, jnp.tanh(lo), jnp.where(lane < 128, lo, jax.nn.sigmoid(lo))).astype(BF16)
    l12 = _dot(act[:, 0:128], w12_ref[...])
    g = _dot(act[:, 128:384], w3_ref[...])
    log_w = -_softplus(-(w0_ref[...] + l12[:, 0:WIDTH])) - 0.5
    decay = jnp.exp(-jnp.exp(log_w))
    asig = jax.nn.sigmoid(a0_ref[...] + l12[:, WIDTH:2 * WIDTH])
    bd = _block_diag_ones()
    kk = k * kk_ref[...]
    kk = kk / jnp.maximum(jnp.sqrt(_segsum(kk * kk, bd)), 1e-12)
    kf = k * (1.0 + (asig - 1.0) * ka_ref[...])
    o_r[...] = r
    o_w[...] = decay
    o_k[...] = kf
    o_v[...] = v
    o_a[...] = -kk
    o_b[...] = kk * asig
    o_g[...] = g
    o_bonus[...] = _segsum(r * kf * rk_ref[...], bd) * v


def _rwkv_prep(z, first, mu, w12, w3, w0, a0, k_k, k_a, r_k, row0, n_rows, seq_len, shared):
    if seq_len >= 256:
        tm = _tile(seq_len, 256)
        first_spec = pl.BlockSpec((1, 1, RWKV_COLS), lambda i: (i // (seq_len // tm), 0, 0))
    else:
        tm = _tile(n_rows, 128)
        nseq = tm // seq_len
        first_spec = pl.BlockSpec((nseq, 1, RWKV_COLS), lambda i: (i, 0, 0))
    rb = row0 // tm
    vec = lambda n: pl.BlockSpec((1, n), lambda i: (0, 0))
    out = pl.BlockSpec((tm, WIDTH), lambda i: (i, 0))
    out_all = pl.BlockSpec((tm, WIDTH), lambda i: (rb + i, 0))
    hbm = pl.BlockSpec(memory_space=pl.ANY)
    extra = tuple(shared)
    return pl.pallas_call(
        functools.partial(_prep_kernel, seq_len=seq_len, tiles_per_seq=max(seq_len // tm, 1)),
        grid=(n_rows // tm,),
        in_specs=[pl.BlockSpec((tm, RWKV_COLS), lambda i: (rb + i, 0)), first_spec, vec(RWKV_COLS),
                  pl.BlockSpec((128, 2 * WIDTH), lambda i: (0, 0)), pl.BlockSpec((256, WIDTH), lambda i: (0, 0)),
                  vec(WIDTH), vec(WIDTH), vec(WIDTH), vec(WIDTH), vec(WIDTH)] + [hbm] * len(extra),
        out_specs=[out] * 6 + [out_all] * 2,
        out_shape=[jax.ShapeDtypeStruct((n_rows, WIDTH), F32)] * 6 + [jax.ShapeDtypeStruct((z.shape[0], WIDTH), F32)] * 2,
        input_output_aliases={10: 6, 11: 7},
        scratch_shapes=[pltpu.VMEM((1, RWKV_COLS), F32)],
        compiler_params=_cp(("arbitrary",)),
        name=f"rwkv_prep_{seq_len}",
    )(z, first, mu, w12, w3, w0, a0, k_k, k_a, r_k, *extra)


SCAN_SUB = 64
SCAN_UNROLL = 4


def _wkv_kernel(r_ref, w_ref, k_ref, v_ref, a_ref, b_ref, s0_ref, y_ref, sT_ref, S_s, Z_s, *, groups, n_pairs, tb_len):
    tb = pl.program_id(1)
    r2 = lax.broadcasted_iota(jnp.int32, (2 * LANES, 2 * LANES), 0)
    c2 = lax.broadcasted_iota(jnp.int32, (2 * LANES, 2 * LANES), 1)
    ones_bd = jnp.where((r2 >> 6) == (c2 >> 6), 1.0, 0.0).astype(BF16)
    vrow = lax.broadcasted_iota(jnp.int32, (HEAD_DIM, LANES), 0)
    lane = lax.broadcasted_iota(jnp.int32, (HEAD_DIM, LANES), 1)
    eye2 = (lane & (HEAD_DIM - 1)) == vrow
    zero_half = jnp.zeros((HEAD_DIM, LANES), BF16)
    H = HEAD_DIM

    @pl.when(tb == 0)
    def _():
        for g in range(groups):
            for p in range(n_pairs):
                S_s[g * n_pairs + p] = jnp.concatenate([s0_ref[g, 2 * p], s0_ref[g, 2 * p + 1]], axis=1)
        Z_s[...] = jnp.zeros_like(Z_s)

    def lanes(p):
        return slice(p * LANES, (p + 1) * LANES)

    def run(t0, n):
        def step(i, carry):
            t = t0 + i
            tp = jnp.maximum(t - 1, 0)
            hit = (lane & (SCAN_SUB - 1)) == (i - 1)
            for g in range(groups):
                aa = a_ref[g, pl.ds(t, 1), :]
                ww = w_ref[g, pl.ds(t, 1), :]
                kr = k_ref[g, pl.ds(t, 1), :]
                vv = v_ref[g, pl.ds(t, 1), :]
                bb = b_ref[g, pl.ds(t, 1), :]
                rp = r_ref[g, pl.ds(tp, 1), :]
                lhs_s, xs = [], []
                for p in range(n_pairs):
                    S = S_s[g * n_pairs + p]
                    lhs_s.append(jnp.concatenate([(S * aa[:, lanes(p)]).astype(BF16),
                                                  (S * rp[:, lanes(p)]).astype(BF16)], axis=1))
                    xs.append(jnp.where(eye2, vv[:, lanes(p)], 0.0).astype(BF16))
                lhs_x = [jnp.concatenate(xs[p:p + 2], axis=1) for p in range(0, n_pairs, 2)]
                res = _dot(jnp.concatenate(lhs_s + lhs_x, axis=0), ones_bd)
                for p in range(n_pairs):
                    idx = g * n_pairs + p
                    rs = res[p * H:(p + 1) * H]
                    vb = res[(n_pairs + p // 2) * H:(n_pairs + p // 2 + 1) * H, (p % 2) * LANES:(p % 2 + 1) * LANES]
                    S_s[idx] = S_s[idx] * ww[:, lanes(p)] + rs[:, :LANES] * bb[:, lanes(p)] + vb * kr[:, lanes(p)]
                    Z_s[idx] = jnp.where(hit, rs[:, LANES:], Z_s[idx])
            return carry

        lax.fori_loop(0, n, step, 0, unroll=SCAN_UNROLL)
        hit_last = (lane & (SCAN_SUB - 1)) == (n - 1)
        for g in range(groups):
            rl = r_ref[g, pl.ds(t0 + n - 1, 1), :]
            lhs = [jnp.concatenate([(S_s[g * n_pairs + p] * rl[:, lanes(p)]).astype(BF16), zero_half], axis=1)
                   for p in range(n_pairs)]
            res = _dot(jnp.concatenate(lhs, axis=0), ones_bd)
            for p in range(n_pairs):
                idx = g * n_pairs + p
                zt = jnp.where(hit_last, res[p * H:(p + 1) * H, :LANES], Z_s[idx]).T
                y_ref[g, pl.ds(t0, n), p * LANES:p * LANES + H] = zt[0:n, :]
                y_ref[g, pl.ds(t0, n), p * LANES + H:(p + 1) * LANES] = zt[SCAN_SUB:SCAN_SUB + n, :]

    if tb_len <= SCAN_SUB:
        run(0, tb_len)
    else:
        def outer(blk, carry):
            run(pl.multiple_of(blk * SCAN_SUB, SCAN_SUB), SCAN_SUB)
            return carry
        lax.fori_loop(0, tb_len // SCAN_SUB, outer, 0)

    @pl.when(tb == pl.num_programs(1) - 1)
    def _():
        for g in range(groups):
            for p in range(n_pairs):
                S = S_s[g * n_pairs + p]
                sT_ref[g, 2 * p] = S[:, 0:H]
                sT_ref[g, 2 * p + 1] = S[:, H:LANES]


def _wkv_scan(r, w, k, v, a, b, s0, n_seq, seq_len):
    nh = WIDTH // HEAD_DIM
    groups = 2 if n_seq % 2 == 0 else 1
    tb_len = min(seq_len, 128)
    nt = seq_len // tb_len
    blk = pl.BlockSpec((groups, tb_len, WIDTH), lambda s, t: (s, t, 0))
    st = pl.BlockSpec((groups, nh, HEAD_DIM, HEAD_DIM), lambda s, t: (s, 0, 0, 0))
    n_pairs = WIDTH // LANES
    seq3 = lambda x: x.reshape(n_seq, seq_len, WIDTH)
    y, s_new = pl.pallas_call(
        functools.partial(_wkv_kernel, groups=groups, n_pairs=n_pairs, tb_len=tb_len),
        grid=(n_seq // groups, nt),
        in_specs=[blk] * 6 + [st],
        out_specs=[blk, st],
        out_shape=[jax.ShapeDtypeStruct((n_seq, seq_len, WIDTH), F32),
                   jax.ShapeDtypeStruct((n_seq, nh, HEAD_DIM, HEAD_DIM), F32)],
        scratch_shapes=[pltpu.VMEM((groups * n_pairs, HEAD_DIM, LANES), F32),
                        pltpu.VMEM((groups * n_pairs, HEAD_DIM, LANES), F32)],
        compiler_params=_cp(("arbitrary", "arbitrary")),
        name=f"wkv_scan_{seq_len}",
    )(seq3(r), seq3(w), seq3(k), seq3(v), seq3(a), seq3(b), s0)
    return y.reshape(n_seq * seq_len, WIDTH), s_new


def _outproj_kernel(x_ref, y_ref, bonus_ref, g_ref, of_ref, lw_ref, lb_ref, wo_ref, gf_ref, wrh_ref, wrl_ref, br_ref,
                    x1_ref, h_ref, route_ref):
    bd = _block_diag_ones()
    y = y_ref[...]
    mu = _segsum(y, bd) * (1.0 / HEAD_DIM)
    d = y - mu
    var = _segsum(d * d, bd) * (1.0 / HEAD_DIM)
    yn = d * lax.rsqrt(var + LNX_EPS) * lw_ref[...] + lb_ref[...]
    o_r = ((yn + bonus_ref[...]) * g_ref[...]).astype(BF16)
    o_f = of_ref[...].astype(BF16)
    x1 = x_ref[...] + _dot(o_r, wo_ref[0:WIDTH, :]) + _dot(o_f, wo_ref[WIDTH:2 * WIDTH, :])
    x1_ref[...] = x1
    h = _rms_rows(x1, gf_ref[...])
    h_ref[...] = h
    hi, lo = _split2(h)
    logits = _dot(hi, wrh_ref[...]) + _dot(lo, wrh_ref[...]) + _dot(hi, wrl_ref[...]) + br_ref[...]
    tm = logits.shape[0]
    lane = lax.broadcasted_iota(jnp.int32, (tm, LANES), 1)
    neg = -jnp.inf
    lg = jnp.where(lane < N_GROUPS, logits, neg)
    mg = jnp.max(lg, axis=-1, keepdims=True)
    pg_top = 1.0 / jnp.sum(jnp.exp(lg - mg), axis=-1, keepdims=True)
    g_sel = jnp.min(jnp.where(lg == mg, lane, LANES), axis=-1, keepdims=True)
    in_grp = (lane >= N_GROUPS) & (lane < N_GROUPS + N_EXPERTS) & (((lane - N_GROUPS) >> 3) == g_sel)
    le = jnp.where(in_grp, logits, neg)
    m1 = jnp.max(le, axis=-1, keepdims=True)
    i1 = jnp.min(jnp.where(le == m1, lane, LANES), axis=-1, keepdims=True)
    le2 = jnp.where(lane == i1, neg, le)
    m2 = jnp.max(le2, axis=-1, keepdims=True)
    i2 = jnp.min(jnp.where(le2 == m2, lane, LANES), axis=-1, keepdims=True)
    e2 = jnp.exp(m2 - m1)
    gate1 = pg_top / (1.0 + e2)
    gate2 = pg_top * e2 / (1.0 + e2)
    route = jnp.where(lane == 0, (i1 - N_GROUPS).astype(F32),
                      jnp.where(lane == 1, (i2 - N_GROUPS).astype(F32),
                                jnp.where(lane == 2, gate1, jnp.where(lane == 3, gate2, 0.0))))
    route_ref[...] = route


def _outproj(x, y, bonus, g, o_f, lnx_w, lnx_b, w_out, g_ffn, wr_hi, wr_lo, b_r):
    m, d = x.shape
    tm = _tile(m, 256)
    row = lambda n: pl.BlockSpec((tm, n), lambda i: (i, 0))
    vec = lambda n: pl.BlockSpec((1, n), lambda i: (0, 0))
    full = lambda a: pl.BlockSpec(a.shape, lambda i: (0, 0))
    return pl.pallas_call(
        _outproj_kernel,
        grid=(m // tm,),
        in_specs=[row(d), row(WIDTH), row(WIDTH), row(WIDTH), row(WIDTH), vec(WIDTH), vec(WIDTH),
                  full(w_out), vec(d), full(wr_hi), full(wr_lo), vec(LANES)],
        out_specs=[row(d), row(d), row(LANES)],
        out_shape=[jax.ShapeDtypeStruct((m, d), F32), jax.ShapeDtypeStruct((m, d), F32),
                   jax.ShapeDtypeStruct((m, LANES), F32)],
        compiler_params=_cp(("arbitrary",)),
        name="outproj_router",
    )(x, y, bonus, g, o_f, lnx_w, lnx_b, w_out, g_ffn, wr_hi, wr_lo, b_r)


def _expert_kernel(plan_ref, tok_ref, nu_ref, h_hbm, wg_hbm, wu_hbm, wd_hbm, o_ref, xbuf, wg, wu, wd, sem, wsem):
    i = pl.program_id(0)
    n_used = nu_ref[0]
    first, nxt, wslot = plan_ref[1, i], plan_ref[2, i], plan_ref[3, i]

    def row_copy(tok, r, slot):
        return pltpu.make_async_copy(h_hbm.at[pl.ds(tok, 1), :], xbuf.at[slot, pl.ds(r, 1), :], sem.at[slot])

    def gather(blk, slot):
        def start(r, c):
            row_copy(tok_ref[blk * MOE_BLOCK + r], r, slot).start()
            return c
        lax.fori_loop(0, MOE_BLOCK, start, 0, unroll=8)

    def weight_copies(expert, slot):
        return (pltpu.make_async_copy(wg_hbm.at[expert], wg.at[slot], wsem.at[0, slot]),
                pltpu.make_async_copy(wu_hbm.at[expert], wu.at[slot], wsem.at[1, slot]),
                pltpu.make_async_copy(wd_hbm.at[expert], wd.at[slot], wsem.at[2, slot]))

    @pl.when(i == 0)
    def _():
        gather(0, 0)
        for c in weight_copies(plan_ref[0, 0], 0):
            c.start()

    @pl.when(i < n_used)
    def _():
        slot = i & 1

        @pl.when(first == 1)
        def _():
            for c in weight_copies(0, wslot):
                c.wait()

            @pl.when(nxt >= 0)
            def _():
                for c in weight_copies(nxt, 1 - wslot):
                    c.start()

        def wait(r, c):
            row_copy(0, r, slot).wait()
            return c
        lax.fori_loop(0, MOE_BLOCK, wait, 0, unroll=8)

        @pl.when(i + 1 < n_used)
        def _():
            gather(i + 1, 1 - slot)

        x = xbuf[slot].astype(BF16)
        gate = _dot(x, wg[wslot].astype(BF16))
        up = _dot(x, wu[wslot].astype(BF16))
        act = (gate * jax.nn.sigmoid(gate) * up).astype(BF16)
        o_ref[...] = _dot(act, wd[wslot].astype(BF16))

    @pl.when(i >= n_used)
    def _():
        o_ref[...] = jnp.zeros_like(o_ref)


def _experts(plan, slot_tok, n_used, h, w_gate, w_up, w_down):
    n_blocks = plan.shape[1]
    d = h.shape[1]
    de = w_gate.shape[2]
    hbm = pl.BlockSpec(memory_space=pl.ANY)
    gs = pltpu.PrefetchScalarGridSpec(
        num_scalar_prefetch=3,
        grid=(n_blocks,),
        in_specs=[hbm, hbm, hbm, hbm],
        out_specs=pl.BlockSpec((MOE_BLOCK, d), lambda i, plan, tok, nu: (i, 0)),
        scratch_shapes=[pltpu.VMEM((2, MOE_BLOCK, d), F32), pltpu.VMEM((2, d, de), F32), pltpu.VMEM((2, d, de), F32),
                        pltpu.VMEM((2, de, d), F32), pltpu.SemaphoreType.DMA((2,)), pltpu.SemaphoreType.DMA((3, 2))],
    )
    return pl.pallas_call(
        _expert_kernel,
        grid_spec=gs,
        out_shape=jax.ShapeDtypeStruct((n_blocks * MOE_BLOCK, d), F32),
        compiler_params=_cp(("arbitrary",), 58 * 1024 * 1024),
        name="experts",
    )(plan, slot_tok, n_used, h, w_gate, w_up, w_down)


def _ple_kernel(dest_ref, x1_ref, route_ref, p_ref, yb_hbm, gp_ref, wpg_ref, bpg_ref, wpp_ref, gfin_ref, op_ref, os_ref,
                buf, sem, *, prompt_tiles):
    i = pl.program_id(0)
    tm = x1_ref.shape[0]

    def row_copy(slot_row, r, k, half):
        return pltpu.make_async_copy(yb_hbm.at[pl.ds(slot_row, 1), :], buf.at[half, k, pl.ds(r, 1), :], sem.at[half])

    def gather(tile, half):
        def start(r, c):
            for k in range(2):
                row_copy(dest_ref[(tile * tm + r) * 2 + k], r, k, half).start()
            return c
        lax.fori_loop(0, tm, start, 0, unroll=8)

    @pl.when(i == 0)
    def _():
        gather(0, 0)

    half = i & 1

    def wait(r, c):
        for k in range(2):
            row_copy(0, r, k, half).wait()
        return c
    lax.fori_loop(0, tm, wait, 0, unroll=8)

    @pl.when(i + 1 < pl.num_programs(0))
    def _():
        gather(i + 1, 1 - half)

    route = route_ref[...]
    moe = buf[half, 0] * route[:, 2:3] + buf[half, 1] * route[:, 3:4]
    x2 = x1_ref[...] + moe
    hn = _rms_rows(x2, gp_ref[...]).astype(BF16)
    gate = jax.nn.sigmoid(_dot(hn, wpg_ref[...]) + bpg_ref[...])
    pe = _dot(p_ref[...].astype(BF16), wpp_ref[...])
    x3 = x2 + gate * pe
    y = _rms_rows(x3, gfin_ref[...])

    @pl.when(i < prompt_tiles)
    def _():
        op_ref[...] = y

    @pl.when(i >= prompt_tiles)
    def _():
        os_ref[...] = y


def _ple(dest, x1, route, p, yb, g_ple, w_pg, b_pg, w_pp, g_final, n_p):
    m, d = x1.shape
    tm = _tile(math.gcd(n_p, m - n_p), 256)
    npt = n_p // tm
    pd = p.shape[1]
    gs = pltpu.PrefetchScalarGridSpec(
        num_scalar_prefetch=1,
        grid=(m // tm,),
        in_specs=[pl.BlockSpec((tm, d), lambda i, ds: (i, 0)),
                  pl.BlockSpec((tm, LANES), lambda i, ds: (i, 0)),
                  pl.BlockSpec((tm, pd), lambda i, ds: (i, 0)),
                  pl.BlockSpec(memory_space=pl.ANY),
                  pl.BlockSpec((1, d), lambda i, ds: (0, 0)),
                  pl.BlockSpec((d, d), lambda i, ds: (0, 0)),
                  pl.BlockSpec((1, d), lambda i, ds: (0, 0)),
                  pl.BlockSpec((pd, d), lambda i, ds: (0, 0)),
                  pl.BlockSpec((1, d), lambda i, ds: (0, 0))],
        out_specs=[pl.BlockSpec((tm, d), lambda i, ds: (jnp.minimum(i, npt - 1), 0)),
                   pl.BlockSpec((tm, d), lambda i, ds: (jnp.maximum(i - npt, 0), 0))],
        scratch_shapes=[pltpu.VMEM((2, 2, tm, d), F32), pltpu.SemaphoreType.DMA((2,))],
    )
    return pl.pallas_call(
        functools.partial(_ple_kernel, prompt_tiles=npt),
        grid_spec=gs,
        out_shape=[jax.ShapeDtypeStruct((n_p, d), F32), jax.ShapeDtypeStruct((m - n_p, d), F32)],
        compiler_params=_cp(("arbitrary",)),
        name="moe_combine_ple",
    )(dest, x1, route, p, yb, g_ple, w_pg, b_pg, w_pp, g_final)


def _dispatch(route, n_tok):
    expert = route[:, 0:2].astype(jnp.int32)
    flat_e = expert.reshape(-1)
    n_assign = flat_e.shape[0]
    onehot = (flat_e[:, None] == jnp.arange(N_EXPERTS, dtype=jnp.int32)[None, :]).astype(jnp.int32)
    csum = jnp.cumsum(onehot, axis=0)
    rank = jnp.sum(csum * onehot, axis=1) - 1
    counts = csum[-1]
    pcounts = (counts + MOE_BLOCK - 1) // MOE_BLOCK * MOE_BLOCK
    pend = jnp.cumsum(pcounts)
    pstart = pend - pcounts
    dest = (pstart[flat_e] + rank).astype(jnp.int32)
    n_blocks = -(-n_assign // MOE_BLOCK) + N_EXPERTS
    slot_tok = jnp.zeros((n_blocks * MOE_BLOCK,), jnp.int32).at[dest].set(jnp.arange(n_assign, dtype=jnp.int32) // 2)
    n_used = (pend[-1] // MOE_BLOCK).astype(jnp.int32)
    blk = jnp.arange(n_blocks, dtype=jnp.int32)
    block_e = jnp.searchsorted(pend, jnp.minimum(blk, n_used - 1) * MOE_BLOCK, side='right').astype(jnp.int32)
    block_e = jnp.minimum(block_e, N_EXPERTS - 1)
    first = jnp.concatenate([jnp.ones((1,), jnp.int32), (block_e[1:] != block_e[:-1]).astype(jnp.int32)])
    first = jnp.where(blk < n_used, first, 0)
    used = counts > 0
    eidx = jnp.arange(N_EXPERTS, dtype=jnp.int32)
    later = jnp.flip(lax.cummin(jnp.flip(jnp.where(used, eidx, N_EXPERTS))))
    next_used = jnp.concatenate([later[1:], jnp.full((1,), N_EXPERTS, jnp.int32)])
    next_used = jnp.where(next_used < N_EXPERTS, next_used, -1)
    ordinal = jnp.cumsum(used.astype(jnp.int32)) - 1
    plan = jnp.stack([block_e, first, next_used[block_e], ordinal[block_e] & 1]).astype(jnp.int32)
    return plan, slot_tok, n_used.reshape(1), dest


def kernel(x_prompt, x_sample, cache_k, cache_v, cache_logf, state_wkv, state_shift, page_table, p_prompt, p_sample, g_attn, w_in, mu_shift, w0, w_up, a0, a_up, g_up, k_k, k_a, r_k, lnx_w, lnx_b, b_f, q_norm, k_norm, g_fox_o, w_out, g_ffn, w_rg, b_rg, w_re, b_re, w_e_gate, w_e_up, w_e_down, g_ple, w_pg, b_pg, w_pp, g_final):
    depth = g_attn.shape[0]
    assert depth == 1
    batch, seq, d = x_prompt.shape
    db, dec_seq, _ = x_sample.shape
    assert dec_seq == DEC_SEQ
    nh = WIDTH // HEAD_DIM
    n_p = batch * seq
    n_s = db * dec_seq
    m = n_p + n_s
    rwkv_in = mu_shift.shape[1]
    n_pool = cache_k.shape[1]
    n_pages = page_table.shape[1]
    row = lambda a: a.reshape(1, -1)

    x_all = jnp.concatenate([x_prompt.reshape(n_p, d), x_sample.reshape(n_s, d)], axis=0)
    p_all = jnp.concatenate([p_prompt[0].reshape(n_p, -1), p_sample[0].reshape(n_s, -1)], axis=0)

    wi = w_in[0]
    zc = lambda n: jnp.zeros((d, n), F32)
    w_z = jnp.concatenate([wi[:, :rwkv_in], zc(C_F - rwkv_in), wi[:, rwkv_in + 4 * WIDTH:], zc(C_Q - C_F - nh),
                           wi[:, rwkv_in:rwkv_in + 4 * WIDTH]], axis=1).astype(BF16)
    w12 = jnp.zeros((128, 2 * WIDTH), F32).at[0:64, 0:WIDTH].set(w_up[0]).at[64:128, WIDTH:].set(a_up[0]).astype(BF16)
    w3 = jnp.zeros((256, WIDTH), F32).at[0:g_up.shape[1]].set(g_up[0]).astype(BF16)
    w_r = jnp.zeros((d, LANES), F32).at[:, 0:N_GROUPS].set(w_rg[0]).at[:, N_GROUPS:N_GROUPS + N_EXPERTS].set(w_re[0])
    wr_hi = w_r.astype(BF16)
    wr_lo = (w_r - wr_hi.astype(F32)).astype(BF16)
    b_r = jnp.zeros((1, LANES), F32).at[0, 0:N_GROUPS].set(b_rg[0]).at[0, N_GROUPS:N_GROUPS + N_EXPERTS].set(b_re[0])
    pad_cols = lambda a, n: jnp.pad(a, ((0, 0), (0, n - a.shape[1])))
    mu_pad = pad_cols(mu_shift, RWKV_COLS)
    bf_pad = pad_cols(b_f, LANES)
    qn_t = jnp.tile(q_norm, (1, nh))
    kn_t = jnp.tile(k_norm, (1, nh))

    z = _inproj(x_all, g_attn, w_z)

    qn_s, kn_s, vv_s, logf, qa, ka, vt, kt_p, vt_p = _foxpost(z, qn_t, kn_t, bf_pad, batch, seq)

    o_f = _flash_prompt(qa, ka, vt, z, g_fox_o, batch, seq)
    lf_pages_t = cache_logf[0].transpose(0, 2, 1)
    lf_new_t = pad_cols(logf[n_p:, :nh].reshape(db, dec_seq, nh).transpose(0, 2, 1).reshape(db * nh, dec_seq), PAGE)
    o_f = _paged_sample(page_table, qn_s, kn_s, vv_s, lf_new_t.reshape(db, nh, PAGE),
                        cache_k[0].transpose(0, 2, 3, 1), cache_v[0].transpose(0, 2, 3, 1), lf_pages_t, z, g_fox_o, o_f, n_p)

    first_p = jnp.zeros((batch, 1, RWKV_COLS), F32)
    first_s = pad_cols(state_shift[0], RWKV_COLS).reshape(db, 1, RWKV_COLS)
    prep_args = (mu_pad, w12, w3, w0, a0, k_k, k_a, row(r_k))
    prep_p = _rwkv_prep(z, first_p, *prep_args, 0, n_p, seq, shared=(jnp.zeros((m, WIDTH), F32),) * 2)
    prep_s = _rwkv_prep(z, first_s, *prep_args, n_p, n_s, dec_seq, shared=prep_p[6:8])
    g_, bonus = prep_s[6:8]
    y_p, wkv_p = _wkv_scan(*prep_p[:6], jnp.zeros((batch, nh, HEAD_DIM, HEAD_DIM), F32), batch, seq)
    y_s, wkv_s = _wkv_scan(*prep_s[:6], state_wkv[0], db, dec_seq)
    y = jnp.concatenate([y_p, y_s], axis=0)

    x1, h2, route = _outproj(x_all, y, bonus, g_, o_f, lnx_w, lnx_b, w_out[0].astype(BF16), g_ffn, wr_hi, wr_lo, b_r)

    plan, slot_tok, n_used, dest = _dispatch(route, m)
    yb = _experts(plan, slot_tok, n_used, h2, w_e_gate[0], w_e_up[0], w_e_down[0])

    y_out_p, y_out_s = _ple(dest, x1, route, p_all, yb, g_ple, w_pg[0].astype(BF16), b_pg, w_pp[0].astype(BF16),
                            row(g_final), n_p)

    shift_p = jnp.concatenate([z[(b + 1) * seq - 1:(b + 1) * seq, :rwkv_in] for b in range(batch)], axis=0)
    shift_s = z[n_p + dec_seq - 1::dec_seq, :rwkv_in]
    heads_t = lambda a: a.reshape(batch, nh, HEAD_DIM, seq).transpose(0, 3, 1, 2)[None]
    heads = lambda a: a.reshape(1, db, dec_seq, nh, HEAD_DIM)
    return (y_out_p.reshape(batch, seq, d), y_out_s.reshape(db, dec_seq, d),
            heads_t(kt_p), heads_t(vt_p), logf[:n_p, :nh].reshape(1, batch, seq, nh),
            wkv_p[None], shift_p[None],
            heads(kn_s), heads(vv_s), logf[n_p:, :nh].reshape(1, db, dec_seq, nh),
            wkv_s[None], shift_s[None])
```

```python
import functools
import math

import jax
import jax.numpy as jnp
import numpy as np
from jax import lax
from jax.experimental import pallas as pl
from jax.experimental.pallas import tpu as pltpu

F32 = jnp.float32
BF16 = jnp.bfloat16

HEAD_DIM = 64
LANES = 128
RMS_EPS = 1e-6
LNX_EPS = 64e-5
PAGE = 128
MOE_BLOCK = 384
N_GROUPS = 8
N_EXPERTS = 64
DEC_SEQ = 8

C_LORA = 3072
C_F = 3456
RWKV_COLS = 3584
C_Q, C_FK, C_FV, C_OG = 4096, 5120, 6144, 7168
Z_COLS = 8192
WIDTH = 1024

VMEM_LIMIT = 52 * 1024 * 1024


def _cp(sem, vmem=VMEM_LIMIT):
    return pltpu.CompilerParams(dimension_semantics=sem, vmem_limit_bytes=vmem)


def _tile(n, pref):
    for t in (1024, 512, 256, 128, 64, 32, 16, 8):
        if t <= pref and n % t == 0:
            return t
    raise ValueError(f"no tile for {n}")


def _split2(x):
    hi = x.astype(BF16)
    lo = (x - hi.astype(F32)).astype(BF16)
    return hi, lo


def _split3(x):
    hi = x.astype(BF16)
    r1 = x - hi.astype(F32)
    mid = r1.astype(BF16)
    lo = (r1 - mid.astype(F32)).astype(BF16)
    return hi, mid, lo


def _dot(a, b):
    return jnp.dot(a, b, preferred_element_type=F32)


def _dot_nt(a, b):
    return lax.dot_general(a, b, (((1,), (1,)), ((), ())), preferred_element_type=F32)


def _block_diag_ones():
    r = lax.broadcasted_iota(jnp.int32, (LANES, LANES), 0)
    c = lax.broadcasted_iota(jnp.int32, (LANES, LANES), 1)
    return jnp.where((r >> 6) == (c >> 6), 1.0, 0.0).astype(BF16)


def _segsum(x, bd):
    outs = []
    for j in range(x.shape[1] // LANES):
        hi, lo = _split2(x[:, j * LANES:(j + 1) * LANES])
        outs.append(_dot(hi, bd) + _dot(lo, bd))
    return outs[0] if len(outs) == 1 else jnp.concatenate(outs, axis=1)


def _softplus(x):
    return jnp.maximum(x, 0.0) + jnp.log1p(jnp.exp(-jnp.abs(x)))


def _rms_rows(x, g):
    return x * lax.rsqrt(jnp.mean(x * x, axis=-1, keepdims=True) + RMS_EPS) * g


def _row_maps(prompt_tiles):
    return (lambda i, *_: (jnp.minimum(i, prompt_tiles - 1), 0)), (lambda i, *_: (jnp.maximum(i - prompt_tiles, 0), 0))


def _inproj_kernel(xp_ref, xs_ref, g_ref, w_ref, o_ref, h_ref, *, prompt_tiles):
    @pl.when(pl.program_id(1) == 0)
    def _():
        x = jnp.where(pl.program_id(0) < prompt_tiles, xp_ref[...], xs_ref[...])
        h_ref[...] = _rms_rows(x, g_ref[...]).astype(BF16)

    o_ref[...] = _dot(h_ref[...], w_ref[...])


def _inproj(x_p, x_s, g, w):
    (n_p, d), n_s = x_p.shape, x_s.shape[0]
    n = w.shape[1]
    tm, tn = _tile(math.gcd(n_p, n_s), 512), _tile(n, 1024)
    pmap, smap = _row_maps(n_p // tm)
    return pl.pallas_call(
        functools.partial(_inproj_kernel, prompt_tiles=n_p // tm),
        grid=((n_p + n_s) // tm, n // tn),
        in_specs=[pl.BlockSpec((tm, d), pmap),
                  pl.BlockSpec((tm, d), smap),
                  pl.BlockSpec((1, d), lambda i, j: (0, 0)),
                  pl.BlockSpec((d, tn), lambda i, j: (0, j))],
        out_specs=pl.BlockSpec((tm, tn), lambda i, j: (i, j)),
        out_shape=jax.ShapeDtypeStruct((n_p + n_s, n), F32),
        scratch_shapes=[pltpu.VMEM((tm, d), BF16)],
        compiler_params=_cp(("arbitrary", "arbitrary")),
        name="inproj",
    )(x_p, x_s, g, w)


AUG = 2 * HEAD_DIM


def _aug_constants():
    nh = WIDTH // HEAD_DIM
    pq = np.zeros((3 * LANES, WIDTH), np.float32)
    pk = np.zeros((3 * LANES, WIDTH), np.float32)
    one_q = np.zeros((1, WIDTH), np.float32)
    one_k = np.zeros((1, WIDTH), np.float32)
    for h in range(nh):
        for comp in range(3):
            pq[comp * LANES + h, h * HEAD_DIM + comp] = 1.0
            pk[comp * LANES + h, h * HEAD_DIM + 3 + comp] = -1.0
            one_q[0, h * HEAD_DIM + 3 + comp] = 1.0
            one_k[0, h * HEAD_DIM + comp] = 1.0
    return jnp.asarray(pq, BF16), jnp.asarray(pk, BF16), jnp.asarray(one_q), jnp.asarray(one_k)


def _foxpost_kernel(q_ref, k_ref, v_ref, f_ref, qn_ref, kn_ref, bf_ref, pq_ref, pk_ref, oneq_ref, onek_ref,
                    oq, ok, ov, olf, oqa, oka, ovt, okt32, ovt32, carry, *, tiles_per_seq, prompt_tiles):
    i = pl.program_id(0)
    bd = _block_diag_ones()
    nh = WIDTH // HEAD_DIM

    def head_norm(x, g):
        ms = _segsum(x * x, bd) * (1.0 / HEAD_DIM)
        return x * lax.rsqrt(ms + RMS_EPS) * g

    qn = head_norm(q_ref[...], qn_ref[...])
    kn = head_norm(k_ref[...], kn_ref[...])
    v = v_ref[...]
    vt = v.T
    ovt[...] = vt.astype(BF16)

    @pl.when(i < prompt_tiles)
    def _():
        okt32[0] = kn.T
        ovt32[0] = vt

    @pl.when(i >= prompt_tiles)
    def _():
        oq[...] = qn
        ok[...] = kn
        ov[...] = v

    lf = -_softplus(-(f_ref[...] + bf_ref[...]))
    olf[...] = lf

    @pl.when(i % tiles_per_seq == 0)
    def _():
        carry[...] = jnp.zeros_like(carry)

    tm = lf.shape[0]
    r = lax.broadcasted_iota(jnp.int32, (tm, tm), 0)
    c = lax.broadcasted_iota(jnp.int32, (tm, tm), 1)
    tri = jnp.where(c <= r, 1.0, 0.0).astype(BF16)
    hi, mid, lo = _split3(lf)
    cum = _dot(tri, hi) + _dot(tri, mid) + _dot(tri, lo) + carry[...]
    carry[...] = cum[tm - 1:tm, :]

    c3 = jnp.concatenate(_split3(cum), axis=1)
    aug_q = _dot(c3, pq_ref[...]) + oneq_ref[...]
    aug_k = _dot(c3, pk_ref[...]) + onek_ref[...]
    qs = qn * (HEAD_DIM ** -0.5)

    def interleave(x, aug):
        pieces = []
        for h in range(nh):
            sl = slice(h * HEAD_DIM, (h + 1) * HEAD_DIM)
            pieces += [x[:, sl], aug[:, sl]]
        return jnp.concatenate(pieces, axis=1).astype(BF16)

    oqa[...] = interleave(qs, aug_q)
    oka[...] = interleave(kn, aug_k)


def _foxpost(z, q_norm_t, k_norm_t, bf_pad, batch, seq_len):
    m = z.shape[0]
    nh = WIDTH // HEAD_DIM
    n_p = batch * seq_len
    tm = min(_tile(math.gcd(n_p, m - n_p), 512), seq_len)
    npt, tps = n_p // tm, seq_len // tm
    sample = pl.BlockSpec((tm, WIDTH), lambda i: (jnp.maximum(i - npt, 0), 0))
    prompt_t = pl.BlockSpec((1, WIDTH, tm), lambda i: (jnp.minimum(i, npt - 1) // tps, 0, jnp.minimum(i, npt - 1) % tps))
    row = lambda c: pl.BlockSpec((tm, WIDTH), lambda i, c=c: (i, c // WIDTH))
    vec = pl.BlockSpec((1, WIDTH), lambda i: (0, 0))
    small = pl.BlockSpec((tm, LANES), lambda i: (i, 0))
    wide = pl.BlockSpec((tm, nh * AUG), lambda i: (i, 0))
    place = pl.BlockSpec((3 * LANES, WIDTH), lambda i: (0, 0))
    return pl.pallas_call(
        functools.partial(_foxpost_kernel, tiles_per_seq=tps, prompt_tiles=npt),
        grid=(m // tm,),
        in_specs=[row(C_Q), row(C_FK), row(C_FV),
                  pl.BlockSpec((tm, LANES), lambda i: (i, C_F // LANES)),
                  vec, vec, pl.BlockSpec((1, LANES), lambda i: (0, 0)), place, place, vec, vec],
        out_specs=[sample] * 3 + [small, wide, wide, pl.BlockSpec((WIDTH, tm), lambda i: (0, i)), prompt_t, prompt_t],
        out_shape=[jax.ShapeDtypeStruct((m - n_p, WIDTH), F32)] * 3 + [jax.ShapeDtypeStruct((m, LANES), F32),
                                                                        jax.ShapeDtypeStruct((m, nh * AUG), BF16),
                                                                        jax.ShapeDtypeStruct((m, nh * AUG), BF16),
                                                                        jax.ShapeDtypeStruct((WIDTH, m), BF16),
                                                                        jax.ShapeDtypeStruct((batch, WIDTH, seq_len), F32),
                                                                        jax.ShapeDtypeStruct((batch, WIDTH, seq_len), F32)],
        scratch_shapes=[pltpu.VMEM((1, LANES), F32)],
        compiler_params=_cp(("arbitrary",)),
        name="foxpost",
    )(z, z, z, z, q_norm_t, k_norm_t, bf_pad, *_aug_constants())


def _flash_kernel(qi_tab, ki_tab, qa_ref, ka_ref, vt_ref, og_ref, g_ref, init_hbm, o_ref, m_s, l_s, acc_s):
    s_idx = pl.program_id(2)
    qi = qi_tab[s_idx]
    ki = ki_tab[s_idx]
    tq = qa_ref.shape[0]
    tk = ka_ref.shape[0]

    @pl.when(ki == 0)
    def _():
        m_s[...] = jnp.full_like(m_s, -jnp.inf)
        l_s[...] = jnp.zeros_like(l_s)
        acc_s[...] = jnp.zeros_like(acc_s)

    def update(masked):
        for hh in range(2):
            st = _dot_nt(ka_ref[:, hh * AUG:(hh + 1) * AUG], qa_ref[:, hh * AUG:(hh + 1) * AUG])
            if masked:
                r = lax.broadcasted_iota(jnp.int32, (tk, tq), 0)
                c = lax.broadcasted_iota(jnp.int32, (tk, tq), 1)
                st = jnp.where(r <= c, st, -jnp.inf)
            m_prev = m_s[hh]
            m_new = jnp.maximum(m_prev, jnp.max(st, axis=0, keepdims=True))
            alpha = jnp.exp(m_prev - m_new)
            p = jnp.exp(st - m_new)
            l_s[hh] = alpha * l_s[hh] + jnp.sum(p, axis=0, keepdims=True)
            acc_s[hh] = alpha * acc_s[hh] + _dot(vt_ref[hh * HEAD_DIM:(hh + 1) * HEAD_DIM, :], p.astype(BF16))
            m_s[hh] = m_new

    @pl.when(ki < qi)
    def _():
        update(False)

    @pl.when(ki == qi)
    def _():
        update(True)
        outs = []
        for hh in range(2):
            sl = slice(hh * HEAD_DIM, (hh + 1) * HEAD_DIM)
            ot = acc_s[hh] / l_s[hh]
            ont = ot * lax.rsqrt(jnp.mean(ot * ot, axis=0, keepdims=True) + RMS_EPS)
            outs.append(ont.T * g_ref[:, sl] * jax.nn.sigmoid(og_ref[:, sl]))
        o_ref[...] = jnp.concatenate(outs, axis=1)


def _flash_prompt(qa, ka, vt, z, g_fox_o, batch, seq):
    tq = _tile(seq, 512)
    nq = seq // tq
    pairs = [(a, b) for a in range(nq) for b in range(a + 1)]
    qi_tab = jnp.array([a for a, _ in pairs], jnp.int32)
    ki_tab = jnp.array([b for _, b in pairs], jnp.int32)
    npairs = WIDTH // LANES
    qmap = lambda b, p, s, qt, kt: (b * nq + qt[s], p)
    kmap = lambda b, p, s, qt, kt: (b * nq + kt[s], p)
    gs = pltpu.PrefetchScalarGridSpec(
        num_scalar_prefetch=2,
        grid=(batch, npairs, len(pairs)),
        in_specs=[pl.BlockSpec((tq, 2 * AUG), qmap),
                  pl.BlockSpec((tq, 2 * AUG), kmap),
                  pl.BlockSpec((LANES, tq), lambda b, p, s, qt, kt: (p, b * nq + kt[s])),
                  pl.BlockSpec((tq, LANES), lambda b, p, s, qt, kt: (b * nq + qt[s], C_OG // LANES + p)),
                  pl.BlockSpec((1, LANES), lambda b, p, s, qt, kt: (0, p)),
                  pl.BlockSpec(memory_space=pl.ANY)],
        out_specs=pl.BlockSpec((tq, LANES), qmap),
        scratch_shapes=[pltpu.VMEM((2, 1, tq), F32), pltpu.VMEM((2, 1, tq), F32), pltpu.VMEM((2, HEAD_DIM, tq), F32)],
    )
    return pl.pallas_call(
        _flash_kernel,
        grid_spec=gs,
        out_shape=jax.ShapeDtypeStruct((qa.shape[0], WIDTH), F32),
        input_output_aliases={7: 0},
        compiler_params=_cp(("arbitrary", "arbitrary", "arbitrary")),
        name="flash_prompt",
    )(qi_tab, ki_tab, qa, ka, vt, z, g_fox_o, jnp.zeros((qa.shape[0], WIDTH), F32))


PAGE_GROUP = 4
PAGE_BUFS = 12


def _paged_kernel(pt_ref, q_ref, kn_ref, vn_ref, lfn_ref, og_ref, g_ref, ck_hbm, cv_hbm, lf_hbm, of_hbm, o_ref,
                  kbuf, vbuf, lbuf, sem, *, n_pages):
    b = pl.program_id(0)
    total = pl.num_programs(0) * n_pages
    depth = PAGE_BUFS - PAGE_GROUP
    nh = WIDTH // HEAD_DIM
    rows = nh * DEC_SEQ

    def copies(page, slot):
        return (pltpu.make_async_copy(ck_hbm.at[page], kbuf.at[slot], sem.at[0, slot]),
                pltpu.make_async_copy(cv_hbm.at[page], vbuf.at[slot], sem.at[1, slot]),
                pltpu.make_async_copy(lf_hbm.at[page], lbuf.at[slot], sem.at[2, slot]))

    def fetch(g):
        bb = g // n_pages
        for c in copies(pt_ref[bb, g - bb * n_pages], g % PAGE_BUFS):
            c.start()

    @pl.when(b == 0)
    def _():
        for g in range(depth):
            fetch(g)

    q = (q_ref[...] * (HEAD_DIM ** -0.5)).astype(BF16)
    qh = [q[:, h * HEAD_DIM:(h + 1) * HEAD_DIM] for h in range(nh)]
    def cumulate(lf, carry):
        n = lf.shape[1]
        r = lax.broadcasted_iota(jnp.int32, (n, n), 0)
        c = lax.broadcasted_iota(jnp.int32, (n, n), 1)
        upper = jnp.where(r <= c, 1.0, 0.0).astype(BF16)
        hi, mid, lo = _split3(lf)
        cum = _dot(hi, upper) + _dot(mid, upper) + _dot(lo, upper) + carry
        return cum, cum[:, n - 1:n]

    def attend(pages, feature_major, ck, valid, m, l, acc):
        qk = _dot if feature_major else _dot_nt
        pv_dot = _dot_nt if feature_major else _dot
        s = jnp.concatenate([jnp.concatenate([qk(qh[h], kh[h]) for kh, _ in pages], axis=1) for h in range(nh)], axis=0)
        s = s - jnp.concatenate([jnp.broadcast_to(ck[h:h + 1, :], (DEC_SEQ, ck.shape[1])) for h in range(nh)], axis=0)
        if valid is not None:
            s = jnp.where(valid, s, -jnp.inf)
        m_new = jnp.maximum(m, jnp.max(s, axis=-1, keepdims=True))
        alpha = jnp.exp(m - m_new)
        p = jnp.exp(s - m_new)
        l = alpha * l + jnp.sum(p, axis=-1, keepdims=True)
        pb = p.astype(BF16)
        pv = []
        for h in range(nh):
            rows_h = slice(h * DEC_SEQ, (h + 1) * DEC_SEQ)
            pv.append(sum(pv_dot(pb[rows_h, i * PAGE:(i + 1) * PAGE], vh[h]) for i, (_, vh) in enumerate(pages)))
        return m_new, l, alpha * acc + jnp.concatenate(pv, axis=0)

    def pair_step(jj, carry):
        m, l, acc, ccar = carry
        g0 = b * n_pages + PAGE_GROUP * jj
        slots = [(g0 + u) % PAGE_BUFS for u in range(PAGE_GROUP)]
        for slot in slots:
            for cp in copies(0, slot):
                cp.wait()
        for u in range(PAGE_GROUP):
            @pl.when(g0 + depth + u < total)
            def _():
                fetch(g0 + depth + u)

        ck, ccar = cumulate(jnp.concatenate([lbuf[slot] for slot in slots], axis=1), ccar)
        pages = [([kbuf[slot, h].astype(BF16) for h in range(nh)],
                  [vbuf[slot, h].astype(BF16) for h in range(nh)]) for slot in slots]
        m, l, acc = attend(pages, True, ck, None, m, l, acc)
        return m, l, acc, ccar

    init = (jnp.full((rows, 1), -jnp.inf, F32), jnp.zeros((rows, 1), F32), jnp.zeros((rows, HEAD_DIM), F32),
            jnp.zeros((nh, 1), F32))
    m, l, acc, ccar = lax.fori_loop(0, n_pages // PAGE_GROUP, pair_step, init)

    pad = jnp.zeros((PAGE - DEC_SEQ, HEAD_DIM), BF16)
    kn = kn_ref[...].astype(BF16)
    vn = vn_ref[...].astype(BF16)
    k_heads = [jnp.concatenate([kn[:, h * HEAD_DIM:(h + 1) * HEAD_DIM], pad], axis=0) for h in range(nh)]
    v_heads = [jnp.concatenate([vn[:, h * HEAD_DIM:(h + 1) * HEAD_DIM], pad], axis=0) for h in range(nh)]
    ck, _ = cumulate(lfn_ref[0], ccar)
    rr = lax.broadcasted_iota(jnp.int32, (rows, PAGE), 0)
    cc = lax.broadcasted_iota(jnp.int32, (rows, PAGE), 1)
    m, l, acc = attend([(k_heads, v_heads)], False, ck, cc <= (rr & (DEC_SEQ - 1)), m, l, acc)
    o = acc / l
    on = o * lax.rsqrt(jnp.mean(o * o, axis=-1, keepdims=True) + RMS_EPS)
    out = jnp.concatenate([on[h * DEC_SEQ:(h + 1) * DEC_SEQ, :] for h in range(nh)], axis=1)
    o_ref[...] = out * g_ref[...] * jax.nn.sigmoid(og_ref[...])


def _paged_sample(page_table, qn, kn, v, lf_new_t, cache_k, cache_v, lf_pages_t, z, g_fox_o, o_f, row0):
    db, n_pages = page_table.shape
    assert n_pages % PAGE_GROUP == 0 and n_pages >= PAGE_BUFS
    nh = WIDTH // HEAD_DIM
    rb = row0 // DEC_SEQ
    newmap = lambda b, pt: (b, 0)
    hbm = pl.BlockSpec(memory_space=pl.ANY)
    gs = pltpu.PrefetchScalarGridSpec(
        num_scalar_prefetch=1,
        grid=(db,),
        in_specs=[pl.BlockSpec((DEC_SEQ, WIDTH), newmap),
                  pl.BlockSpec((DEC_SEQ, WIDTH), newmap),
                  pl.BlockSpec((DEC_SEQ, WIDTH), newmap),
                  pl.BlockSpec((1, nh, PAGE), lambda b, pt: (b, 0, 0)),
                  pl.BlockSpec((DEC_SEQ, WIDTH), lambda b, pt: (rb + b, C_OG // WIDTH)),
                  pl.BlockSpec((1, WIDTH), lambda b, pt: (0, 0)),
                  hbm, hbm, hbm, hbm],
        out_specs=pl.BlockSpec((DEC_SEQ, WIDTH), lambda b, pt: (rb + b, 0)),
        scratch_shapes=[pltpu.VMEM((PAGE_BUFS, nh, HEAD_DIM, PAGE), F32), pltpu.VMEM((PAGE_BUFS, nh, HEAD_DIM, PAGE), F32),
                        pltpu.VMEM((PAGE_BUFS, nh, PAGE), F32), pltpu.SemaphoreType.DMA((3, PAGE_BUFS))],
    )
    return pl.pallas_call(
        functools.partial(_paged_kernel, n_pages=n_pages),
        grid_spec=gs,
        out_shape=jax.ShapeDtypeStruct(o_f.shape, F32),
        input_output_aliases={10: 0},
        compiler_params=_cp(("arbitrary",)),
        name="paged_sample",
    )(page_table, qn, kn, v, lf_new_t, z, g_fox_o, cache_k, cache_v, lf_pages_t, o_f)


def _prep_kernel(z_ref, first_ref, mu_ref, w12_ref, w3_ref, w0_ref, a0_ref, kk_ref, ka_ref, rk_ref,
                 g_hbm, bonus_hbm, o_r, o_w, o_k, o_v, o_a, o_b, o_g, o_bonus, carry, *, seq_len, tiles_per_seq):
    z = z_ref[...]
    tm = z.shape[0]
    row = lax.broadcasted_iota(jnp.int32, (tm, 1), 0)
    rolled = pltpu.roll(z, 1, 0)
    if tiles_per_seq >= 1 and seq_len >= tm:
        li = pl.program_id(0) % tiles_per_seq

        @pl.when(li == 0)
        def _():
            carry[...] = first_ref[0]

        zp = jnp.where(row == 0, carry[...], rolled)
        carry[...] = z[tm - 1:tm, :]
    else:
        nseq = tm // seq_len
        first = first_ref[...]
        exp = jnp.broadcast_to(first, (nseq, seq_len, first.shape[-1])).reshape(tm, first.shape[-1])
        zp = jnp.where((row & (seq_len - 1)) == 0, exp, rolled)
    zm = z + (zp - z) * mu_ref[...]
    r = zm[:, 0:WIDTH]
    k = zm[:, WIDTH:2 * WIDTH]
    v = zm[:, 2 * WIDTH:3 * WIDTH]
    lo = zm[:, C_LORA:C_LORA + 384]
    lane = lax.broadcasted_iota(jnp.int32, (tm, 384), 1)
    act = jnp.where(lane < 64, jnp.tanh(lo), jnp.where(lane < 128, lo, jax.nn.sigmoid(lo))).astype(BF16)
    l12 = _dot(act[:, 0:128], w12_ref[...])
    g = _dot(act[:, 128:384], w3_ref[...])
    log_w = -_softplus(-(w0_ref[...] + l12[:, 0:WIDTH])) - 0.5
    decay = jnp.exp(-jnp.exp(log_w))
    asig = jax.nn.sigmoid(a0_ref[...] + l12[:, WIDTH:2 * WIDTH])
    bd = _block_diag_ones()
    kk = k * kk_ref[...]
    kk = kk / jnp.maximum(jnp.sqrt(_segsum(kk * kk, bd)), 1e-12)
    kf = k * (1.0 + (asig - 1.0) * ka_ref[...])
    o_r[...] = r
    o_w[...] = decay
    o_k[...] = kf
    o_v[...] = v
    o_a[...] = -kk
    o_b[...] = kk * asig
    o_g[...] = g
    o_bonus[...] = _segsum(r * kf * rk_ref[...], bd) * v


def _rwkv_prep(z, first, mu, w12, w3, w0, a0, k_k, k_a, r_k, row0, n_rows, seq_len, shared):
    if seq_len >= 256:
        tm = _tile(seq_len, 256)
        first_spec = pl.BlockSpec((1, 1, RWKV_COLS), lambda i: (i // (seq_len // tm), 0, 0))
    else:
        tm = _tile(n_rows, 128)
        nseq = tm // seq_len
        first_spec = pl.BlockSpec((nseq, 1, RWKV_COLS), lambda i: (i, 0, 0))
    rb = row0 // tm
    vec = lambda n: pl.BlockSpec((1, n), lambda i: (0, 0))
    out = pl.BlockSpec((tm, WIDTH), lambda i: (i, 0))
    out_all = pl.BlockSpec((tm, WIDTH), lambda i: (rb + i, 0))
    hbm = pl.BlockSpec(memory_space=pl.ANY)
    extra = tuple(shared)
    return pl.pallas_call(
        functools.partial(_prep_kernel, seq_len=seq_len, tiles_per_seq=max(seq_len // tm, 1)),
        grid=(n_rows // tm,),
        in_specs=[pl.BlockSpec((tm, RWKV_COLS), lambda i: (rb + i, 0)), first_spec, vec(RWKV_COLS),
                  pl.BlockSpec((128, 2 * WIDTH), lambda i: (0, 0)), pl.BlockSpec((256, WIDTH), lambda i: (0, 0)),
                  vec(WIDTH), vec(WIDTH), vec(WIDTH), vec(WIDTH), vec(WIDTH)] + [hbm] * len(extra),
        out_specs=[out] * 6 + [out_all] * 2,
        out_shape=[jax.ShapeDtypeStruct((n_rows, WIDTH), F32)] * 6 + [jax.ShapeDtypeStruct((z.shape[0], WIDTH), F32)] * 2,
        input_output_aliases={10: 6, 11: 7},
        scratch_shapes=[pltpu.VMEM((1, RWKV_COLS), F32)],
        compiler_params=_cp(("arbitrary",)),
        name=f"rwkv_prep_{seq_len}",
    )(z, first, mu, w12, w3, w0, a0, k_k, k_a, r_k, *extra)


SCAN_SUB = 64
SCAN_SPLIT = 2
SCAN_UNROLL = 4


def _wkv_kernel(r_ref, w_ref, k_ref, v_ref, a_ref, b_ref, s0_ref, y_ref, sT_ref, S_s, Z_s, *, groups, n_pairs, tb_len):
    tb = pl.program_id(1)
    r2 = lax.broadcasted_iota(jnp.int32, (2 * LANES, 2 * LANES), 0)
    c2 = lax.broadcasted_iota(jnp.int32, (2 * LANES, 2 * LANES), 1)
    ones_bd = jnp.where((r2 >> 6) == (c2 >> 6), 1.0, 0.0).astype(BF16)
    vrow = lax.broadcasted_iota(jnp.int32, (HEAD_DIM, LANES), 0)
    lane = lax.broadcasted_iota(jnp.int32, (HEAD_DIM, LANES), 1)
    eye2 = (lane & (HEAD_DIM - 1)) == vrow
    zero_half = jnp.zeros((HEAD_DIM, LANES), BF16)
    H = HEAD_DIM

    @pl.when(tb == 0)
    def _():
        for g in range(groups):
            for p in range(n_pairs):
                S_s[g * n_pairs + p] = jnp.concatenate([s0_ref[g, 2 * p], s0_ref[g, 2 * p + 1]], axis=1)
        Z_s[...] = jnp.zeros_like(Z_s)

    def lanes(p):
        return slice(p * LANES, (p + 1) * LANES)

    def run(t0, n):
        def step(i, carry):
            t = t0 + i
            tp = jnp.maximum(t - 1, 0)
            hit = (lane & (SCAN_SUB - 1)) == (i - 1)
            for g in range(groups):
                aa = a_ref[g, pl.ds(t, 1), :]
                ww = w_ref[g, pl.ds(t, 1), :]
                kr = k_ref[g, pl.ds(t, 1), :]
                vv = v_ref[g, pl.ds(t, 1), :]
                bb = b_ref[g, pl.ds(t, 1), :]
                rp = r_ref[g, pl.ds(tp, 1), :]
                npg = n_pairs // SCAN_SPLIT
                for p0 in range(0, n_pairs, npg):
                    lhs_s, xs = [], []
                    for p in range(p0, p0 + npg):
                        S = S_s[g * n_pairs + p]
                        lhs_s.append(jnp.concatenate([(S * aa[:, lanes(p)]).astype(BF16),
                                                      (S * rp[:, lanes(p)]).astype(BF16)], axis=1))
                        xs.append(jnp.where(eye2, vv[:, lanes(p)], 0.0).astype(BF16))
                    lhs_x = [jnp.concatenate(xs[q:q + 2], axis=1) for q in range(0, npg, 2)]
                    res = _dot(jnp.concatenate(lhs_s + lhs_x, axis=0), ones_bd)
                    for q in range(npg):
                        p = p0 + q
                        idx = g * n_pairs + p
                        rs = res[q * H:(q + 1) * H]
                        vb = res[(npg + q // 2) * H:(npg + q // 2 + 1) * H, (q % 2) * LANES:(q % 2 + 1) * LANES]
                        S_s[idx] = S_s[idx] * ww[:, lanes(p)] + rs[:, :LANES] * bb[:, lanes(p)] + vb * kr[:, lanes(p)]
                        Z_s[idx] = jnp.where(hit, rs[:, LANES:], Z_s[idx])
            return carry

        lax.fori_loop(0, n, step, 0, unroll=SCAN_UNROLL)
        hit_last = (lane & (SCAN_SUB - 1)) == (n - 1)
        for g in range(groups):
            rl = r_ref[g, pl.ds(t0 + n - 1, 1), :]
            lhs = [jnp.concatenate([(S_s[g * n_pairs + p] * rl[:, lanes(p)]).astype(BF16), zero_half], axis=1)
                   for p in range(n_pairs)]
            res = _dot(jnp.concatenate(lhs, axis=0), ones_bd)
            for p in range(n_pairs):
                idx = g * n_pairs + p
                zt = jnp.where(hit_last, res[p * H:(p + 1) * H, :LANES], Z_s[idx]).T
                y_ref[g, pl.ds(t0, n), p * LANES:p * LANES + H] = zt[0:n, :]
                y_ref[g, pl.ds(t0, n), p * LANES + H:(p + 1) * LANES] = zt[SCAN_SUB:SCAN_SUB + n, :]

    if tb_len <= SCAN_SUB:
        run(0, tb_len)
    else:
        def outer(blk, carry):
            run(pl.multiple_of(blk * SCAN_SUB, SCAN_SUB), SCAN_SUB)
            return carry
        lax.fori_loop(0, tb_len // SCAN_SUB, outer, 0)

    @pl.when(tb == pl.num_programs(1) - 1)
    def _():
        for g in range(groups):
            for p in range(n_pairs):
                S = S_s[g * n_pairs + p]
                sT_ref[g, 2 * p] = S[:, 0:H]
                sT_ref[g, 2 * p + 1] = S[:, H:LANES]


def _wkv_scan(r, w, k, v, a, b, s0, n_seq, seq_len):
    nh = WIDTH // HEAD_DIM
    groups = 2 if n_seq % 2 == 0 else 1
    tb_len = min(seq_len, 128)
    nt = seq_len // tb_len
    blk = pl.BlockSpec((groups, tb_len, WIDTH), lambda s, t: (s, t, 0))
    st = pl.BlockSpec((groups, nh, HEAD_DIM, HEAD_DIM), lambda s, t: (s, 0, 0, 0))
    n_pairs = WIDTH // LANES
    seq3 = lambda x: x.reshape(n_seq, seq_len, WIDTH)
    y, s_new = pl.pallas_call(
        functools.partial(_wkv_kernel, groups=groups, n_pairs=n_pairs, tb_len=tb_len),
        grid=(n_seq // groups, nt),
        in_specs=[blk] * 6 + [st],
        out_specs=[blk, st],
        out_shape=[jax.ShapeDtypeStruct((n_seq, seq_len, WIDTH), F32),
                   jax.ShapeDtypeStruct((n_seq, nh, HEAD_DIM, HEAD_DIM), F32)],
        scratch_shapes=[pltpu.VMEM((groups * n_pairs, HEAD_DIM, LANES), F32),
                        pltpu.VMEM((groups * n_pairs, HEAD_DIM, LANES), F32)],
        compiler_params=_cp(("arbitrary", "arbitrary")),
        name=f"wkv_scan_{seq_len}",
    )(seq3(r), seq3(w), seq3(k), seq3(v), seq3(a), seq3(b), s0)
    return y.reshape(n_seq * seq_len, WIDTH), s_new


def _outproj_kernel(xp_ref, xs_ref, yp_ref, ys_ref, bonus_ref, g_ref, of_ref, lw_ref, lb_ref, wo_ref, gf_ref,
                    wrg_ref, wre_ref, brg_ref, bre_ref, x1_ref, h_ref, route_ref, wrh_s, wrl_s, br_s, *, prompt_tiles):
    i = pl.program_id(0)

    @pl.when(i == 0)
    def _():
        pad = LANES - N_GROUPS - N_EXPERTS
        w_r = jnp.concatenate([wrg_ref[...], wre_ref[...], jnp.zeros((wrg_ref.shape[0], pad), F32)], axis=1)
        hi = w_r.astype(BF16)
        wrh_s[...] = hi
        wrl_s[...] = (w_r - hi.astype(F32)).astype(BF16)
        br_s[...] = jnp.concatenate([brg_ref[...], bre_ref[...], jnp.zeros((1, pad), F32)], axis=1)

    is_prompt = i < prompt_tiles
    bd = _block_diag_ones()
    y = jnp.where(is_prompt, yp_ref[...], ys_ref[...])
    mu = _segsum(y, bd) * (1.0 / HEAD_DIM)
    d = y - mu
    var = _segsum(d * d, bd) * (1.0 / HEAD_DIM)
    yn = d * lax.rsqrt(var + LNX_EPS) * lw_ref[...] + lb_ref[...]
    o_r = ((yn + bonus_ref[...]) * g_ref[...]).astype(BF16)
    o_f = of_ref[...].astype(BF16)
    x = jnp.where(is_prompt, xp_ref[...], xs_ref[...])
    x1 = x + _dot(o_r, wo_ref[0:WIDTH, :]) + _dot(o_f, wo_ref[WIDTH:2 * WIDTH, :])
    x1_ref[...] = x1
    h = _rms_rows(x1, gf_ref[...])
    h_ref[...] = h
    hi, lo = _split2(h)
    logits = _dot(hi, wrh_s[...]) + _dot(lo, wrh_s[...]) + _dot(hi, wrl_s[...]) + br_s[...]
    tm = logits.shape[0]
    lane = lax.broadcasted_iota(jnp.int32, (tm, LANES), 1)
    neg = -jnp.inf
    lg = jnp.where(lane < N_GROUPS, logits, neg)
    mg = jnp.max(lg, axis=-1, keepdims=True)
    pg_top = 1.0 / jnp.sum(jnp.exp(lg - mg), axis=-1, keepdims=True)
    g_sel = jnp.min(jnp.where(lg == mg, lane, LANES), axis=-1, keepdims=True)
    in_grp = (lane >= N_GROUPS) & (lane < N_GROUPS + N_EXPERTS) & (((lane - N_GROUPS) >> 3) == g_sel)
    le = jnp.where(in_grp, logits, neg)
    m1 = jnp.max(le, axis=-1, keepdims=True)
    i1 = jnp.min(jnp.where(le == m1, lane, LANES), axis=-1, keepdims=True)
    le2 = jnp.where(lane == i1, neg, le)
    m2 = jnp.max(le2, axis=-1, keepdims=True)
    i2 = jnp.min(jnp.where(le2 == m2, lane, LANES), axis=-1, keepdims=True)
    e2 = jnp.exp(m2 - m1)
    gate1 = pg_top / (1.0 + e2)
    gate2 = pg_top * e2 / (1.0 + e2)
    route = jnp.where(lane == 0, (i1 - N_GROUPS).astype(F32),
                      jnp.where(lane == 1, (i2 - N_GROUPS).astype(F32),
                                jnp.where(lane == 2, gate1, jnp.where(lane == 3, gate2, 0.0))))
    route_ref[...] = route


def _outproj(x_p, x_s, y_p, y_s, bonus, g, o_f, lnx_w, lnx_b, w_out, g_ffn, w_rg, w_re, b_rg, b_re):
    (n_p, d), n_s = x_p.shape, x_s.shape[0]
    m = n_p + n_s
    tm = _tile(math.gcd(n_p, n_s), 256)
    pmap, smap = _row_maps(n_p // tm)
    row = lambda n: pl.BlockSpec((tm, n), lambda i: (i, 0))
    vec = lambda n: pl.BlockSpec((1, n), lambda i: (0, 0))
    full = lambda a: pl.BlockSpec(a.shape, lambda i: (0, 0))
    return pl.pallas_call(
        functools.partial(_outproj_kernel, prompt_tiles=n_p // tm),
        grid=(m // tm,),
        in_specs=[pl.BlockSpec((tm, d), pmap), pl.BlockSpec((tm, d), smap),
                  pl.BlockSpec((tm, WIDTH), pmap), pl.BlockSpec((tm, WIDTH), smap),
                  row(WIDTH), row(WIDTH), row(WIDTH), vec(WIDTH), vec(WIDTH),
                  full(w_out), vec(d), full(w_rg), full(w_re), full(b_rg), full(b_re)],
        out_specs=[row(d), row(d), row(LANES)],
        out_shape=[jax.ShapeDtypeStruct((m, d), F32), jax.ShapeDtypeStruct((m, d), F32),
                   jax.ShapeDtypeStruct((m, LANES), F32)],
        scratch_shapes=[pltpu.VMEM((d, LANES), BF16), pltpu.VMEM((d, LANES), BF16), pltpu.VMEM((1, LANES), F32)],
        compiler_params=_cp(("arbitrary",)),
        name="outproj_router",
    )(x_p, x_s, y_p, y_s, bonus, g, o_f, lnx_w, lnx_b, w_out, g_ffn, w_rg, w_re, b_rg, b_re)


def _expert_kernel(plan_ref, tok_ref, nu_ref, h_hbm, wg_hbm, wu_hbm, wd_hbm, o_ref, xbuf, wg, wu, wd, sem, wsem):
    i = pl.program_id(0)
    n_used = nu_ref[0]
    first, nxt, wslot = plan_ref[1, i], plan_ref[2, i], plan_ref[3, i]

    def row_copy(tok, r, slot):
        return pltpu.make_async_copy(h_hbm.at[pl.ds(tok, 1), :], xbuf.at[slot, pl.ds(r, 1), :], sem.at[slot])

    def gather(blk, slot):
        def start(r, c):
            row_copy(tok_ref[blk * MOE_BLOCK + r], r, slot).start()
            return c
        lax.fori_loop(0, MOE_BLOCK, start, 0, unroll=8)

    def weight_copies(expert, slot):
        return (pltpu.make_async_copy(wg_hbm.at[expert], wg.at[slot], wsem.at[0, slot]),
                pltpu.make_async_copy(wu_hbm.at[expert], wu.at[slot], wsem.at[1, slot]),
                pltpu.make_async_copy(wd_hbm.at[expert], wd.at[slot], wsem.at[2, slot]))

    @pl.when(i == 0)
    def _():
        gather(0, 0)
        for c in weight_copies(plan_ref[0, 0], 0):
            c.start()

    @pl.when(i < n_used)
    def _():
        slot = i & 1

        @pl.when(first == 1)
        def _():
            for c in weight_copies(0, wslot):
                c.wait()

            @pl.when(nxt >= 0)
            def _():
                for c in weight_copies(nxt, 1 - wslot):
                    c.start()

        def wait(r, c):
            row_copy(0, r, slot).wait()
            return c
        lax.fori_loop(0, MOE_BLOCK, wait, 0, unroll=8)

        @pl.when(i + 1 < n_used)
        def _():
            gather(i + 1, 1 - slot)

        x = xbuf[slot].astype(BF16)
        gate = _dot(x, wg[wslot].astype(BF16))
        up = _dot(x, wu[wslot].astype(BF16))
        act = (gate * jax.nn.sigmoid(gate) * up).astype(BF16)
        o_ref[...] = _dot(act, wd[wslot].astype(BF16))

    @pl.when(i >= n_used)
    def _():
        o_ref[...] = jnp.zeros_like(o_ref)


def _experts(plan, slot_tok, n_used, h, w_gate, w_up, w_down):
    n_blocks = plan.shape[1]
    d = h.shape[1]
    de = w_gate.shape[2]
    hbm = pl.BlockSpec(memory_space=pl.ANY)
    gs = pltpu.PrefetchScalarGridSpec(
        num_scalar_prefetch=3,
        grid=(n_blocks,),
        in_specs=[hbm, hbm, hbm, hbm],
        out_specs=pl.BlockSpec((MOE_BLOCK, d), lambda i, plan, tok, nu: (i, 0)),
        scratch_shapes=[pltpu.VMEM((2, MOE_BLOCK, d), F32), pltpu.VMEM((2, d, de), F32), pltpu.VMEM((2, d, de), F32),
                        pltpu.VMEM((2, de, d), F32), pltpu.SemaphoreType.DMA((2,)), pltpu.SemaphoreType.DMA((3, 2))],
    )
    return pl.pallas_call(
        _expert_kernel,
        grid_spec=gs,
        out_shape=jax.ShapeDtypeStruct((n_blocks * MOE_BLOCK, d), F32),
        compiler_params=_cp(("arbitrary",), 58 * 1024 * 1024),
        name="experts",
    )(plan, slot_tok, n_used, h, w_gate, w_up, w_down)


def _ple_kernel(dest_ref, x1_ref, route_ref, p_ref, yb_hbm, gp_ref, wpg_ref, bpg_ref, wpp_ref, gfin_ref, op_ref, os_ref,
                buf, sem, *, prompt_tiles):
    i = pl.program_id(0)
    tm = x1_ref.shape[0]

    def row_copy(slot_row, r, k, half):
        return pltpu.make_async_copy(yb_hbm.at[pl.ds(slot_row, 1), :], buf.at[half, k, pl.ds(r, 1), :], sem.at[half])

    def gather(tile, half):
        def start(r, c):
            for k in range(2):
                row_copy(dest_ref[(tile * tm + r) * 2 + k], r, k, half).start()
            return c
        lax.fori_loop(0, tm, start, 0, unroll=8)

    @pl.when(i == 0)
    def _():
        gather(0, 0)

    half = i & 1

    def wait(r, c):
        for k in range(2):
            row_copy(0, r, k, half).wait()
        return c
    lax.fori_loop(0, tm, wait, 0, unroll=8)

    @pl.when(i + 1 < pl.num_programs(0))
    def _():
        gather(i + 1, 1 - half)

    route = route_ref[...]
    moe = buf[half, 0] * route[:, 2:3] + buf[half, 1] * route[:, 3:4]
    x2 = x1_ref[...] + moe
    hn = _rms_rows(x2, gp_ref[...]).astype(BF16)
    gate = jax.nn.sigmoid(_dot(hn, wpg_ref[...]) + bpg_ref[...])
    pe = _dot(p_ref[...].astype(BF16), wpp_ref[...])
    x3 = x2 + gate * pe
    y = _rms_rows(x3, gfin_ref[...])

    @pl.when(i < prompt_tiles)
    def _():
        op_ref[...] = y

    @pl.when(i >= prompt_tiles)
    def _():
        os_ref[...] = y


def _ple(dest, x1, route, p, yb, g_ple, w_pg, b_pg, w_pp, g_final, n_p):
    m, d = x1.shape
    tm = _tile(math.gcd(n_p, m - n_p), 256)
    npt = n_p // tm
    pd = p.shape[1]
    gs = pltpu.PrefetchScalarGridSpec(
        num_scalar_prefetch=1,
        grid=(m // tm,),
        in_specs=[pl.BlockSpec((tm, d), lambda i, ds: (i, 0)),
                  pl.BlockSpec((tm, LANES), lambda i, ds: (i, 0)),
                  pl.BlockSpec((tm, pd), lambda i, ds: (i, 0)),
                  pl.BlockSpec(memory_space=pl.ANY),
                  pl.BlockSpec((1, d), lambda i, ds: (0, 0)),
                  pl.BlockSpec((d, d), lambda i, ds: (0, 0)),
                  pl.BlockSpec((1, d), lambda i, ds: (0, 0)),
                  pl.BlockSpec((pd, d), lambda i, ds: (0, 0)),
                  pl.BlockSpec((1, d), lambda i, ds: (0, 0))],
        out_specs=[pl.BlockSpec((tm, d), lambda i, ds: (jnp.minimum(i, npt - 1), 0)),
                   pl.BlockSpec((tm, d), lambda i, ds: (jnp.maximum(i - npt, 0), 0))],
        scratch_shapes=[pltpu.VMEM((2, 2, tm, d), F32), pltpu.SemaphoreType.DMA((2,))],
    )
    return pl.pallas_call(
        functools.partial(_ple_kernel, prompt_tiles=npt),
        grid_spec=gs,
        out_shape=[jax.ShapeDtypeStruct((n_p, d), F32), jax.ShapeDtypeStruct((m - n_p, d), F32)],
        compiler_params=_cp(("arbitrary",)),
        name="moe_combine_ple",
    )(dest, x1, route, p, yb, g_ple, w_pg, b_pg, w_pp, g_final)


def _dispatch(route, n_tok):
    expert = route[:, 0:2].astype(jnp.int32)
    flat_e = expert.reshape(-1)
    n_assign = flat_e.shape[0]
    onehot = (flat_e[:, None] == jnp.arange(N_EXPERTS, dtype=jnp.int32)[None, :]).astype(jnp.int32)
    csum = jnp.cumsum(onehot, axis=0)
    rank = jnp.sum(csum * onehot, axis=1) - 1
    counts = csum[-1]
    pcounts = (counts + MOE_BLOCK - 1) // MOE_BLOCK * MOE_BLOCK
    pend = jnp.cumsum(pcounts)
    pstart = pend - pcounts
    dest = (pstart[flat_e] + rank).astype(jnp.int32)
    n_blocks = -(-n_assign // MOE_BLOCK) + N_EXPERTS
    slot_tok = jnp.zeros((n_blocks * MOE_BLOCK,), jnp.int32).at[dest].set(jnp.arange(n_assign, dtype=jnp.int32) // 2)
    n_used = (pend[-1] // MOE_BLOCK).astype(jnp.int32)
    blk = jnp.arange(n_blocks, dtype=jnp.int32)
    block_e = jnp.searchsorted(pend, jnp.minimum(blk, n_used - 1) * MOE_BLOCK, side='right').astype(jnp.int32)
    block_e = jnp.minimum(block_e, N_EXPERTS - 1)
    first = jnp.concatenate([jnp.ones((1,), jnp.int32), (block_e[1:] != block_e[:-1]).astype(jnp.int32)])
    first = jnp.where(blk < n_used, first, 0)
    used = counts > 0
    eidx = jnp.arange(N_EXPERTS, dtype=jnp.int32)
    later = jnp.flip(lax.cummin(jnp.flip(jnp.where(used, eidx, N_EXPERTS))))
    next_used = jnp.concatenate([later[1:], jnp.full((1,), N_EXPERTS, jnp.int32)])
    next_used = jnp.where(next_used < N_EXPERTS, next_used, -1)
    ordinal = jnp.cumsum(used.astype(jnp.int32)) - 1
    plan = jnp.stack([block_e, first, next_used[block_e], ordinal[block_e] & 1]).astype(jnp.int32)
    return plan, slot_tok, n_used.reshape(1), dest


def kernel(x_prompt, x_sample, cache_k, cache_v, cache_logf, state_wkv, state_shift, page_table, p_prompt, p_sample, g_attn, w_in, mu_shift, w0, w_up, a0, a_up, g_up, k_k, k_a, r_k, lnx_w, lnx_b, b_f, q_norm, k_norm, g_fox_o, w_out, g_ffn, w_rg, b_rg, w_re, b_re, w_e_gate, w_e_up, w_e_down, g_ple, w_pg, b_pg, w_pp, g_final):
    depth = g_attn.shape[0]
    assert depth == 1
    batch, seq, d = x_prompt.shape
    db, dec_seq, _ = x_sample.shape
    assert dec_seq == DEC_SEQ
    nh = WIDTH // HEAD_DIM
    n_p = batch * seq
    n_s = db * dec_seq
    m = n_p + n_s
    rwkv_in = mu_shift.shape[1]
    n_pool = cache_k.shape[1]
    n_pages = page_table.shape[1]
    row = lambda a: a.reshape(1, -1)

    x_p, x_s = x_prompt.reshape(n_p, d), x_sample.reshape(n_s, d)
    p_all = jnp.concatenate([p_prompt[0].reshape(n_p, -1), p_sample[0].reshape(n_s, -1)], axis=0)

    wi = w_in[0]
    zc = lambda n: jnp.zeros((d, n), F32)
    w_z = jnp.concatenate([wi[:, :rwkv_in], zc(C_F - rwkv_in), wi[:, rwkv_in + 4 * WIDTH:], zc(C_Q - C_F - nh),
                           wi[:, rwkv_in:rwkv_in + 4 * WIDTH]], axis=1).astype(BF16)
    w12 = jnp.zeros((128, 2 * WIDTH), F32).at[0:64, 0:WIDTH].set(w_up[0]).at[64:128, WIDTH:].set(a_up[0]).astype(BF16)
    w3 = jnp.zeros((256, WIDTH), F32).at[0:g_up.shape[1]].set(g_up[0]).astype(BF16)
    pad_cols = lambda a, n: jnp.pad(a, ((0, 0), (0, n - a.shape[1])))
    mu_pad = pad_cols(mu_shift, RWKV_COLS)
    bf_pad = pad_cols(b_f, LANES)
    qn_t = jnp.tile(q_norm, (1, nh))
    kn_t = jnp.tile(k_norm, (1, nh))

    z = _inproj(x_p, x_s, g_attn, w_z)

    qn_s, kn_s, vv_s, logf, qa, ka, vt, kt_p, vt_p = _foxpost(z, qn_t, kn_t, bf_pad, batch, seq)

    o_f = _flash_prompt(qa, ka, vt, z, g_fox_o, batch, seq)
    lf_pages_t = cache_logf[0].transpose(0, 2, 1)
    lf_new_t = pad_cols(logf[n_p:, :nh].reshape(db, dec_seq, nh).transpose(0, 2, 1).reshape(db * nh, dec_seq), PAGE)
    o_f = _paged_sample(page_table, qn_s, kn_s, vv_s, lf_new_t.reshape(db, nh, PAGE),
                        cache_k[0].transpose(0, 2, 3, 1), cache_v[0].transpose(0, 2, 3, 1), lf_pages_t, z, g_fox_o, o_f, n_p)

    first_p = jnp.zeros((batch, 1, RWKV_COLS), F32)
    first_s = pad_cols(state_shift[0], RWKV_COLS).reshape(db, 1, RWKV_COLS)
    prep_args = (mu_pad, w12, w3, w0, a0, k_k, k_a, row(r_k))
    prep_p = _rwkv_prep(z, first_p, *prep_args, 0, n_p, seq, shared=(jnp.zeros((m, WIDTH), F32),) * 2)
    prep_s = _rwkv_prep(z, first_s, *prep_args, n_p, n_s, dec_seq, shared=prep_p[6:8])
    g_, bonus = prep_s[6:8]
    y_p, wkv_p = _wkv_scan(*prep_p[:6], jnp.zeros((batch, nh, HEAD_DIM, HEAD_DIM), F32), batch, seq)
    y_s, wkv_s = _wkv_scan(*prep_s[:6], state_wkv[0], db, dec_seq)

    x1, h2, route = _outproj(x_p, x_s, y_p, y_s, bonus, g_, o_f, lnx_w, lnx_b, w_out[0].astype(BF16), g_ffn,
                             w_rg[0], w_re[0], b_rg, b_re)

    plan, slot_tok, n_used, dest = _dispatch(route, m)
    yb = _experts(plan, slot_tok, n_used, h2, w_e_gate[0], w_e_up[0], w_e_down[0])

    y_out_p, y_out_s = _ple(dest, x1, route, p_all, yb, g_ple, w_pg[0].astype(BF16), b_pg, w_pp[0].astype(BF16),
                            row(g_final), n_p)

    shift_p = jnp.concatenate([z[(b + 1) * seq - 1:(b + 1) * seq, :rwkv_in] for b in range(batch)], axis=0)
    shift_s = z[n_p + dec_seq - 1::dec_seq, :rwkv_in]
    heads_t = lambda a: a.reshape(batch, nh, HEAD_DIM, seq).transpose(0, 3, 1, 2)[None]
    heads = lambda a: a.reshape(1, db, dec_seq, nh, HEAD_DIM)
    return (y_out_p.reshape(batch, seq, d), y_out_s.reshape(db, dec_seq, d),
            heads_t(kt_p), heads_t(vt_p), logf[:n_p, :nh].reshape(1, batch, seq, nh),
            wkv_p[None], shift_p[None],
            heads(kn_s), heads(vv_s), logf[n_p:, :nh].reshape(1, db, dec_seq, nh),
            wkv_s[None], shift_s[None])
```

```python
import functools
import math

import jax
import jax.numpy as jnp
import numpy as np
from jax import lax
from jax.experimental import pallas as pl
from jax.experimental.pallas import tpu as pltpu

F32 = jnp.float32
BF16 = jnp.bfloat16

HEAD_DIM = 64
LANES = 128
RMS_EPS = 1e-6
LNX_EPS = 64e-5
PAGE = 128
MOE_BLOCK = 384
N_GROUPS = 8
N_EXPERTS = 64
DEC_SEQ = 8

C_LORA = 3072
C_F = 3456
RWKV_COLS = 3584
C_Q, C_FK, C_FV, C_OG = 4096, 5120, 6144, 7168
Z_COLS = 8192
WIDTH = 1024

VMEM_LIMIT = 52 * 1024 * 1024


def _cp(sem, vmem=VMEM_LIMIT):
    return pltpu.CompilerParams(dimension_semantics=sem, vmem_limit_bytes=vmem)


def _tile(n, pref):
    for t in (1024, 512, 256, 128, 64, 32, 16, 8):
        if t <= pref and n % t == 0:
            return t
    raise ValueError(f"no tile for {n}")


def _split2(x):
    hi = x.astype(BF16)
    lo = (x - hi.astype(F32)).astype(BF16)
    return hi, lo


def _split3(x):
    hi = x.astype(BF16)
    r1 = x - hi.astype(F32)
    mid = r1.astype(BF16)
    lo = (r1 - mid.astype(F32)).astype(BF16)
    return hi, mid, lo


def _dot(a, b):
    return jnp.dot(a, b, preferred_element_type=F32)


def _dot_nt(a, b):
    return lax.dot_general(a, b, (((1,), (1,)), ((), ())), preferred_element_type=F32)


def _block_diag_ones():
    r = lax.broadcasted_iota(jnp.int32, (LANES, LANES), 0)
    c = lax.broadcasted_iota(jnp.int32, (LANES, LANES), 1)
    return jnp.where((r >> 6) == (c >> 6), 1.0, 0.0).astype(BF16)


def _segsum(x, bd):
    outs = []
    for j in range(x.shape[1] // LANES):
        hi, lo = _split2(x[:, j * LANES:(j + 1) * LANES])
        outs.append(_dot(hi, bd) + _dot(lo, bd))
    return outs[0] if len(outs) == 1 else jnp.concatenate(outs, axis=1)


def _softplus(x):
    return jnp.maximum(x, 0.0) + jnp.log1p(jnp.exp(-jnp.abs(x)))


def _rms_rows(x, g):
    return x * lax.rsqrt(jnp.mean(x * x, axis=-1, keepdims=True) + RMS_EPS) * g


def _row_maps(prompt_tiles):
    return (lambda i, *_: (jnp.minimum(i, prompt_tiles - 1), 0)), (lambda i, *_: (jnp.maximum(i - prompt_tiles, 0), 0))


def _inproj_kernel(xp_ref, xs_ref, g_ref, w_ref, o_ref, h_ref, *, prompt_tiles):
    @pl.when(pl.program_id(1) == 0)
    def _():
        x = jnp.where(pl.program_id(0) < prompt_tiles, xp_ref[...], xs_ref[...])
        h_ref[...] = _rms_rows(x, g_ref[...]).astype(BF16)

    o_ref[...] = _dot(h_ref[...], w_ref[...])


def _inproj(x_p, x_s, g, w):
    (n_p, d), n_s = x_p.shape, x_s.shape[0]
    n = w.shape[1]
    tm, tn = _tile(math.gcd(n_p, n_s), 1024), _tile(n, 512)
    pmap, smap = _row_maps(n_p // tm)
    return pl.pallas_call(
        functools.partial(_inproj_kernel, prompt_tiles=n_p // tm),
        grid=((n_p + n_s) // tm, n // tn),
        in_specs=[pl.BlockSpec((tm, d), pmap),
                  pl.BlockSpec((tm, d), smap),
                  pl.BlockSpec((1, d), lambda i, j: (0, 0)),
                  pl.BlockSpec((d, tn), lambda i, j: (0, j))],
        out_specs=pl.BlockSpec((tm, tn), lambda i, j: (i, j)),
        out_shape=jax.ShapeDtypeStruct((n_p + n_s, n), F32),
        scratch_shapes=[pltpu.VMEM((tm, d), BF16)],
        compiler_params=_cp(("arbitrary", "arbitrary"), 58 * 1024 * 1024),
        name="inproj",
    )(x_p, x_s, g, w)


AUG = 2 * HEAD_DIM


def _aug_constants():
    nh = WIDTH // HEAD_DIM
    pq = np.zeros((3 * LANES, WIDTH), np.float32)
    pk = np.zeros((3 * LANES, WIDTH), np.float32)
    one_q = np.zeros((1, WIDTH), np.float32)
    one_k = np.zeros((1, WIDTH), np.float32)
    for h in range(nh):
        for comp in range(3):
            pq[comp * LANES + h, h * HEAD_DIM + comp] = 1.0
            pk[comp * LANES + h, h * HEAD_DIM + 3 + comp] = -1.0
            one_q[0, h * HEAD_DIM + 3 + comp] = 1.0
            one_k[0, h * HEAD_DIM + comp] = 1.0
    return jnp.asarray(pq, BF16), jnp.asarray(pk, BF16), jnp.asarray(one_q), jnp.asarray(one_k)


def _foxpost_kernel(q_ref, k_ref, v_ref, f_ref, qn_ref, kn_ref, bf_ref, pq_ref, pk_ref, oneq_ref, onek_ref,
                    oq, ok, ov, olf, oqa, oka, ovt, okt32, ovt32, carry, *, tiles_per_seq, prompt_tiles):
    i = pl.program_id(0)
    bd = _block_diag_ones()
    nh = WIDTH // HEAD_DIM

    def head_norm(x, g):
        ms = _segsum(x * x, bd) * (1.0 / HEAD_DIM)
        return x * lax.rsqrt(ms + RMS_EPS) * g

    qn = head_norm(q_ref[...], qn_ref[...])
    kn = head_norm(k_ref[...], kn_ref[...])
    v = v_ref[...]
    vt = v.T
    ovt[...] = vt.astype(BF16)

    @pl.when(i < prompt_tiles)
    def _():
        okt32[0] = kn.T
        ovt32[0] = vt

    @pl.when(i >= prompt_tiles)
    def _():
        oq[...] = qn
        ok[...] = kn
        ov[...] = v

    lf = -_softplus(-(f_ref[...] + bf_ref[...]))
    olf[...] = lf

    @pl.when(i % tiles_per_seq == 0)
    def _():
        carry[...] = jnp.zeros_like(carry)

    tm = lf.shape[0]
    r = lax.broadcasted_iota(jnp.int32, (tm, tm), 0)
    c = lax.broadcasted_iota(jnp.int32, (tm, tm), 1)
    tri = jnp.where(c <= r, 1.0, 0.0).astype(BF16)
    hi, mid, lo = _split3(lf)
    cum = _dot(tri, hi) + _dot(tri, mid) + _dot(tri, lo) + carry[...]
    carry[...] = cum[tm - 1:tm, :]

    c3 = jnp.concatenate(_split3(cum), axis=1)
    aug_q = _dot(c3, pq_ref[...]) + oneq_ref[...]
    aug_k = _dot(c3, pk_ref[...]) + onek_ref[...]
    qs = qn * (HEAD_DIM ** -0.5)

    def interleave(x, aug):
        pieces = []
        for h in range(nh):
            sl = slice(h * HEAD_DIM, (h + 1) * HEAD_DIM)
            pieces += [x[:, sl], aug[:, sl]]
        return jnp.concatenate(pieces, axis=1).astype(BF16)

    oqa[...] = interleave(qs, aug_q)
    oka[...] = interleave(kn, aug_k)


def _foxpost(z, q_norm_t, k_norm_t, bf_pad, batch, seq_len):
    m = z.shape[0]
    nh = WIDTH // HEAD_DIM
    n_p = batch * seq_len
    tm = min(_tile(math.gcd(n_p, m - n_p), 512), seq_len)
    npt, tps = n_p // tm, seq_len // tm
    sample = pl.BlockSpec((tm, WIDTH), lambda i: (jnp.maximum(i - npt, 0), 0))
    prompt_t = pl.BlockSpec((1, WIDTH, tm), lambda i: (jnp.minimum(i, npt - 1) // tps, 0, jnp.minimum(i, npt - 1) % tps))
    row = lambda c: pl.BlockSpec((tm, WIDTH), lambda i, c=c: (i, c // WIDTH))
    vec = pl.BlockSpec((1, WIDTH), lambda i: (0, 0))
    small = pl.BlockSpec((tm, LANES), lambda i: (i, 0))
    wide = pl.BlockSpec((tm, nh * AUG), lambda i: (i, 0))
    place = pl.BlockSpec((3 * LANES, WIDTH), lambda i: (0, 0))
    return pl.pallas_call(
        functools.partial(_foxpost_kernel, tiles_per_seq=tps, prompt_tiles=npt),
        grid=(m // tm,),
        in_specs=[row(C_Q), row(C_FK), row(C_FV),
                  pl.BlockSpec((tm, LANES), lambda i: (i, C_F // LANES)),
                  vec, vec, pl.BlockSpec((1, LANES), lambda i: (0, 0)), place, place, vec, vec],
        out_specs=[sample] * 3 + [small, wide, wide, pl.BlockSpec((WIDTH, tm), lambda i: (0, i)), prompt_t, prompt_t],
        out_shape=[jax.ShapeDtypeStruct((m - n_p, WIDTH), F32)] * 3 + [jax.ShapeDtypeStruct((m, LANES), F32),
                                                                        jax.ShapeDtypeStruct((m, nh * AUG), BF16),
                                                                        jax.ShapeDtypeStruct((m, nh * AUG), BF16),
                                                                        jax.ShapeDtypeStruct((WIDTH, m), BF16),
                                                                        jax.ShapeDtypeStruct((batch, WIDTH, seq_len), F32),
                                                                        jax.ShapeDtypeStruct((batch, WIDTH, seq_len), F32)],
        scratch_shapes=[pltpu.VMEM((1, LANES), F32)],
        compiler_params=_cp(("arbitrary",)),
        name="foxpost",
    )(z, z, z, z, q_norm_t, k_norm_t, bf_pad, *_aug_constants())


def _flash_kernel(qi_tab, ki_tab, qa_ref, ka_ref, vt_ref, og_ref, g_ref, init_hbm, o_ref, m_s, l_s, acc_s):
    s_idx = pl.program_id(2)
    qi = qi_tab[s_idx]
    ki = ki_tab[s_idx]
    tq = qa_ref.shape[0]
    tk = ka_ref.shape[0]

    @pl.when(ki == 0)
    def _():
        m_s[...] = jnp.full_like(m_s, -jnp.inf)
        l_s[...] = jnp.zeros_like(l_s)
        acc_s[...] = jnp.zeros_like(acc_s)

    def update(masked):
        for hh in range(2):
            st = _dot_nt(ka_ref[:, hh * AUG:(hh + 1) * AUG], qa_ref[:, hh * AUG:(hh + 1) * AUG])
            if masked:
                r = lax.broadcasted_iota(jnp.int32, (tk, tq), 0)
                c = lax.broadcasted_iota(jnp.int32, (tk, tq), 1)
                st = jnp.where(r <= c, st, -jnp.inf)
            m_prev = m_s[hh]
            m_new = jnp.maximum(m_prev, jnp.max(st, axis=0, keepdims=True))
            alpha = jnp.exp(m_prev - m_new)
            p = jnp.exp(st - m_new)
            l_s[hh] = alpha * l_s[hh] + jnp.sum(p, axis=0, keepdims=True)
            acc_s[hh] = alpha * acc_s[hh] + _dot(vt_ref[hh * HEAD_DIM:(hh + 1) * HEAD_DIM, :], p.astype(BF16))
            m_s[hh] = m_new

    @pl.when(ki < qi)
    def _():
        update(False)

    @pl.when(ki == qi)
    def _():
        update(True)
        outs = []
        for hh in range(2):
            sl = slice(hh * HEAD_DIM, (hh + 1) * HEAD_DIM)
            ot = acc_s[hh] / l_s[hh]
            ont = ot * lax.rsqrt(jnp.mean(ot * ot, axis=0, keepdims=True) + RMS_EPS)
            outs.append(ont.T * g_ref[:, sl] * jax.nn.sigmoid(og_ref[:, sl]))
        o_ref[...] = jnp.concatenate(outs, axis=1)


def _flash_prompt(qa, ka, vt, z, g_fox_o, batch, seq):
    tq = _tile(seq, 512)
    nq = seq // tq
    pairs = [(a, b) for a in range(nq) for b in range(a + 1)]
    qi_tab = jnp.array([a for a, _ in pairs], jnp.int32)
    ki_tab = jnp.array([b for _, b in pairs], jnp.int32)
    npairs = WIDTH // LANES
    qmap = lambda b, p, s, qt, kt: (b * nq + qt[s], p)
    kmap = lambda b, p, s, qt, kt: (b * nq + kt[s], p)
    gs = pltpu.PrefetchScalarGridSpec(
        num_scalar_prefetch=2,
        grid=(batch, npairs, len(pairs)),
        in_specs=[pl.BlockSpec((tq, 2 * AUG), qmap),
                  pl.BlockSpec((tq, 2 * AUG), kmap),
                  pl.BlockSpec((LANES, tq), lambda b, p, s, qt, kt: (p, b * nq + kt[s])),
                  pl.BlockSpec((tq, LANES), lambda b, p, s, qt, kt: (b * nq + qt[s], C_OG // LANES + p)),
                  pl.BlockSpec((1, LANES), lambda b, p, s, qt, kt: (0, p)),
                  pl.BlockSpec(memory_space=pl.ANY)],
        out_specs=pl.BlockSpec((tq, LANES), qmap),
        scratch_shapes=[pltpu.VMEM((2, 1, tq), F32), pltpu.VMEM((2, 1, tq), F32), pltpu.VMEM((2, HEAD_DIM, tq), F32)],
    )
    return pl.pallas_call(
        _flash_kernel,
        grid_spec=gs,
        out_shape=jax.ShapeDtypeStruct((qa.shape[0], WIDTH), F32),
        input_output_aliases={7: 0},
        compiler_params=_cp(("arbitrary", "arbitrary", "arbitrary")),
        name="flash_prompt",
    )(qi_tab, ki_tab, qa, ka, vt, z, g_fox_o, jnp.zeros((qa.shape[0], WIDTH), F32))


PAGE_GROUP = 4
PAGE_BUFS = 12


def _paged_kernel(pt_ref, q_ref, kn_ref, vn_ref, lfn_ref, og_ref, g_ref, ck_hbm, cv_hbm, lf_hbm, of_hbm, o_ref,
                  kbuf, vbuf, lbuf, sem, *, n_pages):
    b = pl.program_id(0)
    total = pl.num_programs(0) * n_pages
    depth = PAGE_BUFS - PAGE_GROUP
    nh = WIDTH // HEAD_DIM
    rows = nh * DEC_SEQ

    def copies(page, slot):
        return (pltpu.make_async_copy(ck_hbm.at[page], kbuf.at[slot], sem.at[0, slot]),
                pltpu.make_async_copy(cv_hbm.at[page], vbuf.at[slot], sem.at[1, slot]),
                pltpu.make_async_copy(lf_hbm.at[page], lbuf.at[slot], sem.at[2, slot]))

    def fetch(g):
        bb = g // n_pages
        for c in copies(pt_ref[bb, g - bb * n_pages], g % PAGE_BUFS):
            c.start()

    @pl.when(b == 0)
    def _():
        for g in range(depth):
            fetch(g)

    q = (q_ref[...] * (HEAD_DIM ** -0.5)).astype(BF16)
    qh = [q[:, h * HEAD_DIM:(h + 1) * HEAD_DIM] for h in range(nh)]
    def cumulate(lf, carry):
        n = lf.shape[1]
        r = lax.broadcasted_iota(jnp.int32, (n, n), 0)
        c = lax.broadcasted_iota(jnp.int32, (n, n), 1)
        upper = jnp.where(r <= c, 1.0, 0.0).astype(BF16)
        hi, mid, lo = _split3(lf)
        cum = _dot(hi, upper) + _dot(mid, upper) + _dot(lo, upper) + carry
        return cum, cum[:, n - 1:n]

    def attend(pages, feature_major, ck, valid, m, l, acc):
        qk = _dot if feature_major else _dot_nt
        pv_dot = _dot_nt if feature_major else _dot
        s = jnp.concatenate([jnp.concatenate([qk(qh[h], kh[h]) for kh, _ in pages], axis=1) for h in range(nh)], axis=0)
        s = s - jnp.concatenate([jnp.broadcast_to(ck[h:h + 1, :], (DEC_SEQ, ck.shape[1])) for h in range(nh)], axis=0)
        if valid is not None:
            s = jnp.where(valid, s, -jnp.inf)
        m_new = jnp.maximum(m, jnp.max(s, axis=-1, keepdims=True))
        alpha = jnp.exp(m - m_new)
        p = jnp.exp(s - m_new)
        l = alpha * l + jnp.sum(p, axis=-1, keepdims=True)
        pb = p.astype(BF16)
        pv = []
        for h in range(nh):
            rows_h = slice(h * DEC_SEQ, (h + 1) * DEC_SEQ)
            pv.append(sum(pv_dot(pb[rows_h, i * PAGE:(i + 1) * PAGE], vh[h]) for i, (_, vh) in enumerate(pages)))
        return m_new, l, alpha * acc + jnp.concatenate(pv, axis=0)

    def pair_step(jj, carry):
        m, l, acc, ccar = carry
        g0 = b * n_pages + PAGE_GROUP * jj
        slots = [(g0 + u) % PAGE_BUFS for u in range(PAGE_GROUP)]
        for slot in slots:
            for cp in copies(0, slot):
                cp.wait()
        for u in range(PAGE_GROUP):
            @pl.when(g0 + depth + u < total)
            def _():
                fetch(g0 + depth + u)

        ck, ccar = cumulate(jnp.concatenate([lbuf[slot] for slot in slots], axis=1), ccar)
        pages = [([kbuf[slot, h].astype(BF16) for h in range(nh)],
                  [vbuf[slot, h].astype(BF16) for h in range(nh)]) for slot in slots]
        m, l, acc = attend(pages, True, ck, None, m, l, acc)
        return m, l, acc, ccar

    init = (jnp.full((rows, 1), -jnp.inf, F32), jnp.zeros((rows, 1), F32), jnp.zeros((rows, HEAD_DIM), F32),
            jnp.zeros((nh, 1), F32))
    m, l, acc, ccar = lax.fori_loop(0, n_pages // PAGE_GROUP, pair_step, init)

    pad = jnp.zeros((PAGE - DEC_SEQ, HEAD_DIM), BF16)
    kn = kn_ref[...].astype(BF16)
    vn = vn_ref[...].astype(BF16)
    k_heads = [jnp.concatenate([kn[:, h * HEAD_DIM:(h + 1) * HEAD_DIM], pad], axis=0) for h in range(nh)]
    v_heads = [jnp.concatenate([vn[:, h * HEAD_DIM:(h + 1) * HEAD_DIM], pad], axis=0) for h in range(nh)]
    ck, _ = cumulate(lfn_ref[0], ccar)
    rr = lax.broadcasted_iota(jnp.int32, (rows, PAGE), 0)
    cc = lax.broadcasted_iota(jnp.int32, (rows, PAGE), 1)
    m, l, acc = attend([(k_heads, v_heads)], False, ck, cc <= (rr & (DEC_SEQ - 1)), m, l, acc)
    o = acc / l
    on = o * lax.rsqrt(jnp.mean(o * o, axis=-1, keepdims=True) + RMS_EPS)
    out = jnp.concatenate([on[h * DEC_SEQ:(h + 1) * DEC_SEQ, :] for h in range(nh)], axis=1)
    o_ref[...] = out * g_ref[...] * jax.nn.sigmoid(og_ref[...])


def _paged_sample(page_table, qn, kn, v, lf_new_t, cache_k, cache_v, lf_pages_t, z, g_fox_o, o_f, row0):
    db, n_pages = page_table.shape
    assert n_pages % PAGE_GROUP == 0 and n_pages >= PAGE_BUFS
    nh = WIDTH // HEAD_DIM
    rb = row0 // DEC_SEQ
    newmap = lambda b, pt: (b, 0)
    hbm = pl.BlockSpec(memory_space=pl.ANY)
    gs = pltpu.PrefetchScalarGridSpec(
        num_scalar_prefetch=1,
        grid=(db,),
        in_specs=[pl.BlockSpec((DEC_SEQ, WIDTH), newmap),
                  pl.BlockSpec((DEC_SEQ, WIDTH), newmap),
                  pl.BlockSpec((DEC_SEQ, WIDTH), newmap),
                  pl.BlockSpec((1, nh, PAGE), lambda b, pt: (b, 0, 0)),
                  pl.BlockSpec((DEC_SEQ, WIDTH), lambda b, pt: (rb + b, C_OG // WIDTH)),
                  pl.BlockSpec((1, WIDTH), lambda b, pt: (0, 0)),
                  hbm, hbm, hbm, hbm],
        out_specs=pl.BlockSpec((DEC_SEQ, WIDTH), lambda b, pt: (rb + b, 0)),
        scratch_shapes=[pltpu.VMEM((PAGE_BUFS, nh, HEAD_DIM, PAGE), F32), pltpu.VMEM((PAGE_BUFS, nh, HEAD_DIM, PAGE), F32),
                        pltpu.VMEM((PAGE_BUFS, nh, PAGE), F32), pltpu.SemaphoreType.DMA((3, PAGE_BUFS))],
    )
    return pl.pallas_call(
        functools.partial(_paged_kernel, n_pages=n_pages),
        grid_spec=gs,
        out_shape=jax.ShapeDtypeStruct(o_f.shape, F32),
        input_output_aliases={10: 0},
        compiler_params=_cp(("arbitrary",)),
        name="paged_sample",
    )(page_table, qn, kn, v, lf_new_t, z, g_fox_o, cache_k, cache_v, lf_pages_t, o_f)


def _prep_kernel(z_ref, first_ref, mu_ref, w12_ref, w3_ref, w0_ref, a0_ref, kk_ref, ka_ref, rk_ref,
                 g_hbm, bonus_hbm, o_r, o_w, o_k, o_v, o_a, o_b, o_g, o_bonus, carry, *, seq_len, tiles_per_seq):
    z = z_ref[...]
    tm = z.shape[0]
    row = lax.broadcasted_iota(jnp.int32, (tm, 1), 0)
    rolled = pltpu.roll(z, 1, 0)
    if tiles_per_seq >= 1 and seq_len >= tm:
        li = pl.program_id(0) % tiles_per_seq

        @pl.when(li == 0)
        def _():
            carry[...] = first_ref[0]

        zp = jnp.where(row == 0, carry[...], rolled)
        carry[...] = z[tm - 1:tm, :]
    else:
        nseq = tm // seq_len
        first = first_ref[...]
        exp = jnp.broadcast_to(first, (nseq, seq_len, first.shape[-1])).reshape(tm, first.shape[-1])
        zp = jnp.where((row & (seq_len - 1)) == 0, exp, rolled)
    zm = z + (zp - z) * mu_ref[...]
    r = zm[:, 0:WIDTH]
    k = zm[:, WIDTH:2 * WIDTH]
    v = zm[:, 2 * WIDTH:3 * WIDTH]
    lo = zm[:, C_LORA:C_LORA + 384]
    lane = lax.broadcasted_iota(jnp.int32, (tm, 384), 1)
    act = jnp.where(lane < 64, jnp.tanh(lo), jnp.where(lane < 128, lo, jax.nn.sigmoid(lo))).astype(BF16)
    l12 = _dot(act[:, 0:128], w12_ref[...])
    g = _dot(act[:, 128:384], w3_ref[...])
    log_w = -_softplus(-(w0_ref[...] + l12[:, 0:WIDTH])) - 0.5
    decay = jnp.exp(-jnp.exp(log_w))
    asig = jax.nn.sigmoid(a0_ref[...] + l12[:, WIDTH:2 * WIDTH])
    bd = _block_diag_ones()
    kk = k * kk_ref[...]
    kk = kk / jnp.maximum(jnp.sqrt(_segsum(kk * kk, bd)), 1e-12)
    kf = k * (1.0 + (asig - 1.0) * ka_ref[...])
    o_r[...] = r
    o_w[...] = decay
    o_k[...] = kf
    o_v[...] = v
    o_a[...] = -kk
    o_b[...] = kk * asig
    o_g[...] = g
    o_bonus[...] = _segsum(r * kf * rk_ref[...], bd) * v


def _rwkv_prep(z, first, mu, w12, w3, w0, a0, k_k, k_a, r_k, row0, n_rows, seq_len, shared):
    if seq_len >= 256:
        tm = _tile(seq_len, 256)
        first_spec = pl.BlockSpec((1, 1, RWKV_COLS), lambda i: (i // (seq_len // tm), 0, 0))
    else:
        tm = _tile(n_rows, 128)
        nseq = tm // seq_len
        first_spec = pl.BlockSpec((nseq, 1, RWKV_COLS), lambda i: (i, 0, 0))
    rb = row0 // tm
    vec = lambda n: pl.BlockSpec((1, n), lambda i: (0, 0))
    out = pl.BlockSpec((tm, WIDTH), lambda i: (i, 0))
    out_all = pl.BlockSpec((tm, WIDTH), lambda i: (rb + i, 0))
    hbm = pl.BlockSpec(memory_space=pl.ANY)
    extra = tuple(shared)
    return pl.pallas_call(
        functools.partial(_prep_kernel, seq_len=seq_len, tiles_per_seq=max(seq_len // tm, 1)),
        grid=(n_rows // tm,),
        in_specs=[pl.BlockSpec((tm, RWKV_COLS), lambda i: (rb + i, 0)), first_spec, vec(RWKV_COLS),
                  pl.BlockSpec((128, 2 * WIDTH), lambda i: (0, 0)), pl.BlockSpec((256, WIDTH), lambda i: (0, 0)),
                  vec(WIDTH), vec(WIDTH), vec(WIDTH), vec(WIDTH), vec(WIDTH)] + [hbm] * len(extra),
        out_specs=[out] * 6 + [out_all] * 2,
        out_shape=[jax.ShapeDtypeStruct((n_rows, WIDTH), F32)] * 6 + [jax.ShapeDtypeStruct((z.shape[0], WIDTH), F32)] * 2,
        input_output_aliases={10: 6, 11: 7},
        scratch_shapes=[pltpu.VMEM((1, RWKV_COLS), F32)],
        compiler_params=_cp(("arbitrary",)),
        name=f"rwkv_prep_{seq_len}",
    )(z, first, mu, w12, w3, w0, a0, k_k, k_a, r_k, *extra)


SCAN_SUB = 64
SCAN_SPLIT = 2
SCAN_UNROLL = 4


def _wkv_kernel(r_ref, w_ref, k_ref, v_ref, a_ref, b_ref, s0_ref, y_ref, sT_ref, S_s, Z_s, *, groups, n_pairs, tb_len):
    tb = pl.program_id(1)
    r2 = lax.broadcasted_iota(jnp.int32, (2 * LANES, 2 * LANES), 0)
    c2 = lax.broadcasted_iota(jnp.int32, (2 * LANES, 2 * LANES), 1)
    ones_bd = jnp.where((r2 >> 6) == (c2 >> 6), 1.0, 0.0).astype(BF16)
    vrow = lax.broadcasted_iota(jnp.int32, (HEAD_DIM, LANES), 0)
    lane = lax.broadcasted_iota(jnp.int32, (HEAD_DIM, LANES), 1)
    eye2 = (lane & (HEAD_DIM - 1)) == vrow
    zero_half = jnp.zeros((HEAD_DIM, LANES), BF16)
    H = HEAD_DIM

    @pl.when(tb == 0)
    def _():
        for g in range(groups):
            for p in range(n_pairs):
                S_s[g * n_pairs + p] = jnp.concatenate([s0_ref[g, 2 * p], s0_ref[g, 2 * p + 1]], axis=1)
        Z_s[...] = jnp.zeros_like(Z_s)

    def lanes(p):
        return slice(p * LANES, (p + 1) * LANES)

    def run(t0, n):
        def step(i, carry):
            t = t0 + i
            tp = jnp.maximum(t - 1, 0)
            hit = (lane & (SCAN_SUB - 1)) == (i - 1)
            for g in range(groups):
                aa = a_ref[g, pl.ds(t, 1), :]
                ww = w_ref[g, pl.ds(t, 1), :]
                kr = k_ref[g, pl.ds(t, 1), :]
                vv = v_ref[g, pl.ds(t, 1), :]
                bb = b_ref[g, pl.ds(t, 1), :]
                rp = r_ref[g, pl.ds(tp, 1), :]
                npg = n_pairs // SCAN_SPLIT
                for p0 in range(0, n_pairs, npg):
                    lhs_s, xs = [], []
                    for p in range(p0, p0 + npg):
                        S = S_s[g * n_pairs + p]
                        lhs_s.append(jnp.concatenate([(S * aa[:, lanes(p)]).astype(BF16),
                                                      (S * rp[:, lanes(p)]).astype(BF16)], axis=1))
                        xs.append(jnp.where(eye2, vv[:, lanes(p)], 0.0).astype(BF16))
                    lhs_x = [jnp.concatenate(xs[q:q + 2], axis=1) for q in range(0, npg, 2)]
                    res = _dot(jnp.concatenate(lhs_s + lhs_x, axis=0), ones_bd)
                    for q in range(npg):
                        p = p0 + q
                        idx = g * n_pairs + p
                        rs = res[q * H:(q + 1) * H]
                        vb = res[(npg + q // 2) * H:(npg + q // 2 + 1) * H, (q % 2) * LANES:(q % 2 + 1) * LANES]
                        S_s[idx] = S_s[idx] * ww[:, lanes(p)] + rs[:, :LANES] * bb[:, lanes(p)] + vb * kr[:, lanes(p)]
                        Z_s[idx] = jnp.where(hit, rs[:, LANES:], Z_s[idx])
            return carry

        lax.fori_loop(0, n, step, 0, unroll=SCAN_UNROLL)
        hit_last = (lane & (SCAN_SUB - 1)) == (n - 1)
        for g in range(groups):
            rl = r_ref[g, pl.ds(t0 + n - 1, 1), :]
            lhs = [jnp.concatenate([(S_s[g * n_pairs + p] * rl[:, lanes(p)]).astype(BF16), zero_half], axis=1)
                   for p in range(n_pairs)]
            res = _dot(jnp.concatenate(lhs, axis=0), ones_bd)
            for p in range(n_pairs):
                idx = g * n_pairs + p
                zt = jnp.where(hit_last, res[p * H:(p + 1) * H, :LANES], Z_s[idx]).T
                y_ref[g, pl.ds(t0, n), p * LANES:p * LANES + H] = zt[0:n, :]
                y_ref[g, pl.ds(t0, n), p * LANES + H:(p + 1) * LANES] = zt[SCAN_SUB:SCAN_SUB + n, :]

    if tb_len <= SCAN_SUB:
        run(0, tb_len)
    else:
        def outer(blk, carry):
            run(pl.multiple_of(blk * SCAN_SUB, SCAN_SUB), SCAN_SUB)
            return carry
        lax.fori_loop(0, tb_len // SCAN_SUB, outer, 0)

    @pl.when(tb == pl.num_programs(1) - 1)
    def _():
        for g in range(groups):
            for p in range(n_pairs):
                S = S_s[g * n_pairs + p]
                sT_ref[g, 2 * p] = S[:, 0:H]
                sT_ref[g, 2 * p + 1] = S[:, H:LANES]


def _wkv_scan(r, w, k, v, a, b, s0, n_seq, seq_len):
    nh = WIDTH // HEAD_DIM
    groups = 2 if n_seq % 2 == 0 else 1
    tb_len = min(seq_len, 128)
    nt = seq_len // tb_len
    blk = pl.BlockSpec((groups, tb_len, WIDTH), lambda s, t: (s, t, 0))
    st = pl.BlockSpec((groups, nh, HEAD_DIM, HEAD_DIM), lambda s, t: (s, 0, 0, 0))
    n_pairs = WIDTH // LANES
    seq3 = lambda x: x.reshape(n_seq, seq_len, WIDTH)
    y, s_new = pl.pallas_call(
        functools.partial(_wkv_kernel, groups=groups, n_pairs=n_pairs, tb_len=tb_len),
        grid=(n_seq // groups, nt),
        in_specs=[blk] * 6 + [st],
        out_specs=[blk, st],
        out_shape=[jax.ShapeDtypeStruct((n_seq, seq_len, WIDTH), F32),
                   jax.ShapeDtypeStruct((n_seq, nh, HEAD_DIM, HEAD_DIM), F32)],
        scratch_shapes=[pltpu.VMEM((groups * n_pairs, HEAD_DIM, LANES), F32),
                        pltpu.VMEM((groups * n_pairs, HEAD_DIM, LANES), F32)],
        compiler_params=_cp(("arbitrary", "arbitrary")),
        name=f"wkv_scan_{seq_len}",
    )(seq3(r), seq3(w), seq3(k), seq3(v), seq3(a), seq3(b), s0)
    return y.reshape(n_seq * seq_len, WIDTH), s_new


def _outproj_kernel(xp_ref, xs_ref, yp_ref, ys_ref, bonus_ref, g_ref, of_ref, lw_ref, lb_ref, wo_ref, gf_ref,
                    wrg_ref, wre_ref, brg_ref, bre_ref, x1_ref, h_ref, route_ref, wrh_s, wrl_s, br_s, *, prompt_tiles):
    i = pl.program_id(0)

    @pl.when(i == 0)
    def _():
        pad = LANES - N_GROUPS - N_EXPERTS
        w_r = jnp.concatenate([wrg_ref[...], wre_ref[...], jnp.zeros((pad, wrg_ref.shape[1]), F32)], axis=0)
        hi = w_r.astype(BF16)
        wrh_s[...] = hi
        wrl_s[...] = (w_r - hi.astype(F32)).astype(BF16)
        br_s[...] = jnp.concatenate([brg_ref[...], bre_ref[...], jnp.zeros((1, pad), F32)], axis=1)

    is_prompt = i < prompt_tiles
    bd = _block_diag_ones()
    y = jnp.where(is_prompt, yp_ref[...], ys_ref[...])
    mu = _segsum(y, bd) * (1.0 / HEAD_DIM)
    d = y - mu
    var = _segsum(d * d, bd) * (1.0 / HEAD_DIM)
    yn = d * lax.rsqrt(var + LNX_EPS) * lw_ref[...] + lb_ref[...]
    o_r = ((yn + bonus_ref[...]) * g_ref[...]).astype(BF16)
    o_f = of_ref[...].astype(BF16)
    x = jnp.where(is_prompt, xp_ref[...], xs_ref[...])
    x1 = x + _dot(o_r, wo_ref[0:WIDTH, :]) + _dot(o_f, wo_ref[WIDTH:2 * WIDTH, :])
    x1_ref[...] = x1
    h = _rms_rows(x1, gf_ref[...])
    h_ref[...] = h
    hi, lo = _split2(h)
    logits = _dot_nt(hi, wrh_s[...]) + _dot_nt(lo, wrh_s[...]) + _dot_nt(hi, wrl_s[...]) + br_s[...]
    tm = logits.shape[0]
    lane = lax.broadcasted_iota(jnp.int32, (tm, LANES), 1)
    neg = -jnp.inf
    lg = jnp.where(lane < N_GROUPS, logits, neg)
    mg = jnp.max(lg, axis=-1, keepdims=True)
    pg_top = 1.0 / jnp.sum(jnp.exp(lg - mg), axis=-1, keepdims=True)
    g_sel = jnp.min(jnp.where(lg == mg, lane, LANES), axis=-1, keepdims=True)
    in_grp = (lane >= N_GROUPS) & (lane < N_GROUPS + N_EXPERTS) & (((lane - N_GROUPS) >> 3) == g_sel)
    le = jnp.where(in_grp, logits, neg)
    m1 = jnp.max(le, axis=-1, keepdims=True)
    i1 = jnp.min(jnp.where(le == m1, lane, LANES), axis=-1, keepdims=True)
    le2 = jnp.where(lane == i1, neg, le)
    m2 = jnp.max(le2, axis=-1, keepdims=True)
    i2 = jnp.min(jnp.where(le2 == m2, lane, LANES), axis=-1, keepdims=True)
    e2 = jnp.exp(m2 - m1)
    gate1 = pg_top / (1.0 + e2)
    gate2 = pg_top * e2 / (1.0 + e2)
    route = jnp.where(lane == 0, (i1 - N_GROUPS).astype(F32),
                      jnp.where(lane == 1, (i2 - N_GROUPS).astype(F32),
                                jnp.where(lane == 2, gate1, jnp.where(lane == 3, gate2, 0.0))))
    route_ref[...] = route


def _outproj(x_p, x_s, y_p, y_s, bonus, g, o_f, lnx_w, lnx_b, w_out, g_ffn, w_rg, w_re, b_rg, b_re):
    (n_p, d), n_s = x_p.shape, x_s.shape[0]
    m = n_p + n_s
    tm = _tile(math.gcd(n_p, n_s), 256)
    pmap, smap = _row_maps(n_p // tm)
    row = lambda n: pl.BlockSpec((tm, n), lambda i: (i, 0))
    vec = lambda n: pl.BlockSpec((1, n), lambda i: (0, 0))
    full = lambda a: pl.BlockSpec(a.shape, lambda i: (0, 0))
    return pl.pallas_call(
        functools.partial(_outproj_kernel, prompt_tiles=n_p // tm),
        grid=(m // tm,),
        in_specs=[pl.BlockSpec((tm, d), pmap), pl.BlockSpec((tm, d), smap),
                  pl.BlockSpec((tm, WIDTH), pmap), pl.BlockSpec((tm, WIDTH), smap),
                  row(WIDTH), row(WIDTH), row(WIDTH), vec(WIDTH), vec(WIDTH),
                  full(w_out), vec(d), full(w_rg), full(w_re), full(b_rg), full(b_re)],
        out_specs=[row(d), row(d), row(LANES)],
        out_shape=[jax.ShapeDtypeStruct((m, d), F32), jax.ShapeDtypeStruct((m, d), F32),
                   jax.ShapeDtypeStruct((m, LANES), F32)],
        scratch_shapes=[pltpu.VMEM((LANES, d), BF16), pltpu.VMEM((LANES, d), BF16), pltpu.VMEM((1, LANES), F32)],
        compiler_params=_cp(("arbitrary",)),
        name="outproj_router",
    )(x_p, x_s, y_p, y_s, bonus, g, o_f, lnx_w, lnx_b, w_out, g_ffn, w_rg, w_re, b_rg, b_re)


def _expert_kernel(plan_ref, tok_ref, nu_ref, h_hbm, wg_hbm, wu_hbm, wd_hbm, o_ref, xbuf, wg, wu, wd, sem, wsem):
    i = pl.program_id(0)
    n_used = nu_ref[0]
    first, nxt, wslot = plan_ref[1, i], plan_ref[2, i], plan_ref[3, i]

    def row_copy(tok, r, slot):
        return pltpu.make_async_copy(h_hbm.at[pl.ds(tok, 1), :], xbuf.at[slot, pl.ds(r, 1), :], sem.at[slot])

    def gather(blk, slot):
        def start(r, c):
            row_copy(tok_ref[blk * MOE_BLOCK + r], r, slot).start()
            return c
        lax.fori_loop(0, plan_ref[4, blk], start, 0)

    def weight_copies(expert, slot):
        return (pltpu.make_async_copy(wg_hbm.at[expert], wg.at[slot], wsem.at[0, slot]),
                pltpu.make_async_copy(wu_hbm.at[expert], wu.at[slot], wsem.at[1, slot]),
                pltpu.make_async_copy(wd_hbm.at[expert], wd.at[slot], wsem.at[2, slot]))

    @pl.when(i == 0)
    def _():
        xbuf[...] = jnp.zeros_like(xbuf)
        gather(0, 0)
        for c in weight_copies(plan_ref[0, 0], 0):
            c.start()

    @pl.when(i < n_used)
    def _():
        slot = i & 1

        @pl.when(first == 1)
        def _():
            for c in weight_copies(0, wslot):
                c.wait()

            @pl.when(nxt >= 0)
            def _():
                for c in weight_copies(nxt, 1 - wslot):
                    c.start()

        def wait(r, c):
            row_copy(0, r, slot).wait()
            return c
        lax.fori_loop(0, plan_ref[4, i], wait, 0)

        @pl.when(i + 1 < n_used)
        def _():
            gather(i + 1, 1 - slot)

        x = xbuf[slot].astype(BF16)
        gate = _dot(x, wg[wslot].astype(BF16))
        up = _dot(x, wu[wslot].astype(BF16))
        act = (gate * jax.nn.sigmoid(gate) * up).astype(BF16)
        o_ref[...] = _dot(act, wd[wslot].astype(BF16))

    @pl.when(i >= n_used)
    def _():
        o_ref[...] = jnp.zeros_like(o_ref)


def _experts(plan, slot_tok, n_used, h, w_gate, w_up, w_down):
    n_blocks = plan.shape[1]
    d = h.shape[1]
    de = w_gate.shape[2]
    hbm = pl.BlockSpec(memory_space=pl.ANY)
    gs = pltpu.PrefetchScalarGridSpec(
        num_scalar_prefetch=3,
        grid=(n_blocks,),
        in_specs=[hbm, hbm, hbm, hbm],
        out_specs=pl.BlockSpec((MOE_BLOCK, d), lambda i, plan, tok, nu: (i, 0)),
        scratch_shapes=[pltpu.VMEM((2, MOE_BLOCK, d), F32), pltpu.VMEM((2, d, de), F32), pltpu.VMEM((2, d, de), F32),
                        pltpu.VMEM((2, de, d), F32), pltpu.SemaphoreType.DMA((2,)), pltpu.SemaphoreType.DMA((3, 2))],
    )
    return pl.pallas_call(
        _expert_kernel,
        grid_spec=gs,
        out_shape=jax.ShapeDtypeStruct((n_blocks * MOE_BLOCK, d), F32),
        compiler_params=_cp(("arbitrary",), 58 * 1024 * 1024),
        name="experts",
    )(plan, slot_tok, n_used, h, w_gate, w_up, w_down)


def _ple_kernel(dest_ref, x1_ref, route_ref, p_ref, yb_hbm, gp_ref, wpg_ref, bpg_ref, wpp_ref, gfin_ref, op_ref, os_ref,
                buf, sem, *, prompt_tiles):
    i = pl.program_id(0)
    tm = x1_ref.shape[0]

    def row_copy(slot_row, r, k, half):
        return pltpu.make_async_copy(yb_hbm.at[pl.ds(slot_row, 1), :], buf.at[half, k, pl.ds(r, 1), :], sem.at[half])

    def gather(tile, half):
        def start(r, c):
            for k in range(2):
                row_copy(dest_ref[(tile * tm + r) * 2 + k], r, k, half).start()
            return c
        lax.fori_loop(0, tm, start, 0, unroll=8)

    @pl.when(i == 0)
    def _():
        gather(0, 0)

    half = i & 1

    def wait(r, c):
        for k in range(2):
            row_copy(0, r, k, half).wait()
        return c
    lax.fori_loop(0, tm, wait, 0, unroll=8)

    @pl.when(i + 1 < pl.num_programs(0))
    def _():
        gather(i + 1, 1 - half)

    route = route_ref[...]
    moe = buf[half, 0] * route[:, 2:3] + buf[half, 1] * route[:, 3:4]
    x2 = x1_ref[...] + moe
    hn = _rms_rows(x2, gp_ref[...]).astype(BF16)
    gate = jax.nn.sigmoid(_dot(hn, wpg_ref[...]) + bpg_ref[...])
    pe = _dot(p_ref[...].astype(BF16), wpp_ref[...])
    x3 = x2 + gate * pe
    y = _rms_rows(x3, gfin_ref[...])

    @pl.when(i < prompt_tiles)
    def _():
        op_ref[...] = y

    @pl.when(i >= prompt_tiles)
    def _():
        os_ref[...] = y


def _ple(dest, x1, route, p, yb, g_ple, w_pg, b_pg, w_pp, g_final, n_p):
    m, d = x1.shape
    tm = _tile(math.gcd(n_p, m - n_p), 256)
    npt = n_p // tm
    pd = p.shape[1]
    gs = pltpu.PrefetchScalarGridSpec(
        num_scalar_prefetch=1,
        grid=(m // tm,),
        in_specs=[pl.BlockSpec((tm, d), lambda i, ds: (i, 0)),
                  pl.BlockSpec((tm, LANES), lambda i, ds: (i, 0)),
                  pl.BlockSpec((tm, pd), lambda i, ds: (i, 0)),
                  pl.BlockSpec(memory_space=pl.ANY),
                  pl.BlockSpec((1, d), lambda i, ds: (0, 0)),
                  pl.BlockSpec((d, d), lambda i, ds: (0, 0)),
                  pl.BlockSpec((1, d), lambda i, ds: (0, 0)),
                  pl.BlockSpec((pd, d), lambda i, ds: (0, 0)),
                  pl.BlockSpec((1, d), lambda i, ds: (0, 0))],
        out_specs=[pl.BlockSpec((tm, d), lambda i, ds: (jnp.minimum(i, npt - 1), 0)),
                   pl.BlockSpec((tm, d), lambda i, ds: (jnp.maximum(i - npt, 0), 0))],
        scratch_shapes=[pltpu.VMEM((2, 2, tm, d), F32), pltpu.SemaphoreType.DMA((2,))],
    )
    return pl.pallas_call(
        functools.partial(_ple_kernel, prompt_tiles=npt),
        grid_spec=gs,
        out_shape=[jax.ShapeDtypeStruct((n_p, d), F32), jax.ShapeDtypeStruct((m - n_p, d), F32)],
        compiler_params=_cp(("arbitrary",)),
        name="moe_combine_ple",
    )(dest, x1, route, p, yb, g_ple, w_pg, b_pg, w_pp, g_final)


def _dispatch(route, n_tok):
    expert = route[:, 0:2].astype(jnp.int32)
    flat_e = expert.reshape(-1)
    n_assign = flat_e.shape[0]
    onehot = (flat_e[:, None] == jnp.arange(N_EXPERTS, dtype=jnp.int32)[None, :]).astype(jnp.int32)
    csum = jnp.cumsum(onehot, axis=0)
    rank = jnp.sum(csum * onehot, axis=1) - 1
    counts = csum[-1]
    pcounts = (counts + MOE_BLOCK - 1) // MOE_BLOCK * MOE_BLOCK
    pend = jnp.cumsum(pcounts)
    pstart = pend - pcounts
    dest = (pstart[flat_e] + rank).astype(jnp.int32)
    n_blocks = -(-n_assign // MOE_BLOCK) + N_EXPERTS
    slot_tok = jnp.zeros((n_blocks * MOE_BLOCK,), jnp.int32).at[dest].set(jnp.arange(n_assign, dtype=jnp.int32) // 2)
    n_used = (pend[-1] // MOE_BLOCK).astype(jnp.int32)
    blk = jnp.arange(n_blocks, dtype=jnp.int32)
    block_e = jnp.searchsorted(pend, jnp.minimum(blk, n_used - 1) * MOE_BLOCK, side='right').astype(jnp.int32)
    block_e = jnp.minimum(block_e, N_EXPERTS - 1)
    first = jnp.concatenate([jnp.ones((1,), jnp.int32), (block_e[1:] != block_e[:-1]).astype(jnp.int32)])
    first = jnp.where(blk < n_used, first, 0)
    used = counts > 0
    eidx = jnp.arange(N_EXPERTS, dtype=jnp.int32)
    later = jnp.flip(lax.cummin(jnp.flip(jnp.where(used, eidx, N_EXPERTS))))
    next_used = jnp.concatenate([later[1:], jnp.full((1,), N_EXPERTS, jnp.int32)])
    next_used = jnp.where(next_used < N_EXPERTS, next_used, -1)
    ordinal = jnp.cumsum(used.astype(jnp.int32)) - 1
    valid = jnp.clip(pstart[block_e] + counts[block_e] - blk * MOE_BLOCK, 0, MOE_BLOCK)
    valid = jnp.where(blk < n_used, valid, 0)
    plan = jnp.stack([block_e, first, next_used[block_e], ordinal[block_e] & 1, valid]).astype(jnp.int32)
    return plan, slot_tok, n_used.reshape(1), dest


def kernel(x_prompt, x_sample, cache_k, cache_v, cache_logf, state_wkv, state_shift, page_table, p_prompt, p_sample, g_attn, w_in, mu_shift, w0, w_up, a0, a_up, g_up, k_k, k_a, r_k, lnx_w, lnx_b, b_f, q_norm, k_norm, g_fox_o, w_out, g_ffn, w_rg, b_rg, w_re, b_re, w_e_gate, w_e_up, w_e_down, g_ple, w_pg, b_pg, w_pp, g_final):
    depth = g_attn.shape[0]
    assert depth == 1
    batch, seq, d = x_prompt.shape
    db, dec_seq, _ = x_sample.shape
    assert dec_seq == DEC_SEQ
    nh = WIDTH // HEAD_DIM
    n_p = batch * seq
    n_s = db * dec_seq
    m = n_p + n_s
    rwkv_in = mu_shift.shape[1]
    n_pool = cache_k.shape[1]
    n_pages = page_table.shape[1]
    row = lambda a: a.reshape(1, -1)

    x_p, x_s = x_prompt.reshape(n_p, d), x_sample.reshape(n_s, d)
    p_all = jnp.concatenate([p_prompt[0].reshape(n_p, -1), p_sample[0].reshape(n_s, -1)], axis=0)

    wi = w_in[0]
    zc = lambda n: jnp.zeros((d, n), F32)
    w_z = jnp.concatenate([wi[:, :rwkv_in], zc(C_F - rwkv_in), wi[:, rwkv_in + 4 * WIDTH:], zc(C_Q - C_F - nh),
                           wi[:, rwkv_in:rwkv_in + 4 * WIDTH]], axis=1).astype(BF16)
    w12 = jnp.zeros((128, 2 * WIDTH), F32).at[0:64, 0:WIDTH].set(w_up[0]).at[64:128, WIDTH:].set(a_up[0]).astype(BF16)
    w3 = jnp.zeros((256, WIDTH), F32).at[0:g_up.shape[1]].set(g_up[0]).astype(BF16)
    pad_cols = lambda a, n: jnp.pad(a, ((0, 0), (0, n - a.shape[1])))
    mu_pad = pad_cols(mu_shift, RWKV_COLS)
    bf_pad = pad_cols(b_f, LANES)
    qn_t = jnp.tile(q_norm, (1, nh))
    kn_t = jnp.tile(k_norm, (1, nh))

    z = _inproj(x_p, x_s, g_attn, w_z)

    qn_s, kn_s, vv_s, logf, qa, ka, vt, kt_p, vt_p = _foxpost(z, qn_t, kn_t, bf_pad, batch, seq)

    o_f = _flash_prompt(qa, ka, vt, z, g_fox_o, batch, seq)
    lf_pages_t = cache_logf[0].transpose(0, 2, 1)
    lf_new_t = pad_cols(logf[n_p:, :nh].reshape(db, dec_seq, nh).transpose(0, 2, 1).reshape(db * nh, dec_seq), PAGE)
    o_f = _paged_sample(page_table, qn_s, kn_s, vv_s, lf_new_t.reshape(db, nh, PAGE),
                        cache_k[0].transpose(0, 2, 3, 1), cache_v[0].transpose(0, 2, 3, 1), lf_pages_t, z, g_fox_o, o_f, n_p)

    first_p = jnp.zeros((batch, 1, RWKV_COLS), F32)
    first_s = pad_cols(state_shift[0], RWKV_COLS).reshape(db, 1, RWKV_COLS)
    prep_args = (mu_pad, w12, w3, w0, a0, k_k, k_a, row(r_k))
    prep_p = _rwkv_prep(z, first_p, *prep_args, 0, n_p, seq, shared=(jnp.zeros((m, WIDTH), F32),) * 2)
    prep_s = _rwkv_prep(z, first_s, *prep_args, n_p, n_s, dec_seq, shared=prep_p[6:8])
    g_, bonus = prep_s[6:8]
    y_p, wkv_p = _wkv_scan(*prep_p[:6], jnp.zeros((batch, nh, HEAD_DIM, HEAD_DIM), F32), batch, seq)
    y_s, wkv_s = _wkv_scan(*prep_s[:6], state_wkv[0], db, dec_seq)

    x1, h2, route = _outproj(x_p, x_s, y_p, y_s, bonus, g_, o_f, lnx_w, lnx_b, w_out[0].astype(BF16), g_ffn,
                             w_rg[0].T, w_re[0].T, b_rg, b_re)

    plan, slot_tok, n_used, dest = _dispatch(route, m)
    yb = _experts(plan, slot_tok, n_used, h2, w_e_gate[0], w_e_up[0], w_e_down[0])

    y_out_p, y_out_s = _ple(dest, x1, route, p_all, yb, g_ple, w_pg[0].astype(BF16), b_pg, w_pp[0].astype(BF16),
                            row(g_final), n_p)

    shift_p = jnp.concatenate([z[(b + 1) * seq - 1:(b + 1) * seq, :rwkv_in] for b in range(batch)], axis=0)
    shift_s = z[n_p + dec_seq - 1::dec_seq, :rwkv_in]
    heads_t = lambda a: a.reshape(batch, nh, HEAD_DIM, seq).transpose(0, 3, 1, 2)[None]
    heads = lambda a: a.reshape(1, db, dec_seq, nh, HEAD_DIM)
    return (y_out_p.reshape(batch, seq, d), y_out_s.reshape(db, dec_seq, d),
            heads_t(kt_p), heads_t(vt_p), logf[:n_p, :nh].reshape(1, batch, seq, nh),
            wkv_p[None], shift_p[None],
            heads(kn_s), heads(vv_s), logf[n_p:, :nh].reshape(1, db, dec_seq, nh),
            wkv_s[None], shift_s[None])
```

```python
import functools
import math

import jax
import jax.numpy as jnp
import numpy as np
from jax import lax
from jax.experimental import pallas as pl
from jax.experimental.pallas import tpu as pltpu

F32 = jnp.float32
BF16 = jnp.bfloat16

HEAD_DIM = 64
LANES = 128
RMS_EPS = 1e-6
LNX_EPS = 64e-5
PAGE = 128
MOE_BLOCK = 384
N_GROUPS = 8
N_EXPERTS = 64
DEC_SEQ = 8

C_LORA = 3072
C_F = 3456
RWKV_COLS = 3584
C_Q, C_FK, C_FV, C_OG = 4096, 5120, 6144, 7168
Z_COLS = 8192
WIDTH = 1024

VMEM_LIMIT = 52 * 1024 * 1024


def _cp(sem, vmem=VMEM_LIMIT):
    return pltpu.CompilerParams(dimension_semantics=sem, vmem_limit_bytes=vmem)


def _tile(n, pref):
    for t in (1024, 512, 256, 128, 64, 32, 16, 8):
        if t <= pref and n % t == 0:
            return t
    raise ValueError(f"no tile for {n}")


def _split2(x):
    hi = x.astype(BF16)
    lo = (x - hi.astype(F32)).astype(BF16)
    return hi, lo


def _split3(x):
    hi = x.astype(BF16)
    r1 = x - hi.astype(F32)
    mid = r1.astype(BF16)
    lo = (r1 - mid.astype(F32)).astype(BF16)
    return hi, mid, lo


def _dot(a, b):
    return jnp.dot(a, b, preferred_element_type=F32)


def _dot_nt(a, b):
    return lax.dot_general(a, b, (((1,), (1,)), ((), ())), preferred_element_type=F32)


def _block_diag_ones():
    r = lax.broadcasted_iota(jnp.int32, (LANES, LANES), 0)
    c = lax.broadcasted_iota(jnp.int32, (LANES, LANES), 1)
    return jnp.where((r >> 6) == (c >> 6), 1.0, 0.0).astype(BF16)


def _segsum(x, bd):
    outs = []
    for j in range(x.shape[1] // LANES):
        hi, lo = _split2(x[:, j * LANES:(j + 1) * LANES])
        outs.append(_dot(hi, bd) + _dot(lo, bd))
    return outs[0] if len(outs) == 1 else jnp.concatenate(outs, axis=1)


def _softplus(x):
    return jnp.maximum(x, 0.0) + jnp.log1p(jnp.exp(-jnp.abs(x)))


def _rms_rows(x, g):
    return x * lax.rsqrt(jnp.mean(x * x, axis=-1, keepdims=True) + RMS_EPS) * g


def _row_maps(prompt_tiles):
    return (lambda i, *_: (jnp.minimum(i, prompt_tiles - 1), 0)), (lambda i, *_: (jnp.maximum(i - prompt_tiles, 0), 0))


def _inproj_kernel(xp_ref, xs_ref, g_ref, w_ref, o_ref, h_ref, *, prompt_tiles):
    @pl.when(pl.program_id(1) == 0)
    def _():
        x = jnp.where(pl.program_id(0) < prompt_tiles, xp_ref[...], xs_ref[...])
        h_ref[...] = _rms_rows(x, g_ref[...]).astype(BF16)

    o_ref[...] = _dot(h_ref[...], w_ref[...])


def _inproj(x_p, x_s, g, w):
    (n_p, d), n_s = x_p.shape, x_s.shape[0]
    n = w.shape[1]
    tm, tn = _tile(math.gcd(n_p, n_s), 1024), _tile(n, 512)
    pmap, smap = _row_maps(n_p // tm)
    return pl.pallas_call(
        functools.partial(_inproj_kernel, prompt_tiles=n_p // tm),
        grid=((n_p + n_s) // tm, n // tn),
        in_specs=[pl.BlockSpec((tm, d), pmap),
                  pl.BlockSpec((tm, d), smap),
                  pl.BlockSpec((1, d), lambda i, j: (0, 0)),
                  pl.BlockSpec((d, tn), lambda i, j: (0, j))],
        out_specs=pl.BlockSpec((tm, tn), lambda i, j: (i, j)),
        out_shape=jax.ShapeDtypeStruct((n_p + n_s, n), F32),
        scratch_shapes=[pltpu.VMEM((tm, d), BF16)],
        compiler_params=_cp(("arbitrary", "arbitrary"), 58 * 1024 * 1024),
        name="inproj",
    )(x_p, x_s, g, w)


AUG = 2 * HEAD_DIM


def _aug_constants():
    nh = WIDTH // HEAD_DIM
    pq = np.zeros((3 * LANES, WIDTH), np.float32)
    pk = np.zeros((3 * LANES, WIDTH), np.float32)
    one_q = np.zeros((1, WIDTH), np.float32)
    one_k = np.zeros((1, WIDTH), np.float32)
    for h in range(nh):
        for comp in range(3):
            pq[comp * LANES + h, h * HEAD_DIM + comp] = 1.0
            pk[comp * LANES + h, h * HEAD_DIM + 3 + comp] = -1.0
            one_q[0, h * HEAD_DIM + 3 + comp] = 1.0
            one_k[0, h * HEAD_DIM + comp] = 1.0
    return jnp.asarray(pq, BF16), jnp.asarray(pk, BF16), jnp.asarray(one_q), jnp.asarray(one_k)


def _foxpost_kernel(q_ref, k_ref, v_ref, f_ref, qn_ref, kn_ref, bf_ref, pq_ref, pk_ref, oneq_ref, onek_ref,
                    oq, ok, ov, olf, oqa, oka, ovt, okt32, ovt32, carry, *, tiles_per_seq, prompt_tiles):
    i = pl.program_id(0)
    bd = _block_diag_ones()
    nh = WIDTH // HEAD_DIM

    def head_norm(x, g):
        ms = _segsum(x * x, bd) * (1.0 / HEAD_DIM)
        return x * lax.rsqrt(ms + RMS_EPS) * g

    qn = head_norm(q_ref[...], qn_ref[...])
    kn = head_norm(k_ref[...], kn_ref[...])
    v = v_ref[...]
    vt = v.T
    ovt[...] = vt.astype(BF16)

    @pl.when(i < prompt_tiles)
    def _():
        okt32[0] = kn.T
        ovt32[0] = vt

    @pl.when(i >= prompt_tiles)
    def _():
        oq[...] = qn
        ok[...] = kn
        ov[...] = v

    lf = -_softplus(-(f_ref[...] + bf_ref[...]))
    olf[...] = lf

    @pl.when(i % tiles_per_seq == 0)
    def _():
        carry[...] = jnp.zeros_like(carry)

    tm = lf.shape[0]
    r = lax.broadcasted_iota(jnp.int32, (tm, tm), 0)
    c = lax.broadcasted_iota(jnp.int32, (tm, tm), 1)
    tri = jnp.where(c <= r, 1.0, 0.0).astype(BF16)
    hi, mid, lo = _split3(lf)
    cum = _dot(tri, hi) + _dot(tri, mid) + _dot(tri, lo) + carry[...]
    carry[...] = cum[tm - 1:tm, :]

    c3 = jnp.concatenate(_split3(cum), axis=1)
    aug_q = _dot(c3, pq_ref[...]) + oneq_ref[...]
    aug_k = _dot(c3, pk_ref[...]) + onek_ref[...]
    qs = qn * (HEAD_DIM ** -0.5)

    def interleave(x, aug):
        pieces = []
        for h in range(nh):
            sl = slice(h * HEAD_DIM, (h + 1) * HEAD_DIM)
            pieces += [x[:, sl], aug[:, sl]]
        return jnp.concatenate(pieces, axis=1).astype(BF16)

    oqa[...] = interleave(qs, aug_q)
    oka[...] = interleave(kn, aug_k)


def _foxpost(z, q_norm_t, k_norm_t, bf_pad, batch, seq_len):
    m = z.shape[0]
    nh = WIDTH // HEAD_DIM
    n_p = batch * seq_len
    tm = min(_tile(math.gcd(n_p, m - n_p), 512), seq_len)
    npt, tps = n_p // tm, seq_len // tm
    sample = pl.BlockSpec((tm, WIDTH), lambda i: (jnp.maximum(i - npt, 0), 0))
    prompt_t = pl.BlockSpec((1, WIDTH, tm), lambda i: (jnp.minimum(i, npt - 1) // tps, 0, jnp.minimum(i, npt - 1) % tps))
    row = lambda c: pl.BlockSpec((tm, WIDTH), lambda i, c=c: (i, c // WIDTH))
    vec = pl.BlockSpec((1, WIDTH), lambda i: (0, 0))
    small = pl.BlockSpec((tm, LANES), lambda i: (i, 0))
    wide = pl.BlockSpec((tm, nh * AUG), lambda i: (i, 0))
    place = pl.BlockSpec((3 * LANES, WIDTH), lambda i: (0, 0))
    return pl.pallas_call(
        functools.partial(_foxpost_kernel, tiles_per_seq=tps, prompt_tiles=npt),
        grid=(m // tm,),
        in_specs=[row(C_Q), row(C_FK), row(C_FV),
                  pl.BlockSpec((tm, LANES), lambda i: (i, C_F // LANES)),
                  vec, vec, pl.BlockSpec((1, LANES), lambda i: (0, 0)), place, place, vec, vec],
        out_specs=[sample] * 3 + [small, wide, wide, pl.BlockSpec((WIDTH, tm), lambda i: (0, i)), prompt_t, prompt_t],
        out_shape=[jax.ShapeDtypeStruct((m - n_p, WIDTH), F32)] * 3 + [jax.ShapeDtypeStruct((m, LANES), F32),
                                                                        jax.ShapeDtypeStruct((m, nh * AUG), BF16),
                                                                        jax.ShapeDtypeStruct((m, nh * AUG), BF16),
                                                                        jax.ShapeDtypeStruct((WIDTH, m), BF16),
                                                                        jax.ShapeDtypeStruct((batch, WIDTH, seq_len), F32),
                                                                        jax.ShapeDtypeStruct((batch, WIDTH, seq_len), F32)],
        scratch_shapes=[pltpu.VMEM((1, LANES), F32)],
        compiler_params=_cp(("arbitrary",)),
        name="foxpost",
    )(z, z, z, z, q_norm_t, k_norm_t, bf_pad, *_aug_constants())


FLASH_HEADS = 8


def _flash_kernel(qi_tab, ki_tab, qa_ref, ka_ref, vt_ref, og_ref, g_ref, init_hbm, o_ref, m_s, l_s, acc_s):
    s_idx = pl.program_id(2)
    qi = qi_tab[s_idx]
    ki = ki_tab[s_idx]
    tq = qa_ref.shape[0]
    tk = ka_ref.shape[0]

    @pl.when(ki == 0)
    def _():
        m_s[...] = jnp.full_like(m_s, -jnp.inf)
        l_s[...] = jnp.zeros_like(l_s)
        acc_s[...] = jnp.zeros_like(acc_s)

    def update(masked):
        for hh in range(FLASH_HEADS):
            st = _dot_nt(ka_ref[:, hh * AUG:(hh + 1) * AUG], qa_ref[:, hh * AUG:(hh + 1) * AUG])
            if masked:
                r = lax.broadcasted_iota(jnp.int32, (tk, tq), 0)
                c = lax.broadcasted_iota(jnp.int32, (tk, tq), 1)
                st = jnp.where(r <= c, st, -jnp.inf)
            m_prev = m_s[hh]
            m_new = jnp.maximum(m_prev, jnp.max(st, axis=0, keepdims=True))
            alpha = jnp.exp(m_prev - m_new)
            p = jnp.exp(st - m_new)
            l_s[hh] = alpha * l_s[hh] + jnp.sum(p, axis=0, keepdims=True)
            acc_s[hh] = alpha * acc_s[hh] + _dot(vt_ref[hh * HEAD_DIM:(hh + 1) * HEAD_DIM, :], p.astype(BF16))
            m_s[hh] = m_new

    @pl.when(ki < qi)
    def _():
        update(False)

    @pl.when(ki == qi)
    def _():
        update(True)
        outs = []
        for hh in range(FLASH_HEADS):
            sl = slice(hh * HEAD_DIM, (hh + 1) * HEAD_DIM)
            ot = acc_s[hh] / l_s[hh]
            ont = ot * lax.rsqrt(jnp.mean(ot * ot, axis=0, keepdims=True) + RMS_EPS)
            outs.append(ont.T * g_ref[:, sl] * jax.nn.sigmoid(og_ref[:, sl]))
        o_ref[...] = jnp.concatenate(outs, axis=1)


def _flash_prompt(qa, ka, vt, z, g_fox_o, batch, seq):
    tq = _tile(seq, 512)
    nq = seq // tq
    pairs = [(a, b) for a in range(nq) for b in range(a + 1)]
    qi_tab = jnp.array([a for a, _ in pairs], jnp.int32)
    ki_tab = jnp.array([b for _, b in pairs], jnp.int32)
    hw = FLASH_HEADS * HEAD_DIM
    npairs = WIDTH // hw
    qmap = lambda b, p, s, qt, kt: (b * nq + qt[s], p)
    kmap = lambda b, p, s, qt, kt: (b * nq + kt[s], p)
    gs = pltpu.PrefetchScalarGridSpec(
        num_scalar_prefetch=2,
        grid=(batch, npairs, len(pairs)),
        in_specs=[pl.BlockSpec((tq, FLASH_HEADS * AUG), qmap),
                  pl.BlockSpec((tq, FLASH_HEADS * AUG), kmap),
                  pl.BlockSpec((hw, tq), lambda b, p, s, qt, kt: (p, b * nq + kt[s])),
                  pl.BlockSpec((tq, hw), lambda b, p, s, qt, kt: (b * nq + qt[s], C_OG // hw + p)),
                  pl.BlockSpec((1, hw), lambda b, p, s, qt, kt: (0, p)),
                  pl.BlockSpec(memory_space=pl.ANY)],
        out_specs=pl.BlockSpec((tq, hw), qmap),
        scratch_shapes=[pltpu.VMEM((FLASH_HEADS, 1, tq), F32), pltpu.VMEM((FLASH_HEADS, 1, tq), F32),
                        pltpu.VMEM((FLASH_HEADS, HEAD_DIM, tq), F32)],
    )
    return pl.pallas_call(
        _flash_kernel,
        grid_spec=gs,
        out_shape=jax.ShapeDtypeStruct((qa.shape[0], WIDTH), F32),
        input_output_aliases={7: 0},
        compiler_params=_cp(("arbitrary", "arbitrary", "arbitrary")),
        name="flash_prompt",
    )(qi_tab, ki_tab, qa, ka, vt, z, g_fox_o, jnp.zeros((qa.shape[0], WIDTH), F32))


PAGE_GROUP = 4
PAGE_BUFS = 12


def _paged_kernel(pt_ref, q_ref, kn_ref, vn_ref, lfn_ref, og_ref, g_ref, ck_hbm, cv_hbm, lf_hbm, of_hbm, o_ref,
                  kbuf, vbuf, lbuf, sem, *, n_pages):
    b = pl.program_id(0)
    total = pl.num_programs(0) * n_pages
    depth = PAGE_BUFS - PAGE_GROUP
    nh = WIDTH // HEAD_DIM
    rows = nh * DEC_SEQ

    def copies(page, slot):
        return (pltpu.make_async_copy(ck_hbm.at[page], kbuf.at[slot], sem.at[0, slot]),
                pltpu.make_async_copy(cv_hbm.at[page], vbuf.at[slot], sem.at[1, slot]),
                pltpu.make_async_copy(lf_hbm.at[page], lbuf.at[slot], sem.at[2, slot]))

    def fetch(g):
        bb = g // n_pages
        for c in copies(pt_ref[bb, g - bb * n_pages], g % PAGE_BUFS):
            c.start()

    @pl.when(b == 0)
    def _():
        for g in range(depth):
            fetch(g)

    q = (q_ref[...] * (HEAD_DIM ** -0.5)).astype(BF16)
    qh = [q[:, h * HEAD_DIM:(h + 1) * HEAD_DIM] for h in range(nh)]
    def cumulate(lf, carry):
        n = lf.shape[1]
        r = lax.broadcasted_iota(jnp.int32, (n, n), 0)
        c = lax.broadcasted_iota(jnp.int32, (n, n), 1)
        upper = jnp.where(r <= c, 1.0, 0.0).astype(BF16)
        hi, mid, lo = _split3(lf)
        cum = _dot(hi, upper) + _dot(mid, upper) + _dot(lo, upper) + carry
        return cum, cum[:, n - 1:n]

    def attend(pages, feature_major, ck, valid, m, l, acc):
        qk = _dot if feature_major else _dot_nt
        pv_dot = _dot_nt if feature_major else _dot
        s = jnp.concatenate([jnp.concatenate([qk(qh[h], kh[h]) for kh, _ in pages], axis=1) for h in range(nh)], axis=0)
        s = s - jnp.concatenate([jnp.broadcast_to(ck[h:h + 1, :], (DEC_SEQ, ck.shape[1])) for h in range(nh)], axis=0)
        if valid is not None:
            s = jnp.where(valid, s, -jnp.inf)
        m_new = jnp.maximum(m, jnp.max(s, axis=-1, keepdims=True))
        alpha = jnp.exp(m - m_new)
        p = jnp.exp(s - m_new)
        l = alpha * l + jnp.sum(p, axis=-1, keepdims=True)
        pb = p.astype(BF16)
        pv = []
        for h in range(nh):
            rows_h = slice(h * DEC_SEQ, (h + 1) * DEC_SEQ)
            pv.append(sum(pv_dot(pb[rows_h, i * PAGE:(i + 1) * PAGE], vh[h]) for i, (_, vh) in enumerate(pages)))
        return m_new, l, alpha * acc + jnp.concatenate(pv, axis=0)

    def pair_step(jj, carry):
        m, l, acc, ccar = carry
        g0 = b * n_pages + PAGE_GROUP * jj
        slots = [(g0 + u) % PAGE_BUFS for u in range(PAGE_GROUP)]
        for slot in slots:
            for cp in copies(0, slot):
                cp.wait()
        for u in range(PAGE_GROUP):
            @pl.when(g0 + depth + u < total)
            def _():
                fetch(g0 + depth + u)

        ck, ccar = cumulate(jnp.concatenate([lbuf[slot] for slot in slots], axis=1), ccar)
        pages = [([kbuf[slot, h].astype(BF16) for h in range(nh)],
                  [vbuf[slot, h].astype(BF16) for h in range(nh)]) for slot in slots]
        m, l, acc = attend(pages, True, ck, None, m, l, acc)
        return m, l, acc, ccar

    init = (jnp.full((rows, 1), -jnp.inf, F32), jnp.zeros((rows, 1), F32), jnp.zeros((rows, HEAD_DIM), F32),
            jnp.zeros((nh, 1), F32))
    m, l, acc, ccar = lax.fori_loop(0, n_pages // PAGE_GROUP, pair_step, init)

    pad = jnp.zeros((PAGE - DEC_SEQ, HEAD_DIM), BF16)
    kn = kn_ref[...].astype(BF16)
    vn = vn_ref[...].astype(BF16)
    k_heads = [jnp.concatenate([kn[:, h * HEAD_DIM:(h + 1) * HEAD_DIM], pad], axis=0) for h in range(nh)]
    v_heads = [jnp.concatenate([vn[:, h * HEAD_DIM:(h + 1) * HEAD_DIM], pad], axis=0) for h in range(nh)]
    ck, _ = cumulate(lfn_ref[0], ccar)
    rr = lax.broadcasted_iota(jnp.int32, (rows, PAGE), 0)
    cc = lax.broadcasted_iota(jnp.int32, (rows, PAGE), 1)
    m, l, acc = attend([(k_heads, v_heads)], False, ck, cc <= (rr & (DEC_SEQ - 1)), m, l, acc)
    o = acc / l
    on = o * lax.rsqrt(jnp.mean(o * o, axis=-1, keepdims=True) + RMS_EPS)
    out = jnp.concatenate([on[h * DEC_SEQ:(h + 1) * DEC_SEQ, :] for h in range(nh)], axis=1)
    o_ref[...] = out * g_ref[...] * jax.nn.sigmoid(og_ref[...])


def _paged_sample(page_table, qn, kn, v, lf_new_t, cache_k, cache_v, lf_pages_t, z, g_fox_o, o_f, row0):
    db, n_pages = page_table.shape
    assert n_pages % PAGE_GROUP == 0 and n_pages >= PAGE_BUFS
    nh = WIDTH // HEAD_DIM
    rb = row0 // DEC_SEQ
    newmap = lambda b, pt: (b, 0)
    hbm = pl.BlockSpec(memory_space=pl.ANY)
    gs = pltpu.PrefetchScalarGridSpec(
        num_scalar_prefetch=1,
        grid=(db,),
        in_specs=[pl.BlockSpec((DEC_SEQ, WIDTH), newmap),
                  pl.BlockSpec((DEC_SEQ, WIDTH), newmap),
                  pl.BlockSpec((DEC_SEQ, WIDTH), newmap),
                  pl.BlockSpec((1, nh, PAGE), lambda b, pt: (b, 0, 0)),
                  pl.BlockSpec((DEC_SEQ, WIDTH), lambda b, pt: (rb + b, C_OG // WIDTH)),
                  pl.BlockSpec((1, WIDTH), lambda b, pt: (0, 0)),
                  hbm, hbm, hbm, hbm],
        out_specs=pl.BlockSpec((DEC_SEQ, WIDTH), lambda b, pt: (rb + b, 0)),
        scratch_shapes=[pltpu.VMEM((PAGE_BUFS, nh, HEAD_DIM, PAGE), F32), pltpu.VMEM((PAGE_BUFS, nh, HEAD_DIM, PAGE), F32),
                        pltpu.VMEM((PAGE_BUFS, nh, PAGE), F32), pltpu.SemaphoreType.DMA((3, PAGE_BUFS))],
    )
    return pl.pallas_call(
        functools.partial(_paged_kernel, n_pages=n_pages),
        grid_spec=gs,
        out_shape=jax.ShapeDtypeStruct(o_f.shape, F32),
        input_output_aliases={10: 0},
        compiler_params=_cp(("arbitrary",)),
        name="paged_sample",
    )(page_table, qn, kn, v, lf_new_t, z, g_fox_o, cache_k, cache_v, lf_pages_t, o_f)


def _prep_kernel(z_ref, first_ref, mu_ref, w12_ref, w3_ref, w0_ref, a0_ref, kk_ref, ka_ref, rk_ref,
                 g_hbm, bonus_hbm, o_r, o_w, o_k, o_v, o_a, o_b, o_g, o_bonus, carry, *, seq_len, tiles_per_seq):
    z = z_ref[...]
    tm = z.shape[0]
    row = lax.broadcasted_iota(jnp.int32, (tm, 1), 0)
    rolled = pltpu.roll(z, 1, 0)
    if tiles_per_seq >= 1 and seq_len >= tm:
        li = pl.program_id(0) % tiles_per_seq

        @pl.when(li == 0)
        def _():
            carry[...] = first_ref[0]

        zp = jnp.where(row == 0, carry[...], rolled)
        carry[...] = z[tm - 1:tm, :]
    else:
        nseq = tm // seq_len
        first = first_ref[...]
        exp = jnp.broadcast_to(first, (nseq, seq_len, first.shape[-1])).reshape(tm, first.shape[-1])
        zp = jnp.where((row & (seq_len - 1)) == 0, exp, rolled)
    zm = z + (zp - z) * mu_ref[...]
    r = zm[:, 0:WIDTH]
    k = zm[:, WIDTH:2 * WIDTH]
    v = zm[:, 2 * WIDTH:3 * WIDTH]
    lo = zm[:, C_LORA:C_LORA + 384]
    lane = lax.broadcasted_iota(jnp.int32, (tm, 384), 1)
    act = jnp.where(lane < 64, jnp.tanh(lo), jnp.where(lane < 128, lo, jax.nn.sigmoid(lo))).astype(BF16)
    l12 = _dot(act[:, 0:128], w12_ref[...])
    g = _dot(act[:, 128:384], w3_ref[...])
    log_w = -_softplus(-(w0_ref[...] + l12[:, 0:WIDTH])) - 0.5
    decay = jnp.exp(-jnp.exp(log_w))
    asig = jax.nn.sigmoid(a0_ref[...] + l12[:, WIDTH:2 * WIDTH])
    bd = _block_diag_ones()
    kk = k * kk_ref[...]
    kk = kk / jnp.maximum(jnp.sqrt(_segsum(kk * kk, bd)), 1e-12)
    kf = k * (1.0 + (asig - 1.0) * ka_ref[...])
    o_r[...] = r
    o_w[...] = decay
    o_k[...] = kf
    o_v[...] = v
    o_a[...] = -kk
    o_b[...] = kk * asig
    o_g[...] = g
    o_bonus[...] = _segsum(r * kf * rk_ref[...], bd) * v


def _rwkv_prep(z, first, mu, w12, w3, w0, a0, k_k, k_a, r_k, row0, n_rows, seq_len, shared):
    if seq_len >= 256:
        tm = _tile(seq_len, 256)
        first_spec = pl.BlockSpec((1, 1, RWKV_COLS), lambda i: (i // (seq_len // tm), 0, 0))
    else:
        tm = _tile(n_rows, 128)
        nseq = tm // seq_len
        first_spec = pl.BlockSpec((nseq, 1, RWKV_COLS), lambda i: (i, 0, 0))
    rb = row0 // tm
    vec = lambda n: pl.BlockSpec((1, n), lambda i: (0, 0))
    out = pl.BlockSpec((tm, WIDTH), lambda i: (i, 0))
    out_all = pl.BlockSpec((tm, WIDTH), lambda i: (rb + i, 0))
    hbm = pl.BlockSpec(memory_space=pl.ANY)
    extra = tuple(shared)
    return pl.pallas_call(
        functools.partial(_prep_kernel, seq_len=seq_len, tiles_per_seq=max(seq_len // tm, 1)),
        grid=(n_rows // tm,),
        in_specs=[pl.BlockSpec((tm, RWKV_COLS), lambda i: (rb + i, 0)), first_spec, vec(RWKV_COLS),
                  pl.BlockSpec((128, 2 * WIDTH), lambda i: (0, 0)), pl.BlockSpec((256, WIDTH), lambda i: (0, 0)),
                  vec(WIDTH), vec(WIDTH), vec(WIDTH), vec(WIDTH), vec(WIDTH)] + [hbm] * len(extra),
        out_specs=[out] * 6 + [out_all] * 2,
        out_shape=[jax.ShapeDtypeStruct((n_rows, WIDTH), F32)] * 6 + [jax.ShapeDtypeStruct((z.shape[0], WIDTH), F32)] * 2,
        input_output_aliases={10: 6, 11: 7},
        scratch_shapes=[pltpu.VMEM((1, RWKV_COLS), F32)],
        compiler_params=_cp(("arbitrary",)),
        name=f"rwkv_prep_{seq_len}",
    )(z, first, mu, w12, w3, w0, a0, k_k, k_a, r_k, *extra)


SCAN_SUB = 64
SCAN_SPLIT = 2
SCAN_UNROLL = 4


def _wkv_kernel(r_ref, w_ref, k_ref, v_ref, a_ref, b_ref, s0_ref, y_ref, sT_ref, S_s, Z_s, *, groups, n_pairs, tb_len):
    tb = pl.program_id(1)
    r2 = lax.broadcasted_iota(jnp.int32, (2 * LANES, 2 * LANES), 0)
    c2 = lax.broadcasted_iota(jnp.int32, (2 * LANES, 2 * LANES), 1)
    ones_bd = jnp.where((r2 >> 6) == (c2 >> 6), 1.0, 0.0).astype(BF16)
    vrow = lax.broadcasted_iota(jnp.int32, (HEAD_DIM, LANES), 0)
    lane = lax.broadcasted_iota(jnp.int32, (HEAD_DIM, LANES), 1)
    eye2 = (lane & (HEAD_DIM - 1)) == vrow
    zero_half = jnp.zeros((HEAD_DIM, LANES), BF16)
    H = HEAD_DIM

    @pl.when(tb == 0)
    def _():
        for g in range(groups):
            for p in range(n_pairs):
                S_s[g * n_pairs + p] = jnp.concatenate([s0_ref[g, 2 * p], s0_ref[g, 2 * p + 1]], axis=1)
        Z_s[...] = jnp.zeros_like(Z_s)

    def lanes(p):
        return slice(p * LANES, (p + 1) * LANES)

    def run(t0, n):
        def step(i, carry):
            t = t0 + i
            tp = jnp.maximum(t - 1, 0)
            hit = (lane & (SCAN_SUB - 1)) == (i - 1)
            for g in range(groups):
                aa = a_ref[g, pl.ds(t, 1), :]
                ww = w_ref[g, pl.ds(t, 1), :]
                kr = k_ref[g, pl.ds(t, 1), :]
                vv = v_ref[g, pl.ds(t, 1), :]
                bb = b_ref[g, pl.ds(t, 1), :]
                rp = r_ref[g, pl.ds(tp, 1), :]
                npg = n_pairs // SCAN_SPLIT
                for p0 in range(0, n_pairs, npg):
                    lhs_s, xs = [], []
                    for p in range(p0, p0 + npg):
                        S = S_s[g * n_pairs + p]
                        lhs_s.append(jnp.concatenate([(S * aa[:, lanes(p)]).astype(BF16),
                                                      (S * rp[:, lanes(p)]).astype(BF16)], axis=1))
                        xs.append(jnp.where(eye2, vv[:, lanes(p)], 0.0).astype(BF16))
                    lhs_x = [jnp.concatenate(xs[q:q + 2], axis=1) for q in range(0, npg, 2)]
                    res = _dot(jnp.concatenate(lhs_s + lhs_x, axis=0), ones_bd)
                    for q in range(npg):
                        p = p0 + q
                        idx = g * n_pairs + p
                        rs = res[q * H:(q + 1) * H]
                        vb = res[(npg + q // 2) * H:(npg + q // 2 + 1) * H, (q % 2) * LANES:(q % 2 + 1) * LANES]
                        S_s[idx] = S_s[idx] * ww[:, lanes(p)] + rs[:, :LANES] * bb[:, lanes(p)] + vb * kr[:, lanes(p)]
                        Z_s[idx] = jnp.where(hit, rs[:, LANES:], Z_s[idx])
            return carry

        lax.fori_loop(0, n, step, 0, unroll=SCAN_UNROLL)
        hit_last = (lane & (SCAN_SUB - 1)) == (n - 1)
        for g in range(groups):
            rl = r_ref[g, pl.ds(t0 + n - 1, 1), :]
            lhs = [jnp.concatenate([(S_s[g * n_pairs + p] * rl[:, lanes(p)]).astype(BF16), zero_half], axis=1)
                   for p in range(n_pairs)]
            res = _dot(jnp.concatenate(lhs, axis=0), ones_bd)
            for p in range(n_pairs):
                idx = g * n_pairs + p
                zt = jnp.where(hit_last, res[p * H:(p + 1) * H, :LANES], Z_s[idx]).T
                y_ref[g, pl.ds(t0, n), p * LANES:p * LANES + H] = zt[0:n, :]
                y_ref[g, pl.ds(t0, n), p * LANES + H:(p + 1) * LANES] = zt[SCAN_SUB:SCAN_SUB + n, :]

    if tb_len <= SCAN_SUB:
        run(0, tb_len)
    else:
        def outer(blk, carry):
            run(pl.multiple_of(blk * SCAN_SUB, SCAN_SUB), SCAN_SUB)
            return carry
        lax.fori_loop(0, tb_len // SCAN_SUB, outer, 0)

    @pl.when(tb == pl.num_programs(1) - 1)
    def _():
        for g in range(groups):
            for p in range(n_pairs):
                S = S_s[g * n_pairs + p]
                sT_ref[g, 2 * p] = S[:, 0:H]
                sT_ref[g, 2 * p + 1] = S[:, H:LANES]


def _wkv_scan(r, w, k, v, a, b, s0, n_seq, seq_len):
    nh = WIDTH // HEAD_DIM
    groups = 2 if n_seq % 2 == 0 else 1
    tb_len = min(seq_len, 128)
    nt = seq_len // tb_len
    blk = pl.BlockSpec((groups, tb_len, WIDTH), lambda s, t: (s, t, 0))
    st = pl.BlockSpec((groups, nh, HEAD_DIM, HEAD_DIM), lambda s, t: (s, 0, 0, 0))
    n_pairs = WIDTH // LANES
    seq3 = lambda x: x.reshape(n_seq, seq_len, WIDTH)
    y, s_new = pl.pallas_call(
        functools.partial(_wkv_kernel, groups=groups, n_pairs=n_pairs, tb_len=tb_len),
        grid=(n_seq // groups, nt),
        in_specs=[blk] * 6 + [st],
        out_specs=[blk, st],
        out_shape=[jax.ShapeDtypeStruct((n_seq, seq_len, WIDTH), F32),
                   jax.ShapeDtypeStruct((n_seq, nh, HEAD_DIM, HEAD_DIM), F32)],
        scratch_shapes=[pltpu.VMEM((groups * n_pairs, HEAD_DIM, LANES), F32),
                        pltpu.VMEM((groups * n_pairs, HEAD_DIM, LANES), F32)],
        compiler_params=_cp(("arbitrary", "arbitrary")),
        name=f"wkv_scan_{seq_len}",
    )(seq3(r), seq3(w), seq3(k), seq3(v), seq3(a), seq3(b), s0)
    return y.reshape(n_seq * seq_len, WIDTH), s_new


def _outproj_kernel(xp_ref, xs_ref, yp_ref, ys_ref, bonus_ref, g_ref, of_ref, lw_ref, lb_ref, wo_ref, gf_ref,
                    wrg_ref, wre_ref, brg_ref, bre_ref, x1_ref, h_ref, route_ref, wrh_s, wrl_s, br_s, *, prompt_tiles):
    i = pl.program_id(0)

    @pl.when(i == 0)
    def _():
        pad = LANES - N_GROUPS - N_EXPERTS
        w_r = jnp.concatenate([wrg_ref[...], wre_ref[...], jnp.zeros((pad, wrg_ref.shape[1]), F32)], axis=0)
        hi = w_r.astype(BF16)
        wrh_s[...] = hi
        wrl_s[...] = (w_r - hi.astype(F32)).astype(BF16)
        br_s[...] = jnp.concatenate([brg_ref[...], bre_ref[...], jnp.zeros((1, pad), F32)], axis=1)

    is_prompt = i < prompt_tiles
    bd = _block_diag_ones()
    y = jnp.where(is_prompt, yp_ref[...], ys_ref[...])
    mu = _segsum(y, bd) * (1.0 / HEAD_DIM)
    d = y - mu
    var = _segsum(d * d, bd) * (1.0 / HEAD_DIM)
    yn = d * lax.rsqrt(var + LNX_EPS) * lw_ref[...] + lb_ref[...]
    o_r = ((yn + bonus_ref[...]) * g_ref[...]).astype(BF16)
    o_f = of_ref[...].astype(BF16)
    x = jnp.where(is_prompt, xp_ref[...], xs_ref[...])
    x1 = x + _dot(o_r, wo_ref[0:WIDTH, :]) + _dot(o_f, wo_ref[WIDTH:2 * WIDTH, :])
    x1_ref[...] = x1
    h = _rms_rows(x1, gf_ref[...])
    h_ref[...] = h
    hi, lo = _split2(h)
    logits = _dot_nt(hi, wrh_s[...]) + _dot_nt(lo, wrh_s[...]) + _dot_nt(hi, wrl_s[...]) + br_s[...]
    tm = logits.shape[0]
    lane = lax.broadcasted_iota(jnp.int32, (tm, LANES), 1)
    neg = -jnp.inf
    lg = jnp.where(lane < N_GROUPS, logits, neg)
    mg = jnp.max(lg, axis=-1, keepdims=True)
    pg_top = 1.0 / jnp.sum(jnp.exp(lg - mg), axis=-1, keepdims=True)
    g_sel = jnp.min(jnp.where(lg == mg, lane, LANES), axis=-1, keepdims=True)
    in_grp = (lane >= N_GROUPS) & (lane < N_GROUPS + N_EXPERTS) & (((lane - N_GROUPS) >> 3) == g_sel)
    le = jnp.where(in_grp, logits, neg)
    m1 = jnp.max(le, axis=-1, keepdims=True)
    i1 = jnp.min(jnp.where(le == m1, lane, LANES), axis=-1, keepdims=True)
    le2 = jnp.where(lane == i1, neg, le)
    m2 = jnp.max(le2, axis=-1, keepdims=True)
    i2 = jnp.min(jnp.where(le2 == m2, lane, LANES), axis=-1, keepdims=True)
    e2 = jnp.exp(m2 - m1)
    gate1 = pg_top / (1.0 + e2)
    gate2 = pg_top * e2 / (1.0 + e2)
    route = jnp.where(lane == 0, (i1 - N_GROUPS).astype(F32),
                      jnp.where(lane == 1, (i2 - N_GROUPS).astype(F32),
                                jnp.where(lane == 2, gate1, jnp.where(lane == 3, gate2, 0.0))))
    route_ref[...] = route


def _outproj(x_p, x_s, y_p, y_s, bonus, g, o_f, lnx_w, lnx_b, w_out, g_ffn, w_rg, w_re, b_rg, b_re):
    (n_p, d), n_s = x_p.shape, x_s.shape[0]
    m = n_p + n_s
    tm = _tile(math.gcd(n_p, n_s), 256)
    pmap, smap = _row_maps(n_p // tm)
    row = lambda n: pl.BlockSpec((tm, n), lambda i: (i, 0))
    vec = lambda n: pl.BlockSpec((1, n), lambda i: (0, 0))
    full = lambda a: pl.BlockSpec(a.shape, lambda i: (0, 0))
    return pl.pallas_call(
        functools.partial(_outproj_kernel, prompt_tiles=n_p // tm),
        grid=(m // tm,),
        in_specs=[pl.BlockSpec((tm, d), pmap), pl.BlockSpec((tm, d), smap),
                  pl.BlockSpec((tm, WIDTH), pmap), pl.BlockSpec((tm, WIDTH), smap),
                  row(WIDTH), row(WIDTH), row(WIDTH), vec(WIDTH), vec(WIDTH),
                  full(w_out), vec(d), full(w_rg), full(w_re), full(b_rg), full(b_re)],
        out_specs=[row(d), row(d), row(LANES)],
        out_shape=[jax.ShapeDtypeStruct((m, d), F32), jax.ShapeDtypeStruct((m, d), F32),
                   jax.ShapeDtypeStruct((m, LANES), F32)],
        scratch_shapes=[pltpu.VMEM((LANES, d), BF16), pltpu.VMEM((LANES, d), BF16), pltpu.VMEM((1, LANES), F32)],
        compiler_params=_cp(("arbitrary",)),
        name="outproj_router",
    )(x_p, x_s, y_p, y_s, bonus, g, o_f, lnx_w, lnx_b, w_out, g_ffn, w_rg, w_re, b_rg, b_re)


def _expert_kernel(plan_ref, tok_ref, nu_ref, h_hbm, wg_hbm, wu_hbm, wd_hbm, o_ref, xbuf, wg, wu, wd, sem, wsem):
    i = pl.program_id(0)
    n_used = nu_ref[0]
    first, nxt, wslot = plan_ref[1, i], plan_ref[2, i], plan_ref[3, i]

    def row_copy(tok, r, slot):
        return pltpu.make_async_copy(h_hbm.at[pl.ds(tok, 1), :], xbuf.at[slot, pl.ds(r, 1), :], sem.at[slot])

    def gather(blk, slot):
        def start(r, c):
            row_copy(tok_ref[blk * MOE_BLOCK + r], r, slot).start()
            return c
        lax.fori_loop(0, plan_ref[4, blk], start, 0)

    def weight_copies(expert, slot):
        return (pltpu.make_async_copy(wg_hbm.at[expert], wg.at[slot], wsem.at[0, slot]),
                pltpu.make_async_copy(wu_hbm.at[expert], wu.at[slot], wsem.at[1, slot]),
                pltpu.make_async_copy(wd_hbm.at[expert], wd.at[slot], wsem.at[2, slot]))

    @pl.when(i == 0)
    def _():
        xbuf[...] = jnp.zeros_like(xbuf)
        gather(0, 0)
        for c in weight_copies(plan_ref[0, 0], 0):
            c.start()

    @pl.when(i < n_used)
    def _():
        slot = i & 1

        @pl.when(first == 1)
        def _():
            for c in weight_copies(0, wslot):
                c.wait()

            @pl.when(nxt >= 0)
            def _():
                for c in weight_copies(nxt, 1 - wslot):
                    c.start()

        def wait(r, c):
            row_copy(0, r, slot).wait()
            return c
        lax.fori_loop(0, plan_ref[4, i], wait, 0)

        @pl.when(i + 1 < n_used)
        def _():
            gather(i + 1, 1 - slot)

        x = xbuf[slot].astype(BF16)
        gate = _dot(x, wg[wslot].astype(BF16))
        up = _dot(x, wu[wslot].astype(BF16))
        act = (gate * jax.nn.sigmoid(gate) * up).astype(BF16)
        o_ref[...] = _dot(act, wd[wslot].astype(BF16))

    @pl.when(i >= n_used)
    def _():
        o_ref[...] = jnp.zeros_like(o_ref)


def _experts(plan, slot_tok, n_used, h, w_gate, w_up, w_down):
    n_blocks = plan.shape[1]
    d = h.shape[1]
    de = w_gate.shape[2]
    hbm = pl.BlockSpec(memory_space=pl.ANY)
    gs = pltpu.PrefetchScalarGridSpec(
        num_scalar_prefetch=3,
        grid=(n_blocks,),
        in_specs=[hbm, hbm, hbm, hbm],
        out_specs=pl.BlockSpec((MOE_BLOCK, d), lambda i, plan, tok, nu: (i, 0)),
        scratch_shapes=[pltpu.VMEM((2, MOE_BLOCK, d), F32), pltpu.VMEM((2, d, de), F32), pltpu.VMEM((2, d, de), F32),
                        pltpu.VMEM((2, de, d), F32), pltpu.SemaphoreType.DMA((2,)), pltpu.SemaphoreType.DMA((3, 2))],
    )
    return pl.pallas_call(
        _expert_kernel,
        grid_spec=gs,
        out_shape=jax.ShapeDtypeStruct((n_blocks * MOE_BLOCK, d), F32),
        compiler_params=_cp(("arbitrary",), 58 * 1024 * 1024),
        name="experts",
    )(plan, slot_tok, n_used, h, w_gate, w_up, w_down)


def _ple_kernel(dest_ref, x1_ref, route_ref, p_ref, yb_hbm, gp_ref, wpg_ref, bpg_ref, wpp_ref, gfin_ref, op_ref, os_ref,
                buf, sem, *, prompt_tiles):
    i = pl.program_id(0)
    tm = x1_ref.shape[0]

    def row_copy(slot_row, r, k, half):
        return pltpu.make_async_copy(yb_hbm.at[pl.ds(slot_row, 1), :], buf.at[half, k, pl.ds(r, 1), :], sem.at[half])

    def gather(tile, half):
        def start(r, c):
            for k in range(2):
                row_copy(dest_ref[(tile * tm + r) * 2 + k], r, k, half).start()
            return c
        lax.fori_loop(0, tm, start, 0, unroll=8)

    @pl.when(i == 0)
    def _():
        gather(0, 0)

    half = i & 1

    def wait(r, c):
        for k in range(2):
            row_copy(0, r, k, half).wait()
        return c
    lax.fori_loop(0, tm, wait, 0, unroll=8)

    @pl.when(i + 1 < pl.num_programs(0))
    def _():
        gather(i + 1, 1 - half)

    route = route_ref[...]
    moe = buf[half, 0] * route[:, 2:3] + buf[half, 1] * route[:, 3:4]
    x2 = x1_ref[...] + moe
    hn = _rms_rows(x2, gp_ref[...]).astype(BF16)
    gate = jax.nn.sigmoid(_dot(hn, wpg_ref[...]) + bpg_ref[...])
    pe = _dot(p_ref[...].astype(BF16), wpp_ref[...])
    x3 = x2 + gate * pe
    y = _rms_rows(x3, gfin_ref[...])

    @pl.when(i < prompt_tiles)
    def _():
        op_ref[...] = y

    @pl.when(i >= prompt_tiles)
    def _():
        os_ref[...] = y


def _ple(dest, x1, route, p, yb, g_ple, w_pg, b_pg, w_pp, g_final, n_p):
    m, d = x1.shape
    tm = _tile(math.gcd(n_p, m - n_p), 256)
    npt = n_p // tm
    pd = p.shape[1]
    gs = pltpu.PrefetchScalarGridSpec(
        num_scalar_prefetch=1,
        grid=(m // tm,),
        in_specs=[pl.BlockSpec((tm, d), lambda i, ds: (i, 0)),
                  pl.BlockSpec((tm, LANES), lambda i, ds: (i, 0)),
                  pl.BlockSpec((tm, pd), lambda i, ds: (i, 0)),
                  pl.BlockSpec(memory_space=pl.ANY),
                  pl.BlockSpec((1, d), lambda i, ds: (0, 0)),
                  pl.BlockSpec((d, d), lambda i, ds: (0, 0)),
                  pl.BlockSpec((1, d), lambda i, ds: (0, 0)),
                  pl.BlockSpec((pd, d), lambda i, ds: (0, 0)),
                  pl.BlockSpec((1, d), lambda i, ds: (0, 0))],
        out_specs=[pl.BlockSpec((tm, d), lambda i, ds: (jnp.minimum(i, npt - 1), 0)),
                   pl.BlockSpec((tm, d), lambda i, ds: (jnp.maximum(i - npt, 0), 0))],
        scratch_shapes=[pltpu.VMEM((2, 2, tm, d), F32), pltpu.SemaphoreType.DMA((2,))],
    )
    return pl.pallas_call(
        functools.partial(_ple_kernel, prompt_tiles=npt),
        grid_spec=gs,
        out_shape=[jax.ShapeDtypeStruct((n_p, d), F32), jax.ShapeDtypeStruct((m - n_p, d), F32)],
        compiler_params=_cp(("arbitrary",)),
        name="moe_combine_ple",
    )(dest, x1, route, p, yb, g_ple, w_pg, b_pg, w_pp, g_final)


def _dispatch(route, n_tok):
    expert = route[:, 0:2].astype(jnp.int32)
    flat_e = expert.reshape(-1)
    n_assign = flat_e.shape[0]
    onehot = (flat_e[:, None] == jnp.arange(N_EXPERTS, dtype=jnp.int32)[None, :]).astype(jnp.int32)
    csum = jnp.cumsum(onehot, axis=0)
    rank = jnp.sum(csum * onehot, axis=1) - 1
    counts = csum[-1]
    pcounts = (counts + MOE_BLOCK - 1) // MOE_BLOCK * MOE_BLOCK
    pend = jnp.cumsum(pcounts)
    pstart = pend - pcounts
    dest = (pstart[flat_e] + rank).astype(jnp.int32)
    n_blocks = -(-n_assign // MOE_BLOCK) + N_EXPERTS
    slot_tok = jnp.zeros((n_blocks * MOE_BLOCK,), jnp.int32).at[dest].set(jnp.arange(n_assign, dtype=jnp.int32) // 2)
    n_used = (pend[-1] // MOE_BLOCK).astype(jnp.int32)
    blk = jnp.arange(n_blocks, dtype=jnp.int32)
    block_e = jnp.searchsorted(pend, jnp.minimum(blk, n_used - 1) * MOE_BLOCK, side='right').astype(jnp.int32)
    block_e = jnp.minimum(block_e, N_EXPERTS - 1)
    first = jnp.concatenate([jnp.ones((1,), jnp.int32), (block_e[1:] != block_e[:-1]).astype(jnp.int32)])
    first = jnp.where(blk < n_used, first, 0)
    used = counts > 0
    eidx = jnp.arange(N_EXPERTS, dtype=jnp.int32)
    later = jnp.flip(lax.cummin(jnp.flip(jnp.where(used, eidx, N_EXPERTS))))
    next_used = jnp.concatenate([later[1:], jnp.full((1,), N_EXPERTS, jnp.int32)])
    next_used = jnp.where(next_used < N_EXPERTS, next_used, -1)
    ordinal = jnp.cumsum(used.astype(jnp.int32)) - 1
    valid = jnp.clip(pstart[block_e] + counts[block_e] - blk * MOE_BLOCK, 0, MOE_BLOCK)
    valid = jnp.where(blk < n_used, valid, 0)
    plan = jnp.stack([block_e, first, next_used[block_e], ordinal[block_e] & 1, valid]).astype(jnp.int32)
    return plan, slot_tok, n_used.reshape(1), dest


def kernel(x_prompt, x_sample, cache_k, cache_v, cache_logf, state_wkv, state_shift, page_table, p_prompt, p_sample, g_attn, w_in, mu_shift, w0, w_up, a0, a_up, g_up, k_k, k_a, r_k, lnx_w, lnx_b, b_f, q_norm, k_norm, g_fox_o, w_out, g_ffn, w_rg, b_rg, w_re, b_re, w_e_gate, w_e_up, w_e_down, g_ple, w_pg, b_pg, w_pp, g_final):
    depth = g_attn.shape[0]
    assert depth == 1
    batch, seq, d = x_prompt.shape
    db, dec_seq, _ = x_sample.shape
    assert dec_seq == DEC_SEQ
    nh = WIDTH // HEAD_DIM
    n_p = batch * seq
    n_s = db * dec_seq
    m = n_p + n_s
    rwkv_in = mu_shift.shape[1]
    n_pool = cache_k.shape[1]
    n_pages = page_table.shape[1]
    row = lambda a: a.reshape(1, -1)

    x_p, x_s = x_prompt.reshape(n_p, d), x_sample.reshape(n_s, d)
    p_all = jnp.concatenate([p_prompt[0].reshape(n_p, -1), p_sample[0].reshape(n_s, -1)], axis=0)

    wi = w_in[0]
    zc = lambda n: jnp.zeros((d, n), F32)
    w_z = jnp.concatenate([wi[:, :rwkv_in], zc(C_F - rwkv_in), wi[:, rwkv_in + 4 * WIDTH:], zc(C_Q - C_F - nh),
                           wi[:, rwkv_in:rwkv_in + 4 * WIDTH]], axis=1).astype(BF16)
    w12 = jnp.zeros((128, 2 * WIDTH), F32).at[0:64, 0:WIDTH].set(w_up[0]).at[64:128, WIDTH:].set(a_up[0]).astype(BF16)
    w3 = jnp.zeros((256, WIDTH), F32).at[0:g_up.shape[1]].set(g_up[0]).astype(BF16)
    pad_cols = lambda a, n: jnp.pad(a, ((0, 0), (0, n - a.shape[1])))
    mu_pad = pad_cols(mu_shift, RWKV_COLS)
    bf_pad = pad_cols(b_f, LANES)
    qn_t = jnp.tile(q_norm, (1, nh))
    kn_t = jnp.tile(k_norm, (1, nh))

    z = _inproj(x_p, x_s, g_attn, w_z)

    qn_s, kn_s, vv_s, logf, qa, ka, vt, kt_p, vt_p = _foxpost(z, qn_t, kn_t, bf_pad, batch, seq)

    o_f = _flash_prompt(qa, ka, vt, z, g_fox_o, batch, seq)
    lf_pages_t = cache_logf[0].transpose(0, 2, 1)
    lf_new_t = pad_cols(logf[n_p:, :nh].reshape(db, dec_seq, nh).transpose(0, 2, 1).reshape(db * nh, dec_seq), PAGE)
    o_f = _paged_sample(page_table, qn_s, kn_s, vv_s, lf_new_t.reshape(db, nh, PAGE),
                        cache_k[0].transpose(0, 2, 3, 1), cache_v[0].transpose(0, 2, 3, 1), lf_pages_t, z, g_fox_o, o_f, n_p)

    first_p = jnp.zeros((batch, 1, RWKV_COLS), F32)
    first_s = pad_cols(state_shift[0], RWKV_COLS).reshape(db, 1, RWKV_COLS)
    prep_args = (mu_pad, w12, w3, w0, a0, k_k, k_a, row(r_k))
    prep_p = _rwkv_prep(z, first_p, *prep_args, 0, n_p, seq, shared=(jnp.zeros((m, WIDTH), F32),) * 2)
    prep_s = _rwkv_prep(z, first_s, *prep_args, n_p, n_s, dec_seq, shared=prep_p[6:8])
    g_, bonus = prep_s[6:8]
    y_p, wkv_p = _wkv_scan(*prep_p[:6], jnp.zeros((batch, nh, HEAD_DIM, HEAD_DIM), F32), batch, seq)
    y_s, wkv_s = _wkv_scan(*prep_s[:6], state_wkv[0], db, dec_seq)

    x1, h2, route = _outproj(x_p, x_s, y_p, y_s, bonus, g_, o_f, lnx_w, lnx_b, w_out[0].astype(BF16), g_ffn,
                             w_rg[0].T, w_re[0].T, b_rg, b_re)

    plan, slot_tok, n_used, dest = _dispatch(route, m)
    yb = _experts(plan, slot_tok, n_used, h2, w_e_gate[0], w_e_up[0], w_e_down[0])

    y_out_p, y_out_s = _ple(dest, x1, route, p_all, yb, g_ple, w_pg[0].astype(BF16), b_pg, w_pp[0].astype(BF16),
                            row(g_final), n_p)

    shift_p = jnp.concatenate([z[(b + 1) * seq - 1:(b + 1) * seq, :rwkv_in] for b in range(batch)], axis=0)
    shift_s = z[n_p + dec_seq - 1::dec_seq, :rwkv_in]
    heads_t = lambda a: a.reshape(batch, nh, HEAD_DIM, seq).transpose(0, 3, 1, 2)[None]
    heads = lambda a: a.reshape(1, db, dec_seq, nh, HEAD_DIM)
    return (y_out_p.reshape(batch, seq, d), y_out_s.reshape(db, dec_seq, d),
            heads_t(kt_p), heads_t(vt_p), logf[:n_p, :nh].reshape(1, batch, seq, nh),
            wkv_p[None], shift_p[None],
            heads(kn_s), heads(vv_s), logf[n_p:, :nh].reshape(1, db, dec_seq, nh),
            wkv_s[None], shift_s[None])
```

```python
import functools
import math

import jax
import jax.numpy as jnp
import numpy as np
from jax import lax
from jax.experimental import pallas as pl
from jax.experimental.pallas import tpu as pltpu

F32 = jnp.float32
BF16 = jnp.bfloat16

HEAD_DIM = 64
LANES = 128
RMS_EPS = 1e-6
LNX_EPS = 64e-5
PAGE = 128
MOE_BLOCK = 384
N_GROUPS = 8
N_EXPERTS = 64
DEC_SEQ = 8

C_LORA = 3072
C_F = 3456
RWKV_COLS = 3584
C_Q, C_FK, C_FV, C_OG = 4096, 5120, 6144, 7168
Z_COLS = 8192
WIDTH = 1024

VMEM_LIMIT = 52 * 1024 * 1024


def _cp(sem, vmem=VMEM_LIMIT):
    return pltpu.CompilerParams(dimension_semantics=sem, vmem_limit_bytes=vmem)


def _tile(n, pref):
    for t in (1024, 512, 256, 128, 64, 32, 16, 8):
        if t <= pref and n % t == 0:
            return t
    raise ValueError(f"no tile for {n}")


def _split2(x):
    hi = x.astype(BF16)
    lo = (x - hi.astype(F32)).astype(BF16)
    return hi, lo


def _split3(x):
    hi = x.astype(BF16)
    r1 = x - hi.astype(F32)
    mid = r1.astype(BF16)
    lo = (r1 - mid.astype(F32)).astype(BF16)
    return hi, mid, lo


def _dot(a, b):
    return jnp.dot(a, b, preferred_element_type=F32)


def _dot_nt(a, b):
    return lax.dot_general(a, b, (((1,), (1,)), ((), ())), preferred_element_type=F32)


def _block_diag_ones():
    r = lax.broadcasted_iota(jnp.int32, (LANES, LANES), 0)
    c = lax.broadcasted_iota(jnp.int32, (LANES, LANES), 1)
    return jnp.where((r >> 6) == (c >> 6), 1.0, 0.0).astype(BF16)


def _segsum(x, bd):
    outs = []
    for j in range(x.shape[1] // LANES):
        hi, lo = _split2(x[:, j * LANES:(j + 1) * LANES])
        outs.append(_dot(hi, bd) + _dot(lo, bd))
    return outs[0] if len(outs) == 1 else jnp.concatenate(outs, axis=1)


def _softplus(x):
    return jnp.maximum(x, 0.0) + jnp.log1p(jnp.exp(-jnp.abs(x)))


def _rms_rows(x, g):
    return x * lax.rsqrt(jnp.mean(x * x, axis=-1, keepdims=True) + RMS_EPS) * g


def _row_maps(prompt_tiles):
    return (lambda i, *_: (jnp.minimum(i, prompt_tiles - 1), 0)), (lambda i, *_: (jnp.maximum(i - prompt_tiles, 0), 0))


def _inproj_kernel(xp_ref, xs_ref, g_ref, w_ref, o_ref, h_ref, *, prompt_tiles):
    @pl.when(pl.program_id(1) == 0)
    def _():
        x = jnp.where(pl.program_id(0) < prompt_tiles, xp_ref[...], xs_ref[...])
        h_ref[...] = _rms_rows(x, g_ref[...]).astype(BF16)

    o_ref[...] = _dot(h_ref[...], w_ref[...])


def _inproj(x_p, x_s, g, w):
    (n_p, d), n_s = x_p.shape, x_s.shape[0]
    n = w.shape[1]
    tm, tn = _tile(math.gcd(n_p, n_s), 1024), _tile(n, 512)
    pmap, smap = _row_maps(n_p // tm)
    return pl.pallas_call(
        functools.partial(_inproj_kernel, prompt_tiles=n_p // tm),
        grid=((n_p + n_s) // tm, n // tn),
        in_specs=[pl.BlockSpec((tm, d), pmap),
                  pl.BlockSpec((tm, d), smap),
                  pl.BlockSpec((1, d), lambda i, j: (0, 0)),
                  pl.BlockSpec((d, tn), lambda i, j: (0, j))],
        out_specs=pl.BlockSpec((tm, tn), lambda i, j: (i, j)),
        out_shape=jax.ShapeDtypeStruct((n_p + n_s, n), F32),
        scratch_shapes=[pltpu.VMEM((tm, d), BF16)],
        compiler_params=_cp(("arbitrary", "arbitrary"), 58 * 1024 * 1024),
        name="inproj",
    )(x_p, x_s, g, w)


AUG = 2 * HEAD_DIM


def _aug_constants():
    nh = WIDTH // HEAD_DIM
    pq = np.zeros((3 * LANES, WIDTH), np.float32)
    pk = np.zeros((3 * LANES, WIDTH), np.float32)
    one_q = np.zeros((1, WIDTH), np.float32)
    one_k = np.zeros((1, WIDTH), np.float32)
    for h in range(nh):
        for comp in range(3):
            pq[comp * LANES + h, h * HEAD_DIM + comp] = 1.0
            pk[comp * LANES + h, h * HEAD_DIM + 3 + comp] = -1.0
            one_q[0, h * HEAD_DIM + 3 + comp] = 1.0
            one_k[0, h * HEAD_DIM + comp] = 1.0
    return jnp.asarray(pq, BF16), jnp.asarray(pk, BF16), jnp.asarray(one_q), jnp.asarray(one_k)


def _foxpost_kernel(q_ref, k_ref, v_ref, f_ref, qn_ref, kn_ref, bf_ref, pq_ref, pk_ref, oneq_ref, onek_ref,
                    oq, ok, ov, olf, oqa, oka, ovt, okt32, ovt32, carry, *, tiles_per_seq, prompt_tiles):
    i = pl.program_id(0)
    bd = _block_diag_ones()
    nh = WIDTH // HEAD_DIM

    def head_norm(x, g):
        ms = _segsum(x * x, bd) * (1.0 / HEAD_DIM)
        return x * lax.rsqrt(ms + RMS_EPS) * g

    qn = head_norm(q_ref[...], qn_ref[...])
    kn = head_norm(k_ref[...], kn_ref[...])
    v = v_ref[...]
    vt = v.T
    ovt[...] = vt.astype(BF16)

    @pl.when(i < prompt_tiles)
    def _():
        okt32[0] = kn.T
        ovt32[0] = vt

    @pl.when(i >= prompt_tiles)
    def _():
        oq[...] = qn
        ok[...] = kn
        ov[...] = v

    lf = -_softplus(-(f_ref[...] + bf_ref[...]))
    olf[...] = lf

    @pl.when(i % tiles_per_seq == 0)
    def _():
        carry[...] = jnp.zeros_like(carry)

    tm = lf.shape[0]
    r = lax.broadcasted_iota(jnp.int32, (tm, tm), 0)
    c = lax.broadcasted_iota(jnp.int32, (tm, tm), 1)
    tri = jnp.where(c <= r, 1.0, 0.0).astype(BF16)
    hi, mid, lo = _split3(lf)
    cum = _dot(tri, hi) + _dot(tri, mid) + _dot(tri, lo) + carry[...]
    carry[...] = cum[tm - 1:tm, :]

    c3 = jnp.concatenate(_split3(cum), axis=1)
    aug_q = _dot(c3, pq_ref[...]) + oneq_ref[...]
    aug_k = _dot(c3, pk_ref[...]) + onek_ref[...]
    qs = qn * (HEAD_DIM ** -0.5)

    def interleave(x, aug):
        pieces = []
        for h in range(nh):
            sl = slice(h * HEAD_DIM, (h + 1) * HEAD_DIM)
            pieces += [x[:, sl], aug[:, sl]]
        return jnp.concatenate(pieces, axis=1).astype(BF16)

    oqa[...] = interleave(qs, aug_q)
    oka[...] = interleave(kn, aug_k)


def _foxpost(z, q_norm_t, k_norm_t, bf_pad, batch, seq_len):
    m = z.shape[0]
    nh = WIDTH // HEAD_DIM
    n_p = batch * seq_len
    tm = min(_tile(math.gcd(n_p, m - n_p), 512), seq_len)
    npt, tps = n_p // tm, seq_len // tm
    sample = pl.BlockSpec((tm, WIDTH), lambda i: (jnp.maximum(i - npt, 0), 0))
    prompt_t = pl.BlockSpec((1, WIDTH, tm), lambda i: (jnp.minimum(i, npt - 1) // tps, 0, jnp.minimum(i, npt - 1) % tps))
    row = lambda c: pl.BlockSpec((tm, WIDTH), lambda i, c=c: (i, c // WIDTH))
    vec = pl.BlockSpec((1, WIDTH), lambda i: (0, 0))
    small = pl.BlockSpec((tm, LANES), lambda i: (i, 0))
    wide = pl.BlockSpec((tm, nh * AUG), lambda i: (i, 0))
    place = pl.BlockSpec((3 * LANES, WIDTH), lambda i: (0, 0))
    return pl.pallas_call(
        functools.partial(_foxpost_kernel, tiles_per_seq=tps, prompt_tiles=npt),
        grid=(m // tm,),
        in_specs=[row(C_Q), row(C_FK), row(C_FV),
                  pl.BlockSpec((tm, LANES), lambda i: (i, C_F // LANES)),
                  vec, vec, pl.BlockSpec((1, LANES), lambda i: (0, 0)), place, place, vec, vec],
        out_specs=[sample] * 3 + [small, wide, wide, pl.BlockSpec((WIDTH, tm), lambda i: (0, i)), prompt_t, prompt_t],
        out_shape=[jax.ShapeDtypeStruct((m - n_p, WIDTH), F32)] * 3 + [jax.ShapeDtypeStruct((m, LANES), F32),
                                                                        jax.ShapeDtypeStruct((m, nh * AUG), BF16),
                                                                        jax.ShapeDtypeStruct((m, nh * AUG), BF16),
                                                                        jax.ShapeDtypeStruct((WIDTH, m), BF16),
                                                                        jax.ShapeDtypeStruct((batch, WIDTH, seq_len), F32),
                                                                        jax.ShapeDtypeStruct((batch, WIDTH, seq_len), F32)],
        scratch_shapes=[pltpu.VMEM((1, LANES), F32)],
        compiler_params=_cp(("arbitrary",)),
        name="foxpost",
    )(z, z, z, z, q_norm_t, k_norm_t, bf_pad, *_aug_constants())


FLASH_HEADS = 16


def _flash_kernel(qi_tab, ki_tab, qa_ref, ka_ref, vt_ref, og_ref, g_ref, init_hbm, o_ref, m_s, l_s, acc_s):
    s_idx = pl.program_id(2)
    qi = qi_tab[s_idx]
    ki = ki_tab[s_idx]
    tq = qa_ref.shape[0]
    tk = ka_ref.shape[0]

    @pl.when(ki == 0)
    def _():
        m_s[...] = jnp.full_like(m_s, -jnp.inf)
        l_s[...] = jnp.zeros_like(l_s)
        acc_s[...] = jnp.zeros_like(acc_s)

    def update(masked):
        for hh in range(FLASH_HEADS):
            st = _dot_nt(ka_ref[:, hh * AUG:(hh + 1) * AUG], qa_ref[:, hh * AUG:(hh + 1) * AUG])
            if masked:
                r = lax.broadcasted_iota(jnp.int32, (tk, tq), 0)
                c = lax.broadcasted_iota(jnp.int32, (tk, tq), 1)
                st = jnp.where(r <= c, st, -jnp.inf)
            m_prev = m_s[hh]
            m_new = jnp.maximum(m_prev, jnp.max(st, axis=0, keepdims=True))
            alpha = jnp.exp(m_prev - m_new)
            p = jnp.exp(st - m_new)
            l_s[hh] = alpha * l_s[hh] + jnp.sum(p, axis=0, keepdims=True)
            acc_s[hh] = alpha * acc_s[hh] + _dot(vt_ref[hh * HEAD_DIM:(hh + 1) * HEAD_DIM, :], p.astype(BF16))
            m_s[hh] = m_new

    @pl.when(ki < qi)
    def _():
        update(False)

    @pl.when(ki == qi)
    def _():
        update(True)
        outs = []
        for hh in range(FLASH_HEADS):
            sl = slice(hh * HEAD_DIM, (hh + 1) * HEAD_DIM)
            ot = acc_s[hh] / l_s[hh]
            ont = ot * lax.rsqrt(jnp.mean(ot * ot, axis=0, keepdims=True) + RMS_EPS)
            outs.append(ont.T * g_ref[:, sl] * jax.nn.sigmoid(og_ref[:, sl]))
        o_ref[...] = jnp.concatenate(outs, axis=1)


def _flash_prompt(qa, ka, vt, z, g_fox_o, batch, seq):
    tq = _tile(seq, 512)
    nq = seq // tq
    pairs = [(a, b) for a in range(nq) for b in range(a + 1)]
    qi_tab = jnp.array([a for a, _ in pairs], jnp.int32)
    ki_tab = jnp.array([b for _, b in pairs], jnp.int32)
    hw = FLASH_HEADS * HEAD_DIM
    npairs = WIDTH // hw
    qmap = lambda b, p, s, qt, kt: (b * nq + qt[s], p)
    kmap = lambda b, p, s, qt, kt: (b * nq + kt[s], p)
    gs = pltpu.PrefetchScalarGridSpec(
        num_scalar_prefetch=2,
        grid=(batch, npairs, len(pairs)),
        in_specs=[pl.BlockSpec((tq, FLASH_HEADS * AUG), qmap),
                  pl.BlockSpec((tq, FLASH_HEADS * AUG), kmap),
                  pl.BlockSpec((hw, tq), lambda b, p, s, qt, kt: (p, b * nq + kt[s])),
                  pl.BlockSpec((tq, hw), lambda b, p, s, qt, kt: (b * nq + qt[s], C_OG // hw + p)),
                  pl.BlockSpec((1, hw), lambda b, p, s, qt, kt: (0, p)),
                  pl.BlockSpec(memory_space=pl.ANY)],
        out_specs=pl.BlockSpec((tq, hw), qmap),
        scratch_shapes=[pltpu.VMEM((FLASH_HEADS, 1, tq), F32), pltpu.VMEM((FLASH_HEADS, 1, tq), F32),
                        pltpu.VMEM((FLASH_HEADS, HEAD_DIM, tq), F32)],
    )
    return pl.pallas_call(
        _flash_kernel,
        grid_spec=gs,
        out_shape=jax.ShapeDtypeStruct((qa.shape[0], WIDTH), F32),
        input_output_aliases={7: 0},
        compiler_params=_cp(("arbitrary", "arbitrary", "arbitrary")),
        name="flash_prompt",
    )(qi_tab, ki_tab, qa, ka, vt, z, g_fox_o, jnp.zeros((qa.shape[0], WIDTH), F32))


PAGE_GROUP = 4
PAGE_BUFS = 12


def _paged_kernel(pt_ref, q_ref, kn_ref, vn_ref, lfn_ref, og_ref, g_ref, ck_hbm, cv_hbm, lf_hbm, of_hbm, o_ref,
                  kbuf, vbuf, lbuf, sem, *, n_pages):
    b = pl.program_id(0)
    total = pl.num_programs(0) * n_pages
    depth = PAGE_BUFS - PAGE_GROUP
    nh = WIDTH // HEAD_DIM
    rows = nh * DEC_SEQ

    def copies(page, slot):
        return (pltpu.make_async_copy(ck_hbm.at[page], kbuf.at[slot], sem.at[0, slot]),
                pltpu.make_async_copy(cv_hbm.at[page], vbuf.at[slot], sem.at[1, slot]),
                pltpu.make_async_copy(lf_hbm.at[page], lbuf.at[slot], sem.at[2, slot]))

    def fetch(g):
        bb = g // n_pages
        for c in copies(pt_ref[bb, g - bb * n_pages], g % PAGE_BUFS):
            c.start()

    @pl.when(b == 0)
    def _():
        for g in range(depth):
            fetch(g)

    q = (q_ref[...] * (HEAD_DIM ** -0.5)).astype(BF16)
    qh = [q[:, h * HEAD_DIM:(h + 1) * HEAD_DIM] for h in range(nh)]
    def cumulate(lf, carry):
        n = lf.shape[1]
        r = lax.broadcasted_iota(jnp.int32, (n, n), 0)
        c = lax.broadcasted_iota(jnp.int32, (n, n), 1)
        upper = jnp.where(r <= c, 1.0, 0.0).astype(BF16)
        hi, mid, lo = _split3(lf)
        cum = _dot(hi, upper) + _dot(mid, upper) + _dot(lo, upper) + carry
        return cum, cum[:, n - 1:n]

    def attend(pages, feature_major, ck, valid, m, l, acc):
        qk = _dot if feature_major else _dot_nt
        pv_dot = _dot_nt if feature_major else _dot
        s = jnp.concatenate([jnp.concatenate([qk(qh[h], kh[h]) for kh, _ in pages], axis=1) for h in range(nh)], axis=0)
        s = s - jnp.concatenate([jnp.broadcast_to(ck[h:h + 1, :], (DEC_SEQ, ck.shape[1])) for h in range(nh)], axis=0)
        if valid is not None:
            s = jnp.where(valid, s, -jnp.inf)
        m_new = jnp.maximum(m, jnp.max(s, axis=-1, keepdims=True))
        alpha = jnp.exp(m - m_new)
        p = jnp.exp(s - m_new)
        l = alpha * l + jnp.sum(p, axis=-1, keepdims=True)
        pb = p.astype(BF16)
        pv = []
        for h in range(nh):
            rows_h = slice(h * DEC_SEQ, (h + 1) * DEC_SEQ)
            pv.append(sum(pv_dot(pb[rows_h, i * PAGE:(i + 1) * PAGE], vh[h]) for i, (_, vh) in enumerate(pages)))
        return m_new, l, alpha * acc + jnp.concatenate(pv, axis=0)

    def pair_step(jj, carry):
        m, l, acc, ccar = carry
        g0 = b * n_pages + PAGE_GROUP * jj
        slots = [(g0 + u) % PAGE_BUFS for u in range(PAGE_GROUP)]
        for slot in slots:
            for cp in copies(0, slot):
                cp.wait()
        for u in range(PAGE_GROUP):
            @pl.when(g0 + depth + u < total)
            def _():
                fetch(g0 + depth + u)

        ck, ccar = cumulate(jnp.concatenate([lbuf[slot] for slot in slots], axis=1), ccar)
        pages = [([kbuf[slot, h].astype(BF16) for h in range(nh)],
                  [vbuf[slot, h].astype(BF16) for h in range(nh)]) for slot in slots]
        m, l, acc = attend(pages, True, ck, None, m, l, acc)
        return m, l, acc, ccar

    init = (jnp.full((rows, 1), -jnp.inf, F32), jnp.zeros((rows, 1), F32), jnp.zeros((rows, HEAD_DIM), F32),
            jnp.zeros((nh, 1), F32))
    m, l, acc, ccar = lax.fori_loop(0, n_pages // PAGE_GROUP, pair_step, init)

    pad = jnp.zeros((PAGE - DEC_SEQ, HEAD_DIM), BF16)
    kn = kn_ref[...].astype(BF16)
    vn = vn_ref[...].astype(BF16)
    k_heads = [jnp.concatenate([kn[:, h * HEAD_DIM:(h + 1) * HEAD_DIM], pad], axis=0) for h in range(nh)]
    v_heads = [jnp.concatenate([vn[:, h * HEAD_DIM:(h + 1) * HEAD_DIM], pad], axis=0) for h in range(nh)]
    ck, _ = cumulate(lfn_ref[0], ccar)
    rr = lax.broadcasted_iota(jnp.int32, (rows, PAGE), 0)
    cc = lax.broadcasted_iota(jnp.int32, (rows, PAGE), 1)
    m, l, acc = attend([(k_heads, v_heads)], False, ck, cc <= (rr & (DEC_SEQ - 1)), m, l, acc)
    o = acc / l
    on = o * lax.rsqrt(jnp.mean(o * o, axis=-1, keepdims=True) + RMS_EPS)
    out = jnp.concatenate([on[h * DEC_SEQ:(h + 1) * DEC_SEQ, :] for h in range(nh)], axis=1)
    o_ref[...] = out * g_ref[...] * jax.nn.sigmoid(og_ref[...])


def _paged_sample(page_table, qn, kn, v, lf_new_t, cache_k, cache_v, lf_pages_t, z, g_fox_o, o_f, row0):
    db, n_pages = page_table.shape
    assert n_pages % PAGE_GROUP == 0 and n_pages >= PAGE_BUFS
    nh = WIDTH // HEAD_DIM
    rb = row0 // DEC_SEQ
    newmap = lambda b, pt: (b, 0)
    hbm = pl.BlockSpec(memory_space=pl.ANY)
    gs = pltpu.PrefetchScalarGridSpec(
        num_scalar_prefetch=1,
        grid=(db,),
        in_specs=[pl.BlockSpec((DEC_SEQ, WIDTH), newmap),
                  pl.BlockSpec((DEC_SEQ, WIDTH), newmap),
                  pl.BlockSpec((DEC_SEQ, WIDTH), newmap),
                  pl.BlockSpec((1, nh, PAGE), lambda b, pt: (b, 0, 0)),
                  pl.BlockSpec((DEC_SEQ, WIDTH), lambda b, pt: (rb + b, C_OG // WIDTH)),
                  pl.BlockSpec((1, WIDTH), lambda b, pt: (0, 0)),
                  hbm, hbm, hbm, hbm],
        out_specs=pl.BlockSpec((DEC_SEQ, WIDTH), lambda b, pt: (rb + b, 0)),
        scratch_shapes=[pltpu.VMEM((PAGE_BUFS, nh, HEAD_DIM, PAGE), F32), pltpu.VMEM((PAGE_BUFS, nh, HEAD_DIM, PAGE), F32),
                        pltpu.VMEM((PAGE_BUFS, nh, PAGE), F32), pltpu.SemaphoreType.DMA((3, PAGE_BUFS))],
    )
    return pl.pallas_call(
        functools.partial(_paged_kernel, n_pages=n_pages),
        grid_spec=gs,
        out_shape=jax.ShapeDtypeStruct(o_f.shape, F32),
        input_output_aliases={10: 0},
        compiler_params=_cp(("arbitrary",)),
        name="paged_sample",
    )(page_table, qn, kn, v, lf_new_t, z, g_fox_o, cache_k, cache_v, lf_pages_t, o_f)


def _prep_kernel(z_ref, first_ref, mu_ref, w12_ref, w3_ref, w0_ref, a0_ref, kk_ref, ka_ref, rk_ref,
                 g_hbm, bonus_hbm, o_r, o_w, o_k, o_v, o_a, o_b, o_g, o_bonus, carry, *, seq_len, tiles_per_seq):
    z = z_ref[...]
    tm = z.shape[0]
    row = lax.broadcasted_iota(jnp.int32, (tm, 1), 0)
    rolled = pltpu.roll(z, 1, 0)
    if tiles_per_seq >= 1 and seq_len >= tm:
        li = pl.program_id(0) % tiles_per_seq

        @pl.when(li == 0)
        def _():
            carry[...] = first_ref[0]

        zp = jnp.where(row == 0, carry[...], rolled)
        carry[...] = z[tm - 1:tm, :]
    else:
        nseq = tm // seq_len
        first = first_ref[...]
        exp = jnp.broadcast_to(first, (nseq, seq_len, first.shape[-1])).reshape(tm, first.shape[-1])
        zp = jnp.where((row & (seq_len - 1)) == 0, exp, rolled)
    zm = z + (zp - z) * mu_ref[...]
    r = zm[:, 0:WIDTH]
    k = zm[:, WIDTH:2 * WIDTH]
    v = zm[:, 2 * WIDTH:3 * WIDTH]
    lo = zm[:, C_LORA:C_LORA + 384]
    lane = lax.broadcasted_iota(jnp.int32, (tm, 384), 1)
    act = jnp.where(lane < 64, jnp.tanh(lo), jnp.where(lane < 128, lo, jax.nn.sigmoid(lo))).astype(BF16)
    l12 = _dot(act[:, 0:128], w12_ref[...])
    g = _dot(act[:, 128:384], w3_ref[...])
    log_w = -_softplus(-(w0_ref[...] + l12[:, 0:WIDTH])) - 0.5
    decay = jnp.exp(-jnp.exp(log_w))
    asig = jax.nn.sigmoid(a0_ref[...] + l12[:, WIDTH:2 * WIDTH])
    bd = _block_diag_ones()
    kk = k * kk_ref[...]
    kk = kk / jnp.maximum(jnp.sqrt(_segsum(kk * kk, bd)), 1e-12)
    kf = k * (1.0 + (asig - 1.0) * ka_ref[...])
    o_r[...] = r
    o_w[...] = decay
    o_k[...] = kf
    o_v[...] = v
    o_a[...] = -kk
    o_b[...] = kk * asig
    o_g[...] = g
    o_bonus[...] = _segsum(r * kf * rk_ref[...], bd) * v


def _rwkv_prep(z, first, mu, w12, w3, w0, a0, k_k, k_a, r_k, row0, n_rows, seq_len, shared):
    if seq_len >= 256:
        tm = _tile(seq_len, 256)
        first_spec = pl.BlockSpec((1, 1, RWKV_COLS), lambda i: (i // (seq_len // tm), 0, 0))
    else:
        tm = _tile(n_rows, 128)
        nseq = tm // seq_len
        first_spec = pl.BlockSpec((nseq, 1, RWKV_COLS), lambda i: (i, 0, 0))
    rb = row0 // tm
    vec = lambda n: pl.BlockSpec((1, n), lambda i: (0, 0))
    out = pl.BlockSpec((tm, WIDTH), lambda i: (i, 0))
    out_all = pl.BlockSpec((tm, WIDTH), lambda i: (rb + i, 0))
    hbm = pl.BlockSpec(memory_space=pl.ANY)
    extra = tuple(shared)
    return pl.pallas_call(
        functools.partial(_prep_kernel, seq_len=seq_len, tiles_per_seq=max(seq_len // tm, 1)),
        grid=(n_rows // tm,),
        in_specs=[pl.BlockSpec((tm, RWKV_COLS), lambda i: (rb + i, 0)), first_spec, vec(RWKV_COLS),
                  pl.BlockSpec((128, 2 * WIDTH), lambda i: (0, 0)), pl.BlockSpec((256, WIDTH), lambda i: (0, 0)),
                  vec(WIDTH), vec(WIDTH), vec(WIDTH), vec(WIDTH), vec(WIDTH)] + [hbm] * len(extra),
        out_specs=[out] * 6 + [out_all] * 2,
        out_shape=[jax.ShapeDtypeStruct((n_rows, WIDTH), F32)] * 6 + [jax.ShapeDtypeStruct((z.shape[0], WIDTH), F32)] * 2,
        input_output_aliases={10: 6, 11: 7},
        scratch_shapes=[pltpu.VMEM((1, RWKV_COLS), F32)],
        compiler_params=_cp(("arbitrary",)),
        name=f"rwkv_prep_{seq_len}",
    )(z, first, mu, w12, w3, w0, a0, k_k, k_a, r_k, *extra)


SCAN_SUB = 64
SCAN_SPLIT = 2
SCAN_UNROLL = 8


def _wkv_kernel(r_ref, w_ref, k_ref, v_ref, a_ref, b_ref, s0_ref, y_ref, sT_ref, S_s, Z_s, *, groups, n_pairs, tb_len):
    tb = pl.program_id(1)
    r2 = lax.broadcasted_iota(jnp.int32, (2 * LANES, 2 * LANES), 0)
    c2 = lax.broadcasted_iota(jnp.int32, (2 * LANES, 2 * LANES), 1)
    ones_bd = jnp.where((r2 >> 6) == (c2 >> 6), 1.0, 0.0).astype(BF16)
    vrow = lax.broadcasted_iota(jnp.int32, (HEAD_DIM, LANES), 0)
    lane = lax.broadcasted_iota(jnp.int32, (HEAD_DIM, LANES), 1)
    eye2 = (lane & (HEAD_DIM - 1)) == vrow
    zero_half = jnp.zeros((HEAD_DIM, LANES), BF16)
    H = HEAD_DIM

    @pl.when(tb == 0)
    def _():
        for g in range(groups):
            for p in range(n_pairs):
                S_s[g * n_pairs + p] = jnp.concatenate([s0_ref[g, 2 * p], s0_ref[g, 2 * p + 1]], axis=1)
        Z_s[...] = jnp.zeros_like(Z_s)

    def lanes(p):
        return slice(p * LANES, (p + 1) * LANES)

    def run(t0, n):
        def step(i, carry):
            t = t0 + i
            tp = jnp.maximum(t - 1, 0)
            hit = (lane & (SCAN_SUB - 1)) == (i - 1)
            for g in range(groups):
                aa = a_ref[g, pl.ds(t, 1), :]
                ww = w_ref[g, pl.ds(t, 1), :]
                kr = k_ref[g, pl.ds(t, 1), :]
                vv = v_ref[g, pl.ds(t, 1), :]
                bb = b_ref[g, pl.ds(t, 1), :]
                rp = r_ref[g, pl.ds(tp, 1), :]
                npg = n_pairs // SCAN_SPLIT
                for p0 in range(0, n_pairs, npg):
                    lhs_s, xs = [], []
                    for p in range(p0, p0 + npg):
                        S = S_s[g * n_pairs + p]
                        lhs_s.append(jnp.concatenate([(S * aa[:, lanes(p)]).astype(BF16),
                                                      (S * rp[:, lanes(p)]).astype(BF16)], axis=1))
                        xs.append(jnp.where(eye2, vv[:, lanes(p)], 0.0).astype(BF16))
                    lhs_x = [jnp.concatenate(xs[q:q + 2], axis=1) for q in range(0, npg, 2)]
                    res = _dot(jnp.concatenate(lhs_s + lhs_x, axis=0), ones_bd)
                    for q in range(npg):
                        p = p0 + q
                        idx = g * n_pairs + p
                        rs = res[q * H:(q + 1) * H]
                        vb = res[(npg + q // 2) * H:(npg + q // 2 + 1) * H, (q % 2) * LANES:(q % 2 + 1) * LANES]
                        S_s[idx] = S_s[idx] * ww[:, lanes(p)] + rs[:, :LANES] * bb[:, lanes(p)] + vb * kr[:, lanes(p)]
                        Z_s[idx] = jnp.where(hit, rs[:, LANES:], Z_s[idx])
            return carry

        lax.fori_loop(0, n, step, 0, unroll=SCAN_UNROLL)
        hit_last = (lane & (SCAN_SUB - 1)) == (n - 1)
        for g in range(groups):
            rl = r_ref[g, pl.ds(t0 + n - 1, 1), :]
            lhs = [jnp.concatenate([(S_s[g * n_pairs + p] * rl[:, lanes(p)]).astype(BF16), zero_half], axis=1)
                   for p in range(n_pairs)]
            res = _dot(jnp.concatenate(lhs, axis=0), ones_bd)
            for p in range(n_pairs):
                idx = g * n_pairs + p
                zt = jnp.where(hit_last, res[p * H:(p + 1) * H, :LANES], Z_s[idx]).T
                y_ref[g, pl.ds(t0, n), p * LANES:p * LANES + H] = zt[0:n, :]
                y_ref[g, pl.ds(t0, n), p * LANES + H:(p + 1) * LANES] = zt[SCAN_SUB:SCAN_SUB + n, :]

    if tb_len <= SCAN_SUB:
        run(0, tb_len)
    else:
        def outer(blk, carry):
            run(pl.multiple_of(blk * SCAN_SUB, SCAN_SUB), SCAN_SUB)
            return carry
        lax.fori_loop(0, tb_len // SCAN_SUB, outer, 0)

    @pl.when(tb == pl.num_programs(1) - 1)
    def _():
        for g in range(groups):
            for p in range(n_pairs):
                S = S_s[g * n_pairs + p]
                sT_ref[g, 2 * p] = S[:, 0:H]
                sT_ref[g, 2 * p + 1] = S[:, H:LANES]


def _wkv_scan(r, w, k, v, a, b, s0, n_seq, seq_len):
    nh = WIDTH // HEAD_DIM
    groups = 2 if n_seq % 2 == 0 else 1
    tb_len = min(seq_len, 128)
    nt = seq_len // tb_len
    blk = pl.BlockSpec((groups, tb_len, WIDTH), lambda s, t: (s, t, 0))
    st = pl.BlockSpec((groups, nh, HEAD_DIM, HEAD_DIM), lambda s, t: (s, 0, 0, 0))
    n_pairs = WIDTH // LANES
    seq3 = lambda x: x.reshape(n_seq, seq_len, WIDTH)
    y, s_new = pl.pallas_call(
        functools.partial(_wkv_kernel, groups=groups, n_pairs=n_pairs, tb_len=tb_len),
        grid=(n_seq // groups, nt),
        in_specs=[blk] * 6 + [st],
        out_specs=[blk, st],
        out_shape=[jax.ShapeDtypeStruct((n_seq, seq_len, WIDTH), F32),
                   jax.ShapeDtypeStruct((n_seq, nh, HEAD_DIM, HEAD_DIM), F32)],
        scratch_shapes=[pltpu.VMEM((groups * n_pairs, HEAD_DIM, LANES), F32),
                        pltpu.VMEM((groups * n_pairs, HEAD_DIM, LANES), F32)],
        compiler_params=_cp(("arbitrary", "arbitrary")),
        name=f"wkv_scan_{seq_len}",
    )(seq3(r), seq3(w), seq3(k), seq3(v), seq3(a), seq3(b), s0)
    return y.reshape(n_seq * seq_len, WIDTH), s_new


def _outproj_kernel(xp_ref, xs_ref, yp_ref, ys_ref, bonus_ref, g_ref, of_ref, lw_ref, lb_ref, wo_ref, gf_ref,
                    wrg_ref, wre_ref, brg_ref, bre_ref, x1_ref, h_ref, route_ref, wrh_s, wrl_s, br_s, *, prompt_tiles):
    i = pl.program_id(0)

    @pl.when(i == 0)
    def _():
        pad = LANES - N_GROUPS - N_EXPERTS
        w_r = jnp.concatenate([wrg_ref[...], wre_ref[...], jnp.zeros((pad, wrg_ref.shape[1]), F32)], axis=0)
        hi = w_r.astype(BF16)
        wrh_s[...] = hi
        wrl_s[...] = (w_r - hi.astype(F32)).astype(BF16)
        br_s[...] = jnp.concatenate([brg_ref[...], bre_ref[...], jnp.zeros((1, pad), F32)], axis=1)

    is_prompt = i < prompt_tiles
    bd = _block_diag_ones()
    y = jnp.where(is_prompt, yp_ref[...], ys_ref[...])
    mu = _segsum(y, bd) * (1.0 / HEAD_DIM)
    d = y - mu
    var = _segsum(d * d, bd) * (1.0 / HEAD_DIM)
    yn = d * lax.rsqrt(var + LNX_EPS) * lw_ref[...] + lb_ref[...]
    o_r = ((yn + bonus_ref[...]) * g_ref[...]).astype(BF16)
    o_f = of_ref[...].astype(BF16)
    x = jnp.where(is_prompt, xp_ref[...], xs_ref[...])
    x1 = x + _dot(o_r, wo_ref[0:WIDTH, :]) + _dot(o_f, wo_ref[WIDTH:2 * WIDTH, :])
    x1_ref[...] = x1
    h = _rms_rows(x1, gf_ref[...])
    h_ref[...] = h
    hi, lo = _split2(h)
    logits = _dot_nt(hi, wrh_s[...]) + _dot_nt(lo, wrh_s[...]) + _dot_nt(hi, wrl_s[...]) + br_s[...]
    tm = logits.shape[0]
    lane = lax.broadcasted_iota(jnp.int32, (tm, LANES), 1)
    neg = -jnp.inf
    lg = jnp.where(lane < N_GROUPS, logits, neg)
    mg = jnp.max(lg, axis=-1, keepdims=True)
    pg_top = 1.0 / jnp.sum(jnp.exp(lg - mg), axis=-1, keepdims=True)
    g_sel = jnp.min(jnp.where(lg == mg, lane, LANES), axis=-1, keepdims=True)
    in_grp = (lane >= N_GROUPS) & (lane < N_GROUPS + N_EXPERTS) & (((lane - N_GROUPS) >> 3) == g_sel)
    le = jnp.where(in_grp, logits, neg)
    m1 = jnp.max(le, axis=-1, keepdims=True)
    i1 = jnp.min(jnp.where(le == m1, lane, LANES), axis=-1, keepdims=True)
    le2 = jnp.where(lane == i1, neg, le)
    m2 = jnp.max(le2, axis=-1, keepdims=True)
    i2 = jnp.min(jnp.where(le2 == m2, lane, LANES), axis=-1, keepdims=True)
    e2 = jnp.exp(m2 - m1)
    gate1 = pg_top / (1.0 + e2)
    gate2 = pg_top * e2 / (1.0 + e2)
    route = jnp.where(lane == 0, (i1 - N_GROUPS).astype(F32),
                      jnp.where(lane == 1, (i2 - N_GROUPS).astype(F32),
                                jnp.where(lane == 2, gate1, jnp.where(lane == 3, gate2, 0.0))))
    route_ref[...] = route


def _outproj(x_p, x_s, y_p, y_s, bonus, g, o_f, lnx_w, lnx_b, w_out, g_ffn, w_rg, w_re, b_rg, b_re):
    (n_p, d), n_s = x_p.shape, x_s.shape[0]
    m = n_p + n_s
    tm = _tile(math.gcd(n_p, n_s), 256)
    pmap, smap = _row_maps(n_p // tm)
    row = lambda n: pl.BlockSpec((tm, n), lambda i: (i, 0))
    vec = lambda n: pl.BlockSpec((1, n), lambda i: (0, 0))
    full = lambda a: pl.BlockSpec(a.shape, lambda i: (0, 0))
    return pl.pallas_call(
        functools.partial(_outproj_kernel, prompt_tiles=n_p // tm),
        grid=(m // tm,),
        in_specs=[pl.BlockSpec((tm, d), pmap), pl.BlockSpec((tm, d), smap),
                  pl.BlockSpec((tm, WIDTH), pmap), pl.BlockSpec((tm, WIDTH), smap),
                  row(WIDTH), row(WIDTH), row(WIDTH), vec(WIDTH), vec(WIDTH),
                  full(w_out), vec(d), full(w_rg), full(w_re), full(b_rg), full(b_re)],
        out_specs=[row(d), row(d), row(LANES)],
        out_shape=[jax.ShapeDtypeStruct((m, d), F32), jax.ShapeDtypeStruct((m, d), F32),
                   jax.ShapeDtypeStruct((m, LANES), F32)],
        scratch_shapes=[pltpu.VMEM((LANES, d), BF16), pltpu.VMEM((LANES, d), BF16), pltpu.VMEM((1, LANES), F32)],
        compiler_params=_cp(("arbitrary",)),
        name="outproj_router",
    )(x_p, x_s, y_p, y_s, bonus, g, o_f, lnx_w, lnx_b, w_out, g_ffn, w_rg, w_re, b_rg, b_re)


def _expert_kernel(plan_ref, tok_ref, nu_ref, h_hbm, wg_hbm, wu_hbm, wd_hbm, o_ref, xbuf, wg, wu, wd, sem, wsem):
    i = pl.program_id(0)
    n_used = nu_ref[0]
    first, nxt, wslot = plan_ref[1, i], plan_ref[2, i], plan_ref[3, i]

    def row_copy(tok, r, slot):
        return pltpu.make_async_copy(h_hbm.at[pl.ds(tok, 1), :], xbuf.at[slot, pl.ds(r, 1), :], sem.at[slot])

    def gather(blk, slot):
        def start(r, c):
            row_copy(tok_ref[blk * MOE_BLOCK + r], r, slot).start()
            return c
        lax.fori_loop(0, plan_ref[4, blk], start, 0)

    def weight_copies(expert, slot):
        return (pltpu.make_async_copy(wg_hbm.at[expert], wg.at[slot], wsem.at[0, slot]),
                pltpu.make_async_copy(wu_hbm.at[expert], wu.at[slot], wsem.at[1, slot]),
                pltpu.make_async_copy(wd_hbm.at[expert], wd.at[slot], wsem.at[2, slot]))

    @pl.when(i == 0)
    def _():
        xbuf[...] = jnp.zeros_like(xbuf)
        gather(0, 0)
        for c in weight_copies(plan_ref[0, 0], 0):
            c.start()

    @pl.when(i < n_used)
    def _():
        slot = i & 1

        @pl.when(first == 1)
        def _():
            for c in weight_copies(0, wslot):
                c.wait()

            @pl.when(nxt >= 0)
            def _():
                for c in weight_copies(nxt, 1 - wslot):
                    c.start()

        def wait(r, c):
            row_copy(0, r, slot).wait()
            return c
        lax.fori_loop(0, plan_ref[4, i], wait, 0)

        @pl.when(i + 1 < n_used)
        def _():
            gather(i + 1, 1 - slot)

        x = xbuf[slot].astype(BF16)
        gate = _dot(x, wg[wslot].astype(BF16))
        up = _dot(x, wu[wslot].astype(BF16))
        act = (gate * jax.nn.sigmoid(gate) * up).astype(BF16)
        o_ref[...] = _dot(act, wd[wslot].astype(BF16))

    @pl.when(i >= n_used)
    def _():
        o_ref[...] = jnp.zeros_like(o_ref)


def _experts(plan, slot_tok, n_used, h, w_gate, w_up, w_down):
    n_blocks = plan.shape[1]
    d = h.shape[1]
    de = w_gate.shape[2]
    hbm = pl.BlockSpec(memory_space=pl.ANY)
    gs = pltpu.PrefetchScalarGridSpec(
        num_scalar_prefetch=3,
        grid=(n_blocks,),
        in_specs=[hbm, hbm, hbm, hbm],
        out_specs=pl.BlockSpec((MOE_BLOCK, d), lambda i, plan, tok, nu: (i, 0)),
        scratch_shapes=[pltpu.VMEM((2, MOE_BLOCK, d), F32), pltpu.VMEM((2, d, de), F32), pltpu.VMEM((2, d, de), F32),
                        pltpu.VMEM((2, de, d), F32), pltpu.SemaphoreType.DMA((2,)), pltpu.SemaphoreType.DMA((3, 2))],
    )
    return pl.pallas_call(
        _expert_kernel,
        grid_spec=gs,
        out_shape=jax.ShapeDtypeStruct((n_blocks * MOE_BLOCK, d), F32),
        compiler_params=_cp(("arbitrary",), 58 * 1024 * 1024),
        name="experts",
    )(plan, slot_tok, n_used, h, w_gate, w_up, w_down)


def _ple_kernel(dest_ref, x1_ref, route_ref, p_ref, yb_hbm, gp_ref, wpg_ref, bpg_ref, wpp_ref, gfin_ref, op_ref, os_ref,
                buf, sem, *, prompt_tiles):
    i = pl.program_id(0)
    tm = x1_ref.shape[0]

    def row_copy(slot_row, r, k, half):
        return pltpu.make_async_copy(yb_hbm.at[pl.ds(slot_row, 1), :], buf.at[half, k, pl.ds(r, 1), :], sem.at[half])

    def gather(tile, half):
        def start(r, c):
            for k in range(2):
                row_copy(dest_ref[(tile * tm + r) * 2 + k], r, k, half).start()
            return c
        lax.fori_loop(0, tm, start, 0, unroll=8)

    @pl.when(i == 0)
    def _():
        gather(0, 0)

    half = i & 1

    def wait(r, c):
        for k in range(2):
            row_copy(0, r, k, half).wait()
        return c
    lax.fori_loop(0, tm, wait, 0, unroll=8)

    @pl.when(i + 1 < pl.num_programs(0))
    def _():
        gather(i + 1, 1 - half)

    route = route_ref[...]
    moe = buf[half, 0] * route[:, 2:3] + buf[half, 1] * route[:, 3:4]
    x2 = x1_ref[...] + moe
    hn = _rms_rows(x2, gp_ref[...]).astype(BF16)
    gate = jax.nn.sigmoid(_dot(hn, wpg_ref[...]) + bpg_ref[...])
    pe = _dot(p_ref[...].astype(BF16), wpp_ref[...])
    x3 = x2 + gate * pe
    y = _rms_rows(x3, gfin_ref[...])

    @pl.when(i < prompt_tiles)
    def _():
        op_ref[...] = y

    @pl.when(i >= prompt_tiles)
    def _():
        os_ref[...] = y


def _ple(dest, x1, route, p, yb, g_ple, w_pg, b_pg, w_pp, g_final, n_p):
    m, d = x1.shape
    tm = _tile(math.gcd(n_p, m - n_p), 256)
    npt = n_p // tm
    pd = p.shape[1]
    gs = pltpu.PrefetchScalarGridSpec(
        num_scalar_prefetch=1,
        grid=(m // tm,),
        in_specs=[pl.BlockSpec((tm, d), lambda i, ds: (i, 0)),
                  pl.BlockSpec((tm, LANES), lambda i, ds: (i, 0)),
                  pl.BlockSpec((tm, pd), lambda i, ds: (i, 0)),
                  pl.BlockSpec(memory_space=pl.ANY),
                  pl.BlockSpec((1, d), lambda i, ds: (0, 0)),
                  pl.BlockSpec((d, d), lambda i, ds: (0, 0)),
                  pl.BlockSpec((1, d), lambda i, ds: (0, 0)),
                  pl.BlockSpec((pd, d), lambda i, ds: (0, 0)),
                  pl.BlockSpec((1, d), lambda i, ds: (0, 0))],
        out_specs=[pl.BlockSpec((tm, d), lambda i, ds: (jnp.minimum(i, npt - 1), 0)),
                   pl.BlockSpec((tm, d), lambda i, ds: (jnp.maximum(i - npt, 0), 0))],
        scratch_shapes=[pltpu.VMEM((2, 2, tm, d), F32), pltpu.SemaphoreType.DMA((2,))],
    )
    return pl.pallas_call(
        functools.partial(_ple_kernel, prompt_tiles=npt),
        grid_spec=gs,
        out_shape=[jax.ShapeDtypeStruct((n_p, d), F32), jax.ShapeDtypeStruct((m - n_p, d), F32)],
        compiler_params=_cp(("arbitrary",)),
        name="moe_combine_ple",
    )(dest, x1, route, p, yb, g_ple, w_pg, b_pg, w_pp, g_final)


def _dispatch(route, n_tok):
    expert = route[:, 0:2].astype(jnp.int32)
    flat_e = expert.reshape(-1)
    n_assign = flat_e.shape[0]
    onehot = (flat_e[:, None] == jnp.arange(N_EXPERTS, dtype=jnp.int32)[None, :]).astype(jnp.int32)
    csum = jnp.cumsum(onehot, axis=0)
    rank = jnp.sum(csum * onehot, axis=1) - 1
    counts = csum[-1]
    pcounts = (counts + MOE_BLOCK - 1) // MOE_BLOCK * MOE_BLOCK
    pend = jnp.cumsum(pcounts)
    pstart = pend - pcounts
    dest = (pstart[flat_e] + rank).astype(jnp.int32)
    n_blocks = -(-n_assign // MOE_BLOCK) + N_EXPERTS
    slot_tok = jnp.zeros((n_blocks * MOE_BLOCK,), jnp.int32).at[dest].set(jnp.arange(n_assign, dtype=jnp.int32) // 2)
    n_used = (pend[-1] // MOE_BLOCK).astype(jnp.int32)
    blk = jnp.arange(n_blocks, dtype=jnp.int32)
    block_e = jnp.searchsorted(pend, jnp.minimum(blk, n_used - 1) * MOE_BLOCK, side='right').astype(jnp.int32)
    block_e = jnp.minimum(block_e, N_EXPERTS - 1)
    first = jnp.concatenate([jnp.ones((1,), jnp.int32), (block_e[1:] != block_e[:-1]).astype(jnp.int32)])
    first = jnp.where(blk < n_used, first, 0)
    used = counts > 0
    eidx = jnp.arange(N_EXPERTS, dtype=jnp.int32)
    later = jnp.flip(lax.cummin(jnp.flip(jnp.where(used, eidx, N_EXPERTS))))
    next_used = jnp.concatenate([later[1:], jnp.full((1,), N_EXPERTS, jnp.int32)])
    next_used = jnp.where(next_used < N_EXPERTS, next_used, -1)
    ordinal = jnp.cumsum(used.astype(jnp.int32)) - 1
    valid = jnp.clip(pstart[block_e] + counts[block_e] - blk * MOE_BLOCK, 0, MOE_BLOCK)
    valid = jnp.where(blk < n_used, valid, 0)
    plan = jnp.stack([block_e, first, next_used[block_e], ordinal[block_e] & 1, valid]).astype(jnp.int32)
    return plan, slot_tok, n_used.reshape(1), dest


def kernel(x_prompt, x_sample, cache_k, cache_v, cache_logf, state_wkv, state_shift, page_table, p_prompt, p_sample, g_attn, w_in, mu_shift, w0, w_up, a0, a_up, g_up, k_k, k_a, r_k, lnx_w, lnx_b, b_f, q_norm, k_norm, g_fox_o, w_out, g_ffn, w_rg, b_rg, w_re, b_re, w_e_gate, w_e_up, w_e_down, g_ple, w_pg, b_pg, w_pp, g_final):
    depth = g_attn.shape[0]
    assert depth == 1
    batch, seq, d = x_prompt.shape
    db, dec_seq, _ = x_sample.shape
    assert dec_seq == DEC_SEQ
    nh = WIDTH // HEAD_DIM
    n_p = batch * seq
    n_s = db * dec_seq
    m = n_p + n_s
    rwkv_in = mu_shift.shape[1]
    n_pool = cache_k.shape[1]
    n_pages = page_table.shape[1]
    row = lambda a: a.reshape(1, -1)

    x_p, x_s = x_prompt.reshape(n_p, d), x_sample.reshape(n_s, d)
    p_all = jnp.concatenate([p_prompt[0].reshape(n_p, -1), p_sample[0].reshape(n_s, -1)], axis=0)

    wi = w_in[0]
    zc = lambda n: jnp.zeros((d, n), F32)
    w_z = jnp.concatenate([wi[:, :rwkv_in], zc(C_F - rwkv_in), wi[:, rwkv_in + 4 * WIDTH:], zc(C_Q - C_F - nh),
                           wi[:, rwkv_in:rwkv_in + 4 * WIDTH]], axis=1).astype(BF16)
    w12 = jnp.zeros((128, 2 * WIDTH), F32).at[0:64, 0:WIDTH].set(w_up[0]).at[64:128, WIDTH:].set(a_up[0]).astype(BF16)
    w3 = jnp.zeros((256, WIDTH), F32).at[0:g_up.shape[1]].set(g_up[0]).astype(BF16)
    pad_cols = lambda a, n: jnp.pad(a, ((0, 0), (0, n - a.shape[1])))
    mu_pad = pad_cols(mu_shift, RWKV_COLS)
    bf_pad = pad_cols(b_f, LANES)
    qn_t = jnp.tile(q_norm, (1, nh))
    kn_t = jnp.tile(k_norm, (1, nh))

    z = _inproj(x_p, x_s, g_attn, w_z)

    qn_s, kn_s, vv_s, logf, qa, ka, vt, kt_p, vt_p = _foxpost(z, qn_t, kn_t, bf_pad, batch, seq)

    o_f = _flash_prompt(qa, ka, vt, z, g_fox_o, batch, seq)
    lf_pages_t = cache_logf[0].transpose(0, 2, 1)
    lf_new_t = pad_cols(logf[n_p:, :nh].reshape(db, dec_seq, nh).transpose(0, 2, 1).reshape(db * nh, dec_seq), PAGE)
    o_f = _paged_sample(page_table, qn_s, kn_s, vv_s, lf_new_t.reshape(db, nh, PAGE),
                        cache_k[0].transpose(0, 2, 3, 1), cache_v[0].transpose(0, 2, 3, 1), lf_pages_t, z, g_fox_o, o_f, n_p)

    first_p = jnp.zeros((batch, 1, RWKV_COLS), F32)
    first_s = pad_cols(state_shift[0], RWKV_COLS).reshape(db, 1, RWKV_COLS)
    prep_args = (mu_pad, w12, w3, w0, a0, k_k, k_a, row(r_k))
    prep_p = _rwkv_prep(z, first_p, *prep_args, 0, n_p, seq, shared=(jnp.zeros((m, WIDTH), F32),) * 2)
    prep_s = _rwkv_prep(z, first_s, *prep_args, n_p, n_s, dec_seq, shared=prep_p[6:8])
    g_, bonus = prep_s[6:8]
    y_p, wkv_p = _wkv_scan(*prep_p[:6], jnp.zeros((batch, nh, HEAD_DIM, HEAD_DIM), F32), batch, seq)
    y_s, wkv_s = _wkv_scan(*prep_s[:6], state_wkv[0], db, dec_seq)

    x1, h2, route = _outproj(x_p, x_s, y_p, y_s, bonus, g_, o_f, lnx_w, lnx_b, w_out[0].astype(BF16), g_ffn,
                             w_rg[0].T, w_re[0].T, b_rg, b_re)

    plan, slot_tok, n_used, dest = _dispatch(route, m)
    yb = _experts(plan, slot_tok, n_used, h2, w_e_gate[0], w_e_up[0], w_e_down[0])

    y_out_p, y_out_s = _ple(dest, x1, route, p_all, yb, g_ple, w_pg[0].astype(BF16), b_pg, w_pp[0].astype(BF16),
                            row(g_final), n_p)

    shift_p = jnp.concatenate([z[(b + 1) * seq - 1:(b + 1) * seq, :rwkv_in] for b in range(batch)], axis=0)
    shift_s = z[n_p + dec_seq - 1::dec_seq, :rwkv_in]
    heads_t = lambda a: a.reshape(batch, nh, HEAD_DIM, seq).transpose(0, 3, 1, 2)[None]
    heads = lambda a: a.reshape(1, db, dec_seq, nh, HEAD_DIM)
    return (y_out_p.reshape(batch, seq, d), y_out_s.reshape(db, dec_seq, d),
            heads_t(kt_p), heads_t(vt_p), logf[:n_p, :nh].reshape(1, batch, seq, nh),
            wkv_p[None], shift_p[None],
            heads(kn_s), heads(vv_s), logf[n_p:, :nh].reshape(1, db, dec_seq, nh),
            wkv_s[None], shift_s[None])
```

```python
import functools
import math

import jax
import jax.numpy as jnp
import numpy as np
from jax import lax
from jax.experimental import pallas as pl
from jax.experimental.pallas import tpu as pltpu

F32 = jnp.float32
BF16 = jnp.bfloat16

HEAD_DIM = 64
LANES = 128
RMS_EPS = 1e-6
LNX_EPS = 64e-5
PAGE = 128
MOE_BLOCK = 384
N_GROUPS = 8
N_EXPERTS = 64
DEC_SEQ = 8

C_LORA = 3072
C_F = 3456
RWKV_COLS = 3584
C_Q, C_FK, C_FV, C_OG = 4096, 5120, 6144, 7168
Z_COLS = 8192
WIDTH = 1024

VMEM_LIMIT = 52 * 1024 * 1024


def _cp(sem, vmem=VMEM_LIMIT):
    return pltpu.CompilerParams(dimension_semantics=sem, vmem_limit_bytes=vmem)


def _tile(n, pref):
    for t in (1024, 512, 256, 128, 64, 32, 16, 8):
        if t <= pref and n % t == 0:
            return t
    raise ValueError(f"no tile for {n}")


def _split2(x):
    hi = x.astype(BF16)
    lo = (x - hi.astype(F32)).astype(BF16)
    return hi, lo


def _split3(x):
    hi = x.astype(BF16)
    r1 = x - hi.astype(F32)
    mid = r1.astype(BF16)
    lo = (r1 - mid.astype(F32)).astype(BF16)
    return hi, mid, lo


def _dot(a, b):
    return jnp.dot(a, b, preferred_element_type=F32)


def _dot_nt(a, b):
    return lax.dot_general(a, b, (((1,), (1,)), ((), ())), preferred_element_type=F32)


def _block_diag_ones():
    r = lax.broadcasted_iota(jnp.int32, (LANES, LANES), 0)
    c = lax.broadcasted_iota(jnp.int32, (LANES, LANES), 1)
    return jnp.where((r >> 6) == (c >> 6), 1.0, 0.0).astype(BF16)


def _segsum(x, bd):
    outs = []
    for j in range(x.shape[1] // LANES):
        hi, lo = _split2(x[:, j * LANES:(j + 1) * LANES])
        outs.append(_dot(hi, bd) + _dot(lo, bd))
    return outs[0] if len(outs) == 1 else jnp.concatenate(outs, axis=1)


def _softplus(x):
    return jnp.maximum(x, 0.0) + jnp.log1p(jnp.exp(-jnp.abs(x)))


def _rms_rows(x, g):
    return x * lax.rsqrt(jnp.mean(x * x, axis=-1, keepdims=True) + RMS_EPS) * g


def _row_maps(prompt_tiles):
    return (lambda i, *_: (jnp.minimum(i, prompt_tiles - 1), 0)), (lambda i, *_: (jnp.maximum(i - prompt_tiles, 0), 0))


def _inproj_kernel(xp_ref, xs_ref, g_ref, w_ref, o_ref, h_ref, *, prompt_tiles):
    @pl.when(pl.program_id(1) == 0)
    def _():
        x = jnp.where(pl.program_id(0) < prompt_tiles, xp_ref[...], xs_ref[...])
        h_ref[...] = _rms_rows(x, g_ref[...]).astype(BF16)

    o_ref[...] = _dot(h_ref[...], w_ref[...])


def _inproj(x_p, x_s, g, w):
    (n_p, d), n_s = x_p.shape, x_s.shape[0]
    n = w.shape[1]
    tm, tn = _tile(math.gcd(n_p, n_s), 1024), _tile(n, 512)
    pmap, smap = _row_maps(n_p // tm)
    return pl.pallas_call(
        functools.partial(_inproj_kernel, prompt_tiles=n_p // tm),
        grid=((n_p + n_s) // tm, n // tn),
        in_specs=[pl.BlockSpec((tm, d), pmap),
                  pl.BlockSpec((tm, d), smap),
                  pl.BlockSpec((1, d), lambda i, j: (0, 0)),
                  pl.BlockSpec((d, tn), lambda i, j: (0, j))],
        out_specs=pl.BlockSpec((tm, tn), lambda i, j: (i, j)),
        out_shape=jax.ShapeDtypeStruct((n_p + n_s, n), F32),
        scratch_shapes=[pltpu.VMEM((tm, d), BF16)],
        compiler_params=_cp(("arbitrary", "arbitrary"), 58 * 1024 * 1024),
        name="inproj",
    )(x_p, x_s, g, w)


AUG = 2 * HEAD_DIM


def _aug_constants():
    nh = WIDTH // HEAD_DIM
    pq = np.zeros((3 * LANES, WIDTH), np.float32)
    pk = np.zeros((3 * LANES, WIDTH), np.float32)
    one_q = np.zeros((1, WIDTH), np.float32)
    one_k = np.zeros((1, WIDTH), np.float32)
    for h in range(nh):
        for comp in range(3):
            pq[comp * LANES + h, h * HEAD_DIM + comp] = 1.0
            pk[comp * LANES + h, h * HEAD_DIM + 3 + comp] = -1.0
            one_q[0, h * HEAD_DIM + 3 + comp] = 1.0
            one_k[0, h * HEAD_DIM + comp] = 1.0
    return jnp.asarray(pq, BF16), jnp.asarray(pk, BF16), jnp.asarray(one_q), jnp.asarray(one_k)


def _foxpost_kernel(q_ref, k_ref, v_ref, f_ref, qn_ref, kn_ref, bf_ref, pq_ref, pk_ref, oneq_ref, onek_ref,
                    oq, ok, ov, olf, oqa, oka, ovt, okt32, ovt32, carry, *, tiles_per_seq, prompt_tiles):
    i = pl.program_id(0)
    bd = _block_diag_ones()
    nh = WIDTH // HEAD_DIM

    def head_norm(x, g):
        ms = _segsum(x * x, bd) * (1.0 / HEAD_DIM)
        return x * lax.rsqrt(ms + RMS_EPS) * g

    qn = head_norm(q_ref[...], qn_ref[...])
    kn = head_norm(k_ref[...], kn_ref[...])
    v = v_ref[...]
    vt = v.T
    ovt[...] = vt.astype(BF16)

    @pl.when(i < prompt_tiles)
    def _():
        okt32[0] = kn.T
        ovt32[0] = vt

    @pl.when(i >= prompt_tiles)
    def _():
        oq[...] = qn
        ok[...] = kn
        ov[...] = v

    lf = -_softplus(-(f_ref[...] + bf_ref[...]))
    olf[...] = lf

    @pl.when(i % tiles_per_seq == 0)
    def _():
        carry[...] = jnp.zeros_like(carry)

    tm = lf.shape[0]
    r = lax.broadcasted_iota(jnp.int32, (tm, tm), 0)
    c = lax.broadcasted_iota(jnp.int32, (tm, tm), 1)
    tri = jnp.where(c <= r, 1.0, 0.0).astype(BF16)
    hi, mid, lo = _split3(lf)
    cum = _dot(tri, hi) + _dot(tri, mid) + _dot(tri, lo) + carry[...]
    carry[...] = cum[tm - 1:tm, :]

    c3 = jnp.concatenate(_split3(cum), axis=1)
    aug_q = _dot(c3, pq_ref[...]) + oneq_ref[...]
    aug_k = _dot(c3, pk_ref[...]) + onek_ref[...]
    qs = qn * (HEAD_DIM ** -0.5)

    def interleave(x, aug):
        pieces = []
        for h in range(nh):
            sl = slice(h * HEAD_DIM, (h + 1) * HEAD_DIM)
            pieces += [x[:, sl], aug[:, sl]]
        return jnp.concatenate(pieces, axis=1).astype(BF16)

    oqa[...] = interleave(qs, aug_q)
    oka[...] = interleave(kn, aug_k)


def _foxpost(z, q_norm_t, k_norm_t, bf_pad, batch, seq_len):
    m = z.shape[0]
    nh = WIDTH // HEAD_DIM
    n_p = batch * seq_len
    tm = min(_tile(math.gcd(n_p, m - n_p), 512), seq_len)
    npt, tps = n_p // tm, seq_len // tm
    sample = pl.BlockSpec((tm, WIDTH), lambda i: (jnp.maximum(i - npt, 0), 0))
    prompt_t = pl.BlockSpec((1, WIDTH, tm), lambda i: (jnp.minimum(i, npt - 1) // tps, 0, jnp.minimum(i, npt - 1) % tps))
    row = lambda c: pl.BlockSpec((tm, WIDTH), lambda i, c=c: (i, c // WIDTH))
    vec = pl.BlockSpec((1, WIDTH), lambda i: (0, 0))
    small = pl.BlockSpec((tm, LANES), lambda i: (i, 0))
    wide = pl.BlockSpec((tm, nh * AUG), lambda i: (i, 0))
    place = pl.BlockSpec((3 * LANES, WIDTH), lambda i: (0, 0))
    return pl.pallas_call(
        functools.partial(_foxpost_kernel, tiles_per_seq=tps, prompt_tiles=npt),
        grid=(m // tm,),
        in_specs=[row(C_Q), row(C_FK), row(C_FV),
                  pl.BlockSpec((tm, LANES), lambda i: (i, C_F // LANES)),
                  vec, vec, pl.BlockSpec((1, LANES), lambda i: (0, 0)), place, place, vec, vec],
        out_specs=[sample] * 3 + [small, wide, wide, pl.BlockSpec((WIDTH, tm), lambda i: (0, i)), prompt_t, prompt_t],
        out_shape=[jax.ShapeDtypeStruct((m - n_p, WIDTH), F32)] * 3 + [jax.ShapeDtypeStruct((m, LANES), F32),
                                                                        jax.ShapeDtypeStruct((m, nh * AUG), BF16),
                                                                        jax.ShapeDtypeStruct((m, nh * AUG), BF16),
                                                                        jax.ShapeDtypeStruct((WIDTH, m), BF16),
                                                                        jax.ShapeDtypeStruct((batch, WIDTH, seq_len), F32),
                                                                        jax.ShapeDtypeStruct((batch, WIDTH, seq_len), F32)],
        scratch_shapes=[pltpu.VMEM((1, LANES), F32)],
        compiler_params=_cp(("arbitrary",)),
        name="foxpost",
    )(z, z, z, z, q_norm_t, k_norm_t, bf_pad, *_aug_constants())


FLASH_HEADS = 16


def _flash_kernel(qi_tab, ki_tab, qa_ref, ka_ref, vt_ref, og_ref, g_ref, init_hbm, o_ref, m_s, l_s, acc_s):
    s_idx = pl.program_id(2)
    qi = qi_tab[s_idx]
    ki = ki_tab[s_idx]
    tq = qa_ref.shape[0]
    tk = ka_ref.shape[0]

    @pl.when(ki == 0)
    def _():
        m_s[...] = jnp.full_like(m_s, -jnp.inf)
        l_s[...] = jnp.zeros_like(l_s)
        acc_s[...] = jnp.zeros_like(acc_s)

    def update(masked):
        for hh in range(FLASH_HEADS):
            st = _dot_nt(ka_ref[:, hh * AUG:(hh + 1) * AUG], qa_ref[:, hh * AUG:(hh + 1) * AUG])
            if masked:
                r = lax.broadcasted_iota(jnp.int32, (tk, tq), 0)
                c = lax.broadcasted_iota(jnp.int32, (tk, tq), 1)
                st = jnp.where(r <= c, st, -jnp.inf)
            m_prev = m_s[hh]
            m_new = jnp.maximum(m_prev, jnp.max(st, axis=0, keepdims=True))
            alpha = jnp.exp(m_prev - m_new)
            p = jnp.exp(st - m_new)
            l_s[hh] = alpha * l_s[hh] + jnp.sum(p, axis=0, keepdims=True)
            acc_s[hh] = alpha * acc_s[hh] + _dot(vt_ref[hh * HEAD_DIM:(hh + 1) * HEAD_DIM, :], p.astype(BF16))
            m_s[hh] = m_new

    @pl.when(ki < qi)
    def _():
        update(False)

    @pl.when(ki == qi)
    def _():
        update(True)
        outs = []
        for hh in range(FLASH_HEADS):
            sl = slice(hh * HEAD_DIM, (hh + 1) * HEAD_DIM)
            ot = acc_s[hh] / l_s[hh]
            ont = ot * lax.rsqrt(jnp.mean(ot * ot, axis=0, keepdims=True) + RMS_EPS)
            outs.append(ont.T * g_ref[:, sl] * jax.nn.sigmoid(og_ref[:, sl]))
        o_ref[...] = jnp.concatenate(outs, axis=1)


def _flash_prompt(qa, ka, vt, z, g_fox_o, batch, seq):
    tq = _tile(seq, 512)
    nq = seq // tq
    pairs = [(a, b) for a in range(nq) for b in range(a + 1)]
    qi_tab = jnp.array([a for a, _ in pairs], jnp.int32)
    ki_tab = jnp.array([b for _, b in pairs], jnp.int32)
    hw = FLASH_HEADS * HEAD_DIM
    npairs = WIDTH // hw
    qmap = lambda b, p, s, qt, kt: (b * nq + qt[s], p)
    kmap = lambda b, p, s, qt, kt: (b * nq + kt[s], p)
    gs = pltpu.PrefetchScalarGridSpec(
        num_scalar_prefetch=2,
        grid=(batch, npairs, len(pairs)),
        in_specs=[pl.BlockSpec((tq, FLASH_HEADS * AUG), qmap),
                  pl.BlockSpec((tq, FLASH_HEADS * AUG), kmap),
                  pl.BlockSpec((hw, tq), lambda b, p, s, qt, kt: (p, b * nq + kt[s])),
                  pl.BlockSpec((tq, hw), lambda b, p, s, qt, kt: (b * nq + qt[s], C_OG // hw + p)),
                  pl.BlockSpec((1, hw), lambda b, p, s, qt, kt: (0, p)),
                  pl.BlockSpec(memory_space=pl.ANY)],
        out_specs=pl.BlockSpec((tq, hw), qmap),
        scratch_shapes=[pltpu.VMEM((FLASH_HEADS, 1, tq), F32), pltpu.VMEM((FLASH_HEADS, 1, tq), F32),
                        pltpu.VMEM((FLASH_HEADS, HEAD_DIM, tq), F32)],
    )
    return pl.pallas_call(
        _flash_kernel,
        grid_spec=gs,
        out_shape=jax.ShapeDtypeStruct((qa.shape[0], WIDTH), F32),
        input_output_aliases={7: 0},
        compiler_params=_cp(("arbitrary", "arbitrary", "arbitrary")),
        name="flash_prompt",
    )(qi_tab, ki_tab, qa, ka, vt, z, g_fox_o, jnp.zeros((qa.shape[0], WIDTH), F32))


PAGE_GROUP = 4
PAGE_BUFS = 12


def _paged_kernel(pt_ref, q_ref, kn_ref, vn_ref, lfn_ref, og_ref, g_ref, ck_hbm, cv_hbm, lf_hbm, of_hbm, o_ref,
                  kbuf, vbuf, lbuf, sem, *, n_pages):
    b = pl.program_id(0)
    total = pl.num_programs(0) * n_pages
    depth = PAGE_BUFS - PAGE_GROUP
    nh = WIDTH // HEAD_DIM
    rows = nh * DEC_SEQ

    def copies(page, slot):
        return (pltpu.make_async_copy(ck_hbm.at[page], kbuf.at[slot], sem.at[0, slot]),
                pltpu.make_async_copy(cv_hbm.at[page], vbuf.at[slot], sem.at[1, slot]),
                pltpu.make_async_copy(lf_hbm.at[page], lbuf.at[slot], sem.at[2, slot]))

    def fetch(g):
        bb = g // n_pages
        for c in copies(pt_ref[bb, g - bb * n_pages], g % PAGE_BUFS):
            c.start()

    @pl.when(b == 0)
    def _():
        for g in range(depth):
            fetch(g)

    q = (q_ref[...] * (HEAD_DIM ** -0.5)).astype(BF16)
    qh = [q[:, h * HEAD_DIM:(h + 1) * HEAD_DIM] for h in range(nh)]
    def cumulate(lf, carry):
        n = lf.shape[1]
        r = lax.broadcasted_iota(jnp.int32, (n, n), 0)
        c = lax.broadcasted_iota(jnp.int32, (n, n), 1)
        upper = jnp.where(r <= c, 1.0, 0.0).astype(BF16)
        hi, mid, lo = _split3(lf)
        cum = _dot(hi, upper) + _dot(mid, upper) + _dot(lo, upper) + carry
        return cum, cum[:, n - 1:n]

    def attend(pages, feature_major, ck, valid, m, l, acc):
        qk = _dot if feature_major else _dot_nt
        pv_dot = _dot_nt if feature_major else _dot
        s = jnp.concatenate([jnp.concatenate([qk(qh[h], kh[h]) for kh, _ in pages], axis=1) for h in range(nh)], axis=0)
        s = s - jnp.concatenate([jnp.broadcast_to(ck[h:h + 1, :], (DEC_SEQ, ck.shape[1])) for h in range(nh)], axis=0)
        if valid is not None:
            s = jnp.where(valid, s, -jnp.inf)
        m_new = jnp.maximum(m, jnp.max(s, axis=-1, keepdims=True))
        alpha = jnp.exp(m - m_new)
        p = jnp.exp(s - m_new)
        l = alpha * l + jnp.sum(p, axis=-1, keepdims=True)
        pb = p.astype(BF16)
        pv = []
        for h in range(nh):
            rows_h = slice(h * DEC_SEQ, (h + 1) * DEC_SEQ)
            pv.append(sum(pv_dot(pb[rows_h, i * PAGE:(i + 1) * PAGE], vh[h]) for i, (_, vh) in enumerate(pages)))
        return m_new, l, alpha * acc + jnp.concatenate(pv, axis=0)

    def pair_step(jj, carry):
        m, l, acc, ccar = carry
        g0 = b * n_pages + PAGE_GROUP * jj
        slots = [(g0 + u) % PAGE_BUFS for u in range(PAGE_GROUP)]
        for slot in slots:
            for cp in copies(0, slot):
                cp.wait()
        for u in range(PAGE_GROUP):
            @pl.when(g0 + depth + u < total)
            def _():
                fetch(g0 + depth + u)

        ck, ccar = cumulate(jnp.concatenate([lbuf[slot] for slot in slots], axis=1), ccar)
        pages = [([kbuf[slot, h].astype(BF16) for h in range(nh)],
                  [vbuf[slot, h].astype(BF16) for h in range(nh)]) for slot in slots]
        m, l, acc = attend(pages, True, ck, None, m, l, acc)
        return m, l, acc, ccar

    init = (jnp.full((rows, 1), -jnp.inf, F32), jnp.zeros((rows, 1), F32), jnp.zeros((rows, HEAD_DIM), F32),
            jnp.zeros((nh, 1), F32))
    m, l, acc, ccar = lax.fori_loop(0, n_pages // PAGE_GROUP, pair_step, init)

    pad = jnp.zeros((PAGE - DEC_SEQ, HEAD_DIM), BF16)
    kn = kn_ref[...].astype(BF16)
    vn = vn_ref[...].astype(BF16)
    k_heads = [jnp.concatenate([kn[:, h * HEAD_DIM:(h + 1) * HEAD_DIM], pad], axis=0) for h in range(nh)]
    v_heads = [jnp.concatenate([vn[:, h * HEAD_DIM:(h + 1) * HEAD_DIM], pad], axis=0) for h in range(nh)]
    ck, _ = cumulate(lfn_ref[0], ccar)
    rr = lax.broadcasted_iota(jnp.int32, (rows, PAGE), 0)
    cc = lax.broadcasted_iota(jnp.int32, (rows, PAGE), 1)
    m, l, acc = attend([(k_heads, v_heads)], False, ck, cc <= (rr & (DEC_SEQ - 1)), m, l, acc)
    o = acc / l
    on = o * lax.rsqrt(jnp.mean(o * o, axis=-1, keepdims=True) + RMS_EPS)
    out = jnp.concatenate([on[h * DEC_SEQ:(h + 1) * DEC_SEQ, :] for h in range(nh)], axis=1)
    o_ref[...] = out * g_ref[...] * jax.nn.sigmoid(og_ref[...])


def _paged_sample(page_table, qn, kn, v, lf_new_t, cache_k, cache_v, lf_pages_t, z, g_fox_o, o_f, row0):
    db, n_pages = page_table.shape
    assert n_pages % PAGE_GROUP == 0 and n_pages >= PAGE_BUFS
    nh = WIDTH // HEAD_DIM
    rb = row0 // DEC_SEQ
    newmap = lambda b, pt: (b, 0)
    hbm = pl.BlockSpec(memory_space=pl.ANY)
    gs = pltpu.PrefetchScalarGridSpec(
        num_scalar_prefetch=1,
        grid=(db,),
        in_specs=[pl.BlockSpec((DEC_SEQ, WIDTH), newmap),
                  pl.BlockSpec((DEC_SEQ, WIDTH), newmap),
                  pl.BlockSpec((DEC_SEQ, WIDTH), newmap),
                  pl.BlockSpec((1, nh, PAGE), lambda b, pt: (b, 0, 0)),
                  pl.BlockSpec((DEC_SEQ, WIDTH), lambda b, pt: (rb + b, C_OG // WIDTH)),
                  pl.BlockSpec((1, WIDTH), lambda b, pt: (0, 0)),
                  hbm, hbm, hbm, hbm],
        out_specs=pl.BlockSpec((DEC_SEQ, WIDTH), lambda b, pt: (rb + b, 0)),
        scratch_shapes=[pltpu.VMEM((PAGE_BUFS, nh, HEAD_DIM, PAGE), F32), pltpu.VMEM((PAGE_BUFS, nh, HEAD_DIM, PAGE), F32),
                        pltpu.VMEM((PAGE_BUFS, nh, PAGE), F32), pltpu.SemaphoreType.DMA((3, PAGE_BUFS))],
    )
    return pl.pallas_call(
        functools.partial(_paged_kernel, n_pages=n_pages),
        grid_spec=gs,
        out_shape=jax.ShapeDtypeStruct(o_f.shape, F32),
        input_output_aliases={10: 0},
        compiler_params=_cp(("arbitrary",)),
        name="paged_sample",
    )(page_table, qn, kn, v, lf_new_t, z, g_fox_o, cache_k, cache_v, lf_pages_t, o_f)


def _prep_kernel(z_ref, first_ref, mu_ref, w12_ref, w3_ref, w0_ref, a0_ref, kk_ref, ka_ref, rk_ref,
                 g_hbm, bonus_hbm, o_r, o_w, o_k, o_v, o_a, o_b, o_g, o_bonus, carry, *, seq_len, tiles_per_seq):
    z = z_ref[...]
    tm = z.shape[0]
    row = lax.broadcasted_iota(jnp.int32, (tm, 1), 0)
    rolled = pltpu.roll(z, 1, 0)
    if tiles_per_seq >= 1 and seq_len >= tm:
        li = pl.program_id(0) % tiles_per_seq

        @pl.when(li == 0)
        def _():
            carry[...] = first_ref[0]

        zp = jnp.where(row == 0, carry[...], rolled)
        carry[...] = z[tm - 1:tm, :]
    else:
        nseq = tm // seq_len
        first = first_ref[...]
        exp = jnp.broadcast_to(first, (nseq, seq_len, first.shape[-1])).reshape(tm, first.shape[-1])
        zp = jnp.where((row & (seq_len - 1)) == 0, exp, rolled)
    zm = z + (zp - z) * mu_ref[...]
    r = zm[:, 0:WIDTH]
    k = zm[:, WIDTH:2 * WIDTH]
    v = zm[:, 2 * WIDTH:3 * WIDTH]
    lo = zm[:, C_LORA:C_LORA + 384]
    lane = lax.broadcasted_iota(jnp.int32, (tm, 384), 1)
    act = jnp.where(lane < 64, jnp.tanh(lo), jnp.where(lane < 128, lo, jax.nn.sigmoid(lo))).astype(BF16)
    l12 = _dot(act[:, 0:128], w12_ref[...])
    g = _dot(act[:, 128:384], w3_ref[...])
    log_w = -_softplus(-(w0_ref[...] + l12[:, 0:WIDTH])) - 0.5
    decay = jnp.exp(-jnp.exp(log_w))
    asig = jax.nn.sigmoid(a0_ref[...] + l12[:, WIDTH:2 * WIDTH])
    bd = _block_diag_ones()
    kk = k * kk_ref[...]
    kk = kk / jnp.maximum(jnp.sqrt(_segsum(kk * kk, bd)), 1e-12)
    kf = k * (1.0 + (asig - 1.0) * ka_ref[...])
    o_r[...] = r
    o_w[...] = decay
    o_k[...] = kf
    o_v[...] = v
    o_a[...] = -kk
    o_b[...] = kk * asig
    o_g[...] = g
    o_bonus[...] = _segsum(r * kf * rk_ref[...], bd) * v


def _rwkv_prep(z, first, mu, w12, w3, w0, a0, k_k, k_a, r_k, row0, n_rows, seq_len, shared):
    if seq_len >= 256:
        tm = _tile(seq_len, 256)
        first_spec = pl.BlockSpec((1, 1, RWKV_COLS), lambda i: (i // (seq_len // tm), 0, 0))
    else:
        tm = _tile(n_rows, 128)
        nseq = tm // seq_len
        first_spec = pl.BlockSpec((nseq, 1, RWKV_COLS), lambda i: (i, 0, 0))
    rb = row0 // tm
    vec = lambda n: pl.BlockSpec((1, n), lambda i: (0, 0))
    out = pl.BlockSpec((tm, WIDTH), lambda i: (i, 0))
    out_all = pl.BlockSpec((tm, WIDTH), lambda i: (rb + i, 0))
    hbm = pl.BlockSpec(memory_space=pl.ANY)
    extra = tuple(shared)
    return pl.pallas_call(
        functools.partial(_prep_kernel, seq_len=seq_len, tiles_per_seq=max(seq_len // tm, 1)),
        grid=(n_rows // tm,),
        in_specs=[pl.BlockSpec((tm, RWKV_COLS), lambda i: (rb + i, 0)), first_spec, vec(RWKV_COLS),
                  pl.BlockSpec((128, 2 * WIDTH), lambda i: (0, 0)), pl.BlockSpec((256, WIDTH), lambda i: (0, 0)),
                  vec(WIDTH), vec(WIDTH), vec(WIDTH), vec(WIDTH), vec(WIDTH)] + [hbm] * len(extra),
        out_specs=[out] * 6 + [out_all] * 2,
        out_shape=[jax.ShapeDtypeStruct((n_rows, WIDTH), F32)] * 6 + [jax.ShapeDtypeStruct((z.shape[0], WIDTH), F32)] * 2,
        input_output_aliases={10: 6, 11: 7},
        scratch_shapes=[pltpu.VMEM((1, RWKV_COLS), F32)],
        compiler_params=_cp(("arbitrary",)),
        name=f"rwkv_prep_{seq_len}",
    )(z, first, mu, w12, w3, w0, a0, k_k, k_a, r_k, *extra)


SCAN_SUB = 64
SCAN_SPLIT = 2
SCAN_UNROLL = 16


def _wkv_kernel(r_ref, w_ref, k_ref, v_ref, a_ref, b_ref, s0_ref, y_ref, sT_ref, S_s, Z_s, *, groups, n_pairs, tb_len):
    tb = pl.program_id(1)
    r2 = lax.broadcasted_iota(jnp.int32, (2 * LANES, 2 * LANES), 0)
    c2 = lax.broadcasted_iota(jnp.int32, (2 * LANES, 2 * LANES), 1)
    ones_bd = jnp.where((r2 >> 6) == (c2 >> 6), 1.0, 0.0).astype(BF16)
    vrow = lax.broadcasted_iota(jnp.int32, (HEAD_DIM, LANES), 0)
    lane = lax.broadcasted_iota(jnp.int32, (HEAD_DIM, LANES), 1)
    eye2 = (lane & (HEAD_DIM - 1)) == vrow
    zero_half = jnp.zeros((HEAD_DIM, LANES), BF16)
    H = HEAD_DIM

    @pl.when(tb == 0)
    def _():
        for g in range(groups):
            for p in range(n_pairs):
                S_s[g * n_pairs + p] = jnp.concatenate([s0_ref[g, 2 * p], s0_ref[g, 2 * p + 1]], axis=1)
        Z_s[...] = jnp.zeros_like(Z_s)

    def lanes(p):
        return slice(p * LANES, (p + 1) * LANES)

    def run(t0, n):
        def step(i, carry):
            t = t0 + i
            tp = jnp.maximum(t - 1, 0)
            hit = (lane & (SCAN_SUB - 1)) == (i - 1)
            for g in range(groups):
                aa = a_ref[g, pl.ds(t, 1), :]
                ww = w_ref[g, pl.ds(t, 1), :]
                kr = k_ref[g, pl.ds(t, 1), :]
                vv = v_ref[g, pl.ds(t, 1), :]
                bb = b_ref[g, pl.ds(t, 1), :]
                rp = r_ref[g, pl.ds(tp, 1), :]
                npg = n_pairs // SCAN_SPLIT
                for p0 in range(0, n_pairs, npg):
                    lhs_s, xs = [], []
                    for p in range(p0, p0 + npg):
                        S = S_s[g * n_pairs + p]
                        lhs_s.append(jnp.concatenate([(S * aa[:, lanes(p)]).astype(BF16),
                                                      (S * rp[:, lanes(p)]).astype(BF16)], axis=1))
                        xs.append(jnp.where(eye2, vv[:, lanes(p)], 0.0).astype(BF16))
                    lhs_x = [jnp.concatenate(xs[q:q + 2], axis=1) for q in range(0, npg, 2)]
                    res = _dot(jnp.concatenate(lhs_s + lhs_x, axis=0), ones_bd)
                    for q in range(npg):
                        p = p0 + q
                        idx = g * n_pairs + p
                        rs = res[q * H:(q + 1) * H]
                        vb = res[(npg + q // 2) * H:(npg + q // 2 + 1) * H, (q % 2) * LANES:(q % 2 + 1) * LANES]
                        S_s[idx] = S_s[idx] * ww[:, lanes(p)] + rs[:, :LANES] * bb[:, lanes(p)] + vb * kr[:, lanes(p)]
                        Z_s[idx] = jnp.where(hit, rs[:, LANES:], Z_s[idx])
            return carry

        lax.fori_loop(0, n, step, 0, unroll=SCAN_UNROLL)
        hit_last = (lane & (SCAN_SUB - 1)) == (n - 1)
        for g in range(groups):
            rl = r_ref[g, pl.ds(t0 + n - 1, 1), :]
            lhs = [jnp.concatenate([(S_s[g * n_pairs + p] * rl[:, lanes(p)]).astype(BF16), zero_half], axis=1)
                   for p in range(n_pairs)]
            res = _dot(jnp.concatenate(lhs, axis=0), ones_bd)
            for p in range(n_pairs):
                idx = g * n_pairs + p
                zt = jnp.where(hit_last, res[p * H:(p + 1) * H, :LANES], Z_s[idx]).T
                y_ref[g, pl.ds(t0, n), p * LANES:p * LANES + H] = zt[0:n, :]
                y_ref[g, pl.ds(t0, n), p * LANES + H:(p + 1) * LANES] = zt[SCAN_SUB:SCAN_SUB + n, :]

    if tb_len <= SCAN_SUB:
        run(0, tb_len)
    else:
        def outer(blk, carry):
            run(pl.multiple_of(blk * SCAN_SUB, SCAN_SUB), SCAN_SUB)
            return carry
        lax.fori_loop(0, tb_len // SCAN_SUB, outer, 0)

    @pl.when(tb == pl.num_programs(1) - 1)
    def _():
        for g in range(groups):
            for p in range(n_pairs):
                S = S_s[g * n_pairs + p]
                sT_ref[g, 2 * p] = S[:, 0:H]
                sT_ref[g, 2 * p + 1] = S[:, H:LANES]


def _wkv_scan(r, w, k, v, a, b, s0, n_seq, seq_len):
    nh = WIDTH // HEAD_DIM
    groups = 2 if n_seq % 2 == 0 else 1
    tb_len = min(seq_len, 128)
    nt = seq_len // tb_len
    blk = pl.BlockSpec((groups, tb_len, WIDTH), lambda s, t: (s, t, 0))
    st = pl.BlockSpec((groups, nh, HEAD_DIM, HEAD_DIM), lambda s, t: (s, 0, 0, 0))
    n_pairs = WIDTH // LANES
    seq3 = lambda x: x.reshape(n_seq, seq_len, WIDTH)
    y, s_new = pl.pallas_call(
        functools.partial(_wkv_kernel, groups=groups, n_pairs=n_pairs, tb_len=tb_len),
        grid=(n_seq // groups, nt),
        in_specs=[blk] * 6 + [st],
        out_specs=[blk, st],
        out_shape=[jax.ShapeDtypeStruct((n_seq, seq_len, WIDTH), F32),
                   jax.ShapeDtypeStruct((n_seq, nh, HEAD_DIM, HEAD_DIM), F32)],
        scratch_shapes=[pltpu.VMEM((groups * n_pairs, HEAD_DIM, LANES), F32),
                        pltpu.VMEM((groups * n_pairs, HEAD_DIM, LANES), F32)],
        compiler_params=_cp(("arbitrary", "arbitrary")),
        name=f"wkv_scan_{seq_len}",
    )(seq3(r), seq3(w), seq3(k), seq3(v), seq3(a), seq3(b), s0)
    return y.reshape(n_seq * seq_len, WIDTH), s_new


def _outproj_kernel(xp_ref, xs_ref, yp_ref, ys_ref, bonus_ref, g_ref, of_ref, lw_ref, lb_ref, wo_ref, gf_ref,
                    wrg_ref, wre_ref, brg_ref, bre_ref, x1_ref, h_ref, route_ref, wrh_s, wrl_s, br_s, *, prompt_tiles):
    i = pl.program_id(0)

    @pl.when(i == 0)
    def _():
        pad = LANES - N_GROUPS - N_EXPERTS
        w_r = jnp.concatenate([wrg_ref[...], wre_ref[...], jnp.zeros((pad, wrg_ref.shape[1]), F32)], axis=0)
        hi = w_r.astype(BF16)
        wrh_s[...] = hi
        wrl_s[...] = (w_r - hi.astype(F32)).astype(BF16)
        br_s[...] = jnp.concatenate([brg_ref[...], bre_ref[...], jnp.zeros((1, pad), F32)], axis=1)

    is_prompt = i < prompt_tiles
    bd = _block_diag_ones()
    y = jnp.where(is_prompt, yp_ref[...], ys_ref[...])
    mu = _segsum(y, bd) * (1.0 / HEAD_DIM)
    d = y - mu
    var = _segsum(d * d, bd) * (1.0 / HEAD_DIM)
    yn = d * lax.rsqrt(var + LNX_EPS) * lw_ref[...] + lb_ref[...]
    o_r = ((yn + bonus_ref[...]) * g_ref[...]).astype(BF16)
    o_f = of_ref[...].astype(BF16)
    x = jnp.where(is_prompt, xp_ref[...], xs_ref[...])
    x1 = x + _dot(o_r, wo_ref[0:WIDTH, :]) + _dot(o_f, wo_ref[WIDTH:2 * WIDTH, :])
    x1_ref[...] = x1
    h = _rms_rows(x1, gf_ref[...])
    h_ref[...] = h
    hi, lo = _split2(h)
    logits = _dot_nt(hi, wrh_s[...]) + _dot_nt(lo, wrh_s[...]) + _dot_nt(hi, wrl_s[...]) + br_s[...]
    tm = logits.shape[0]
    lane = lax.broadcasted_iota(jnp.int32, (tm, LANES), 1)
    neg = -jnp.inf
    lg = jnp.where(lane < N_GROUPS, logits, neg)
    mg = jnp.max(lg, axis=-1, keepdims=True)
    pg_top = 1.0 / jnp.sum(jnp.exp(lg - mg), axis=-1, keepdims=True)
    g_sel = jnp.min(jnp.where(lg == mg, lane, LANES), axis=-1, keepdims=True)
    in_grp = (lane >= N_GROUPS) & (lane < N_GROUPS + N_EXPERTS) & (((lane - N_GROUPS) >> 3) == g_sel)
    le = jnp.where(in_grp, logits, neg)
    m1 = jnp.max(le, axis=-1, keepdims=True)
    i1 = jnp.min(jnp.where(le == m1, lane, LANES), axis=-1, keepdims=True)
    le2 = jnp.where(lane == i1, neg, le)
    m2 = jnp.max(le2, axis=-1, keepdims=True)
    i2 = jnp.min(jnp.where(le2 == m2, lane, LANES), axis=-1, keepdims=True)
    e2 = jnp.exp(m2 - m1)
    gate1 = pg_top / (1.0 + e2)
    gate2 = pg_top * e2 / (1.0 + e2)
    route = jnp.where(lane == 0, (i1 - N_GROUPS).astype(F32),
                      jnp.where(lane == 1, (i2 - N_GROUPS).astype(F32),
                                jnp.where(lane == 2, gate1, jnp.where(lane == 3, gate2, 0.0))))
    route_ref[...] = route


def _outproj(x_p, x_s, y_p, y_s, bonus, g, o_f, lnx_w, lnx_b, w_out, g_ffn, w_rg, w_re, b_rg, b_re):
    (n_p, d), n_s = x_p.shape, x_s.shape[0]
    m = n_p + n_s
    tm = _tile(math.gcd(n_p, n_s), 256)
    pmap, smap = _row_maps(n_p // tm)
    row = lambda n: pl.BlockSpec((tm, n), lambda i: (i, 0))
    vec = lambda n: pl.BlockSpec((1, n), lambda i: (0, 0))
    full = lambda a: pl.BlockSpec(a.shape, lambda i: (0, 0))
    return pl.pallas_call(
        functools.partial(_outproj_kernel, prompt_tiles=n_p // tm),
        grid=(m // tm,),
        in_specs=[pl.BlockSpec((tm, d), pmap), pl.BlockSpec((tm, d), smap),
                  pl.BlockSpec((tm, WIDTH), pmap), pl.BlockSpec((tm, WIDTH), smap),
                  row(WIDTH), row(WIDTH), row(WIDTH), vec(WIDTH), vec(WIDTH),
                  full(w_out), vec(d), full(w_rg), full(w_re), full(b_rg), full(b_re)],
        out_specs=[row(d), row(d), row(LANES)],
        out_shape=[jax.ShapeDtypeStruct((m, d), F32), jax.ShapeDtypeStruct((m, d), F32),
                   jax.ShapeDtypeStruct((m, LANES), F32)],
        scratch_shapes=[pltpu.VMEM((LANES, d), BF16), pltpu.VMEM((LANES, d), BF16), pltpu.VMEM((1, LANES), F32)],
        compiler_params=_cp(("arbitrary",)),
        name="outproj_router",
    )(x_p, x_s, y_p, y_s, bonus, g, o_f, lnx_w, lnx_b, w_out, g_ffn, w_rg, w_re, b_rg, b_re)


def _expert_kernel(plan_ref, tok_ref, nu_ref, h_hbm, wg_hbm, wu_hbm, wd_hbm, o_ref, xbuf, wg, wu, wd, sem, wsem):
    i = pl.program_id(0)
    n_used = nu_ref[0]
    first, nxt, wslot = plan_ref[1, i], plan_ref[2, i], plan_ref[3, i]

    def row_copy(tok, r, slot):
        return pltpu.make_async_copy(h_hbm.at[pl.ds(tok, 1), :], xbuf.at[slot, pl.ds(r, 1), :], sem.at[slot])

    def gather(blk, slot):
        def start(r, c):
            row_copy(tok_ref[blk * MOE_BLOCK + r], r, slot).start()
            return c
        lax.fori_loop(0, plan_ref[4, blk], start, 0)

    def weight_copies(expert, slot):
        return (pltpu.make_async_copy(wg_hbm.at[expert], wg.at[slot], wsem.at[0, slot]),
                pltpu.make_async_copy(wu_hbm.at[expert], wu.at[slot], wsem.at[1, slot]),
                pltpu.make_async_copy(wd_hbm.at[expert], wd.at[slot], wsem.at[2, slot]))

    @pl.when(i == 0)
    def _():
        xbuf[...] = jnp.zeros_like(xbuf)
        gather(0, 0)
        for c in weight_copies(plan_ref[0, 0], 0):
            c.start()

    @pl.when(i < n_used)
    def _():
        slot = i & 1

        @pl.when(first == 1)
        def _():
            for c in weight_copies(0, wslot):
                c.wait()

            @pl.when(nxt >= 0)
            def _():
                for c in weight_copies(nxt, 1 - wslot):
                    c.start()

        def wait(r, c):
            row_copy(0, r, slot).wait()
            return c
        lax.fori_loop(0, plan_ref[4, i], wait, 0)

        @pl.when(i + 1 < n_used)
        def _():
            gather(i + 1, 1 - slot)

        x = xbuf[slot].astype(BF16)
        gate = _dot(x, wg[wslot].astype(BF16))
        up = _dot(x, wu[wslot].astype(BF16))
        act = (gate * jax.nn.sigmoid(gate) * up).astype(BF16)
        o_ref[...] = _dot(act, wd[wslot].astype(BF16))

    @pl.when(i >= n_used)
    def _():
        o_ref[...] = jnp.zeros_like(o_ref)


def _experts(plan, slot_tok, n_used, h, w_gate, w_up, w_down):
    n_blocks = plan.shape[1]
    d = h.shape[1]
    de = w_gate.shape[2]
    hbm = pl.BlockSpec(memory_space=pl.ANY)
    gs = pltpu.PrefetchScalarGridSpec(
        num_scalar_prefetch=3,
        grid=(n_blocks,),
        in_specs=[hbm, hbm, hbm, hbm],
        out_specs=pl.BlockSpec((MOE_BLOCK, d), lambda i, plan, tok, nu: (i, 0)),
        scratch_shapes=[pltpu.VMEM((2, MOE_BLOCK, d), F32), pltpu.VMEM((2, d, de), F32), pltpu.VMEM((2, d, de), F32),
                        pltpu.VMEM((2, de, d), F32), pltpu.SemaphoreType.DMA((2,)), pltpu.SemaphoreType.DMA((3, 2))],
    )
    return pl.pallas_call(
        _expert_kernel,
        grid_spec=gs,
        out_shape=jax.ShapeDtypeStruct((n_blocks * MOE_BLOCK, d), F32),
        compiler_params=_cp(("arbitrary",), 58 * 1024 * 1024),
        name="experts",
    )(plan, slot_tok, n_used, h, w_gate, w_up, w_down)


def _ple_kernel(dest_ref, x1_ref, route_ref, p_ref, yb_hbm, gp_ref, wpg_ref, bpg_ref, wpp_ref, gfin_ref, op_ref, os_ref,
                buf, sem, *, prompt_tiles):
    i = pl.program_id(0)
    tm = x1_ref.shape[0]

    def row_copy(slot_row, r, k, half):
        return pltpu.make_async_copy(yb_hbm.at[pl.ds(slot_row, 1), :], buf.at[half, k, pl.ds(r, 1), :], sem.at[half])

    def gather(tile, half):
        def start(r, c):
            for k in range(2):
                row_copy(dest_ref[(tile * tm + r) * 2 + k], r, k, half).start()
            return c
        lax.fori_loop(0, tm, start, 0, unroll=8)

    @pl.when(i == 0)
    def _():
        gather(0, 0)

    half = i & 1

    def wait(r, c):
        for k in range(2):
            row_copy(0, r, k, half).wait()
        return c
    lax.fori_loop(0, tm, wait, 0, unroll=8)

    @pl.when(i + 1 < pl.num_programs(0))
    def _():
        gather(i + 1, 1 - half)

    route = route_ref[...]
    moe = buf[half, 0] * route[:, 2:3] + buf[half, 1] * route[:, 3:4]
    x2 = x1_ref[...] + moe
    hn = _rms_rows(x2, gp_ref[...]).astype(BF16)
    gate = jax.nn.sigmoid(_dot(hn, wpg_ref[...]) + bpg_ref[...])
    pe = _dot(p_ref[...].astype(BF16), wpp_ref[...])
    x3 = x2 + gate * pe
    y = _rms_rows(x3, gfin_ref[...])

    @pl.when(i < prompt_tiles)
    def _():
        op_ref[...] = y

    @pl.when(i >= prompt_tiles)
    def _():
        os_ref[...] = y


def _ple(dest, x1, route, p, yb, g_ple, w_pg, b_pg, w_pp, g_final, n_p):
    m, d = x1.shape
    tm = _tile(math.gcd(n_p, m - n_p), 256)
    npt = n_p // tm
    pd = p.shape[1]
    gs = pltpu.PrefetchScalarGridSpec(
        num_scalar_prefetch=1,
        grid=(m // tm,),
        in_specs=[pl.BlockSpec((tm, d), lambda i, ds: (i, 0)),
                  pl.BlockSpec((tm, LANES), lambda i, ds: (i, 0)),
                  pl.BlockSpec((tm, pd), lambda i, ds: (i, 0)),
                  pl.BlockSpec(memory_space=pl.ANY),
                  pl.BlockSpec((1, d), lambda i, ds: (0, 0)),
                  pl.BlockSpec((d, d), lambda i, ds: (0, 0)),
                  pl.BlockSpec((1, d), lambda i, ds: (0, 0)),
                  pl.BlockSpec((pd, d), lambda i, ds: (0, 0)),
                  pl.BlockSpec((1, d), lambda i, ds: (0, 0))],
        out_specs=[pl.BlockSpec((tm, d), lambda i, ds: (jnp.minimum(i, npt - 1), 0)),
                   pl.BlockSpec((tm, d), lambda i, ds: (jnp.maximum(i - npt, 0), 0))],
        scratch_shapes=[pltpu.VMEM((2, 2, tm, d), F32), pltpu.SemaphoreType.DMA((2,))],
    )
    return pl.pallas_call(
        functools.partial(_ple_kernel, prompt_tiles=npt),
        grid_spec=gs,
        out_shape=[jax.ShapeDtypeStruct((n_p, d), F32), jax.ShapeDtypeStruct((m - n_p, d), F32)],
        compiler_params=_cp(("arbitrary",)),
        name="moe_combine_ple",
    )(dest, x1, route, p, yb, g_ple, w_pg, b_pg, w_pp, g_final)


def _dispatch(route, n_tok):
    expert = route[:, 0:2].astype(jnp.int32)
    flat_e = expert.reshape(-1)
    n_assign = flat_e.shape[0]
    onehot = (flat_e[:, None] == jnp.arange(N_EXPERTS, dtype=jnp.int32)[None, :]).astype(jnp.int32)
    csum = jnp.cumsum(onehot, axis=0)
    rank = jnp.sum(csum * onehot, axis=1) - 1
    counts = csum[-1]
    pcounts = (counts + MOE_BLOCK - 1) // MOE_BLOCK * MOE_BLOCK
    pend = jnp.cumsum(pcounts)
    pstart = pend - pcounts
    dest = (pstart[flat_e] + rank).astype(jnp.int32)
    n_blocks = -(-n_assign // MOE_BLOCK) + N_EXPERTS
    slot_tok = jnp.zeros((n_blocks * MOE_BLOCK,), jnp.int32).at[dest].set(jnp.arange(n_assign, dtype=jnp.int32) // 2)
    n_used = (pend[-1] // MOE_BLOCK).astype(jnp.int32)
    blk = jnp.arange(n_blocks, dtype=jnp.int32)
    block_e = jnp.searchsorted(pend, jnp.minimum(blk, n_used - 1) * MOE_BLOCK, side='right').astype(jnp.int32)
    block_e = jnp.minimum(block_e, N_EXPERTS - 1)
    first = jnp.concatenate([jnp.ones((1,), jnp.int32), (block_e[1:] != block_e[:-1]).astype(jnp.int32)])
    first = jnp.where(blk < n_used, first, 0)
    used = counts > 0
    eidx = jnp.arange(N_EXPERTS, dtype=jnp.int32)
    later = jnp.flip(lax.cummin(jnp.flip(jnp.where(used, eidx, N_EXPERTS))))
    next_used = jnp.concatenate([later[1:], jnp.full((1,), N_EXPERTS, jnp.int32)])
    next_used = jnp.where(next_used < N_EXPERTS, next_used, -1)
    ordinal = jnp.cumsum(used.astype(jnp.int32)) - 1
    valid = jnp.clip(pstart[block_e] + counts[block_e] - blk * MOE_BLOCK, 0, MOE_BLOCK)
    valid = jnp.where(blk < n_used, valid, 0)
    plan = jnp.stack([block_e, first, next_used[block_e], ordinal[block_e] & 1, valid]).astype(jnp.int32)
    return plan, slot_tok, n_used.reshape(1), dest


def kernel(x_prompt, x_sample, cache_k, cache_v, cache_logf, state_wkv, state_shift, page_table, p_prompt, p_sample, g_attn, w_in, mu_shift, w0, w_up, a0, a_up, g_up, k_k, k_a, r_k, lnx_w, lnx_b, b_f, q_norm, k_norm, g_fox_o, w_out, g_ffn, w_rg, b_rg, w_re, b_re, w_e_gate, w_e_up, w_e_down, g_ple, w_pg, b_pg, w_pp, g_final):
    depth = g_attn.shape[0]
    assert depth == 1
    batch, seq, d = x_prompt.shape
    db, dec_seq, _ = x_sample.shape
    assert dec_seq == DEC_SEQ
    nh = WIDTH // HEAD_DIM
    n_p = batch * seq
    n_s = db * dec_seq
    m = n_p + n_s
    rwkv_in = mu_shift.shape[1]
    n_pool = cache_k.shape[1]
    n_pages = page_table.shape[1]
    row = lambda a: a.reshape(1, -1)

    x_p, x_s = x_prompt.reshape(n_p, d), x_sample.reshape(n_s, d)
    p_all = jnp.concatenate([p_prompt[0].reshape(n_p, -1), p_sample[0].reshape(n_s, -1)], axis=0)

    wi = w_in[0]
    zc = lambda n: jnp.zeros((d, n), F32)
    w_z = jnp.concatenate([wi[:, :rwkv_in], zc(C_F - rwkv_in), wi[:, rwkv_in + 4 * WIDTH:], zc(C_Q - C_F - nh),
                           wi[:, rwkv_in:rwkv_in + 4 * WIDTH]], axis=1).astype(BF16)
    w12 = jnp.zeros((128, 2 * WIDTH), F32).at[0:64, 0:WIDTH].set(w_up[0]).at[64:128, WIDTH:].set(a_up[0]).astype(BF16)
    w3 = jnp.zeros((256, WIDTH), F32).at[0:g_up.shape[1]].set(g_up[0]).astype(BF16)
    pad_cols = lambda a, n: jnp.pad(a, ((0, 0), (0, n - a.shape[1])))
    mu_pad = pad_cols(mu_shift, RWKV_COLS)
    bf_pad = pad_cols(b_f, LANES)
    qn_t = jnp.tile(q_norm, (1, nh))
    kn_t = jnp.tile(k_norm, (1, nh))

    z = _inproj(x_p, x_s, g_attn, w_z)

    qn_s, kn_s, vv_s, logf, qa, ka, vt, kt_p, vt_p = _foxpost(z, qn_t, kn_t, bf_pad, batch, seq)

    o_f = _flash_prompt(qa, ka, vt, z, g_fox_o, batch, seq)
    lf_pages_t = cache_logf[0].transpose(0, 2, 1)
    lf_new_t = pad_cols(logf[n_p:, :nh].reshape(db, dec_seq, nh).transpose(0, 2, 1).reshape(db * nh, dec_seq), PAGE)
    o_f = _paged_sample(page_table, qn_s, kn_s, vv_s, lf_new_t.reshape(db, nh, PAGE),
                        cache_k[0].transpose(0, 2, 3, 1), cache_v[0].transpose(0, 2, 3, 1), lf_pages_t, z, g_fox_o, o_f, n_p)

    first_p = jnp.zeros((batch, 1, RWKV_COLS), F32)
    first_s = pad_cols(state_shift[0], RWKV_COLS).reshape(db, 1, RWKV_COLS)
    prep_args = (mu_pad, w12, w3, w0, a0, k_k, k_a, row(r_k))
    prep_p = _rwkv_prep(z, first_p, *prep_args, 0, n_p, seq, shared=(jnp.zeros((m, WIDTH), F32),) * 2)
    prep_s = _rwkv_prep(z, first_s, *prep_args, n_p, n_s, dec_seq, shared=prep_p[6:8])
    g_, bonus = prep_s[6:8]
    y_p, wkv_p = _wkv_scan(*prep_p[:6], jnp.zeros((batch, nh, HEAD_DIM, HEAD_DIM), F32), batch, seq)
    y_s, wkv_s = _wkv_scan(*prep_s[:6], state_wkv[0], db, dec_seq)

    x1, h2, route = _outproj(x_p, x_s, y_p, y_s, bonus, g_, o_f, lnx_w, lnx_b, w_out[0].astype(BF16), g_ffn,
                             w_rg[0].T, w_re[0].T, b_rg, b_re)

    plan, slot_tok, n_used, dest = _dispatch(route, m)
    yb = _experts(plan, slot_tok, n_used, h2, w_e_gate[0], w_e_up[0], w_e_down[0])

    y_out_p, y_out_s = _ple(dest, x1, route, p_all, yb, g_ple, w_pg[0].astype(BF16), b_pg, w_pp[0].astype(BF16),
                            row(g_final), n_p)

    shift_p = jnp.concatenate([z[(b + 1) * seq - 1:(b + 1) * seq, :rwkv_in] for b in range(batch)], axis=0)
    shift_s = z[n_p + dec_seq - 1::dec_seq, :rwkv_in]
    heads_t = lambda a: a.reshape(batch, nh, HEAD_DIM, seq).transpose(0, 3, 1, 2)[None]
    heads = lambda a: a.reshape(1, db, dec_seq, nh, HEAD_DIM)
    return (y_out_p.reshape(batch, seq, d), y_out_s.reshape(db, dec_seq, d),
            heads_t(kt_p), heads_t(vt_p), logf[:n_p, :nh].reshape(1, batch, seq, nh),
            wkv_p[None], shift_p[None],
            heads(kn_s), heads(vv_s), logf[n_p:, :nh].reshape(1, db, dec_seq, nh),
            wkv_s[None], shift_s[None])
```

```python
import functools
import math

import jax
import jax.numpy as jnp
import numpy as np
from jax import lax
from jax.experimental import pallas as pl
from jax.experimental.pallas import tpu as pltpu

F32 = jnp.float32
BF16 = jnp.bfloat16

HEAD_DIM = 64
LANES = 128
RMS_EPS = 1e-6
LNX_EPS = 64e-5
PAGE = 128
MOE_BLOCK = 384
N_GROUPS = 8
N_EXPERTS = 64
DEC_SEQ = 8

C_LORA = 3072
C_F = 3456
RWKV_COLS = 3584
C_Q, C_FK, C_FV, C_OG = 4096, 5120, 6144, 7168
Z_COLS = 8192
WIDTH = 1024

VMEM_LIMIT = 52 * 1024 * 1024


def _cp(sem, vmem=VMEM_LIMIT):
    return pltpu.CompilerParams(dimension_semantics=sem, vmem_limit_bytes=vmem)


def _tile(n, pref):
    for t in (1024, 512, 256, 128, 64, 32, 16, 8):
        if t <= pref and n % t == 0:
            return t
    raise ValueError(f"no tile for {n}")


def _split2(x):
    hi = x.astype(BF16)
    lo = (x - hi.astype(F32)).astype(BF16)
    return hi, lo


def _split3(x):
    hi = x.astype(BF16)
    r1 = x - hi.astype(F32)
    mid = r1.astype(BF16)
    lo = (r1 - mid.astype(F32)).astype(BF16)
    return hi, mid, lo


def _dot(a, b):
    return jnp.dot(a, b, preferred_element_type=F32)


def _dot_nt(a, b):
    return lax.dot_general(a, b, (((1,), (1,)), ((), ())), preferred_element_type=F32)


def _block_diag_ones():
    r = lax.broadcasted_iota(jnp.int32, (LANES, LANES), 0)
    c = lax.broadcasted_iota(jnp.int32, (LANES, LANES), 1)
    return jnp.where((r >> 6) == (c >> 6), 1.0, 0.0).astype(BF16)


def _segsum(x, bd):
    outs = []
    for j in range(x.shape[1] // LANES):
        hi, lo = _split2(x[:, j * LANES:(j + 1) * LANES])
        outs.append(_dot(hi, bd) + _dot(lo, bd))
    return outs[0] if len(outs) == 1 else jnp.concatenate(outs, axis=1)


def _softplus(x):
    return jnp.maximum(x, 0.0) + jnp.log1p(jnp.exp(-jnp.abs(x)))


def _rms_rows(x, g):
    return x * lax.rsqrt(jnp.mean(x * x, axis=-1, keepdims=True) + RMS_EPS) * g


def _row_maps(prompt_tiles):
    return (lambda i, *_: (jnp.minimum(i, prompt_tiles - 1), 0)), (lambda i, *_: (jnp.maximum(i - prompt_tiles, 0), 0))


def _inproj_kernel(xp_ref, xs_ref, g_ref, w_ref, o_ref, h_ref, *, prompt_tiles):
    @pl.when(pl.program_id(1) == 0)
    def _():
        x = jnp.where(pl.program_id(0) < prompt_tiles, xp_ref[...], xs_ref[...])
        h_ref[...] = _rms_rows(x, g_ref[...]).astype(BF16)

    o_ref[...] = _dot(h_ref[...], w_ref[...])


def _inproj(x_p, x_s, g, w):
    (n_p, d), n_s = x_p.shape, x_s.shape[0]
    n = w.shape[1]
    tm, tn = _tile(math.gcd(n_p, n_s), 1024), _tile(n, 512)
    pmap, smap = _row_maps(n_p // tm)
    return pl.pallas_call(
        functools.partial(_inproj_kernel, prompt_tiles=n_p // tm),
        grid=((n_p + n_s) // tm, n // tn),
        in_specs=[pl.BlockSpec((tm, d), pmap),
                  pl.BlockSpec((tm, d), smap),
                  pl.BlockSpec((1, d), lambda i, j: (0, 0)),
                  pl.BlockSpec((d, tn), lambda i, j: (0, j))],
        out_specs=pl.BlockSpec((tm, tn), lambda i, j: (i, j)),
        out_shape=jax.ShapeDtypeStruct((n_p + n_s, n), F32),
        scratch_shapes=[pltpu.VMEM((tm, d), BF16)],
        compiler_params=_cp(("arbitrary", "arbitrary"), 58 * 1024 * 1024),
        name="inproj",
    )(x_p, x_s, g, w)


AUG = 2 * HEAD_DIM


def _aug_constants():
    nh = WIDTH // HEAD_DIM
    pq = np.zeros((3 * LANES, WIDTH), np.float32)
    pk = np.zeros((3 * LANES, WIDTH), np.float32)
    one_q = np.zeros((1, WIDTH), np.float32)
    one_k = np.zeros((1, WIDTH), np.float32)
    for h in range(nh):
        for comp in range(3):
            pq[comp * LANES + h, h * HEAD_DIM + comp] = 1.0
            pk[comp * LANES + h, h * HEAD_DIM + 3 + comp] = -1.0
            one_q[0, h * HEAD_DIM + 3 + comp] = 1.0
            one_k[0, h * HEAD_DIM + comp] = 1.0
    return jnp.asarray(pq, BF16), jnp.asarray(pk, BF16), jnp.asarray(one_q), jnp.asarray(one_k)


def _foxpost_kernel(q_ref, k_ref, v_ref, f_ref, qn_ref, kn_ref, bf_ref, pq_ref, pk_ref, oneq_ref, onek_ref,
                    oq, ok, ov, olf, oqa, oka, ovt, okt32, ovt32, carry, *, tiles_per_seq, prompt_tiles):
    i = pl.program_id(0)
    bd = _block_diag_ones()
    nh = WIDTH // HEAD_DIM

    def head_norm(x, g):
        ms = _segsum(x * x, bd) * (1.0 / HEAD_DIM)
        return x * lax.rsqrt(ms + RMS_EPS) * g

    qn = head_norm(q_ref[...], qn_ref[...])
    kn = head_norm(k_ref[...], kn_ref[...])
    v = v_ref[...]
    vt = v.T
    ovt[...] = vt.astype(BF16)

    @pl.when(i < prompt_tiles)
    def _():
        okt32[0] = kn.T
        ovt32[0] = vt

    @pl.when(i >= prompt_tiles)
    def _():
        oq[...] = qn
        ok[...] = kn
        ov[...] = v

    lf = -_softplus(-(f_ref[...] + bf_ref[...]))
    olf[...] = lf

    @pl.when(i % tiles_per_seq == 0)
    def _():
        carry[...] = jnp.zeros_like(carry)

    tm = lf.shape[0]
    r = lax.broadcasted_iota(jnp.int32, (tm, tm), 0)
    c = lax.broadcasted_iota(jnp.int32, (tm, tm), 1)
    tri = jnp.where(c <= r, 1.0, 0.0).astype(BF16)
    hi, mid, lo = _split3(lf)
    cum = _dot(tri, hi) + _dot(tri, mid) + _dot(tri, lo) + carry[...]
    carry[...] = cum[tm - 1:tm, :]

    c3 = jnp.concatenate(_split3(cum), axis=1)
    aug_q = _dot(c3, pq_ref[...]) + oneq_ref[...]
    aug_k = _dot(c3, pk_ref[...]) + onek_ref[...]
    qs = qn * (HEAD_DIM ** -0.5)

    def interleave(x, aug):
        pieces = []
        for h in range(nh):
            sl = slice(h * HEAD_DIM, (h + 1) * HEAD_DIM)
            pieces += [x[:, sl], aug[:, sl]]
        return jnp.concatenate(pieces, axis=1).astype(BF16)

    oqa[...] = interleave(qs, aug_q)
    oka[...] = interleave(kn, aug_k)


def _foxpost(z, q_norm_t, k_norm_t, bf_pad, batch, seq_len):
    m = z.shape[0]
    nh = WIDTH // HEAD_DIM
    n_p = batch * seq_len
    tm = min(_tile(math.gcd(n_p, m - n_p), 512), seq_len)
    npt, tps = n_p // tm, seq_len // tm
    sample = pl.BlockSpec((tm, WIDTH), lambda i: (jnp.maximum(i - npt, 0), 0))
    prompt_t = pl.BlockSpec((1, WIDTH, tm), lambda i: (jnp.minimum(i, npt - 1) // tps, 0, jnp.minimum(i, npt - 1) % tps))
    row = lambda c: pl.BlockSpec((tm, WIDTH), lambda i, c=c: (i, c // WIDTH))
    vec = pl.BlockSpec((1, WIDTH), lambda i: (0, 0))
    small = pl.BlockSpec((tm, LANES), lambda i: (i, 0))
    wide = pl.BlockSpec((tm, nh * AUG), lambda i: (i, 0))
    place = pl.BlockSpec((3 * LANES, WIDTH), lambda i: (0, 0))
    return pl.pallas_call(
        functools.partial(_foxpost_kernel, tiles_per_seq=tps, prompt_tiles=npt),
        grid=(m // tm,),
        in_specs=[row(C_Q), row(C_FK), row(C_FV),
                  pl.BlockSpec((tm, LANES), lambda i: (i, C_F // LANES)),
                  vec, vec, pl.BlockSpec((1, LANES), lambda i: (0, 0)), place, place, vec, vec],
        out_specs=[sample] * 3 + [small, wide, wide, pl.BlockSpec((WIDTH, tm), lambda i: (0, i)), prompt_t, prompt_t],
        out_shape=[jax.ShapeDtypeStruct((m - n_p, WIDTH), F32)] * 3 + [jax.ShapeDtypeStruct((m, LANES), F32),
                                                                        jax.ShapeDtypeStruct((m, nh * AUG), BF16),
                                                                        jax.ShapeDtypeStruct((m, nh * AUG), BF16),
                                                                        jax.ShapeDtypeStruct((WIDTH, m), BF16),
                                                                        jax.ShapeDtypeStruct((batch, WIDTH, seq_len), F32),
                                                                        jax.ShapeDtypeStruct((batch, WIDTH, seq_len), F32)],
        scratch_shapes=[pltpu.VMEM((1, LANES), F32)],
        compiler_params=_cp(("arbitrary",)),
        name="foxpost",
    )(z, z, z, z, q_norm_t, k_norm_t, bf_pad, *_aug_constants())


FLASH_HEADS = 16


def _flash_kernel(qi_tab, ki_tab, qa_ref, ka_ref, vt_ref, og_ref, g_ref, init_hbm, o_ref, m_s, l_s, acc_s):
    s_idx = pl.program_id(2)
    qi = qi_tab[s_idx]
    ki = ki_tab[s_idx]
    tq = qa_ref.shape[0]
    tk = ka_ref.shape[0]

    @pl.when(ki == 0)
    def _():
        m_s[...] = jnp.full_like(m_s, -jnp.inf)
        l_s[...] = jnp.zeros_like(l_s)
        acc_s[...] = jnp.zeros_like(acc_s)

    def update(masked):
        for hh in range(FLASH_HEADS):
            st = _dot_nt(ka_ref[:, hh * AUG:(hh + 1) * AUG], qa_ref[:, hh * AUG:(hh + 1) * AUG])
            if masked:
                r = lax.broadcasted_iota(jnp.int32, (tk, tq), 0)
                c = lax.broadcasted_iota(jnp.int32, (tk, tq), 1)
                st = jnp.where(r <= c, st, -jnp.inf)
            m_prev = m_s[hh]
            m_new = jnp.maximum(m_prev, jnp.max(st, axis=0, keepdims=True))
            alpha = jnp.exp(m_prev - m_new)
            p = jnp.exp(st - m_new)
            l_s[hh] = alpha * l_s[hh] + jnp.sum(p, axis=0, keepdims=True)
            acc_s[hh] = alpha * acc_s[hh] + _dot(vt_ref[hh * HEAD_DIM:(hh + 1) * HEAD_DIM, :], p.astype(BF16))
            m_s[hh] = m_new

    @pl.when(ki < qi)
    def _():
        update(False)

    @pl.when(ki == qi)
    def _():
        update(True)
        outs = []
        for hh in range(FLASH_HEADS):
            sl = slice(hh * HEAD_DIM, (hh + 1) * HEAD_DIM)
            ot = acc_s[hh] / l_s[hh]
            ont = ot * lax.rsqrt(jnp.mean(ot * ot, axis=0, keepdims=True) + RMS_EPS)
            outs.append(ont.T * g_ref[:, sl] * jax.nn.sigmoid(og_ref[:, sl]))
        o_ref[...] = jnp.concatenate(outs, axis=1)


def _flash_prompt(qa, ka, vt, z, g_fox_o, batch, seq):
    tq = _tile(seq, 512)
    nq = seq // tq
    pairs = [(a, b) for a in range(nq) for b in range(a + 1)]
    qi_tab = jnp.array([a for a, _ in pairs], jnp.int32)
    ki_tab = jnp.array([b for _, b in pairs], jnp.int32)
    hw = FLASH_HEADS * HEAD_DIM
    npairs = WIDTH // hw
    qmap = lambda b, p, s, qt, kt: (b * nq + qt[s], p)
    kmap = lambda b, p, s, qt, kt: (b * nq + kt[s], p)
    gs = pltpu.PrefetchScalarGridSpec(
        num_scalar_prefetch=2,
        grid=(batch, npairs, len(pairs)),
        in_specs=[pl.BlockSpec((tq, FLASH_HEADS * AUG), qmap),
                  pl.BlockSpec((tq, FLASH_HEADS * AUG), kmap),
                  pl.BlockSpec((hw, tq), lambda b, p, s, qt, kt: (p, b * nq + kt[s])),
                  pl.BlockSpec((tq, hw), lambda b, p, s, qt, kt: (b * nq + qt[s], C_OG // hw + p)),
                  pl.BlockSpec((1, hw), lambda b, p, s, qt, kt: (0, p)),
                  pl.BlockSpec(memory_space=pl.ANY)],
        out_specs=pl.BlockSpec((tq, hw), qmap),
        scratch_shapes=[pltpu.VMEM((FLASH_HEADS, 1, tq), F32), pltpu.VMEM((FLASH_HEADS, 1, tq), F32),
                        pltpu.VMEM((FLASH_HEADS, HEAD_DIM, tq), F32)],
    )
    return pl.pallas_call(
        _flash_kernel,
        grid_spec=gs,
        out_shape=jax.ShapeDtypeStruct((qa.shape[0], WIDTH), F32),
        input_output_aliases={7: 0},
        compiler_params=_cp(("arbitrary", "arbitrary", "arbitrary")),
        name="flash_prompt",
    )(qi_tab, ki_tab, qa, ka, vt, z, g_fox_o, jnp.zeros((qa.shape[0], WIDTH), F32))


PAGE_GROUP = 4
PAGE_BUFS = 12


def _paged_kernel(pt_ref, q_ref, kn_ref, vn_ref, lfn_ref, og_ref, g_ref, ck_hbm, cv_hbm, lf_hbm, of_hbm, o_ref,
                  kbuf, vbuf, lbuf, sem, *, n_pages):
    b = pl.program_id(0)
    total = pl.num_programs(0) * n_pages
    depth = PAGE_BUFS - PAGE_GROUP
    nh = WIDTH // HEAD_DIM
    rows = nh * DEC_SEQ

    def copies(page, slot):
        return (pltpu.make_async_copy(ck_hbm.at[page], kbuf.at[slot], sem.at[0, slot]),
                pltpu.make_async_copy(cv_hbm.at[page], vbuf.at[slot], sem.at[1, slot]),
                pltpu.make_async_copy(lf_hbm.at[page], lbuf.at[slot], sem.at[2, slot]))

    def fetch(g):
        bb = g // n_pages
        for c in copies(pt_ref[bb, g - bb * n_pages], g % PAGE_BUFS):
            c.start()

    @pl.when(b == 0)
    def _():
        for g in range(depth):
            fetch(g)

    q = (q_ref[...] * (HEAD_DIM ** -0.5)).astype(BF16)
    qh = [q[:, h * HEAD_DIM:(h + 1) * HEAD_DIM] for h in range(nh)]
    def cumulate(lf, carry):
        n = lf.shape[1]
        r = lax.broadcasted_iota(jnp.int32, (n, n), 0)
        c = lax.broadcasted_iota(jnp.int32, (n, n), 1)
        upper = jnp.where(r <= c, 1.0, 0.0).astype(BF16)
        hi, mid, lo = _split3(lf)
        cum = _dot(hi, upper) + _dot(mid, upper) + _dot(lo, upper) + carry
        return cum, cum[:, n - 1:n]

    def attend(pages, feature_major, ck, valid, m, l, acc):
        qk = _dot if feature_major else _dot_nt
        pv_dot = _dot_nt if feature_major else _dot
        s = jnp.concatenate([jnp.concatenate([qk(qh[h], kh[h]) for kh, _ in pages], axis=1) for h in range(nh)], axis=0)
        s = s - jnp.concatenate([jnp.broadcast_to(ck[h:h + 1, :], (DEC_SEQ, ck.shape[1])) for h in range(nh)], axis=0)
        if valid is not None:
            s = jnp.where(valid, s, -jnp.inf)
        m_new = jnp.maximum(m, jnp.max(s, axis=-1, keepdims=True))
        alpha = jnp.exp(m - m_new)
        p = jnp.exp(s - m_new)
        l = alpha * l + jnp.sum(p, axis=-1, keepdims=True)
        pb = p.astype(BF16)
        pv = []
        for h in range(nh):
            rows_h = slice(h * DEC_SEQ, (h + 1) * DEC_SEQ)
            pv.append(sum(pv_dot(pb[rows_h, i * PAGE:(i + 1) * PAGE], vh[h]) for i, (_, vh) in enumerate(pages)))
        return m_new, l, alpha * acc + jnp.concatenate(pv, axis=0)

    def pair_step(jj, carry):
        m, l, acc, ccar = carry
        g0 = b * n_pages + PAGE_GROUP * jj
        slots = [(g0 + u) % PAGE_BUFS for u in range(PAGE_GROUP)]
        for slot in slots:
            for cp in copies(0, slot):
                cp.wait()
        for u in range(PAGE_GROUP):
            @pl.when(g0 + depth + u < total)
            def _():
                fetch(g0 + depth + u)

        ck, ccar = cumulate(jnp.concatenate([lbuf[slot] for slot in slots], axis=1), ccar)
        pages = [([kbuf[slot, h].astype(BF16) for h in range(nh)],
                  [vbuf[slot, h].astype(BF16) for h in range(nh)]) for slot in slots]
        m, l, acc = attend(pages, True, ck, None, m, l, acc)
        return m, l, acc, ccar

    init = (jnp.full((rows, 1), -jnp.inf, F32), jnp.zeros((rows, 1), F32), jnp.zeros((rows, HEAD_DIM), F32),
            jnp.zeros((nh, 1), F32))
    m, l, acc, ccar = lax.fori_loop(0, n_pages // PAGE_GROUP, pair_step, init)

    pad = jnp.zeros((PAGE - DEC_SEQ, HEAD_DIM), BF16)
    kn = kn_ref[...].astype(BF16)
    vn = vn_ref[...].astype(BF16)
    k_heads = [jnp.concatenate([kn[:, h * HEAD_DIM:(h + 1) * HEAD_DIM], pad], axis=0) for h in range(nh)]
    v_heads = [jnp.concatenate([vn[:, h * HEAD_DIM:(h + 1) * HEAD_DIM], pad], axis=0) for h in range(nh)]
    ck, _ = cumulate(lfn_ref[0], ccar)
    rr = lax.broadcasted_iota(jnp.int32, (rows, PAGE), 0)
    cc = lax.broadcasted_iota(jnp.int32, (rows, PAGE), 1)
    m, l, acc = attend([(k_heads, v_heads)], False, ck, cc <= (rr & (DEC_SEQ - 1)), m, l, acc)
    o = acc / l
    on = o * lax.rsqrt(jnp.mean(o * o, axis=-1, keepdims=True) + RMS_EPS)
    out = jnp.concatenate([on[h * DEC_SEQ:(h + 1) * DEC_SEQ, :] for h in range(nh)], axis=1)
    o_ref[...] = out * g_ref[...] * jax.nn.sigmoid(og_ref[...])


def _paged_sample(page_table, qn, kn, v, lf_new_t, cache_k, cache_v, lf_pages_t, z, g_fox_o, o_f, row0):
    db, n_pages = page_table.shape
    assert n_pages % PAGE_GROUP == 0 and n_pages >= PAGE_BUFS
    nh = WIDTH // HEAD_DIM
    rb = row0 // DEC_SEQ
    newmap = lambda b, pt: (b, 0)
    hbm = pl.BlockSpec(memory_space=pl.ANY)
    gs = pltpu.PrefetchScalarGridSpec(
        num_scalar_prefetch=1,
        grid=(db,),
        in_specs=[pl.BlockSpec((DEC_SEQ, WIDTH), newmap),
                  pl.BlockSpec((DEC_SEQ, WIDTH), newmap),
                  pl.BlockSpec((DEC_SEQ, WIDTH), newmap),
                  pl.BlockSpec((1, nh, PAGE), lambda b, pt: (b, 0, 0)),
                  pl.BlockSpec((DEC_SEQ, WIDTH), lambda b, pt: (rb + b, C_OG // WIDTH)),
                  pl.BlockSpec((1, WIDTH), lambda b, pt: (0, 0)),
                  hbm, hbm, hbm, hbm],
        out_specs=pl.BlockSpec((DEC_SEQ, WIDTH), lambda b, pt: (rb + b, 0)),
        scratch_shapes=[pltpu.VMEM((PAGE_BUFS, nh, HEAD_DIM, PAGE), F32), pltpu.VMEM((PAGE_BUFS, nh, HEAD_DIM, PAGE), F32),
                        pltpu.VMEM((PAGE_BUFS, nh, PAGE), F32), pltpu.SemaphoreType.DMA((3, PAGE_BUFS))],
    )
    return pl.pallas_call(
        functools.partial(_paged_kernel, n_pages=n_pages),
        grid_spec=gs,
        out_shape=jax.ShapeDtypeStruct(o_f.shape, F32),
        input_output_aliases={10: 0},
        compiler_params=_cp(("arbitrary",)),
        name="paged_sample",
    )(page_table, qn, kn, v, lf_new_t, z, g_fox_o, cache_k, cache_v, lf_pages_t, o_f)


def _prep_kernel(z_ref, first_ref, mu_ref, w12_ref, w3_ref, w0_ref, a0_ref, kk_ref, ka_ref, rk_ref,
                 g_hbm, bonus_hbm, o_r, o_w, o_k, o_v, o_a, o_b, o_g, o_bonus, carry, *, seq_len, tiles_per_seq):
    z = z_ref[...]
    tm = z.shape[0]
    row = lax.broadcasted_iota(jnp.int32, (tm, 1), 0)
    rolled = pltpu.roll(z, 1, 0)
    if tiles_per_seq >= 1 and seq_len >= tm:
        li = pl.program_id(0) % tiles_per_seq

        @pl.when(li == 0)
        def _():
            carry[...] = first_ref[0]

        zp = jnp.where(row == 0, carry[...], rolled)
        carry[...] = z[tm - 1:tm, :]
    else:
        nseq = tm // seq_len
        first = first_ref[...]
        exp = jnp.broadcast_to(first, (nseq, seq_len, first.shape[-1])).reshape(tm, first.shape[-1])
        zp = jnp.where((row & (seq_len - 1)) == 0, exp, rolled)
    zm = z + (zp - z) * mu_ref[...]
    r = zm[:, 0:WIDTH]
    k = zm[:, WIDTH:2 * WIDTH]
    v = zm[:, 2 * WIDTH:3 * WIDTH]
    lo = zm[:, C_LORA:C_LORA + 384]
    lane = lax.broadcasted_iota(jnp.int32, (tm, 384), 1)
    act = jnp.where(lane < 64, jnp.tanh(lo), jnp.where(lane < 128, lo, jax.nn.sigmoid(lo))).astype(BF16)
    l12 = _dot(act[:, 0:128], w12_ref[...])
    g = _dot(act[:, 128:384], w3_ref[...])
    log_w = -_softplus(-(w0_ref[...] + l12[:, 0:WIDTH])) - 0.5
    decay = jnp.exp(-jnp.exp(log_w))
    asig = jax.nn.sigmoid(a0_ref[...] + l12[:, WIDTH:2 * WIDTH])
    bd = _block_diag_ones()
    kk = k * kk_ref[...]
    kk = kk / jnp.maximum(jnp.sqrt(_segsum(kk * kk, bd)), 1e-12)
    kf = k * (1.0 + (asig - 1.0) * ka_ref[...])
    o_r[...] = r
    o_w[...] = decay
    o_k[...] = kf
    o_v[...] = v
    o_a[...] = -kk
    o_b[...] = kk * asig
    o_g[...] = g
    o_bonus[...] = _segsum(r * kf * rk_ref[...], bd) * v


def _rwkv_prep(z, first, mu, w12, w3, w0, a0, k_k, k_a, r_k, row0, n_rows, seq_len, shared):
    if seq_len >= 256:
        tm = _tile(seq_len, 256)
        first_spec = pl.BlockSpec((1, 1, RWKV_COLS), lambda i: (i // (seq_len // tm), 0, 0))
    else:
        tm = _tile(n_rows, 128)
        nseq = tm // seq_len
        first_spec = pl.BlockSpec((nseq, 1, RWKV_COLS), lambda i: (i, 0, 0))
    rb = row0 // tm
    vec = lambda n: pl.BlockSpec((1, n), lambda i: (0, 0))
    out = pl.BlockSpec((tm, WIDTH), lambda i: (i, 0))
    out_all = pl.BlockSpec((tm, WIDTH), lambda i: (rb + i, 0))
    hbm = pl.BlockSpec(memory_space=pl.ANY)
    extra = tuple(shared)
    return pl.pallas_call(
        functools.partial(_prep_kernel, seq_len=seq_len, tiles_per_seq=max(seq_len // tm, 1)),
        grid=(n_rows // tm,),
        in_specs=[pl.BlockSpec((tm, RWKV_COLS), lambda i: (rb + i, 0)), first_spec, vec(RWKV_COLS),
                  pl.BlockSpec((128, 2 * WIDTH), lambda i: (0, 0)), pl.BlockSpec((256, WIDTH), lambda i: (0, 0)),
                  vec(WIDTH), vec(WIDTH), vec(WIDTH), vec(WIDTH), vec(WIDTH)] + [hbm] * len(extra),
        out_specs=[out] * 6 + [out_all] * 2,
        out_shape=[jax.ShapeDtypeStruct((n_rows, WIDTH), F32)] * 6 + [jax.ShapeDtypeStruct((z.shape[0], WIDTH), F32)] * 2,
        input_output_aliases={10: 6, 11: 7},
        scratch_shapes=[pltpu.VMEM((1, RWKV_COLS), F32)],
        compiler_params=_cp(("arbitrary",)),
        name=f"rwkv_prep_{seq_len}",
    )(z, first, mu, w12, w3, w0, a0, k_k, k_a, r_k, *extra)


SCAN_SUB = 64
SCAN_SPLIT = 2
SCAN_UNROLL = 16


def _wkv_kernel(r_ref, w_ref, k_ref, v_ref, a_ref, b_ref, s0_ref, y_ref, sT_ref, S_s, Z_s, *, groups, n_pairs, tb_len):
    tb = pl.program_id(1)
    r2 = lax.broadcasted_iota(jnp.int32, (2 * LANES, 2 * LANES), 0)
    c2 = lax.broadcasted_iota(jnp.int32, (2 * LANES, 2 * LANES), 1)
    ones_bd = jnp.where((r2 >> 6) == (c2 >> 6), 1.0, 0.0).astype(BF16)
    vrow = lax.broadcasted_iota(jnp.int32, (HEAD_DIM, LANES), 0)
    lane = lax.broadcasted_iota(jnp.int32, (HEAD_DIM, LANES), 1)
    eye2 = (lane & (HEAD_DIM - 1)) == vrow
    zero_half = jnp.zeros((HEAD_DIM, LANES), BF16)
    H = HEAD_DIM

    @pl.when(tb == 0)
    def _():
        for g in range(groups):
            for p in range(n_pairs):
                S_s[g * n_pairs + p] = jnp.concatenate([s0_ref[g, 2 * p], s0_ref[g, 2 * p + 1]], axis=1)
        Z_s[...] = jnp.zeros_like(Z_s)

    def lanes(p):
        return slice(p * LANES, (p + 1) * LANES)

    def run(t0, n):
        def step(i, carry):
            t = t0 + i
            tp = jnp.maximum(t - 1, 0)
            hit = (lane & (SCAN_SUB - 1)) == (i - 1)
            for g in range(groups):
                aa = a_ref[g, pl.ds(t, 1), :]
                ww = w_ref[g, pl.ds(t, 1), :]
                kr = k_ref[g, pl.ds(t, 1), :]
                vv = v_ref[g, pl.ds(t, 1), :]
                bb = b_ref[g, pl.ds(t, 1), :]
                rp = r_ref[g, pl.ds(tp, 1), :]
                npg = n_pairs // SCAN_SPLIT
                for p0 in range(0, n_pairs, npg):
                    lhs_s, xs = [], []
                    for p in range(p0, p0 + npg):
                        S = S_s[g * n_pairs + p]
                        lhs_s.append(jnp.concatenate([(S * aa[:, lanes(p)]).astype(BF16),
                                                      (S * rp[:, lanes(p)]).astype(BF16)], axis=1))
                        xs.append(jnp.where(eye2, vv[:, lanes(p)], 0.0).astype(BF16))
                    lhs_x = [jnp.concatenate(xs[q:q + 2], axis=1) for q in range(0, npg, 2)]
                    res = _dot(jnp.concatenate(lhs_s + lhs_x, axis=0), ones_bd)
                    for q in range(npg):
                        p = p0 + q
                        idx = g * n_pairs + p
                        rs = res[q * H:(q + 1) * H]
                        vb = res[(npg + q // 2) * H:(npg + q // 2 + 1) * H, (q % 2) * LANES:(q % 2 + 1) * LANES]
                        S_s[idx] = S_s[idx] * ww[:, lanes(p)] + rs[:, :LANES] * bb[:, lanes(p)] + vb * kr[:, lanes(p)]
                        Z_s[idx] = jnp.where(hit, rs[:, LANES:], Z_s[idx])
            return carry

        lax.fori_loop(0, n, step, 0, unroll=SCAN_UNROLL)
        hit_last = (lane & (SCAN_SUB - 1)) == (n - 1)
        for g in range(groups):
            rl = r_ref[g, pl.ds(t0 + n - 1, 1), :]
            lhs = [jnp.concatenate([(S_s[g * n_pairs + p] * rl[:, lanes(p)]).astype(BF16), zero_half], axis=1)
                   for p in range(n_pairs)]
            res = _dot(jnp.concatenate(lhs, axis=0), ones_bd)
            for p in range(n_pairs):
                idx = g * n_pairs + p
                zt = jnp.where(hit_last, res[p * H:(p + 1) * H, :LANES], Z_s[idx]).T
                y_ref[g, pl.ds(t0, n), p * LANES:p * LANES + H] = zt[0:n, :]
                y_ref[g, pl.ds(t0, n), p * LANES + H:(p + 1) * LANES] = zt[SCAN_SUB:SCAN_SUB + n, :]

    if tb_len <= SCAN_SUB:
        run(0, tb_len)
    else:
        def outer(blk, carry):
            run(pl.multiple_of(blk * SCAN_SUB, SCAN_SUB), SCAN_SUB)
            return carry
        lax.fori_loop(0, tb_len // SCAN_SUB, outer, 0)

    @pl.when(tb == pl.num_programs(1) - 1)
    def _():
        for g in range(groups):
            for p in range(n_pairs):
                S = S_s[g * n_pairs + p]
                sT_ref[g, 2 * p] = S[:, 0:H]
                sT_ref[g, 2 * p + 1] = S[:, H:LANES]


def _wkv_scan(r, w, k, v, a, b, s0, n_seq, seq_len):
    nh = WIDTH // HEAD_DIM
    groups = 2 if n_seq % 2 == 0 else 1
    tb_len = min(seq_len, 128)
    nt = seq_len // tb_len
    blk = pl.BlockSpec((groups, tb_len, WIDTH), lambda s, t: (s, t, 0))
    st = pl.BlockSpec((groups, nh, HEAD_DIM, HEAD_DIM), lambda s, t: (s, 0, 0, 0))
    n_pairs = WIDTH // LANES
    seq3 = lambda x: x.reshape(n_seq, seq_len, WIDTH)
    y, s_new = pl.pallas_call(
        functools.partial(_wkv_kernel, groups=groups, n_pairs=n_pairs, tb_len=tb_len),
        grid=(n_seq // groups, nt),
        in_specs=[blk] * 6 + [st],
        out_specs=[blk, st],
        out_shape=[jax.ShapeDtypeStruct((n_seq, seq_len, WIDTH), F32),
                   jax.ShapeDtypeStruct((n_seq, nh, HEAD_DIM, HEAD_DIM), F32)],
        scratch_shapes=[pltpu.VMEM((groups * n_pairs, HEAD_DIM, LANES), F32),
                        pltpu.VMEM((groups * n_pairs, HEAD_DIM, LANES), F32)],
        compiler_params=_cp(("arbitrary", "arbitrary")),
        name=f"wkv_scan_{seq_len}",
    )(seq3(r), seq3(w), seq3(k), seq3(v), seq3(a), seq3(b), s0)
    return y.reshape(n_seq * seq_len, WIDTH), s_new


def _outproj_kernel(xp_ref, xs_ref, yp_ref, ys_ref, bonus_ref, g_ref, of_ref, lw_ref, lb_ref, wo_ref, gf_ref,
                    wrg_ref, wre_ref, brg_ref, bre_ref, x1_ref, h_ref, route_ref, wrh_s, wrl_s, br_s, *, prompt_tiles):
    i = pl.program_id(0)

    @pl.when(i == 0)
    def _():
        pad = LANES - N_GROUPS - N_EXPERTS
        w_r = jnp.concatenate([wrg_ref[...], wre_ref[...], jnp.zeros((pad, wrg_ref.shape[1]), F32)], axis=0)
        hi = w_r.astype(BF16)
        wrh_s[...] = hi
        wrl_s[...] = (w_r - hi.astype(F32)).astype(BF16)
        br_s[...] = jnp.concatenate([brg_ref[...], bre_ref[...], jnp.zeros((1, pad), F32)], axis=1)

    is_prompt = i < prompt_tiles
    bd = _block_diag_ones()
    y = jnp.where(is_prompt, yp_ref[...], ys_ref[...])
    mu = _segsum(y, bd) * (1.0 / HEAD_DIM)
    d = y - mu
    var = _segsum(d * d, bd) * (1.0 / HEAD_DIM)
    yn = d * lax.rsqrt(var + LNX_EPS) * lw_ref[...] + lb_ref[...]
    o_r = ((yn + bonus_ref[...]) * g_ref[...]).astype(BF16)
    o_f = of_ref[...].astype(BF16)
    x = jnp.where(is_prompt, xp_ref[...], xs_ref[...])
    x1 = x + _dot(o_r, wo_ref[0:WIDTH, :]) + _dot(o_f, wo_ref[WIDTH:2 * WIDTH, :])
    x1_ref[...] = x1
    h = _rms_rows(x1, gf_ref[...])
    h_ref[...] = h
    hi, lo = _split2(h)
    logits = _dot_nt(hi, wrh_s[...]) + _dot_nt(lo, wrh_s[...]) + _dot_nt(hi, wrl_s[...]) + br_s[...]
    tm = logits.shape[0]
    lane = lax.broadcasted_iota(jnp.int32, (tm, LANES), 1)
    neg = -jnp.inf
    lg = jnp.where(lane < N_GROUPS, logits, neg)
    mg = jnp.max(lg, axis=-1, keepdims=True)
    pg_top = 1.0 / jnp.sum(jnp.exp(lg - mg), axis=-1, keepdims=True)
    g_sel = jnp.min(jnp.where(lg == mg, lane, LANES), axis=-1, keepdims=True)
    in_grp = (lane >= N_GROUPS) & (lane < N_GROUPS + N_EXPERTS) & (((lane - N_GROUPS) >> 3) == g_sel)
    le = jnp.where(in_grp, logits, neg)
    m1 = jnp.max(le, axis=-1, keepdims=True)
    i1 = jnp.min(jnp.where(le == m1, lane, LANES), axis=-1, keepdims=True)
    le2 = jnp.where(lane == i1, neg, le)
    m2 = jnp.max(le2, axis=-1, keepdims=True)
    i2 = jnp.min(jnp.where(le2 == m2, lane, LANES), axis=-1, keepdims=True)
    e2 = jnp.exp(m2 - m1)
    gate1 = pg_top / (1.0 + e2)
    gate2 = pg_top * e2 / (1.0 + e2)
    route = jnp.where(lane == 0, (i1 - N_GROUPS).astype(F32),
                      jnp.where(lane == 1, (i2 - N_GROUPS).astype(F32),
                                jnp.where(lane == 2, gate1, jnp.where(lane == 3, gate2, 0.0))))
    route_ref[...] = route


def _outproj(x_p, x_s, y_p, y_s, bonus, g, o_f, lnx_w, lnx_b, w_out, g_ffn, w_rg, w_re, b_rg, b_re):
    (n_p, d), n_s = x_p.shape, x_s.shape[0]
    m = n_p + n_s
    tm = _tile(math.gcd(n_p, n_s), 256)
    pmap, smap = _row_maps(n_p // tm)
    row = lambda n: pl.BlockSpec((tm, n), lambda i: (i, 0))
    vec = lambda n: pl.BlockSpec((1, n), lambda i: (0, 0))
    full = lambda a: pl.BlockSpec(a.shape, lambda i: (0, 0))
    return pl.pallas_call(
        functools.partial(_outproj_kernel, prompt_tiles=n_p // tm),
        grid=(m // tm,),
        in_specs=[pl.BlockSpec((tm, d), pmap), pl.BlockSpec((tm, d), smap),
                  pl.BlockSpec((tm, WIDTH), pmap), pl.BlockSpec((tm, WIDTH), smap),
                  row(WIDTH), row(WIDTH), row(WIDTH), vec(WIDTH), vec(WIDTH),
                  full(w_out), vec(d), full(w_rg), full(w_re), full(b_rg), full(b_re)],
        out_specs=[row(d), row(d), row(LANES)],
        out_shape=[jax.ShapeDtypeStruct((m, d), F32), jax.ShapeDtypeStruct((m, d), F32),
                   jax.ShapeDtypeStruct((m, LANES), F32)],
        scratch_shapes=[pltpu.VMEM((LANES, d), BF16), pltpu.VMEM((LANES, d), BF16), pltpu.VMEM((1, LANES), F32)],
        compiler_params=_cp(("arbitrary",)),
        name="outproj_router",
    )(x_p, x_s, y_p, y_s, bonus, g, o_f, lnx_w, lnx_b, w_out, g_ffn, w_rg, w_re, b_rg, b_re)


def _expert_kernel(plan_ref, tok_ref, nu_ref, h_hbm, wg_hbm, wu_hbm, wd_hbm, o_ref, xbuf, wg, wu, wd, sem, wsem):
    i = pl.program_id(0)
    n_used = nu_ref[0]
    first, nxt, wslot = plan_ref[1, i], plan_ref[2, i], plan_ref[3, i]

    def row_copy(tok, r, slot):
        return pltpu.make_async_copy(h_hbm.at[pl.ds(tok, 1), :], xbuf.at[slot, pl.ds(r, 1), :], sem.at[slot])

    def gather(blk, slot):
        def start(r, c):
            row_copy(tok_ref[blk * MOE_BLOCK + r], r, slot).start()
            return c
        lax.fori_loop(0, plan_ref[4, blk], start, 0)

    def weight_copies(expert, slot):
        return (pltpu.make_async_copy(wg_hbm.at[expert], wg.at[slot], wsem.at[0, slot]),
                pltpu.make_async_copy(wu_hbm.at[expert], wu.at[slot], wsem.at[1, slot]),
                pltpu.make_async_copy(wd_hbm.at[expert], wd.at[slot], wsem.at[2, slot]))

    @pl.when(i == 0)
    def _():
        xbuf[...] = jnp.zeros_like(xbuf)
        gather(0, 0)
        for c in weight_copies(plan_ref[0, 0], 0):
            c.start(priority=1)

    @pl.when(i < n_used)
    def _():
        slot = i & 1

        @pl.when(first == 1)
        def _():
            for c in weight_copies(0, wslot):
                c.wait()

            @pl.when(nxt >= 0)
            def _():
                for c in weight_copies(nxt, 1 - wslot):
                    c.start(priority=1)

        def wait(r, c):
            row_copy(0, r, slot).wait()
            return c
        lax.fori_loop(0, plan_ref[4, i], wait, 0)

        @pl.when(i + 1 < n_used)
        def _():
            gather(i + 1, 1 - slot)

        x = xbuf[slot].astype(BF16)
        gate = _dot(x, wg[wslot].astype(BF16))
        up = _dot(x, wu[wslot].astype(BF16))
        act = (gate * jax.nn.sigmoid(gate) * up).astype(BF16)
        o_ref[...] = _dot(act, wd[wslot].astype(BF16))

    @pl.when(i >= n_used)
    def _():
        o_ref[...] = jnp.zeros_like(o_ref)


def _experts(plan, slot_tok, n_used, h, w_gate, w_up, w_down):
    n_blocks = plan.shape[1]
    d = h.shape[1]
    de = w_gate.shape[2]
    hbm = pl.BlockSpec(memory_space=pl.ANY)
    gs = pltpu.PrefetchScalarGridSpec(
        num_scalar_prefetch=3,
        grid=(n_blocks,),
        in_specs=[hbm, hbm, hbm, hbm],
        out_specs=pl.BlockSpec((MOE_BLOCK, d), lambda i, plan, tok, nu: (i, 0)),
        scratch_shapes=[pltpu.VMEM((2, MOE_BLOCK, d), F32), pltpu.VMEM((2, d, de), F32), pltpu.VMEM((2, d, de), F32),
                        pltpu.VMEM((2, de, d), F32), pltpu.SemaphoreType.DMA((2,)), pltpu.SemaphoreType.DMA((3, 2))],
    )
    return pl.pallas_call(
        _expert_kernel,
        grid_spec=gs,
        out_shape=jax.ShapeDtypeStruct((n_blocks * MOE_BLOCK, d), F32),
        compiler_params=_cp(("arbitrary",), 58 * 1024 * 1024),
        name="experts",
    )(plan, slot_tok, n_used, h, w_gate, w_up, w_down)


def _ple_kernel(dest_ref, x1_ref, route_ref, p_ref, yb_hbm, gp_ref, wpg_ref, bpg_ref, wpp_ref, gfin_ref, op_ref, os_ref,
                buf, sem, *, prompt_tiles):
    i = pl.program_id(0)
    tm = x1_ref.shape[0]

    def row_copy(slot_row, r, k, half):
        return pltpu.make_async_copy(yb_hbm.at[pl.ds(slot_row, 1), :], buf.at[half, k, pl.ds(r, 1), :], sem.at[half])

    def gather(tile, half):
        def start(r, c):
            for k in range(2):
                row_copy(dest_ref[(tile * tm + r) * 2 + k], r, k, half).start(priority=k)
            return c
        lax.fori_loop(0, tm, start, 0, unroll=8)

    @pl.when(i == 0)
    def _():
        gather(0, 0)

    half = i & 1

    def wait(r, c):
        for k in range(2):
            row_copy(0, r, k, half).wait()
        return c
    lax.fori_loop(0, tm, wait, 0, unroll=8)

    @pl.when(i + 1 < pl.num_programs(0))
    def _():
        gather(i + 1, 1 - half)

    route = route_ref[...]
    moe = buf[half, 0] * route[:, 2:3] + buf[half, 1] * route[:, 3:4]
    x2 = x1_ref[...] + moe
    hn = _rms_rows(x2, gp_ref[...]).astype(BF16)
    gate = jax.nn.sigmoid(_dot(hn, wpg_ref[...]) + bpg_ref[...])
    pe = _dot(p_ref[...].astype(BF16), wpp_ref[...])
    x3 = x2 + gate * pe
    y = _rms_rows(x3, gfin_ref[...])

    @pl.when(i < prompt_tiles)
    def _():
        op_ref[...] = y

    @pl.when(i >= prompt_tiles)
    def _():
        os_ref[...] = y


def _ple(dest, x1, route, p, yb, g_ple, w_pg, b_pg, w_pp, g_final, n_p):
    m, d = x1.shape
    tm = _tile(math.gcd(n_p, m - n_p), 256)
    npt = n_p // tm
    pd = p.shape[1]
    gs = pltpu.PrefetchScalarGridSpec(
        num_scalar_prefetch=1,
        grid=(m // tm,),
        in_specs=[pl.BlockSpec((tm, d), lambda i, ds: (i, 0)),
                  pl.BlockSpec((tm, LANES), lambda i, ds: (i, 0)),
                  pl.BlockSpec((tm, pd), lambda i, ds: (i, 0)),
                  pl.BlockSpec(memory_space=pl.ANY),
                  pl.BlockSpec((1, d), lambda i, ds: (0, 0)),
                  pl.BlockSpec((d, d), lambda i, ds: (0, 0)),
                  pl.BlockSpec((1, d), lambda i, ds: (0, 0)),
                  pl.BlockSpec((pd, d), lambda i, ds: (0, 0)),
                  pl.BlockSpec((1, d), lambda i, ds: (0, 0))],
        out_specs=[pl.BlockSpec((tm, d), lambda i, ds: (jnp.minimum(i, npt - 1), 0)),
                   pl.BlockSpec((tm, d), lambda i, ds: (jnp.maximum(i - npt, 0), 0))],
        scratch_shapes=[pltpu.VMEM((2, 2, tm, d), F32), pltpu.SemaphoreType.DMA((2,))],
    )
    return pl.pallas_call(
        functools.partial(_ple_kernel, prompt_tiles=npt),
        grid_spec=gs,
        out_shape=[jax.ShapeDtypeStruct((n_p, d), F32), jax.ShapeDtypeStruct((m - n_p, d), F32)],
        compiler_params=_cp(("arbitrary",)),
        name="moe_combine_ple",
    )(dest, x1, route, p, yb, g_ple, w_pg, b_pg, w_pp, g_final)


def _dispatch(route, n_tok):
    expert = route[:, 0:2].astype(jnp.int32)
    flat_e = expert.reshape(-1)
    n_assign = flat_e.shape[0]
    onehot = (flat_e[:, None] == jnp.arange(N_EXPERTS, dtype=jnp.int32)[None, :]).astype(jnp.int32)
    csum = jnp.cumsum(onehot, axis=0)
    rank = jnp.sum(csum * onehot, axis=1) - 1
    counts = csum[-1]
    pcounts = (counts + MOE_BLOCK - 1) // MOE_BLOCK * MOE_BLOCK
    pend = jnp.cumsum(pcounts)
    pstart = pend - pcounts
    dest = (pstart[flat_e] + rank).astype(jnp.int32)
    n_blocks = -(-n_assign // MOE_BLOCK) + N_EXPERTS
    slot_tok = jnp.zeros((n_blocks * MOE_BLOCK,), jnp.int32).at[dest].set(jnp.arange(n_assign, dtype=jnp.int32) // 2)
    n_used = (pend[-1] // MOE_BLOCK).astype(jnp.int32)
    blk = jnp.arange(n_blocks, dtype=jnp.int32)
    block_e = jnp.searchsorted(pend, jnp.minimum(blk, n_used - 1) * MOE_BLOCK, side='right').astype(jnp.int32)
    block_e = jnp.minimum(block_e, N_EXPERTS - 1)
    first = jnp.concatenate([jnp.ones((1,), jnp.int32), (block_e[1:] != block_e[:-1]).astype(jnp.int32)])
    first = jnp.where(blk < n_used, first, 0)
    used = counts > 0
    eidx = jnp.arange(N_EXPERTS, dtype=jnp.int32)
    later = jnp.flip(lax.cummin(jnp.flip(jnp.where(used, eidx, N_EXPERTS))))
    next_used = jnp.concatenate([later[1:], jnp.full((1,), N_EXPERTS, jnp.int32)])
    next_used = jnp.where(next_used < N_EXPERTS, next_used, -1)
    ordinal = jnp.cumsum(used.astype(jnp.int32)) - 1
    valid = jnp.clip(pstart[block_e] + counts[block_e] - blk * MOE_BLOCK, 0, MOE_BLOCK)
    valid = jnp.where(blk < n_used, valid, 0)
    plan = jnp.stack([block_e, first, next_used[block_e], ordinal[block_e] & 1, valid]).astype(jnp.int32)
    return plan, slot_tok, n_used.reshape(1), dest


def kernel(x_prompt, x_sample, cache_k, cache_v, cache_logf, state_wkv, state_shift, page_table, p_prompt, p_sample, g_attn, w_in, mu_shift, w0, w_up, a0, a_up, g_up, k_k, k_a, r_k, lnx_w, lnx_b, b_f, q_norm, k_norm, g_fox_o, w_out, g_ffn, w_rg, b_rg, w_re, b_re, w_e_gate, w_e_up, w_e_down, g_ple, w_pg, b_pg, w_pp, g_final):
    depth = g_attn.shape[0]
    assert depth == 1
    batch, seq, d = x_prompt.shape
    db, dec_seq, _ = x_sample.shape
    assert dec_seq == DEC_SEQ
    nh = WIDTH // HEAD_DIM
    n_p = batch * seq
    n_s = db * dec_seq
    m = n_p + n_s
    rwkv_in = mu_shift.shape[1]
    n_pool = cache_k.shape[1]
    n_pages = page_table.shape[1]
    row = lambda a: a.reshape(1, -1)

    x_p, x_s = x_prompt.reshape(n_p, d), x_sample.reshape(n_s, d)
    p_all = jnp.concatenate([p_prompt[0].reshape(n_p, -1), p_sample[0].reshape(n_s, -1)], axis=0)

    wi = w_in[0]
    zc = lambda n: jnp.zeros((d, n), F32)
    w_z = jnp.concatenate([wi[:, :rwkv_in], zc(C_F - rwkv_in), wi[:, rwkv_in + 4 * WIDTH:], zc(C_Q - C_F - nh),
                           wi[:, rwkv_in:rwkv_in + 4 * WIDTH]], axis=1).astype(BF16)
    w12 = jnp.zeros((128, 2 * WIDTH), F32).at[0:64, 0:WIDTH].set(w_up[0]).at[64:128, WIDTH:].set(a_up[0]).astype(BF16)
    w3 = jnp.zeros((256, WIDTH), F32).at[0:g_up.shape[1]].set(g_up[0]).astype(BF16)
    pad_cols = lambda a, n: jnp.pad(a, ((0, 0), (0, n - a.shape[1])))
    mu_pad = pad_cols(mu_shift, RWKV_COLS)
    bf_pad = pad_cols(b_f, LANES)
    qn_t = jnp.tile(q_norm, (1, nh))
    kn_t = jnp.tile(k_norm, (1, nh))

    z = _inproj(x_p, x_s, g_attn, w_z)

    qn_s, kn_s, vv_s, logf, qa, ka, vt, kt_p, vt_p = _foxpost(z, qn_t, kn_t, bf_pad, batch, seq)

    o_f = _flash_prompt(qa, ka, vt, z, g_fox_o, batch, seq)
    lf_pages_t = cache_logf[0].transpose(0, 2, 1)
    lf_new_t = pad_cols(logf[n_p:, :nh].reshape(db, dec_seq, nh).transpose(0, 2, 1).reshape(db * nh, dec_seq), PAGE)
    o_f = _paged_sample(page_table, qn_s, kn_s, vv_s, lf_new_t.reshape(db, nh, PAGE),
                        cache_k[0].transpose(0, 2, 3, 1), cache_v[0].transpose(0, 2, 3, 1), lf_pages_t, z, g_fox_o, o_f, n_p)

    first_p = jnp.zeros((batch, 1, RWKV_COLS), F32)
    first_s = pad_cols(state_shift[0], RWKV_COLS).reshape(db, 1, RWKV_COLS)
    prep_args = (mu_pad, w12, w3, w0, a0, k_k, k_a, row(r_k))
    prep_p = _rwkv_prep(z, first_p, *prep_args, 0, n_p, seq, shared=(jnp.zeros((m, WIDTH), F32),) * 2)
    prep_s = _rwkv_prep(z, first_s, *prep_args, n_p, n_s, dec_seq, shared=prep_p[6:8])
    g_, bonus = prep_s[6:8]
    y_p, wkv_p = _wkv_scan(*prep_p[:6], jnp.zeros((batch, nh, HEAD_DIM, HEAD_DIM), F32), batch, seq)
    y_s, wkv_s = _wkv_scan(*prep_s[:6], state_wkv[0], db, dec_seq)

    x1, h2, route = _outproj(x_p, x_s, y_p, y_s, bonus, g_, o_f, lnx_w, lnx_b, w_out[0].astype(BF16), g_ffn,
                             w_rg[0].T, w_re[0].T, b_rg, b_re)

    plan, slot_tok, n_used, dest = _dispatch(route, m)
    yb = _experts(plan, slot_tok, n_used, h2, w_e_gate[0], w_e_up[0], w_e_down[0])

    y_out_p, y_out_s = _ple(dest, x1, route, p_all, yb, g_ple, w_pg[0].astype(BF16), b_pg, w_pp[0].astype(BF16),
                            row(g_final), n_p)

    shift_p = jnp.concatenate([z[(b + 1) * seq - 1:(b + 1) * seq, :rwkv_in] for b in range(batch)], axis=0)
    shift_s = z[n_p + dec_seq - 1::dec_seq, :rwkv_in]
    heads_t = lambda a: a.reshape(batch, nh, HEAD_DIM, seq).transpose(0, 3, 1, 2)[None]
    heads = lambda a: a.reshape(1, db, dec_seq, nh, HEAD_DIM)
    return (y_out_p.reshape(batch, seq, d), y_out_s.reshape(db, dec_seq, d),
            heads_t(kt_p), heads_t(vt_p), logf[:n_p, :nh].reshape(1, batch, seq, nh),
            wkv_p[None], shift_p[None],
            heads(kn_s), heads(vv_s), logf[n_p:, :nh].reshape(1, db, dec_seq, nh),
            wkv_s[None], shift_s[None])
```
